```python
import math
import jax, jax.numpy as jnp
from jax import lax
import numpy as np

D_MODEL = 1024
BATCH = 8
SEQ = 2048
DEPTH = 1

D_RNN = 1024
RNN_BLOCKS = 16
RNN_BLOCK_DIM = D_RNN // RNN_BLOCKS
CONV_WIDTH = 4
LRU_C = 8.0
N_HEADS = 16
HEAD_DIM = 64
N_KV_GROUPS = 4
HEADS_PER_GROUP = N_HEADS // N_KV_GROUPS
Q_DIM = N_HEADS * HEAD_DIM
KV_DIM = N_KV_GROUPS * HEAD_DIM
CMP_BLOCK = 32
CMP_STRIDE = 16
CMP_HIDDEN = 256
SEL_BLOCK = 64
SEL_TOPN = 16
WINDOW = 512
Q_BLOCK = 128
SEL_Q_CHUNK = 32
ROPE_THETA = 10000.0
FORCE_BONUS = 1e4
NEG_INF = -1e30
N_GROUPS = 4
EXPERTS_PER_GROUP = 4
N_EXPERTS = N_GROUPS * EXPERTS_PER_GROUP
TOP_K_IN_GROUP = 2
D_EXPERT = 512
EPS = 1e-6
IN_DIM = 2 * D_RNN + Q_DIM + 6 * KV_DIM + 3 * N_HEADS + 2 * D_MODEL

kernel_name = "hybrid_rglru_nsa_hiermoe"


def _split_points():
    sizes = [D_RNN, D_RNN, Q_DIM] + [KV_DIM] * 6 + [3 * N_HEADS, 2 * D_MODEL]
    return tuple(int(v) for v in np.cumsum(sizes)[:-1])


def rmsnorm(x, g):
    x32 = x.astype(jnp.float32)
    y = x32 * lax.rsqrt(jnp.mean(x32 * x32, axis=-1, keepdims=True) + EPS)
    return y.astype(x.dtype) * g


def rope_tables(pos):
    inv_freq = ROPE_THETA ** (-(jnp.arange(0, HEAD_DIM, 2, dtype=jnp.float32) / HEAD_DIM))
    ang = pos.astype(jnp.float32)[:, None] * inv_freq[None, :]
    return jnp.cos(ang), jnp.sin(ang)


def apply_rope(x, cos, sin):
    x1, x2 = jnp.split(x, 2, axis=-1)
    c = cos[None, :, None, :].astype(x.dtype)
    s = sin[None, :, None, :].astype(x.dtype)
    return jnp.concatenate([x1 * c - x2 * s, x2 * c + x1 * s], axis=-1)


def causal_depthwise_conv(x, w, b):
    y = lax.conv_general_dilated(x, w[:, None, :], window_strides=(1,),
                                 padding=[(CONV_WIDTH - 1, 0)],
                                 dimension_numbers=('NWC', 'WIO', 'NWC'),
                                 feature_group_count=x.shape[-1])
    return y + b


def rg_lru(x, w_a, b_a, w_i, b_i, lam):
    B, S, _ = x.shape
    xb = x.reshape(B, S, RNN_BLOCKS, RNN_BLOCK_DIM)
    r = jax.nn.sigmoid(jnp.einsum('bshi,hij->bshj', xb, w_a).reshape(B, S, D_RNN) + b_a)
    i = jax.nn.sigmoid(jnp.einsum('bshi,hij->bshj', xb, w_i).reshape(B, S, D_RNN) + b_i)
    log_a = -LRU_C * r.astype(jnp.float32) * jax.nn.softplus(-lam.astype(jnp.float32))
    a = jnp.exp(log_a)
    u = jnp.sqrt(-jnp.expm1(2.0 * log_a)) * (i * x).astype(jnp.float32)

    def combine(c1, c2):
        a1, b1 = c1
        a2, b2 = c2
        return a1 * a2, a2 * b1 + b2

    _, h = lax.associative_scan(combine, (a, u), axis=1)
    return h.astype(x.dtype)


def compress_blocks(k, pos_emb, w1, b1, w2, b2):
    B, S = k.shape[:2]
    n_cmp = (S - CMP_BLOCK) // CMP_STRIDE + 1
    idx = np.arange(n_cmp)[:, None] * CMP_STRIDE + np.arange(CMP_BLOCK)[None, :]
    blocks = k[:, idx] + pos_emb[None, None, :, None, :]
    blocks = blocks.transpose(0, 1, 3, 2, 4).reshape(B, n_cmp, N_KV_GROUPS, CMP_BLOCK * HEAD_DIM)
    hid = jax.nn.gelu(blocks @ w1 + b1)
    return hid @ w2 + b2


def _cmp_to_sel_matrix(n_cmp, n_sel):
    c0 = np.arange(n_cmp) * CMP_STRIDE
    s0 = np.arange(n_sel) * SEL_BLOCK
    ov = np.minimum(c0[:, None] + CMP_BLOCK, s0[None, :] + SEL_BLOCK) - np.maximum(c0[:, None], s0[None, :])
    return (np.clip(ov, 0, None) / CMP_BLOCK).astype(np.float32)


def selected_attention(qg, ks, vs, sel_idx, scale):
    B, S = qg.shape[:2]
    n_sel = S // SEL_BLOCK
    n_top = sel_idx.shape[-1]
    kb = ks.reshape(B, n_sel, SEL_BLOCK, N_KV_GROUPS, HEAD_DIM).transpose(0, 3, 1, 2, 4)
    vb = vs.reshape(B, n_sel, SEL_BLOCK, N_KV_GROUPS, HEAD_DIM).transpose(0, 3, 1, 2, 4)
    nc = S // SEL_Q_CHUNK
    qc = jnp.moveaxis(qg.reshape(B, nc, SEL_Q_CHUNK, N_KV_GROUPS, HEADS_PER_GROUP, HEAD_DIM), 1, 0)
    ic = jnp.moveaxis(sel_idx.reshape(B, N_KV_GROUPS, nc, SEL_Q_CHUNK, n_top), 2, 0)
    b_ix = jnp.arange(B)[:, None, None, None]
    g_ix = jnp.arange(N_KV_GROUPS)[None, :, None, None]
    n_keys = n_top * SEL_BLOCK

    def one_chunk(args):
        c, q_c, idx_c = args
        k_sel = kb[b_ix, g_ix, idx_c].reshape(B, N_KV_GROUPS, SEL_Q_CHUNK, n_keys, HEAD_DIM)
        v_sel = vb[b_ix, g_ix, idx_c].reshape(B, N_KV_GROUPS, SEL_Q_CHUNK, n_keys, HEAD_DIM)
        kpos = (idx_c[..., None] * SEL_BLOCK + jnp.arange(SEL_BLOCK)).reshape(B, N_KV_GROUPS, SEL_Q_CHUNK, n_keys)
        qpos = c * SEL_Q_CHUNK + jnp.arange(SEL_Q_CHUNK)
        mask = kpos <= qpos[None, None, :, None]
        s = jnp.einsum('bqghd,bgqkd->bghqk', q_c, k_sel).astype(jnp.float32) * scale
        s = jnp.where(mask[:, :, None], s, NEG_INF)
        p = jax.nn.softmax(s, axis=-1)
        return jnp.einsum('bghqk,bgqkd->bqghd', p.astype(v_sel.dtype), v_sel)

    out = lax.map(one_chunk, (jnp.arange(nc), qc, ic))
    return jnp.moveaxis(out, 0, 1).reshape(B, S, N_KV_GROUPS, HEADS_PER_GROUP, HEAD_DIM)


def window_attention(qg, kw, vw, scale):
    B, S = qg.shape[:2]
    nb = S // Q_BLOCK
    span = WINDOW + Q_BLOCK
    kpad = jnp.pad(kw, ((0, 0), (WINDOW, 0), (0, 0), (0, 0)))
    vpad = jnp.pad(vw, ((0, 0), (WINDOW, 0), (0, 0), (0, 0)))
    qb = jnp.moveaxis(qg.reshape(B, nb, Q_BLOCK, N_KV_GROUPS, HEADS_PER_GROUP, HEAD_DIM), 1, 0)

    def one_block(args):
        i, q_blk = args
        start = i * Q_BLOCK
        k_blk = lax.dynamic_slice_in_dim(kpad, start, span, axis=1)
        v_blk = lax.dynamic_slice_in_dim(vpad, start, span, axis=1)
        qpos = start + jnp.arange(Q_BLOCK)
        kpos = start - WINDOW + jnp.arange(span)
        d = qpos[:, None] - kpos[None, :]
        mask = (kpos[None, :] >= 0) & (d >= 0) & (d < WINDOW)
        s = jnp.einsum('bqghd,bkgd->bghqk', q_blk, k_blk).astype(jnp.float32) * scale
        s = jnp.where(mask, s, NEG_INF)
        p = jax.nn.softmax(s, axis=-1)
        return jnp.einsum('bghqk,bkgd->bqghd', p.astype(v_blk.dtype), v_blk)

    out = lax.map(one_block, (jnp.arange(nb), qb))
    return jnp.moveaxis(out, 0, 1).reshape(B, S, N_KV_GROUPS, HEADS_PER_GROUP, HEAD_DIM)


def nsa_attention(q, kc, vc, ks, vs, kw, vw, gates, cos, sin,
                  cmpk_pos, cmpk_w1, cmpk_b1, cmpk_w2, cmpk_b2,
                  cmpv_pos, cmpv_w1, cmpv_b1, cmpv_w2, cmpv_b2):
    B, S = q.shape[:2]
    scale = HEAD_DIM ** -0.5
    q = apply_rope(q.reshape(B, S, N_HEADS, HEAD_DIM), cos, sin)
    qg = q.reshape(B, S, N_KV_GROUPS, HEADS_PER_GROUP, HEAD_DIM)
    kv = lambda t: t.reshape(B, S, N_KV_GROUPS, HEAD_DIM)
    kc, vc, ks, vs, kw, vw = kv(kc), kv(vc), kv(ks), kv(vs), kv(kw), kv(vw)
    ks = apply_rope(ks, cos, sin)
    kw = apply_rope(kw, cos, sin)

    n_cmp = (S - CMP_BLOCK) // CMP_STRIDE + 1
    kcmp = compress_blocks(kc, cmpk_pos, cmpk_w1, cmpk_b1, cmpk_w2, cmpk_b2)
    vcmp = compress_blocks(vc, cmpv_pos, cmpv_w1, cmpv_b1, cmpv_w2, cmpv_b2)
    ends = np.arange(n_cmp) * CMP_STRIDE + CMP_BLOCK - 1
    ccos, csin = rope_tables(jnp.asarray(ends))
    kcmp = apply_rope(kcmp, ccos, csin)
    valid = ends[None, :] <= np.arange(S)[:, None]
    s = jnp.einsum('bsghd,bngd->bghsn', qg, kcmp).astype(jnp.float32) * scale
    s = jnp.where(valid, s, NEG_INF)
    p = jax.nn.softmax(s, axis=-1)
    p = jnp.where(valid.any(axis=-1)[:, None], p, 0.0)
    o_cmp = jnp.einsum('bghsn,bngd->bsghd', p.astype(vcmp.dtype), vcmp)

    n_sel = S // SEL_BLOCK
    n_top = min(SEL_TOPN, n_sel)
    p_slc = jnp.einsum('bgsn,nm->bgsm', p.sum(axis=2), _cmp_to_sel_matrix(n_cmp, n_sel))
    tblk = np.arange(S)[:, None] // SEL_BLOCK
    blk = np.arange(n_sel)[None, :]
    allowed = blk <= tblk
    forced = (blk == 0) | (blk == tblk) | (blk == tblk - 1)
    score = jnp.where(allowed, p_slc + jnp.where(forced, FORCE_BONUS, 0.0), -1.0)
    _, sel_idx = lax.top_k(score, n_top)
    o_sel = selected_attention(qg, ks, vs, sel_idx, scale)

    o_win = window_attention(qg, kw, vw, scale)

    g = jax.nn.sigmoid(gates).reshape(B, S, 3, N_KV_GROUPS, HEADS_PER_GROUP)[..., None]
    o = g[:, :, 0] * o_cmp + g[:, :, 1] * o_sel + g[:, :, 2] * o_win
    return o.reshape(B, S, Q_DIM)


def hier_moe(h, rg_w, rg_b, re_w, re_b, w_gate, w_up, w_down):
    B, S, D = h.shape
    ht = h.reshape(B * S, D)
    gp = jax.nn.softmax((ht @ rg_w + rg_b).astype(jnp.float32), axis=-1)
    g_w, g_idx = lax.top_k(gp, 1)
    g_idx = g_idx[:, 0]
    el = jnp.einsum('td,gde->tge', ht, re_w) + re_b
    el_sel = jnp.einsum('tg,tge->te', jax.nn.one_hot(g_idx, N_GROUPS, dtype=el.dtype), el)
    ep = jax.nn.softmax(el_sel.astype(jnp.float32), axis=-1)
    top_w, top_i = lax.top_k(ep, TOP_K_IN_GROUP)
    top_w = top_w / jnp.sum(top_w, axis=-1, keepdims=True)
    ids = g_idx[:, None] * EXPERTS_PER_GROUP + top_i
    comb = jnp.einsum('tk,tke->te', g_w * top_w,
                      jax.nn.one_hot(ids, N_EXPERTS, dtype=jnp.float32)).astype(ht.dtype)
    y = jnp.zeros_like(ht)
    for e in range(N_EXPERTS):
        y = y + comb[:, e:e + 1] * ((jax.nn.silu(ht @ w_gate[e]) * (ht @ w_up[e])) @ w_down[e])
    return y.reshape(B, S, D)


def setup_inputs(seed: int = 0) -> dict:
    key = jax.random.key(seed)
    ks = iter(jax.random.split(key, 40))
    L = DEPTH
    nrm = lambda shape, fan_in: jax.random.normal(next(ks), shape, jnp.float32) * fan_in ** -0.5
    gain = lambda shape: 1.0 + 0.01 * jax.random.normal(next(ks), shape, jnp.float32)
    bias = lambda shape: 0.01 * jax.random.normal(next(ks), shape, jnp.float32)
    a_c = jax.random.uniform(next(ks), (L, D_RNN), jnp.float32, 0.9, 0.999)
    a0 = a_c ** (1.0 / LRU_C)
    return {
        "x": jax.random.normal(next(ks), (BATCH, SEQ, D_MODEL), jnp.float32),
        "norm1_g": gain((L, D_MODEL)),
        "w_in": nrm((L, D_MODEL, IN_DIM), D_MODEL),
        "conv_w": nrm((L, CONV_WIDTH, D_RNN), CONV_WIDTH),
        "conv_b": bias((L, D_RNN)),
        "lru_wa": nrm((L, RNN_BLOCKS, RNN_BLOCK_DIM, RNN_BLOCK_DIM), RNN_BLOCK_DIM),
        "lru_ba": bias((L, D_RNN)),
        "lru_wi": nrm((L, RNN_BLOCKS, RNN_BLOCK_DIM, RNN_BLOCK_DIM), RNN_BLOCK_DIM),
        "lru_bi": bias((L, D_RNN)),
        "lru_lambda": jnp.log(a0) - jnp.log1p(-a0),
        "w_rnn_out": nrm((L, D_RNN, D_MODEL), D_RNN),
        "cmpk_pos": 0.02 * jax.random.normal(next(ks), (L, CMP_BLOCK, HEAD_DIM), jnp.float32),
        "cmpk_w1": nrm((L, CMP_BLOCK * HEAD_DIM, CMP_HIDDEN), CMP_BLOCK * HEAD_DIM),
        "cmpk_b1": bias((L, CMP_HIDDEN)),
        "cmpk_w2": nrm((L, CMP_HIDDEN, HEAD_DIM), CMP_HIDDEN),
        "cmpk_b2": bias((L, HEAD_DIM)),
        "cmpv_pos": 0.02 * jax.random.normal(next(ks), (L, CMP_BLOCK, HEAD_DIM), jnp.float32),
        "cmpv_w1": nrm((L, CMP_BLOCK * HEAD_DIM, CMP_HIDDEN), CMP_BLOCK * HEAD_DIM),
        "cmpv_b1": bias((L, CMP_HIDDEN)),
        "cmpv_w2": nrm((L, CMP_HIDDEN, HEAD_DIM), CMP_HIDDEN),
        "cmpv_b2": bias((L, HEAD_DIM)),
        "w_nsa_out": nrm((L, Q_DIM, D_MODEL), Q_DIM),
        "w_mix_out": nrm((L, D_MODEL, D_MODEL), D_MODEL),
        "norm2_g": gain((L, D_MODEL)),
        "router_group_w": nrm((L, D_MODEL, N_GROUPS), D_MODEL),
        "router_group_b": bias((L, N_GROUPS)),
        "router_expert_w": nrm((L, N_GROUPS, D_MODEL, EXPERTS_PER_GROUP), D_MODEL),
        "router_expert_b": bias((L, N_GROUPS, EXPERTS_PER_GROUP)),
        "expert_w_gate": nrm((L, N_EXPERTS, D_MODEL, D_EXPERT), D_MODEL),
        "expert_w_up": nrm((L, N_EXPERTS, D_MODEL, D_EXPERT), D_MODEL),
        "expert_w_down": nrm((L, N_EXPERTS, D_EXPERT, D_MODEL), D_EXPERT),
        "final_norm_g": gain((D_MODEL,)),
    }


def reference(x, norm1_g, w_in, conv_w, conv_b, lru_wa, lru_ba, lru_wi, lru_bi, lru_lambda,
              w_rnn_out, cmpk_pos, cmpk_w1, cmpk_b1, cmpk_w2, cmpk_b2,
              cmpv_pos, cmpv_w1, cmpv_b1, cmpv_w2, cmpv_b2, w_nsa_out, w_mix_out, norm2_g,
              router_group_w, router_group_b, router_expert_w, router_expert_b,
              expert_w_gate, expert_w_up, expert_w_down, final_norm_g):
    B, S, _ = x.shape
    cos, sin = rope_tables(jnp.arange(S))
    splits = _split_points()
    for l in range(DEPTH):
        h = rmsnorm(x, norm1_g[l])
        proj = h @ w_in[l]
        xr, gr, q, kc, vc, ksl, vsl, kw, vw, nsa_g, merge_g = jnp.split(proj, splits, axis=-1)
        hr = rg_lru(causal_depthwise_conv(xr, conv_w[l], conv_b[l]),
                    lru_wa[l], lru_ba[l], lru_wi[l], lru_bi[l], lru_lambda[l])
        y_a = (jax.nn.gelu(gr) * hr) @ w_rnn_out[l]
        o_nsa = nsa_attention(q, kc, vc, ksl, vsl, kw, vw, nsa_g, cos, sin,
                              cmpk_pos[l], cmpk_w1[l], cmpk_b1[l], cmpk_w2[l], cmpk_b2[l],
                              cmpv_pos[l], cmpv_w1[l], cmpv_b1[l], cmpv_w2[l], cmpv_b2[l])
        y_b = o_nsa @ w_nsa_out[l]
        g_a, g_b = jnp.split(jax.nn.sigmoid(merge_g), 2, axis=-1)
        x = x + (g_a * y_a + g_b * y_b) @ w_mix_out[l]
        h2 = rmsnorm(x, norm2_g[l])
        x = x + hier_moe(h2, router_group_w[l], router_group_b[l], router_expert_w[l],
                         router_expert_b[l], expert_w_gate[l], expert_w_up[l], expert_w_down[l])
    return rmsnorm(x, final_norm_g)
```

```python
import functools

import numpy as np
import jax
import jax.numpy as jnp
from jax import lax
from jax.experimental import pallas as pl
from jax.experimental.pallas import tpu as pltpu

D_MODEL = 1024
D_RNN = 1024
RNN_BLOCKS = 16
RNN_BLOCK_DIM = D_RNN // RNN_BLOCKS
CONV_WIDTH = 4
LRU_C = 8.0
N_HEADS = 16
HEAD_DIM = 64
HALF_DIM = HEAD_DIM // 2
N_KV_GROUPS = 4
HEADS_PER_GROUP = N_HEADS // N_KV_GROUPS
Q_DIM = N_HEADS * HEAD_DIM
KV_DIM = N_KV_GROUPS * HEAD_DIM
CMP_BLOCK = 32
CMP_STRIDE = 16
CMP_HIDDEN = 256
SEL_BLOCK = 64
SEL_TOPN = 16
WINDOW = 512
ROPE_THETA = 10000.0
FORCE_BONUS = 1e4
NEG_INF = -1e30
N_GROUPS = 4
EXPERTS_PER_GROUP = 4
N_EXPERTS = N_GROUPS * EXPERTS_PER_GROUP
D_EXPERT = 512
EPS = 1e-6

LANES = 128
SUBLANES = 8
VMEM_LIMIT = 56 * 1024 * 1024

BF16 = jnp.bfloat16
F32 = jnp.float32

COL_XR = 0
COL_GR = COL_XR + D_RNN
COL_Q = COL_GR + D_RNN
COL_MG = COL_Q + Q_DIM
COL_KCV = COL_MG + 2 * D_MODEL
COL_KS = COL_KCV + 2 * KV_DIM
COL_VS = COL_KS + KV_DIM
COL_KW = COL_VS + KV_DIM
COL_VW = COL_KW + KV_DIM
COL_NG = COL_VW + KV_DIM
N_PROJ = COL_NG + N_KV_GROUPS * LANES

TM_PROJ = 256
TS_RNN = 512
TR_CMP = 512
TQ = 128
TK = 128
TM_POST = 256
TM_MOE = 1024
N_CMP_PAD = 128
N_WIN_TILES = WINDOW // TK + 1


def _dot(a, b):
    return jnp.dot(a, b, preferred_element_type=F32)


def _dot_t(a, b):
    return lax.dot_general(a, b, (((1,), (1,)), ((), ())), preferred_element_type=F32)


def _lane_iota(shape):
    return lax.broadcasted_iota(jnp.int32, shape, len(shape) - 1)


def _row_iota(shape):
    return lax.broadcasted_iota(jnp.int32, shape, 0)


def _rope(x, cos, sin_signed):
    width = x.shape[-1]
    reps = width // cos.shape[-1]
    if reps > 1:
        cos = jnp.concatenate([cos] * reps, axis=1)
        sin_signed = jnp.concatenate([sin_signed] * reps, axis=1)
    first_half = (_lane_iota(x.shape) & (HEAD_DIM - 1)) < HALF_DIM
    partner = jnp.where(first_half, pltpu.roll(x, width - HALF_DIM, 1), pltpu.roll(x, HALF_DIM, 1))
    return x * cos + partner * sin_signed


def _dup_heads(x):
    out = []
    low = _lane_iota((x.shape[0], LANES)) < HEAD_DIM
    for c in range(x.shape[1] // LANES):
        xc = x[:, c * LANES:(c + 1) * LANES]
        rolled = pltpu.roll(xc, HEAD_DIM, 1)
        out.append(jnp.where(low, xc, rolled))
        out.append(jnp.where(low, rolled, xc))
    return jnp.concatenate(out, axis=1)


def _inproj_kernel(x_ref, g_ref, w_ref, cos_ref, sin_ref,
                   xg_ref, q_ref, mg_ref, kcv_ref, ks_ref, vs_ref, kw_ref, vw_ref, ng_ref):
    x = x_ref[...]
    y = x * lax.rsqrt(jnp.mean(x * x, axis=-1, keepdims=True) + EPS)
    h = (y * g_ref[...]).astype(BF16)
    cos = cos_ref[...]
    sin = sin_ref[...]

    def mm(lo, width):
        return _dot(h, w_ref[:, lo:lo + width])

    xg_ref[:, 0:D_RNN] = mm(COL_XR, D_RNN)
    xg_ref[:, D_RNN:2 * D_RNN] = mm(COL_GR, D_RNN)
    q_ref[...] = (_rope(mm(COL_Q, Q_DIM), cos, sin) * (HEAD_DIM ** -0.5)).astype(BF16)
    mg_ref[:, 0:D_MODEL] = jax.nn.sigmoid(mm(COL_MG, D_MODEL))
    mg_ref[:, D_MODEL:2 * D_MODEL] = jax.nn.sigmoid(mm(COL_MG + D_MODEL, D_MODEL))
    kcv_ref[...] = mm(COL_KCV, 2 * KV_DIM)
    ks_ref[...] = _dup_heads(_rope(mm(COL_KS, KV_DIM), cos, sin)).astype(BF16)
    vs_ref[...] = _dup_heads(mm(COL_VS, KV_DIM)).astype(BF16)
    kw_ref[...] = _dup_heads(_rope(mm(COL_KW, KV_DIM), cos, sin)).astype(BF16)
    vw_ref[...] = _dup_heads(mm(COL_VW, KV_DIM)).astype(BF16)
    ng_ref[...] = jax.nn.sigmoid(mm(COL_NG, N_KV_GROUPS * LANES))


def _inproj(x2, norm_g, w_proj, cos, sin, seq):
    t = x2.shape[0]
    tm = TM_PROJ
    pos_blocks = seq // tm
    row = lambda i: (i, 0)
    const = lambda i: (0, 0)
    out_shape = (
        jax.ShapeDtypeStruct((t, 2 * D_RNN), F32),
        jax.ShapeDtypeStruct((t, Q_DIM), BF16),
        jax.ShapeDtypeStruct((t, 2 * D_MODEL), F32),
        jax.ShapeDtypeStruct((t, 2 * KV_DIM), F32),
        jax.ShapeDtypeStruct((t, N_KV_GROUPS * LANES), BF16),
        jax.ShapeDtypeStruct((t, N_KV_GROUPS * LANES), BF16),
        jax.ShapeDtypeStruct((t, N_KV_GROUPS * LANES), BF16),
        jax.ShapeDtypeStruct((t, N_KV_GROUPS * LANES), BF16),
        jax.ShapeDtypeStruct((t, N_KV_GROUPS * LANES), F32),
    )
    return pl.pallas_call(
        _inproj_kernel,
        grid=(t // tm,),
        in_specs=[
            pl.BlockSpec((tm, D_MODEL), row),
            pl.BlockSpec((1, D_MODEL), const),
            pl.BlockSpec((D_MODEL, N_PROJ), const, pipeline_mode=pl.Buffered(1)),
            pl.BlockSpec((tm, LANES), lambda i: (i % pos_blocks, 0)),
            pl.BlockSpec((tm, LANES), lambda i: (i % pos_blocks, 0)),
        ],
        out_specs=tuple(pl.BlockSpec((tm, s.shape[1]), row) for s in out_shape),
        out_shape=out_shape,
        compiler_params=pltpu.CompilerParams(
            dimension_semantics=("parallel",), vmem_limit_bytes=VMEM_LIMIT),
        name="inproj",
    )(x2, norm_g, w_proj, cos, sin)


def _rnn_kernel(xr_ref, gr_ref, cw_ref, cb_ref, wa_ref, ba_ref, wi_ref, bi_ref, lam_ref, wo_ref,
                y_ref, tail_s, carry_s, a_s, u_s, h_s):
    ts = xr_ref.shape[0]

    @pl.when(pl.program_id(1) == 0)
    def _():
        tail_s[...] = jnp.zeros_like(tail_s)
        carry_s[...] = jnp.zeros_like(carry_s)

    x = xr_ref[...]
    xext = jnp.concatenate([tail_s[...], x], axis=0)
    tail_s[...] = x[ts - SUBLANES:ts, :]
    conv = cb_ref[...]
    for k in range(CONV_WIDTH):
        back = CONV_WIDTH - 1 - k
        shifted = xext if back == 0 else pltpu.roll(xext, back, 0)
        conv = conv + cw_ref[k:k + 1, :] * shifted[SUBLANES:SUBLANES + ts, :]

    cb16 = conv.astype(BF16)
    blk = wa_ref.shape[1]
    r_pre = jnp.concatenate(
        [_dot(cb16[:, j * blk:(j + 1) * blk], wa_ref[j]) for j in range(D_RNN // blk)], axis=1)
    i_pre = jnp.concatenate(
        [_dot(cb16[:, j * blk:(j + 1) * blk], wi_ref[j]) for j in range(D_RNN // blk)], axis=1)
    r = jax.nn.sigmoid(r_pre + ba_ref[...])
    gate_i = jax.nn.sigmoid(i_pre + bi_ref[...])
    neg_lam = -lam_ref[...]
    softplus = jnp.maximum(neg_lam, 0.0) + jnp.log1p(jnp.exp(-jnp.abs(neg_lam)))
    log_a = (-LRU_C) * r * softplus
    a = jnp.exp(log_a)
    a_s[...] = a
    u_s[...] = jnp.sqrt(-jnp.tanh(log_a) * (a * a + 1.0)) * (gate_i * conv)

    row = _row_iota((SUBLANES, D_RNN))

    def body(k, carry):
        off = pl.multiple_of(k * SUBLANES, SUBLANES)
        a = a_s[pl.ds(off, SUBLANES), :]
        b = u_s[pl.ds(off, SUBLANES), :]
        for sh in (1, 2, 4):
            keep = row >= sh
            a_prev = jnp.where(keep, pltpu.roll(a, sh, 0), 1.0)
            b_prev = jnp.where(keep, pltpu.roll(b, sh, 0), 0.0)
            b = a * b_prev + b
            a = a * a_prev
        h = a * carry + b
        h_s[pl.ds(off, SUBLANES), :] = h
        return jnp.broadcast_to(h[SUBLANES - 1:SUBLANES, :], (SUBLANES, D_RNN))

    carry_s[...] = lax.fori_loop(0, ts // SUBLANES, body, carry_s[...])
    gated = (jax.nn.gelu(gr_ref[...]) * h_s[...]).astype(BF16)
    y_ref[...] = _dot(gated, wo_ref[...])


def _rnn(xg, conv_w, conv_b, wa_bd, ba, wi_bd, bi, lam, w_out, batch, seq):
    t = xg.shape[0]
    ts = TS_RNN
    nt = seq // ts
    const2 = lambda b, s: (0, 0)
    const3 = lambda b, s: (0, 0, 0)
    nblk, blk = wa_bd.shape[0], wa_bd.shape[1]
    return pl.pallas_call(
        _rnn_kernel,
        grid=(batch, nt),
        in_specs=[
            pl.BlockSpec((ts, D_RNN), lambda b, s: (b * nt + s, 0)),
            pl.BlockSpec((ts, D_RNN), lambda b, s: (b * nt + s, 1)),
            pl.BlockSpec((CONV_WIDTH, D_RNN), const2),
            pl.BlockSpec((1, D_RNN), const2),
            pl.BlockSpec((nblk, blk, blk), const3),
            pl.BlockSpec((1, D_RNN), const2),
            pl.BlockSpec((nblk, blk, blk), const3),
            pl.BlockSpec((1, D_RNN), const2),
            pl.BlockSpec((1, D_RNN), const2),
            pl.BlockSpec((D_RNN, D_MODEL), const2),
        ],
        out_specs=pl.BlockSpec((ts, D_MODEL), lambda b, s: (b * nt + s, 0)),
        out_shape=jax.ShapeDtypeStruct((t, D_MODEL), F32),
        scratch_shapes=[
            pltpu.VMEM((SUBLANES, D_RNN), F32),
            pltpu.VMEM((SUBLANES, D_RNN), F32),
            pltpu.VMEM((ts, D_RNN), F32),
            pltpu.VMEM((ts, D_RNN), F32),
            pltpu.VMEM((ts, D_RNN), F32),
        ],
        compiler_params=pltpu.CompilerParams(
            dimension_semantics=("parallel", "arbitrary"), vmem_limit_bytes=VMEM_LIMIT),
        name="rnn",
    )(xg, xg, conv_w, conv_b, wa_bd, ba, wi_bd, bi, lam, w_out)


def _compress_kernel(x_ref, pos_ref, w1_ref, b1_ref, w2_ref, b2_ref, cos_ref, sin_ref, o_ref):
    x = x_ref[0]
    tr = x.shape[0]
    half = x.shape[1]
    first = _dot((x + pos_ref[0, 0:1, :]).astype(BF16), w1_ref[0, 0:half, :])
    second = _dot((x + pos_ref[0, 1:2, :]).astype(BF16), w1_ref[0, half:2 * half, :])
    hid = jax.nn.gelu(first + pltpu.roll(second, tr - 1, 0) + b1_ref[0])
    out = _dot(hid.astype(BF16), w2_ref[0]) + b2_ref[0]
    o_ref[0] = _rope(out, cos_ref[0], sin_ref[0]).astype(BF16)


def _compress(x_blk, pos, w1, b1, w2, b2, cos, sin):
    rows = x_blk.shape[1]
    tr = TR_CMP
    sel = lambda k, r: (k, 0, 0)
    return pl.pallas_call(
        _compress_kernel,
        grid=(2, rows // tr),
        in_specs=[
            pl.BlockSpec((1, tr, x_blk.shape[2]), lambda k, r: (k, r, 0)),
            pl.BlockSpec((1, 2, pos.shape[2]), sel),
            pl.BlockSpec((1,) + w1.shape[1:], sel),
            pl.BlockSpec((1, 1, CMP_HIDDEN), sel),
            pl.BlockSpec((1,) + w2.shape[1:], sel),
            pl.BlockSpec((1, 1, LANES), sel),
            pl.BlockSpec((1, tr, LANES), sel),
            pl.BlockSpec((1, tr, LANES), sel),
        ],
        out_specs=pl.BlockSpec((1, tr, LANES), lambda k, r: (k, r, 0)),
        out_shape=jax.ShapeDtypeStruct((2, rows, LANES), BF16),
        compiler_params=pltpu.CompilerParams(
            dimension_semantics=("parallel", "parallel"), vmem_limit_bytes=VMEM_LIMIT),
        name="compress",
    )(x_blk, pos, w1, b1, w2, b2, cos, sin)


def _softmax_pv(s, v):
    m = jnp.max(s, axis=1, keepdims=True)
    p = jnp.exp(s - m)
    denom = jnp.sum(p, axis=1, keepdims=True)
    return _dot(p.astype(BF16), v) / denom


def _attn_kernel(q_ref, kc_ref, vc_ref, ks_ref, vs_ref, kw_ref, vw_ref, ng_ref, e_ref, m_ref,
                 o_ref, s_s, mx_s, l_s, acc_s):
    i = pl.program_id(2)
    tq = q_ref.shape[0]
    rows = HEADS_PER_GROUP * tq

    q = q_ref[...]
    low = _lane_iota((tq, LANES)) < HEAD_DIM
    zero = jnp.zeros((tq, LANES), BF16)
    heads = []
    for hh in range(HEADS_PER_GROUP):
        pair = q[:, (hh // 2) * LANES:(hh // 2 + 1) * LANES]
        heads.append(jnp.where(low if hh % 2 == 0 else jnp.logical_not(low), pair, zero))
    q4 = jnp.concatenate(heads, axis=0)

    lane4 = _lane_iota((rows, LANES))
    qpos4 = i * tq + (_row_iota((rows, LANES)) & (tq - 1))

    sc = _dot_t(q4, kc_ref[0])
    sc = jnp.where(lane4 * CMP_STRIDE + (CMP_BLOCK - 1) <= qpos4, sc, NEG_INF)
    mc = jnp.max(sc, axis=1, keepdims=True)
    pc = jnp.exp(sc - mc)
    pc = pc / jnp.sum(pc, axis=1, keepdims=True)
    pc = jnp.where(qpos4 >= CMP_BLOCK - 1, pc, 0.0)
    o_cmp = _dot(pc.astype(BF16), vc_ref[0])

    psum = pc[0:tq] + pc[tq:2 * tq] + pc[2 * tq:3 * tq] + pc[3 * tq:4 * tq]
    p_hi = psum.astype(BF16)
    rem = psum - p_hi.astype(F32)
    p_mid = rem.astype(BF16)
    p_lo = (rem - p_mid.astype(F32)).astype(BF16)
    cs = m_ref[...]
    p_slc = _dot(p_hi, cs) + _dot(p_mid, cs) + _dot(p_lo, cs)

    blk = _lane_iota((tq, LANES))
    qpos = i * tq + _row_iota((tq, LANES))
    tblk = qpos >> 6
    n_sel = SEL_BLOCK // 2
    forced = (blk == 0) | (blk == tblk) | (blk == tblk - 1)
    score = jnp.where(blk <= tblk, p_slc + jnp.where(forced, FORCE_BONUS, 0.0), -1.0)
    score = jnp.where(blk < n_sel, score, -2.0)
    rank = jnp.zeros((tq, LANES), F32)
    for j in range(n_sel):
        cj = score[:, j:j + 1]
        beats = (cj > score) | ((cj == score) & (blk > j))
        rank = rank + jnp.where(beats, 1.0, 0.0)
    bias = jnp.where((rank < SEL_TOPN) | (blk >= n_sel), 0.0, NEG_INF).astype(BF16)

    qa = jnp.concatenate([q4, jnp.concatenate([bias] * HEADS_PER_GROUP, axis=0)], axis=1)
    mx_s[...] = jnp.full((rows, LANES), NEG_INF, F32)

    def scores(j, carry):
        off = pl.multiple_of(j * TK, TK)
        ka = jnp.concatenate([ks_ref[pl.ds(off, TK), :], e_ref[pl.ds(off, TK), :]], axis=1)
        s = _dot_t(qa, ka)
        s = jnp.where(j * TK + lane4 <= qpos4, s, NEG_INF)
        s_s[j] = s
        mx_s[...] = jnp.maximum(mx_s[...], s)
        return carry

    lax.fori_loop(0, i + 1, scores, 0)
    m_sel = jnp.max(mx_s[...], axis=1, keepdims=True)
    l_s[...] = jnp.zeros((rows, LANES), F32)
    acc_s[...] = jnp.zeros((rows, LANES), F32)

    def values(j, carry):
        off = pl.multiple_of(j * TK, TK)
        p = jnp.exp(s_s[j] - m_sel)
        l_s[...] += p
        acc_s[...] += _dot(p.astype(BF16), vs_ref[pl.ds(off, TK), :])
        return carry

    lax.fori_loop(0, i + 1, values, 0)
    o_sel = acc_s[...] / jnp.sum(l_s[...], axis=1, keepdims=True)

    k_tiles, v_tiles, masks = [], [], []
    for jj in range(N_WIN_TILES):
        j = i - (N_WIN_TILES - 1) + jj
        off = pl.multiple_of(jnp.maximum(j, 0) * TK, TK)
        k_tiles.append(kw_ref[pl.ds(off, TK), :])
        v_tiles.append(vw_ref[pl.ds(off, TK), :])
        kpos = j * TK + lane4
        d = qpos4 - kpos
        masks.append((kpos >= 0) & (d >= 0) & (d < WINDOW))
    sw = _dot_t(q4, jnp.concatenate(k_tiles, axis=0))
    sw = jnp.where(jnp.concatenate(masks, axis=1), sw, NEG_INF)
    o_win = _softmax_pv(sw, jnp.concatenate(v_tiles, axis=0))

    ng = ng_ref[...]
    outs = []
    for hh in range(HEADS_PER_GROUP):
        r0 = hh * tq
        col = lambda br: ng[:, br * HEADS_PER_GROUP + hh:br * HEADS_PER_GROUP + hh + 1]
        outs.append(col(0) * o_cmp[r0:r0 + tq] + col(1) * o_sel[r0:r0 + tq] + col(2) * o_win[r0:r0 + tq])
    o_ref[...] = jnp.concatenate(
        [jnp.where(low, outs[0], outs[1]), jnp.where(low, outs[2], outs[3])], axis=1).astype(o_ref.dtype)


def _attn(q, kvc, ksd, vsd, kwd, vwd, ng, e_mat, cs_mat, batch, seq):
    t = q.shape[0]
    tq = TQ
    nq = seq // tq
    qrow = lambda b, g, i: (b * nq + i, g)
    kv = lambda b, g, i: (b, g)
    const = lambda b, g, i: (0, 0)
    rows = HEADS_PER_GROUP * tq
    return pl.pallas_call(
        _attn_kernel,
        grid=(batch, N_KV_GROUPS, nq),
        in_specs=[
            pl.BlockSpec((tq, HEADS_PER_GROUP * HEAD_DIM), qrow),
            pl.BlockSpec((1, N_CMP_PAD, LANES), lambda b, g, i: (0, b * N_KV_GROUPS + g, 0)),
            pl.BlockSpec((1, N_CMP_PAD, LANES), lambda b, g, i: (1, b * N_KV_GROUPS + g, 0)),
            pl.BlockSpec((seq, LANES), kv),
            pl.BlockSpec((seq, LANES), kv),
            pl.BlockSpec((seq, LANES), kv),
            pl.BlockSpec((seq, LANES), kv),
            pl.BlockSpec((tq, LANES), qrow),
            pl.BlockSpec((seq, LANES), const),
            pl.BlockSpec((N_CMP_PAD, LANES), const),
        ],
        out_specs=pl.BlockSpec((tq, HEADS_PER_GROUP * HEAD_DIM), qrow),
        out_shape=jax.ShapeDtypeStruct((t, Q_DIM), BF16),
        scratch_shapes=[
            pltpu.VMEM((seq // TK, rows, TK), F32),
            pltpu.VMEM((rows, LANES), F32),
            pltpu.VMEM((rows, LANES), F32),
            pltpu.VMEM((rows, LANES), F32),
        ],
        compiler_params=pltpu.CompilerParams(
            dimension_semantics=("parallel", "parallel", "arbitrary"), vmem_limit_bytes=VMEM_LIMIT),
        name="attn",
    )(q, kvc, kvc, ksd, vsd, kwd, vwd, ng, e_mat, cs_mat)


def _post_kernel(o_ref, ya_ref, mg_ref, x_ref, wn_ref, wm_ref, g2_ref, wr_hi_ref, wr_lo_ref, rb_ref,
                 x1_ref, h2_ref, comb_ref):
    y_b = _dot(o_ref[...], wn_ref[...])
    mixed = mg_ref[:, 0:D_MODEL] * ya_ref[...] + mg_ref[:, D_MODEL:2 * D_MODEL] * y_b
    x1 = x_ref[...] + _dot(mixed.astype(BF16), wm_ref[...])
    x1_ref[...] = x1
    h2 = x1 * lax.rsqrt(jnp.mean(x1 * x1, axis=-1, keepdims=True) + EPS) * g2_ref[...]
    h_hi = h2.astype(BF16)
    h2_ref[...] = h_hi
    h_lo = (h2 - h_hi.astype(F32)).astype(BF16)
    logits = (_dot(h_hi, wr_hi_ref[...]) + _dot(h_lo, wr_hi_ref[...]) + _dot(h_hi, wr_lo_ref[...])
              + rb_ref[...])

    lane = _lane_iota(logits.shape).astype(F32)
    is_grp = (lane >= N_EXPERTS) & (lane < N_EXPERTS + N_GROUPS)
    gl = jnp.where(is_grp, logits, NEG_INF)
    ge = jnp.exp(gl - jnp.max(gl, axis=1, keepdims=True))
    gp = ge / jnp.sum(ge, axis=1, keepdims=True)
    g_w = jnp.max(gp, axis=1, keepdims=True)
    big = float(4 * LANES)
    g_first = jnp.min(jnp.where(is_grp & (gp == g_w), lane, big), axis=1, keepdims=True)
    grp_lo = (g_first - N_EXPERTS) * EXPERTS_PER_GROUP

    in_grp = (lane >= grp_lo) & (lane < grp_lo + EXPERTS_PER_GROUP)
    el = jnp.where(in_grp, logits, NEG_INF)
    ee = jnp.exp(el - jnp.max(el, axis=1, keepdims=True))
    ep = ee / jnp.sum(ee, axis=1, keepdims=True)
    w1 = jnp.max(ep, axis=1, keepdims=True)
    i1 = jnp.min(jnp.where(in_grp & (ep == w1), lane, big), axis=1, keepdims=True)
    rest = jnp.where(in_grp & (lane != i1), ep, -1.0)
    w2 = jnp.max(rest, axis=1, keepdims=True)
    i2 = jnp.min(jnp.where(rest == w2, lane, big), axis=1, keepdims=True)
    den = w1 + w2
    comb_ref[...] = jnp.where(lane == i1, g_w * (w1 / den), jnp.where(lane == i2, g_w * (w2 / den), 0.0))


def _post(o_nsa, y_a, mg, x2, w_nsa, w_mix, g2, wr_hi, wr_lo, rb):
    t = x2.shape[0]
    tm = TM_POST
    row = lambda i: (i, 0)
    const = lambda i: (0, 0)
    return pl.pallas_call(
        _post_kernel,
        grid=(t // tm,),
        in_specs=[
            pl.BlockSpec((tm, Q_DIM), row),
            pl.BlockSpec((tm, D_MODEL), row),
            pl.BlockSpec((tm, 2 * D_MODEL), row),
            pl.BlockSpec((tm, D_MODEL), row),
            pl.BlockSpec((Q_DIM, D_MODEL), const),
            pl.BlockSpec((D_MODEL, D_MODEL), const),
            pl.BlockSpec((1, D_MODEL), const),
            pl.BlockSpec((D_MODEL, LANES), const),
            pl.BlockSpec((D_MODEL, LANES), const),
            pl.BlockSpec((1, LANES), const),
        ],
        out_specs=(
            pl.BlockSpec((tm, D_MODEL), row),
            pl.BlockSpec((tm, D_MODEL), row),
            pl.BlockSpec((tm, LANES), row),
        ),
        out_shape=(
            jax.ShapeDtypeStruct((t, D_MODEL), F32),
            jax.ShapeDtypeStruct((t, D_MODEL), BF16),
            jax.ShapeDtypeStruct((t, LANES), F32),
        ),
        compiler_params=pltpu.CompilerParams(
            dimension_semantics=("parallel",), vmem_limit_bytes=VMEM_LIMIT),
        name="post",
    )(o_nsa, y_a, mg, x2, w_nsa, w_mix, g2, wr_hi, wr_lo, rb)


def _moe_kernel(h_ref, comb_ref, x1_ref, wg_ref, wu_ref, wd_ref, gf_ref, o_ref, acc_s):
    e = pl.program_id(1)

    @pl.when(e == 0)
    def _():
        acc_s[...] = jnp.zeros_like(acc_s)

    h = h_ref[...]
    act = jax.nn.silu(_dot(h, wg_ref[0])) * _dot(h, wu_ref[0])
    y = _dot(act.astype(BF16), wd_ref[0])
    comb = comb_ref[...]
    weight = jnp.sum(jnp.where(_lane_iota(comb.shape) == e, comb, 0.0), axis=1, keepdims=True)
    acc_s[...] += weight * y

    @pl.when(e == N_EXPERTS - 1)
    def _():
        x = x1_ref[...] + acc_s[...]
        o_ref[...] = x * lax.rsqrt(jnp.mean(x * x, axis=-1, keepdims=True) + EPS) * gf_ref[...]


def _moe(h2, comb, x1, wg, wu, wd, gf):
    t = h2.shape[0]
    tm = TM_MOE
    row = lambda i, e: (i, 0)
    return pl.pallas_call(
        _moe_kernel,
        grid=(t // tm, N_EXPERTS),
        in_specs=[
            pl.BlockSpec((tm, D_MODEL), row),
            pl.BlockSpec((tm, LANES), row),
            pl.BlockSpec((tm, D_MODEL), row),
            pl.BlockSpec((1, D_MODEL, D_EXPERT), lambda i, e: (e, 0, 0)),
            pl.BlockSpec((1, D_MODEL, D_EXPERT), lambda i, e: (e, 0, 0)),
            pl.BlockSpec((1, D_EXPERT, D_MODEL), lambda i, e: (e, 0, 0)),
            pl.BlockSpec((1, D_MODEL), lambda i, e: (0, 0)),
        ],
        out_specs=pl.BlockSpec((tm, D_MODEL), row),
        out_shape=jax.ShapeDtypeStruct((t, D_MODEL), F32),
        scratch_shapes=[pltpu.VMEM((tm, D_MODEL), F32)],
        compiler_params=pltpu.CompilerParams(
            dimension_semantics=("parallel", "arbitrary"), vmem_limit_bytes=VMEM_LIMIT),
        name="moe",
    )(h2, comb, x1, wg, wu, wd, gf)


def _split_w_in(w):
    sizes = [D_RNN, D_RNN, Q_DIM] + [KV_DIM] * 6 + [3 * N_HEADS, 2 * D_MODEL]
    pts = np.cumsum(sizes)[:-1]
    return jnp.split(w, [int(p) for p in pts], axis=-1)


def _proj_weight(w_in):
    xr, gr, q, kc, vc, ksl, vsl, kw, vw, nsa_g, merge_g = _split_w_in(w_in)
    ng = nsa_g.reshape(D_MODEL, 3, N_KV_GROUPS, HEADS_PER_GROUP).transpose(0, 2, 1, 3)
    ng = ng.reshape(D_MODEL, N_KV_GROUPS, 3 * HEADS_PER_GROUP)
    ng = jnp.pad(ng, ((0, 0), (0, 0), (0, LANES - 3 * HEADS_PER_GROUP))).reshape(D_MODEL, N_KV_GROUPS * LANES)
    return jnp.concatenate([xr, gr, q, merge_g, kc, vc, ksl, vsl, kw, vw, ng], axis=1).astype(BF16)


def _rope_tables(pos, width):
    inv_freq = ROPE_THETA ** (-(jnp.arange(0, HEAD_DIM, 2, dtype=F32) / HEAD_DIM))
    ang = pos.astype(F32)[:, None] * inv_freq[None, :]
    cos, sin = jnp.cos(ang), jnp.sin(ang)
    reps = width // HEAD_DIM
    return jnp.tile(jnp.concatenate([cos, cos], axis=1), (1, reps)), jnp.tile(jnp.concatenate([-sin, sin], axis=1), (1, reps))


def _block_diag(w, per):
    nb, d = w.shape[0], w.shape[1]
    w = w.reshape(nb // per, per, d, d)
    eye = jnp.eye(per, dtype=w.dtype)
    return jnp.einsum('npij,pq->npiqj', w, eye).reshape(nb // per, per * d, per * d)


def _cmp_to_sel():
    n_sel_pad = LANES
    c0 = np.arange(N_CMP_PAD) * CMP_STRIDE
    s0 = np.arange(n_sel_pad) * SEL_BLOCK
    ov = np.minimum(c0[:, None] + CMP_BLOCK, s0[None, :] + SEL_BLOCK) - np.maximum(c0[:, None], s0[None, :])
    m = np.clip(ov, 0, None) / CMP_BLOCK
    m[:, SEL_BLOCK // 2:] = 0.0
    m[N_CMP_PAD - 1, :] = 0.0
    return m.astype(np.float32)


def kernel(x, norm1_g, w_in, conv_w, conv_b, lru_wa, lru_ba, lru_wi, lru_bi, lru_lambda, w_rnn_out, cmpk_pos, cmpk_w1, cmpk_b1, cmpk_w2, cmpk_b2, cmpv_pos, cmpv_w1, cmpv_b1, cmpv_w2, cmpv_b2, w_nsa_out, w_mix_out, norm2_g, router_group_w, router_group_b, router_expert_w, router_expert_b, expert_w_gate, expert_w_up, expert_w_down, final_norm_g):
    batch, seq, _ = x.shape
    t = batch * seq
    assert w_in.shape[0] == 1, "the final norm is fused into the expert kernel: single layer only"
    x2 = x.reshape(t, D_MODEL)

    cos, sin = _rope_tables(jnp.arange(seq), LANES)
    cmp_ends = jnp.arange(N_CMP_PAD) * CMP_STRIDE + (CMP_BLOCK - 1)
    ccos, csin = _rope_tables(cmp_ends, LANES)
    reps = TR_CMP // N_CMP_PAD
    cmp_cos = jnp.stack([jnp.tile(ccos, (reps, 1)), jnp.ones((TR_CMP, LANES), F32)])
    cmp_sin = jnp.stack([jnp.tile(csin, (reps, 1)), jnp.zeros((TR_CMP, LANES), F32)])
    key_blk = np.arange(seq)[:, None] // SEL_BLOCK
    e_mat = jnp.asarray((key_blk == np.arange(LANES)[None, :]).astype(np.float32), dtype=BF16)
    cs_mat = jnp.asarray(_cmp_to_sel(), dtype=BF16)

    l = 0
    xg, q, mg, kcv, ksd, vsd, kwd, vwd, ng = _inproj(
        x2, norm1_g[l][None, :], _proj_weight(w_in[l]), cos, sin, seq)

    y_a = _rnn(xg, conv_w[l], conv_b[l][None, :],
               _block_diag(lru_wa[l], 4).astype(BF16), lru_ba[l][None, :],
               _block_diag(lru_wi[l], 4).astype(BF16), lru_bi[l][None, :],
               lru_lambda[l][None, :], w_rnn_out[l].astype(BF16), batch, seq)

    n_rows = seq // CMP_STRIDE
    x_blk = kcv.reshape(batch, n_rows, CMP_STRIDE, 2, N_KV_GROUPS, HEAD_DIM)
    x_blk = x_blk.transpose(3, 0, 4, 1, 2, 5).reshape(2, batch * N_KV_GROUPS * n_rows, CMP_STRIDE * HEAD_DIM)
    half = CMP_STRIDE * HEAD_DIM
    pos = jnp.stack([cmpk_pos[l].reshape(2, half), cmpv_pos[l].reshape(2, half)])
    w1 = jnp.stack([cmpk_w1[l], cmpv_w1[l]]).astype(BF16)
    b1 = jnp.stack([cmpk_b1[l], cmpv_b1[l]])[:, None, :]
    w2 = jnp.stack([cmpk_w2[l], cmpv_w2[l]])
    w2 = jnp.concatenate([w2, w2], axis=2).astype(BF16)
    b2 = jnp.stack([cmpk_b2[l], cmpv_b2[l]])
    b2 = jnp.concatenate([b2, b2], axis=1)[:, None, :]
    kvc = _compress(x_blk, pos, w1, b1, w2, b2, cmp_cos, cmp_sin)

    o_nsa = _attn(q, kvc, ksd, vsd, kwd, vwd, ng, e_mat, cs_mat, batch, seq)

    wr = jnp.concatenate([
        router_expert_w[l].transpose(1, 0, 2).reshape(D_MODEL, N_EXPERTS),
        router_group_w[l],
        jnp.zeros((D_MODEL, LANES - N_EXPERTS - N_GROUPS), F32)], axis=1)
    wr_hi = wr.astype(BF16)
    wr_lo = (wr - wr_hi.astype(F32)).astype(BF16)
    rb = jnp.concatenate([router_expert_b[l].reshape(N_EXPERTS), router_group_b[l],
                          jnp.zeros((LANES - N_EXPERTS - N_GROUPS,), F32)])[None, :]
    x1, h2, comb = _post(o_nsa, y_a, mg, x2, w_nsa_out[l].astype(BF16), w_mix_out[l].astype(BF16),
                         norm2_g[l][None, :], wr_hi, wr_lo, rb)

    out = _moe(h2, comb, x1, expert_w_gate[l].astype(BF16), expert_w_up[l].astype(BF16),
               expert_w_down[l].astype(BF16), final_norm_g[None, :])
    return out.reshape(batch, seq, D_MODEL)
```

```python
import functools

import numpy as np
import jax
import jax.numpy as jnp
from jax import lax
from jax.experimental import pallas as pl
from jax.experimental.pallas import tpu as pltpu

D_MODEL = 1024
D_RNN = 1024
RNN_BLOCKS = 16
RNN_BLOCK_DIM = D_RNN // RNN_BLOCKS
CONV_WIDTH = 4
LRU_C = 8.0
N_HEADS = 16
HEAD_DIM = 64
HALF_DIM = HEAD_DIM // 2
N_KV_GROUPS = 4
HEADS_PER_GROUP = N_HEADS // N_KV_GROUPS
Q_DIM = N_HEADS * HEAD_DIM
KV_DIM = N_KV_GROUPS * HEAD_DIM
CMP_BLOCK = 32
CMP_STRIDE = 16
CMP_HIDDEN = 256
SEL_BLOCK = 64
SEL_TOPN = 16
WINDOW = 512
ROPE_THETA = 10000.0
FORCE_BONUS = 1e4
NEG_INF = -1e30
N_GROUPS = 4
EXPERTS_PER_GROUP = 4
N_EXPERTS = N_GROUPS * EXPERTS_PER_GROUP
D_EXPERT = 512
EPS = 1e-6

LANES = 128
SUBLANES = 8
VMEM_LIMIT = 56 * 1024 * 1024

BF16 = jnp.bfloat16
F32 = jnp.float32

COL_XR = 0
COL_GR = COL_XR + D_RNN
COL_Q = COL_GR + D_RNN
COL_MG = COL_Q + Q_DIM
COL_KCV = COL_MG + 2 * D_MODEL
COL_KS = COL_KCV + 2 * KV_DIM
COL_VS = COL_KS + KV_DIM
COL_KW = COL_VS + KV_DIM
COL_VW = COL_KW + KV_DIM
COL_NG = COL_VW + KV_DIM
N_PROJ = COL_NG + N_KV_GROUPS * LANES

TM_PROJ = 256
TS_RNN = 512
TR_CMP = 512
TQ = 128
TK = 128
CK_SEL = 512
TM_POST = 256
TM_MOE = 1024
N_CMP_PAD = 128
N_WIN_TILES = WINDOW // TK + 1


def _dot(a, b):
    return jnp.dot(a, b, preferred_element_type=F32)


def _dot_t(a, b):
    return lax.dot_general(a, b, (((1,), (1,)), ((), ())), preferred_element_type=F32)


def _lane_iota(shape):
    return lax.broadcasted_iota(jnp.int32, shape, len(shape) - 1)


def _row_iota(shape):
    return lax.broadcasted_iota(jnp.int32, shape, 0)


def _rope(x, cos, sin_signed):
    width = x.shape[-1]
    reps = width // cos.shape[-1]
    if reps > 1:
        cos = jnp.concatenate([cos] * reps, axis=1)
        sin_signed = jnp.concatenate([sin_signed] * reps, axis=1)
    first_half = (_lane_iota(x.shape) & (HEAD_DIM - 1)) < HALF_DIM
    partner = jnp.where(first_half, pltpu.roll(x, width - HALF_DIM, 1), pltpu.roll(x, HALF_DIM, 1))
    return x * cos + partner * sin_signed


def _spread_heads(x, fill=None):
    out = []
    low = _lane_iota((x.shape[0], LANES)) < HEAD_DIM
    for c in range(x.shape[1] // LANES):
        xc = x[:, c * LANES:(c + 1) * LANES]
        rolled = pltpu.roll(xc, HEAD_DIM, 1)
        out.append(jnp.where(low, xc, rolled if fill is None else fill))
        out.append(jnp.where(low, rolled, xc if fill is None else fill))
    return jnp.concatenate(out, axis=1)


def _inproj_kernel(x_ref, g_ref, w_ref, cos_ref, sin_ref,
                   xg_ref, q_ref, mg_ref, kcv_ref, ks_ref, vs_ref, kw_ref, vw_ref, ng_ref):
    x = x_ref[...]
    y = x * lax.rsqrt(jnp.mean(x * x, axis=-1, keepdims=True) + EPS)
    h = (y * g_ref[...]).astype(BF16)
    cos = cos_ref[...]
    sin = sin_ref[...]

    def mm(lo, width):
        return _dot(h, w_ref[:, lo:lo + width])

    xg_ref[:, 0:D_RNN] = mm(COL_XR, D_RNN)
    xg_ref[:, D_RNN:2 * D_RNN] = mm(COL_GR, D_RNN)
    q_ref[...] = (_rope(mm(COL_Q, Q_DIM), cos, sin) * (HEAD_DIM ** -0.5)).astype(BF16)
    mg_ref[:, 0:D_MODEL] = jax.nn.sigmoid(mm(COL_MG, D_MODEL))
    mg_ref[:, D_MODEL:2 * D_MODEL] = jax.nn.sigmoid(mm(COL_MG + D_MODEL, D_MODEL))
    kcv_ref[...] = mm(COL_KCV, 2 * KV_DIM)
    ks_ref[...] = _spread_heads(_rope(mm(COL_KS, KV_DIM), cos, sin)).astype(BF16)
    vs_ref[...] = _spread_heads(mm(COL_VS, KV_DIM), 1.0).astype(BF16)
    kw_ref[...] = _spread_heads(_rope(mm(COL_KW, KV_DIM), cos, sin)).astype(BF16)
    vw_ref[...] = _spread_heads(mm(COL_VW, KV_DIM), 1.0).astype(BF16)
    ng_ref[...] = jax.nn.sigmoid(mm(COL_NG, N_KV_GROUPS * LANES))


def _inproj(x2, norm_g, w_proj, cos, sin, seq):
    t = x2.shape[0]
    tm = TM_PROJ
    pos_blocks = seq // tm
    row = lambda i: (i, 0)
    const = lambda i: (0, 0)
    out_shape = (
        jax.ShapeDtypeStruct((t, 2 * D_RNN), F32),
        jax.ShapeDtypeStruct((t, Q_DIM), BF16),
        jax.ShapeDtypeStruct((t, 2 * D_MODEL), F32),
        jax.ShapeDtypeStruct((t, 2 * KV_DIM), F32),
        jax.ShapeDtypeStruct((t, N_KV_GROUPS * LANES), BF16),
        jax.ShapeDtypeStruct((t, N_KV_GROUPS * LANES), BF16),
        jax.ShapeDtypeStruct((t, N_KV_GROUPS * LANES), BF16),
        jax.ShapeDtypeStruct((t, N_KV_GROUPS * LANES), BF16),
        jax.ShapeDtypeStruct((t, N_KV_GROUPS * LANES), F32),
    )
    return pl.pallas_call(
        _inproj_kernel,
        grid=(t // tm,),
        in_specs=[
            pl.BlockSpec((tm, D_MODEL), row),
            pl.BlockSpec((1, D_MODEL), const),
            pl.BlockSpec((D_MODEL, N_PROJ), const, pipeline_mode=pl.Buffered(1)),
            pl.BlockSpec((tm, LANES), lambda i: (i % pos_blocks, 0)),
            pl.BlockSpec((tm, LANES), lambda i: (i % pos_blocks, 0)),
        ],
        out_specs=tuple(pl.BlockSpec((tm, s.shape[1]), row) for s in out_shape),
        out_shape=out_shape,
        compiler_params=pltpu.CompilerParams(
            dimension_semantics=("parallel",), vmem_limit_bytes=VMEM_LIMIT),
        name="inproj",
    )(x2, norm_g, w_proj, cos, sin)


def _rnn_kernel(xr_ref, gr_ref, cw_ref, cb_ref, wa_ref, ba_ref, wi_ref, bi_ref, lam_ref, wo_ref,
                y_ref, tail_s, carry_s, a_s, u_s, h_s):
    ts = xr_ref.shape[0]

    @pl.when(pl.program_id(1) == 0)
    def _():
        tail_s[...] = jnp.zeros_like(tail_s)
        carry_s[...] = jnp.zeros_like(carry_s)

    x = xr_ref[...]
    xext = jnp.concatenate([tail_s[...], x], axis=0)
    tail_s[...] = x[ts - SUBLANES:ts, :]
    conv = cb_ref[...]
    for k in range(CONV_WIDTH):
        back = CONV_WIDTH - 1 - k
        shifted = xext if back == 0 else pltpu.roll(xext, back, 0)
        conv = conv + cw_ref[k:k + 1, :] * shifted[SUBLANES:SUBLANES + ts, :]

    cb16 = conv.astype(BF16)
    blk = wa_ref.shape[1]
    r_pre = jnp.concatenate(
        [_dot(cb16[:, j * blk:(j + 1) * blk], wa_ref[j]) for j in range(D_RNN // blk)], axis=1)
    i_pre = jnp.concatenate(
        [_dot(cb16[:, j * blk:(j + 1) * blk], wi_ref[j]) for j in range(D_RNN // blk)], axis=1)
    r = jax.nn.sigmoid(r_pre + ba_ref[...])
    gate_i = jax.nn.sigmoid(i_pre + bi_ref[...])
    neg_lam = -lam_ref[...]
    softplus = jnp.maximum(neg_lam, 0.0) + jnp.log1p(jnp.exp(-jnp.abs(neg_lam)))
    log_a = (-LRU_C) * r * softplus
    a = jnp.exp(log_a)
    a_s[...] = a
    u_s[...] = jnp.sqrt(-jnp.tanh(log_a) * (a * a + 1.0)) * (gate_i * conv)

    row = _row_iota((SUBLANES, D_RNN))

    def body(k, carry):
        off = pl.multiple_of(k * SUBLANES, SUBLANES)
        a = a_s[pl.ds(off, SUBLANES), :]
        b = u_s[pl.ds(off, SUBLANES), :]
        for sh in (1, 2, 4):
            keep = row >= sh
            a_prev = jnp.where(keep, pltpu.roll(a, sh, 0), 1.0)
            b_prev = jnp.where(keep, pltpu.roll(b, sh, 0), 0.0)
            b = a * b_prev + b
            a = a * a_prev
        h = a * carry + b
        h_s[pl.ds(off, SUBLANES), :] = h
        return jnp.broadcast_to(h[SUBLANES - 1:SUBLANES, :], (SUBLANES, D_RNN))

    carry_s[...] = lax.fori_loop(0, ts // SUBLANES, body, carry_s[...])
    gated = (jax.nn.gelu(gr_ref[...]) * h_s[...]).astype(BF16)
    y_ref[...] = _dot(gated, wo_ref[...])


def _rnn(xg, conv_w, conv_b, wa_bd, ba, wi_bd, bi, lam, w_out, batch, seq):
    t = xg.shape[0]
    ts = TS_RNN
    nt = seq // ts
    const2 = lambda b, s: (0, 0)
    const3 = lambda b, s: (0, 0, 0)
    nblk, blk = wa_bd.shape[0], wa_bd.shape[1]
    return pl.pallas_call(
        _rnn_kernel,
        grid=(batch, nt),
        in_specs=[
            pl.BlockSpec((ts, D_RNN), lambda b, s: (b * nt + s, 0)),
            pl.BlockSpec((ts, D_RNN), lambda b, s: (b * nt + s, 1)),
            pl.BlockSpec((CONV_WIDTH, D_RNN), const2),
            pl.BlockSpec((1, D_RNN), const2),
            pl.BlockSpec((nblk, blk, blk), const3),
            pl.BlockSpec((1, D_RNN), const2),
            pl.BlockSpec((nblk, blk, blk), const3),
            pl.BlockSpec((1, D_RNN), const2),
            pl.BlockSpec((1, D_RNN), const2),
            pl.BlockSpec((D_RNN, D_MODEL), const2),
        ],
        out_specs=pl.BlockSpec((ts, D_MODEL), lambda b, s: (b * nt + s, 0)),
        out_shape=jax.ShapeDtypeStruct((t, D_MODEL), F32),
        scratch_shapes=[
            pltpu.VMEM((SUBLANES, D_RNN), F32),
            pltpu.VMEM((SUBLANES, D_RNN), F32),
            pltpu.VMEM((ts, D_RNN), F32),
            pltpu.VMEM((ts, D_RNN), F32),
            pltpu.VMEM((ts, D_RNN), F32),
        ],
        compiler_params=pltpu.CompilerParams(
            dimension_semantics=("parallel", "arbitrary"), vmem_limit_bytes=VMEM_LIMIT),
        name="rnn",
    )(xg, xg, conv_w, conv_b, wa_bd, ba, wi_bd, bi, lam, w_out)


def _compress_kernel(x_ref, pos_ref, w1_ref, b1_ref, w2_ref, b2_ref, cos_ref, sin_ref, o_ref):
    x = x_ref[0]
    tr = x.shape[0]
    half = x.shape[1]
    first = _dot((x + pos_ref[0, 0:1, :]).astype(BF16), w1_ref[0, 0:half, :])
    second = _dot((x + pos_ref[0, 1:2, :]).astype(BF16), w1_ref[0, half:2 * half, :])
    hid = jax.nn.gelu(first + pltpu.roll(second, tr - 1, 0) + b1_ref[0])
    out = _dot(hid.astype(BF16), w2_ref[0]) + b2_ref[0]
    o_ref[0] = _rope(out, cos_ref[0], sin_ref[0]).astype(BF16)


def _compress(x_blk, pos, w1, b1, w2, b2, cos, sin):
    rows = x_blk.shape[1]
    tr = TR_CMP
    sel = lambda k, r: (k, 0, 0)
    return pl.pallas_call(
        _compress_kernel,
        grid=(2, rows // tr),
        in_specs=[
            pl.BlockSpec((1, tr, x_blk.shape[2]), lambda k, r: (k, r, 0)),
            pl.BlockSpec((1, 2, pos.shape[2]), sel),
            pl.BlockSpec((1,) + w1.shape[1:], sel),
            pl.BlockSpec((1, 1, CMP_HIDDEN), sel),
            pl.BlockSpec((1,) + w2.shape[1:], sel),
            pl.BlockSpec((1, 1, LANES), sel),
            pl.BlockSpec((1, tr, LANES), sel),
            pl.BlockSpec((1, tr, LANES), sel),
        ],
        out_specs=pl.BlockSpec((1, tr, LANES), lambda k, r: (k, r, 0)),
        out_shape=jax.ShapeDtypeStruct((2, rows, LANES), BF16),
        compiler_params=pltpu.CompilerParams(
            dimension_semantics=("parallel", "parallel"), vmem_limit_bytes=VMEM_LIMIT),
        name="compress",
    )(x_blk, pos, w1, b1, w2, b2, cos, sin)


def _lane_tile_max(s):
    tiles = [s[:, c * LANES:(c + 1) * LANES] for c in range(s.shape[1] // LANES)]
    while len(tiles) > 1:
        tiles = [jnp.maximum(tiles[k], tiles[k + 1]) if k + 1 < len(tiles) else tiles[k]
                 for k in range(0, len(tiles), 2)]
    return tiles[0]


def _attn_kernel(q_ref, kc_ref, vc_ref, ks_ref, vs_ref, kw_ref, vw_ref, ng_ref, e_ref, m_ref,
                 o_ref, s_s, qa_s, mx_s, acc_s):
    i = pl.program_id(2)
    tq = q_ref.shape[0]
    rows = HEADS_PER_GROUP * tq
    ck = s_s.shape[2]

    q = q_ref[...]
    low = _lane_iota((tq, LANES)) < HEAD_DIM
    zero = jnp.zeros((tq, LANES), BF16)
    heads = []
    for hh in range(HEADS_PER_GROUP):
        pair = q[:, (hh // 2) * LANES:(hh // 2 + 1) * LANES]
        heads.append(jnp.where(low if hh % 2 == 0 else jnp.logical_not(low), pair, zero))
    q4 = jnp.concatenate(heads, axis=0)

    lane4 = _lane_iota((rows, LANES))
    qpos4 = i * tq + (_row_iota((rows, LANES)) & (tq - 1))

    sc = _dot_t(q4, kc_ref[0])
    sc = jnp.where(lane4 * CMP_STRIDE + (CMP_BLOCK - 1) <= qpos4, sc, NEG_INF)
    mc = jnp.max(sc, axis=1, keepdims=True)
    pc = jnp.exp(sc - mc)
    pc = pc / jnp.sum(pc, axis=1, keepdims=True)
    pc = jnp.where(qpos4 >= CMP_BLOCK - 1, pc, 0.0)
    o_cmp = _dot(pc.astype(BF16), vc_ref[0])

    psum = pc[0:tq] + pc[tq:2 * tq] + pc[2 * tq:3 * tq] + pc[3 * tq:4 * tq]
    p_hi = psum.astype(BF16)
    rem = psum - p_hi.astype(F32)
    p_mid = rem.astype(BF16)
    p_lo = (rem - p_mid.astype(F32)).astype(BF16)
    cs = m_ref[...]
    p_slc = _dot(p_hi, cs) + _dot(p_mid, cs) + _dot(p_lo, cs)

    n_sel = SEL_BLOCK // 2
    p_slc_t = p_slc.T[0:n_sel, :]
    blk = _row_iota((n_sel, tq))
    tblk = (i * tq + _lane_iota((n_sel, tq))) >> 6
    forced = (blk == 0) | (blk == tblk) | (blk == tblk - 1)
    score = jnp.where(blk <= tblk, p_slc_t + jnp.where(forced, FORCE_BONUS, 0.0), -1.0)
    rank = jnp.zeros((n_sel, tq), F32)
    for j in range(n_sel):
        sj = score[j:j + 1, :]
        beats = (sj > score) | ((sj == score) & (blk > j))
        rank = rank + jnp.where(beats, 1.0, 0.0)
    bias_t = jnp.where(rank < SEL_TOPN, 0.0, NEG_INF)
    bias = jnp.concatenate([bias_t, jnp.zeros((LANES - n_sel, tq), F32)], axis=0).T.astype(BF16)

    k_tiles, v_tiles = [], []
    for jj in range(N_WIN_TILES):
        j = i - (N_WIN_TILES - 1) + jj
        off = pl.multiple_of(jnp.maximum(j, 0) * TK, TK)
        k_tiles.append(kw_ref[pl.ds(off, TK), :])
        v_tiles.append(vw_ref[pl.ds(off, TK), :])
    sw = _dot_t(q4, jnp.concatenate(k_tiles, axis=0))
    rel = (_row_iota((rows, TK)) & (tq - 1)) - _lane_iota((rows, TK))
    sw_tiles = []
    for jj in range(N_WIN_TILES):
        tile = sw[:, jj * TK:(jj + 1) * TK]
        if jj == 0:
            tile = jnp.where(rel < 0, tile, NEG_INF)
        if jj == N_WIN_TILES - 1:
            tile = jnp.where(rel >= 0, tile, NEG_INF)
        else:
            tile = jnp.where(i - (N_WIN_TILES - 1) + jj >= 0, tile, NEG_INF)
        sw_tiles.append(tile)
    sw = jnp.concatenate(sw_tiles, axis=1)
    m_win = jnp.max(_lane_tile_max(sw), axis=1, keepdims=True)
    acc_win = _dot(jnp.exp(sw - m_win).astype(BF16), jnp.concatenate(v_tiles, axis=0))

    qa_s[:, 0:LANES] = q4
    qa_s[:, LANES:2 * LANES] = jnp.concatenate([bias] * HEADS_PER_GROUP, axis=0)
    mx_s[...] = jnp.full((rows, LANES), NEG_INF, F32)
    n_full = i // (ck // tq)

    def score_chunk(c, masked):
        off = pl.multiple_of(c * ck, ck)
        ka = jnp.concatenate([ks_ref[pl.ds(off, ck), :], e_ref[pl.ds(off, ck), :]], axis=1)
        s = _dot_t(qa_s[...], ka)
        if masked:
            kpos = c * ck + _lane_iota((rows, ck))
            s = jnp.where(kpos <= i * tq + (_row_iota((rows, ck)) & (tq - 1)), s, NEG_INF)
        s_s[c] = s
        mx_s[...] = jnp.maximum(mx_s[...], _lane_tile_max(s))

    def scores(c, carry):
        score_chunk(c, False)
        return carry

    lax.fori_loop(0, n_full, scores, 0)
    score_chunk(n_full, True)
    m_sel = jnp.max(mx_s[...], axis=1, keepdims=True)
    mx_s[...] = jnp.broadcast_to(m_sel, (rows, LANES))
    acc_s[...] = jnp.zeros((rows, LANES), F32)

    def values(c, carry):
        off = pl.multiple_of(c * ck, ck)
        m_b = mx_s[...]
        p = jnp.exp(s_s[c] - jnp.concatenate([m_b] * (ck // LANES), axis=1))
        acc_s[...] += _dot(p.astype(BF16), vs_ref[pl.ds(off, ck), :])
        return carry

    lax.fori_loop(0, n_full + 1, values, 0)
    acc_sel = acc_s[...]

    ng = ng_ref[...]
    outs = []
    for hh in range(HEADS_PER_GROUP):
        r0 = hh * tq
        col = lambda br: ng[:, br * HEADS_PER_GROUP + hh:br * HEADS_PER_GROUP + hh + 1]
        parts = []
        for acc in (acc_sel[r0:r0 + tq], acc_win[r0:r0 + tq]):
            swapped = pltpu.roll(acc, HEAD_DIM, 1)
            parts.append(acc / swapped if hh % 2 == 0 else swapped / acc)
        outs.append(col(0) * o_cmp[r0:r0 + tq] + col(1) * parts[0] + col(2) * parts[1])
    o_ref[...] = jnp.concatenate(
        [jnp.where(low, outs[0], outs[1]), jnp.where(low, outs[2], outs[3])], axis=1).astype(o_ref.dtype)


def _attn(q, kvc, ksd, vsd, kwd, vwd, ng, e_mat, cs_mat, batch, seq):
    t = q.shape[0]
    tq = TQ
    nq = seq // tq
    qrow = lambda b, g, i: (b * nq + i, g)
    kv = lambda b, g, i: (b, g)
    const = lambda b, g, i: (0, 0)
    rows = HEADS_PER_GROUP * tq
    return pl.pallas_call(
        _attn_kernel,
        grid=(batch, N_KV_GROUPS, nq),
        in_specs=[
            pl.BlockSpec((tq, HEADS_PER_GROUP * HEAD_DIM), qrow),
            pl.BlockSpec((1, N_CMP_PAD, LANES), lambda b, g, i: (0, b * N_KV_GROUPS + g, 0)),
            pl.BlockSpec((1, N_CMP_PAD, LANES), lambda b, g, i: (1, b * N_KV_GROUPS + g, 0)),
            pl.BlockSpec((seq, LANES), kv),
            pl.BlockSpec((seq, LANES), kv),
            pl.BlockSpec((seq, LANES), kv),
            pl.BlockSpec((seq, LANES), kv),
            pl.BlockSpec((tq, LANES), qrow),
            pl.BlockSpec((seq, LANES), const),
            pl.BlockSpec((N_CMP_PAD, LANES), const),
        ],
        out_specs=pl.BlockSpec((tq, HEADS_PER_GROUP * HEAD_DIM), qrow),
        out_shape=jax.ShapeDtypeStruct((t, Q_DIM), BF16),
        scratch_shapes=[
            pltpu.VMEM((seq // CK_SEL, rows, CK_SEL), F32),
            pltpu.VMEM((rows, 2 * LANES), BF16),
            pltpu.VMEM((rows, LANES), F32),
            pltpu.VMEM((rows, LANES), F32),
        ],
        compiler_params=pltpu.CompilerParams(
            dimension_semantics=("parallel", "parallel", "arbitrary"), vmem_limit_bytes=VMEM_LIMIT),
        name="attn",
    )(q, kvc, kvc, ksd, vsd, kwd, vwd, ng, e_mat, cs_mat)


def _post_kernel(o_ref, ya_ref, mg_ref, x_ref, wn_ref, wm_ref, g2_ref, wr_hi_ref, wr_lo_ref, rb_ref,
                 x1_ref, h2_ref, comb_ref):
    y_b = _dot(o_ref[...], wn_ref[...])
    mixed = mg_ref[:, 0:D_MODEL] * ya_ref[...] + mg_ref[:, D_MODEL:2 * D_MODEL] * y_b
    x1 = x_ref[...] + _dot(mixed.astype(BF16), wm_ref[...])
    x1_ref[...] = x1
    h2 = x1 * lax.rsqrt(jnp.mean(x1 * x1, axis=-1, keepdims=True) + EPS) * g2_ref[...]
    h_hi = h2.astype(BF16)
    h2_ref[...] = h_hi
    h_lo = (h2 - h_hi.astype(F32)).astype(BF16)
    logits = (_dot(h_hi, wr_hi_ref[...]) + _dot(h_lo, wr_hi_ref[...]) + _dot(h_hi, wr_lo_ref[...])
              + rb_ref[...])

    lane = _lane_iota(logits.shape).astype(F32)
    is_grp = (lane >= N_EXPERTS) & (lane < N_EXPERTS + N_GROUPS)
    gl = jnp.where(is_grp, logits, NEG_INF)
    ge = jnp.exp(gl - jnp.max(gl, axis=1, keepdims=True))
    gp = ge / jnp.sum(ge, axis=1, keepdims=True)
    g_w = jnp.max(gp, axis=1, keepdims=True)
    big = float(4 * LANES)
    g_first = jnp.min(jnp.where(is_grp & (gp == g_w), lane, big), axis=1, keepdims=True)
    grp_lo = (g_first - N_EXPERTS) * EXPERTS_PER_GROUP

    in_grp = (lane >= grp_lo) & (lane < grp_lo + EXPERTS_PER_GROUP)
    el = jnp.where(in_grp, logits, NEG_INF)
    ee = jnp.exp(el - jnp.max(el, axis=1, keepdims=True))
    ep = ee / jnp.sum(ee, axis=1, keepdims=True)
    w1 = jnp.max(ep, axis=1, keepdims=True)
    i1 = jnp.min(jnp.where(in_grp & (ep == w1), lane, big), axis=1, keepdims=True)
    rest = jnp.where(in_grp & (lane != i1), ep, -1.0)
    w2 = jnp.max(rest, axis=1, keepdims=True)
    i2 = jnp.min(jnp.where(rest == w2, lane, big), axis=1, keepdims=True)
    den = w1 + w2
    comb_ref[...] = jnp.where(lane == i1, g_w * (w1 / den), jnp.where(lane == i2, g_w * (w2 / den), 0.0))


def _post(o_nsa, y_a, mg, x2, w_nsa, w_mix, g2, wr_hi, wr_lo, rb):
    t = x2.shape[0]
    tm = TM_POST
    row = lambda i: (i, 0)
    const = lambda i: (0, 0)
    return pl.pallas_call(
        _post_kernel,
        grid=(t // tm,),
        in_specs=[
            pl.BlockSpec((tm, Q_DIM), row),
            pl.BlockSpec((tm, D_MODEL), row),
            pl.BlockSpec((tm, 2 * D_MODEL), row),
            pl.BlockSpec((tm, D_MODEL), row),
            pl.BlockSpec((Q_DIM, D_MODEL), const),
            pl.BlockSpec((D_MODEL, D_MODEL), const),
            pl.BlockSpec((1, D_MODEL), const),
            pl.BlockSpec((D_MODEL, LANES), const),
            pl.BlockSpec((D_MODEL, LANES), const),
            pl.BlockSpec((1, LANES), const),
        ],
        out_specs=(
            pl.BlockSpec((tm, D_MODEL), row),
            pl.BlockSpec((tm, D_MODEL), row),
            pl.BlockSpec((tm, LANES), row),
        ),
        out_shape=(
            jax.ShapeDtypeStruct((t, D_MODEL), F32),
            jax.ShapeDtypeStruct((t, D_MODEL), BF16),
            jax.ShapeDtypeStruct((t, LANES), F32),
        ),
        compiler_params=pltpu.CompilerParams(
            dimension_semantics=("parallel",), vmem_limit_bytes=VMEM_LIMIT),
        name="post",
    )(o_nsa, y_a, mg, x2, w_nsa, w_mix, g2, wr_hi, wr_lo, rb)


def _moe_kernel(h_ref, comb_ref, x1_ref, wg_ref, wu_ref, wd_ref, gf_ref, o_ref, acc_s):
    e = pl.program_id(1)

    @pl.when(e == 0)
    def _():
        acc_s[...] = jnp.zeros_like(acc_s)

    h = h_ref[...]
    act = jax.nn.silu(_dot(h, wg_ref[0])) * _dot(h, wu_ref[0])
    y = _dot(act.astype(BF16), wd_ref[0])
    comb = comb_ref[...]
    weight = jnp.sum(jnp.where(_lane_iota(comb.shape) == e, comb, 0.0), axis=1, keepdims=True)
    acc_s[...] += weight * y

    @pl.when(e == N_EXPERTS - 1)
    def _():
        x = x1_ref[...] + acc_s[...]
        o_ref[...] = x * lax.rsqrt(jnp.mean(x * x, axis=-1, keepdims=True) + EPS) * gf_ref[...]


def _moe(h2, comb, x1, wg, wu, wd, gf):
    t = h2.shape[0]
    tm = TM_MOE
    row = lambda i, e: (i, 0)
    return pl.pallas_call(
        _moe_kernel,
        grid=(t // tm, N_EXPERTS),
        in_specs=[
            pl.BlockSpec((tm, D_MODEL), row),
            pl.BlockSpec((tm, LANES), row),
            pl.BlockSpec((tm, D_MODEL), row),
            pl.BlockSpec((1, D_MODEL, D_EXPERT), lambda i, e: (e, 0, 0)),
            pl.BlockSpec((1, D_MODEL, D_EXPERT), lambda i, e: (e, 0, 0)),
            pl.BlockSpec((1, D_EXPERT, D_MODEL), lambda i, e: (e, 0, 0)),
            pl.BlockSpec((1, D_MODEL), lambda i, e: (0, 0)),
        ],
        out_specs=pl.BlockSpec((tm, D_MODEL), row),
        out_shape=jax.ShapeDtypeStruct((t, D_MODEL), F32),
        scratch_shapes=[pltpu.VMEM((tm, D_MODEL), F32)],
        compiler_params=pltpu.CompilerParams(
            dimension_semantics=("parallel", "arbitrary"), vmem_limit_bytes=VMEM_LIMIT),
        name="moe",
    )(h2, comb, x1, wg, wu, wd, gf)


def _split_w_in(w):
    sizes = [D_RNN, D_RNN, Q_DIM] + [KV_DIM] * 6 + [3 * N_HEADS, 2 * D_MODEL]
    pts = np.cumsum(sizes)[:-1]
    return jnp.split(w, [int(p) for p in pts], axis=-1)


def _proj_weight(w_in):
    xr, gr, q, kc, vc, ksl, vsl, kw, vw, nsa_g, merge_g = _split_w_in(w_in)
    ng = nsa_g.reshape(D_MODEL, 3, N_KV_GROUPS, HEADS_PER_GROUP).transpose(0, 2, 1, 3)
    ng = ng.reshape(D_MODEL, N_KV_GROUPS, 3 * HEADS_PER_GROUP)
    ng = jnp.pad(ng, ((0, 0), (0, 0), (0, LANES - 3 * HEADS_PER_GROUP))).reshape(D_MODEL, N_KV_GROUPS * LANES)
    return jnp.concatenate([xr, gr, q, merge_g, kc, vc, ksl, vsl, kw, vw, ng], axis=1).astype(BF16)


def _rope_tables(pos, width):
    inv_freq = ROPE_THETA ** (-(jnp.arange(0, HEAD_DIM, 2, dtype=F32) / HEAD_DIM))
    ang = pos.astype(F32)[:, None] * inv_freq[None, :]
    cos, sin = jnp.cos(ang), jnp.sin(ang)
    reps = width // HEAD_DIM
    return jnp.tile(jnp.concatenate([cos, cos], axis=1), (1, reps)), jnp.tile(jnp.concatenate([-sin, sin], axis=1), (1, reps))


def _block_diag(w, per):
    nb, d = w.shape[0], w.shape[1]
    w = w.reshape(nb // per, per, d, d)
    eye = jnp.eye(per, dtype=w.dtype)
    return jnp.einsum('npij,pq->npiqj', w, eye).reshape(nb // per, per * d, per * d)


def _cmp_to_sel():
    n_sel_pad = LANES
    c0 = np.arange(N_CMP_PAD) * CMP_STRIDE
    s0 = np.arange(n_sel_pad) * SEL_BLOCK
    ov = np.minimum(c0[:, None] + CMP_BLOCK, s0[None, :] + SEL_BLOCK) - np.maximum(c0[:, None], s0[None, :])
    m = np.clip(ov, 0, None) / CMP_BLOCK
    m[:, SEL_BLOCK // 2:] = 0.0
    m[N_CMP_PAD - 1, :] = 0.0
    return m.astype(np.float32)


def kernel(x, norm1_g, w_in, conv_w, conv_b, lru_wa, lru_ba, lru_wi, lru_bi, lru_lambda, w_rnn_out, cmpk_pos, cmpk_w1, cmpk_b1, cmpk_w2, cmpk_b2, cmpv_pos, cmpv_w1, cmpv_b1, cmpv_w2, cmpv_b2, w_nsa_out, w_mix_out, norm2_g, router_group_w, router_group_b, router_expert_w, router_expert_b, expert_w_gate, expert_w_up, expert_w_down, final_norm_g):
    batch, seq, _ = x.shape
    t = batch * seq
    assert w_in.shape[0] == 1, "the final norm is fused into the expert kernel: single layer only"
    x2 = x.reshape(t, D_MODEL)

    cos, sin = _rope_tables(jnp.arange(seq), LANES)
    cmp_ends = jnp.arange(N_CMP_PAD) * CMP_STRIDE + (CMP_BLOCK - 1)
    ccos, csin = _rope_tables(cmp_ends, LANES)
    reps = TR_CMP // N_CMP_PAD
    cmp_cos = jnp.stack([jnp.tile(ccos, (reps, 1)), jnp.ones((TR_CMP, LANES), F32)])
    cmp_sin = jnp.stack([jnp.tile(csin, (reps, 1)), jnp.zeros((TR_CMP, LANES), F32)])
    key_blk = np.arange(seq)[:, None] // SEL_BLOCK
    e_mat = jnp.asarray((key_blk == np.arange(LANES)[None, :]).astype(np.float32), dtype=BF16)
    cs_mat = jnp.asarray(_cmp_to_sel(), dtype=BF16)

    l = 0
    xg, q, mg, kcv, ksd, vsd, kwd, vwd, ng = _inproj(
        x2, norm1_g[l][None, :], _proj_weight(w_in[l]), cos, sin, seq)

    y_a = _rnn(xg, conv_w[l], conv_b[l][None, :],
               _block_diag(lru_wa[l], 4).astype(BF16), lru_ba[l][None, :],
               _block_diag(lru_wi[l], 4).astype(BF16), lru_bi[l][None, :],
               lru_lambda[l][None, :], w_rnn_out[l].astype(BF16), batch, seq)

    n_rows = seq // CMP_STRIDE
    x_blk = kcv.reshape(batch, n_rows, CMP_STRIDE, 2, N_KV_GROUPS, HEAD_DIM)
    x_blk = x_blk.transpose(3, 0, 4, 1, 2, 5).reshape(2, batch * N_KV_GROUPS * n_rows, CMP_STRIDE * HEAD_DIM)
    half = CMP_STRIDE * HEAD_DIM
    pos = jnp.stack([cmpk_pos[l].reshape(2, half), cmpv_pos[l].reshape(2, half)])
    w1 = jnp.stack([cmpk_w1[l], cmpv_w1[l]]).astype(BF16)
    b1 = jnp.stack([cmpk_b1[l], cmpv_b1[l]])[:, None, :]
    w2 = jnp.stack([cmpk_w2[l], cmpv_w2[l]])
    w2 = jnp.concatenate([w2, w2], axis=2).astype(BF16)
    b2 = jnp.stack([cmpk_b2[l], cmpv_b2[l]])
    b2 = jnp.concatenate([b2, b2], axis=1)[:, None, :]
    kvc = _compress(x_blk, pos, w1, b1, w2, b2, cmp_cos, cmp_sin)

    o_nsa = _attn(q, kvc, ksd, vsd, kwd, vwd, ng, e_mat, cs_mat, batch, seq)

    wr = jnp.concatenate([
        router_expert_w[l].transpose(1, 0, 2).reshape(D_MODEL, N_EXPERTS),
        router_group_w[l],
        jnp.zeros((D_MODEL, LANES - N_EXPERTS - N_GROUPS), F32)], axis=1)
    wr_hi = wr.astype(BF16)
    wr_lo = (wr - wr_hi.astype(F32)).astype(BF16)
    rb = jnp.concatenate([router_expert_b[l].reshape(N_EXPERTS), router_group_b[l],
                          jnp.zeros((LANES - N_EXPERTS - N_GROUPS,), F32)])[None, :]
    x1, h2, comb = _post(o_nsa, y_a, mg, x2, w_nsa_out[l].astype(BF16), w_mix_out[l].astype(BF16),
                         norm2_g[l][None, :], wr_hi, wr_lo, rb)

    out = _moe(h2, comb, x1, expert_w_gate[l].astype(BF16), expert_w_up[l].astype(BF16),
               expert_w_down[l].astype(BF16), final_norm_g[None, :])
    return out.reshape(batch, seq, D_MODEL)
```

```python
import functools

import numpy as np
import jax
import jax.numpy as jnp
from jax import lax
from jax.experimental import pallas as pl
from jax.experimental.pallas import tpu as pltpu

D_MODEL = 1024
D_RNN = 1024
RNN_BLOCKS = 16
RNN_BLOCK_DIM = D_RNN // RNN_BLOCKS
CONV_WIDTH = 4
LRU_C = 8.0
N_HEADS = 16
HEAD_DIM = 64
HALF_DIM = HEAD_DIM // 2
N_KV_GROUPS = 4
HEADS_PER_GROUP = N_HEADS // N_KV_GROUPS
Q_DIM = N_HEADS * HEAD_DIM
KV_DIM = N_KV_GROUPS * HEAD_DIM
CMP_BLOCK = 32
CMP_STRIDE = 16
CMP_HIDDEN = 256
SEL_BLOCK = 64
SEL_TOPN = 16
WINDOW = 512
ROPE_THETA = 10000.0
FORCE_BONUS = 1e4
NEG_INF = -1e30
N_GROUPS = 4
EXPERTS_PER_GROUP = 4
N_EXPERTS = N_GROUPS * EXPERTS_PER_GROUP
D_EXPERT = 512
EPS = 1e-6

LANES = 128
SUBLANES = 8
VMEM_LIMIT = 56 * 1024 * 1024

BF16 = jnp.bfloat16
F32 = jnp.float32

COL_XR = 0
COL_GR = COL_XR + D_RNN
COL_Q = COL_GR + D_RNN
COL_MG = COL_Q + Q_DIM
COL_KCV = COL_MG + 2 * D_MODEL
COL_KS = COL_KCV + 2 * KV_DIM
COL_VS = COL_KS + KV_DIM
COL_KW = COL_VS + KV_DIM
COL_VW = COL_KW + KV_DIM
COL_NG = COL_VW + KV_DIM
N_PROJ = COL_NG + N_KV_GROUPS * LANES

TM_PROJ = 256
TS_RNN = 512
TR_CMP = 512
TQ = 128
TQ_SELECT = 512
TK = 128
CK_SEL = 512
TM_POST = 512
TM_MOE = 1024
N_CMP_PAD = 128
N_WIN_TILES = WINDOW // TK + 1


def _dot(a, b):
    return jnp.dot(a, b, preferred_element_type=F32)


def _dot_t(a, b):
    return lax.dot_general(a, b, (((1,), (1,)), ((), ())), preferred_element_type=F32)


def _lane_iota(shape):
    return lax.broadcasted_iota(jnp.int32, shape, len(shape) - 1)


def _row_iota(shape):
    return lax.broadcasted_iota(jnp.int32, shape, 0)


def _rope(x, cos, sin_signed):
    width = x.shape[-1]
    reps = width // cos.shape[-1]
    if reps > 1:
        cos = jnp.concatenate([cos] * reps, axis=1)
        sin_signed = jnp.concatenate([sin_signed] * reps, axis=1)
    first_half = (_lane_iota(x.shape) & (HEAD_DIM - 1)) < HALF_DIM
    partner = jnp.where(first_half, pltpu.roll(x, width - HALF_DIM, 1), pltpu.roll(x, HALF_DIM, 1))
    return x * cos + partner * sin_signed


def _spread_heads(x, fill=None):
    out = []
    low = _lane_iota((x.shape[0], LANES)) < HEAD_DIM
    for c in range(x.shape[1] // LANES):
        xc = x[:, c * LANES:(c + 1) * LANES]
        rolled = pltpu.roll(xc, HEAD_DIM, 1)
        out.append(jnp.where(low, xc, rolled if fill is None else fill))
        out.append(jnp.where(low, rolled, xc if fill is None else fill))
    return jnp.concatenate(out, axis=1)


def _inproj_kernel(x_ref, g_ref, w_ref, cos_ref, sin_ref,
                   xg_ref, q_ref, mg_ref, kcv_ref, ks_ref, vs_ref, kw_ref, vw_ref, ng_ref):
    x = x_ref[...]
    y = x * lax.rsqrt(jnp.mean(x * x, axis=-1, keepdims=True) + EPS)
    h = (y * g_ref[...]).astype(BF16)
    cos = cos_ref[...]
    sin = sin_ref[...]

    def mm(lo, width):
        return _dot(h, w_ref[:, lo:lo + width])

    xg_ref[:, 0:D_RNN] = mm(COL_XR, D_RNN)
    xg_ref[:, D_RNN:2 * D_RNN] = mm(COL_GR, D_RNN)
    q_ref[...] = (_rope(mm(COL_Q, Q_DIM), cos, sin) * (HEAD_DIM ** -0.5)).astype(BF16)
    mg_ref[:, 0:D_MODEL] = jax.nn.sigmoid(mm(COL_MG, D_MODEL))
    mg_ref[:, D_MODEL:2 * D_MODEL] = jax.nn.sigmoid(mm(COL_MG + D_MODEL, D_MODEL))
    kcv_ref[...] = mm(COL_KCV, 2 * KV_DIM)
    ks_ref[...] = _spread_heads(_rope(mm(COL_KS, KV_DIM), cos, sin)).astype(BF16)
    vs_ref[...] = _spread_heads(mm(COL_VS, KV_DIM), 1.0).astype(BF16)
    kw_ref[...] = _spread_heads(_rope(mm(COL_KW, KV_DIM), cos, sin)).astype(BF16)
    vw_ref[...] = _spread_heads(mm(COL_VW, KV_DIM), 1.0).astype(BF16)
    ng_ref[...] = jax.nn.sigmoid(mm(COL_NG, N_KV_GROUPS * LANES))


def _inproj(x2, norm_g, w_proj, cos, sin, seq):
    t = x2.shape[0]
    tm = TM_PROJ
    pos_blocks = seq // tm
    row = lambda i: (i, 0)
    const = lambda i: (0, 0)
    out_shape = (
        jax.ShapeDtypeStruct((t, 2 * D_RNN), F32),
        jax.ShapeDtypeStruct((t, Q_DIM), BF16),
        jax.ShapeDtypeStruct((t, 2 * D_MODEL), F32),
        jax.ShapeDtypeStruct((t, 2 * KV_DIM), F32),
        jax.ShapeDtypeStruct((t, N_KV_GROUPS * LANES), BF16),
        jax.ShapeDtypeStruct((t, N_KV_GROUPS * LANES), BF16),
        jax.ShapeDtypeStruct((t, N_KV_GROUPS * LANES), BF16),
        jax.ShapeDtypeStruct((t, N_KV_GROUPS * LANES), BF16),
        jax.ShapeDtypeStruct((t, N_KV_GROUPS * LANES), F32),
    )
    return pl.pallas_call(
        _inproj_kernel,
        grid=(t // tm,),
        in_specs=[
            pl.BlockSpec((tm, D_MODEL), row),
            pl.BlockSpec((1, D_MODEL), const),
            pl.BlockSpec((D_MODEL, N_PROJ), const, pipeline_mode=pl.Buffered(1)),
            pl.BlockSpec((tm, LANES), lambda i: (i % pos_blocks, 0)),
            pl.BlockSpec((tm, LANES), lambda i: (i % pos_blocks, 0)),
        ],
        out_specs=tuple(pl.BlockSpec((tm, s.shape[1]), row) for s in out_shape),
        out_shape=out_shape,
        compiler_params=pltpu.CompilerParams(
            dimension_semantics=("parallel",), vmem_limit_bytes=VMEM_LIMIT),
        name="inproj",
    )(x2, norm_g, w_proj, cos, sin)


def _rnn_kernel(xr_ref, gr_ref, cw_ref, cb_ref, wa_ref, ba_ref, wi_ref, bi_ref, lam_ref, wo_ref,
                y_ref, tail_s, carry_s, a_s, u_s, h_s):
    ts = xr_ref.shape[0]

    @pl.when(pl.program_id(1) == 0)
    def _():
        tail_s[...] = jnp.zeros_like(tail_s)
        carry_s[...] = jnp.zeros_like(carry_s)

    x = xr_ref[...]
    xext = jnp.concatenate([tail_s[...], x], axis=0)
    tail_s[...] = x[ts - SUBLANES:ts, :]
    conv = cb_ref[...]
    for k in range(CONV_WIDTH):
        back = CONV_WIDTH - 1 - k
        shifted = xext if back == 0 else pltpu.roll(xext, back, 0)
        conv = conv + cw_ref[k:k + 1, :] * shifted[SUBLANES:SUBLANES + ts, :]

    cb16 = conv.astype(BF16)
    blk = wa_ref.shape[1]
    r_pre = jnp.concatenate(
        [_dot(cb16[:, j * blk:(j + 1) * blk], wa_ref[j]) for j in range(D_RNN // blk)], axis=1)
    i_pre = jnp.concatenate(
        [_dot(cb16[:, j * blk:(j + 1) * blk], wi_ref[j]) for j in range(D_RNN // blk)], axis=1)
    r = jax.nn.sigmoid(r_pre + ba_ref[...])
    gate_i = jax.nn.sigmoid(i_pre + bi_ref[...])
    neg_lam = -lam_ref[...]
    softplus = jnp.maximum(neg_lam, 0.0) + jnp.log1p(jnp.exp(-jnp.abs(neg_lam)))
    log_a = (-LRU_C) * r * softplus
    a = jnp.exp(log_a)
    a_s[...] = a
    u_s[...] = jnp.sqrt(-jnp.tanh(log_a) * (a * a + 1.0)) * (gate_i * conv)

    row = _row_iota((SUBLANES, D_RNN))

    def body(k, carry):
        off = pl.multiple_of(k * SUBLANES, SUBLANES)
        a = a_s[pl.ds(off, SUBLANES), :]
        b = u_s[pl.ds(off, SUBLANES), :]
        for sh in (1, 2, 4):
            keep = row >= sh
            a_prev = jnp.where(keep, pltpu.roll(a, sh, 0), 1.0)
            b_prev = jnp.where(keep, pltpu.roll(b, sh, 0), 0.0)
            b = a * b_prev + b
            a = a * a_prev
        h = a * carry + b
        h_s[pl.ds(off, SUBLANES), :] = h
        return jnp.broadcast_to(h[SUBLANES - 1:SUBLANES, :], (SUBLANES, D_RNN))

    carry_s[...] = lax.fori_loop(0, ts // SUBLANES, body, carry_s[...])
    gated = (jax.nn.gelu(gr_ref[...]) * h_s[...]).astype(BF16)
    y_ref[...] = _dot(gated, wo_ref[...])


def _rnn(xg, conv_w, conv_b, wa_bd, ba, wi_bd, bi, lam, w_out, batch, seq):
    t = xg.shape[0]
    ts = TS_RNN
    nt = seq // ts
    const2 = lambda b, s: (0, 0)
    const3 = lambda b, s: (0, 0, 0)
    nblk, blk = wa_bd.shape[0], wa_bd.shape[1]
    return pl.pallas_call(
        _rnn_kernel,
        grid=(batch, nt),
        in_specs=[
            pl.BlockSpec((ts, D_RNN), lambda b, s: (b * nt + s, 0)),
            pl.BlockSpec((ts, D_RNN), lambda b, s: (b * nt + s, 1)),
            pl.BlockSpec((CONV_WIDTH, D_RNN), const2),
            pl.BlockSpec((1, D_RNN), const2),
            pl.BlockSpec((nblk, blk, blk), const3),
            pl.BlockSpec((1, D_RNN), const2),
            pl.BlockSpec((nblk, blk, blk), const3),
            pl.BlockSpec((1, D_RNN), const2),
            pl.BlockSpec((1, D_RNN), const2),
            pl.BlockSpec((D_RNN, D_MODEL), const2),
        ],
        out_specs=pl.BlockSpec((ts, D_MODEL), lambda b, s: (b * nt + s, 0)),
        out_shape=jax.ShapeDtypeStruct((t, D_MODEL), F32),
        scratch_shapes=[
            pltpu.VMEM((SUBLANES, D_RNN), F32),
            pltpu.VMEM((SUBLANES, D_RNN), F32),
            pltpu.VMEM((ts, D_RNN), F32),
            pltpu.VMEM((ts, D_RNN), F32),
            pltpu.VMEM((ts, D_RNN), F32),
        ],
        compiler_params=pltpu.CompilerParams(
            dimension_semantics=("parallel", "arbitrary"), vmem_limit_bytes=VMEM_LIMIT),
        name="rnn",
    )(xg, xg, conv_w, conv_b, wa_bd, ba, wi_bd, bi, lam, w_out)


def _compress_kernel(x_ref, pos_ref, w1_ref, b1_ref, w2_ref, b2_ref, cos_ref, sin_ref, o_ref):
    x = x_ref[0]
    tr = x.shape[0]
    half = x.shape[1]
    first = _dot((x + pos_ref[0, 0:1, :]).astype(BF16), w1_ref[0, 0:half, :])
    second = _dot((x + pos_ref[0, 1:2, :]).astype(BF16), w1_ref[0, half:2 * half, :])
    hid = jax.nn.gelu(first + pltpu.roll(second, tr - 1, 0) + b1_ref[0])
    out = _dot(hid.astype(BF16), w2_ref[0]) + b2_ref[0]
    o_ref[0] = _rope(out, cos_ref[0], sin_ref[0]).astype(BF16)


def _compress(x_blk, pos, w1, b1, w2, b2, cos, sin):
    rows = x_blk.shape[1]
    tr = TR_CMP
    sel = lambda k, r: (k, 0, 0)
    return pl.pallas_call(
        _compress_kernel,
        grid=(2, rows // tr),
        in_specs=[
            pl.BlockSpec((1, tr, x_blk.shape[2]), lambda k, r: (k, r, 0)),
            pl.BlockSpec((1, 2, pos.shape[2]), sel),
            pl.BlockSpec((1,) + w1.shape[1:], sel),
            pl.BlockSpec((1, 1, CMP_HIDDEN), sel),
            pl.BlockSpec((1,) + w2.shape[1:], sel),
            pl.BlockSpec((1, 1, LANES), sel),
            pl.BlockSpec((1, tr, LANES), sel),
            pl.BlockSpec((1, tr, LANES), sel),
        ],
        out_specs=pl.BlockSpec((1, tr, LANES), lambda k, r: (k, r, 0)),
        out_shape=jax.ShapeDtypeStruct((2, rows, LANES), BF16),
        compiler_params=pltpu.CompilerParams(
            dimension_semantics=("parallel", "parallel"), vmem_limit_bytes=VMEM_LIMIT),
        name="compress",
    )(x_blk, pos, w1, b1, w2, b2, cos, sin)


def _lane_tile_max(s):
    tiles = [s[:, c * LANES:(c + 1) * LANES] for c in range(s.shape[1] // LANES)]
    while len(tiles) > 1:
        tiles = [jnp.maximum(tiles[k], tiles[k + 1]) if k + 1 < len(tiles) else tiles[k]
                 for k in range(0, len(tiles), 2)]
    return tiles[0]


def _stack_heads(q):
    tq = q.shape[0]
    low = _lane_iota((tq, LANES)) < HEAD_DIM
    zero = jnp.zeros((tq, LANES), BF16)
    heads = []
    for hh in range(HEADS_PER_GROUP):
        pair = q[:, (hh // 2) * LANES:(hh // 2 + 1) * LANES]
        heads.append(jnp.where(low if hh % 2 == 0 else jnp.logical_not(low), pair, zero))
    return jnp.concatenate(heads, axis=0)


def _select_kernel(q_ref, kc_ref, vc_ref, ng_ref, m_ref, bias_ref, ocmp_ref):
    tq = TQ
    for sub in range(q_ref.shape[0] // tq):
        bias, o_cmp = _select_tile(pl.program_id(2) * (q_ref.shape[0] // tq) + sub,
                                   q_ref[sub * tq:(sub + 1) * tq, :], kc_ref, vc_ref,
                                   ng_ref[sub * tq:(sub + 1) * tq, :], m_ref)
        bias_ref[sub * tq:(sub + 1) * tq, :] = bias
        ocmp_ref[sub * tq:(sub + 1) * tq, :] = o_cmp


def _select_tile(i, q, kc_ref, vc_ref, ng, m_ref):
    tq = q.shape[0]
    rows = HEADS_PER_GROUP * tq
    low = _lane_iota((tq, LANES)) < HEAD_DIM
    q4 = _stack_heads(q)
    lane4 = _lane_iota((rows, LANES))
    qpos4 = i * tq + (_row_iota((rows, LANES)) & (tq - 1))

    sc = _dot_t(q4, kc_ref[0])
    sc = jnp.where(lane4 * CMP_STRIDE + (CMP_BLOCK - 1) <= qpos4, sc, NEG_INF)
    mc = jnp.max(sc, axis=1, keepdims=True)
    pc = jnp.exp(sc - mc)
    pc = pc / jnp.sum(pc, axis=1, keepdims=True)
    pc = jnp.where(qpos4 >= CMP_BLOCK - 1, pc, 0.0)
    o_cmp = _dot(pc.astype(BF16), vc_ref[0])

    psum = pc[0:tq] + pc[tq:2 * tq] + pc[2 * tq:3 * tq] + pc[3 * tq:4 * tq]
    p_hi = psum.astype(BF16)
    rem = psum - p_hi.astype(F32)
    p_mid = rem.astype(BF16)
    p_lo = (rem - p_mid.astype(F32)).astype(BF16)
    cs = m_ref[...]
    p_slc = _dot(p_hi, cs) + _dot(p_mid, cs) + _dot(p_lo, cs)

    n_sel = SEL_BLOCK // 2
    p_slc_t = p_slc.T[0:n_sel, :]
    blk = _row_iota((n_sel, tq))
    tblk = (i * tq + _lane_iota((n_sel, tq))) >> 6
    forced = (blk == 0) | (blk == tblk) | (blk == tblk - 1)
    score = jnp.where(blk <= tblk, p_slc_t + jnp.where(forced, FORCE_BONUS, 0.0), -1.0)
    rank = jnp.zeros((n_sel, tq), F32)
    for j in range(n_sel):
        sj = score[j:j + 1, :]
        beats = (sj > score) | ((sj == score) & (blk > j))
        rank = rank + jnp.where(beats, 1.0, 0.0)
    bias_t = jnp.where(rank < SEL_TOPN, 0.0, NEG_INF)
    bias = jnp.concatenate([bias_t, jnp.zeros((LANES - n_sel, tq), F32)], axis=0).T.astype(BF16)

    gated = [ng[:, hh:hh + 1] * o_cmp[hh * tq:(hh + 1) * tq] for hh in range(HEADS_PER_GROUP)]
    o_pairs = jnp.concatenate([jnp.where(low, gated[0], gated[1]), jnp.where(low, gated[2], gated[3])], axis=1)
    return bias, o_pairs


def _attn_kernel(*refs):
    n_full = pl.program_id(2) // (CK_SEL // TQ)
    for n_chunks in range(1, refs[3].shape[0] // CK_SEL + 1):
        pl.when(n_full == n_chunks - 1)(functools.partial(_attn_step, n_chunks, *refs))


def _attn_step(n_chunks, q_ref, bias_ref, ocmp_ref, ks_ref, vs_ref, kw_ref, vw_ref, ng_ref, e_ref, o_ref):
    i = pl.program_id(2)
    tq = q_ref.shape[0]
    rows = HEADS_PER_GROUP * tq
    ck = CK_SEL
    low = _lane_iota((tq, LANES)) < HEAD_DIM
    q4 = _stack_heads(q_ref[...])

    k_tiles, v_tiles = [], []
    for jj in range(N_WIN_TILES):
        j = i - (N_WIN_TILES - 1) + jj
        off = pl.multiple_of(jnp.maximum(j, 0) * TK, TK)
        k_tiles.append(kw_ref[pl.ds(off, TK), :])
        v_tiles.append(vw_ref[pl.ds(off, TK), :])
    sw = _dot_t(q4, jnp.concatenate(k_tiles, axis=0))
    rel = (_row_iota((rows, TK)) & (tq - 1)) - _lane_iota((rows, TK))
    sw_tiles = []
    for jj in range(N_WIN_TILES):
        tile = sw[:, jj * TK:(jj + 1) * TK]
        if jj == 0:
            tile = jnp.where(rel < 0, tile, NEG_INF)
        if jj == N_WIN_TILES - 1:
            tile = jnp.where(rel >= 0, tile, NEG_INF)
        else:
            tile = jnp.where(i - (N_WIN_TILES - 1) + jj >= 0, tile, NEG_INF)
        sw_tiles.append(tile)
    sw = jnp.concatenate(sw_tiles, axis=1)
    m_win = jnp.max(_lane_tile_max(sw), axis=1, keepdims=True)
    acc_win = _dot(jnp.exp(sw - m_win).astype(BF16), jnp.concatenate(v_tiles, axis=0))

    qa = jnp.concatenate([q4, jnp.concatenate([bias_ref[...]] * HEADS_PER_GROUP, axis=0)], axis=1)
    rel_q = i * tq - (n_chunks - 1) * ck + (_row_iota((rows, ck)) & (tq - 1))
    m_run = None
    acc_sel = None
    for c in range(n_chunks):
        ka = jnp.concatenate([ks_ref[c * ck:(c + 1) * ck, :], e_ref[c * ck:(c + 1) * ck, :]], axis=1)
        s = _dot_t(qa, ka)
        if c == n_chunks - 1:
            s = jnp.where(_lane_iota((rows, ck)) <= rel_q, s, NEG_INF)
        m_new = jnp.max(_lane_tile_max(s), axis=1, keepdims=True)
        if c > 0:
            m_new = jnp.maximum(m_run, m_new)
            acc_sel = acc_sel * jnp.exp(m_run - m_new)
        part = _dot(jnp.exp(s - m_new).astype(BF16), vs_ref[c * ck:(c + 1) * ck, :])
        acc_sel = part if c == 0 else acc_sel + part
        m_run = m_new

    ng = ng_ref[...]
    outs = []
    for hh in range(HEADS_PER_GROUP):
        r0 = hh * tq
        col = lambda br: ng[:, br * HEADS_PER_GROUP + hh:br * HEADS_PER_GROUP + hh + 1]
        parts = []
        for acc in (acc_sel[r0:r0 + tq], acc_win[r0:r0 + tq]):
            swapped = pltpu.roll(acc, HEAD_DIM, 1)
            parts.append(acc / swapped if hh % 2 == 0 else swapped / acc)
        outs.append(col(1) * parts[0] + col(2) * parts[1])
    o_ref[...] = (ocmp_ref[...] + jnp.concatenate(
        [jnp.where(low, outs[0], outs[1]), jnp.where(low, outs[2], outs[3])], axis=1)).astype(o_ref.dtype)


def _select(q, kvc, ng, cs_mat, batch, seq):
    t = q.shape[0]
    tq = TQ_SELECT
    nq = seq // tq
    qrow = lambda b, g, i: (b * nq + i, g)
    return pl.pallas_call(
        _select_kernel,
        grid=(batch, N_KV_GROUPS, nq),
        in_specs=[
            pl.BlockSpec((tq, HEADS_PER_GROUP * HEAD_DIM), qrow),
            pl.BlockSpec((1, N_CMP_PAD, LANES), lambda b, g, i: (0, b * N_KV_GROUPS + g, 0)),
            pl.BlockSpec((1, N_CMP_PAD, LANES), lambda b, g, i: (1, b * N_KV_GROUPS + g, 0)),
            pl.BlockSpec((tq, LANES), qrow),
            pl.BlockSpec((N_CMP_PAD, LANES), lambda b, g, i: (0, 0)),
        ],
        out_specs=(
            pl.BlockSpec((tq, LANES), qrow),
            pl.BlockSpec((tq, HEADS_PER_GROUP * HEAD_DIM), qrow),
        ),
        out_shape=(
            jax.ShapeDtypeStruct((t, N_KV_GROUPS * LANES), BF16),
            jax.ShapeDtypeStruct((t, Q_DIM), F32),
        ),
        compiler_params=pltpu.CompilerParams(
            dimension_semantics=("parallel", "parallel", "parallel"), vmem_limit_bytes=VMEM_LIMIT),
        name="select",
    )(q, kvc, kvc, ng, cs_mat)


def _attn(q, bias, o_cmp, ksd, vsd, kwd, vwd, ng, e_mat, batch, seq):
    t = q.shape[0]
    tq = TQ
    nq = seq // tq
    qrow = lambda b, g, i: (b * nq + i, g)
    kv = lambda b, g, i: (b, g)
    return pl.pallas_call(
        _attn_kernel,
        grid=(batch, N_KV_GROUPS, nq),
        in_specs=[
            pl.BlockSpec((tq, HEADS_PER_GROUP * HEAD_DIM), qrow),
            pl.BlockSpec((tq, LANES), qrow),
            pl.BlockSpec((tq, HEADS_PER_GROUP * HEAD_DIM), qrow),
            pl.BlockSpec((seq, LANES), kv),
            pl.BlockSpec((seq, LANES), kv),
            pl.BlockSpec((seq, LANES), kv),
            pl.BlockSpec((seq, LANES), kv),
            pl.BlockSpec((tq, LANES), qrow),
            pl.BlockSpec((seq, LANES), lambda b, g, i: (0, 0)),
        ],
        out_specs=pl.BlockSpec((tq, HEADS_PER_GROUP * HEAD_DIM), qrow),
        out_shape=jax.ShapeDtypeStruct((t, Q_DIM), BF16),
        compiler_params=pltpu.CompilerParams(
            dimension_semantics=("parallel", "parallel", "arbitrary"), vmem_limit_bytes=VMEM_LIMIT),
        name="attn",
    )(q, bias, o_cmp, ksd, vsd, kwd, vwd, ng, e_mat)


def _post_kernel(o_ref, ya_ref, mg_ref, x_ref, wn_ref, wm_ref, g2_ref, wr_hi_ref, wr_lo_ref, rb_ref,
                 x1_ref, h2_ref, comb_ref):
    y_b = _dot(o_ref[...], wn_ref[...])
    mixed = mg_ref[:, 0:D_MODEL] * ya_ref[...] + mg_ref[:, D_MODEL:2 * D_MODEL] * y_b
    x1 = x_ref[...] + _dot(mixed.astype(BF16), wm_ref[...])
    x1_ref[...] = x1
    h2 = x1 * lax.rsqrt(jnp.mean(x1 * x1, axis=-1, keepdims=True) + EPS) * g2_ref[...]
    h_hi = h2.astype(BF16)
    h2_ref[...] = h_hi
    h_lo = (h2 - h_hi.astype(F32)).astype(BF16)
    logits = (_dot(h_hi, wr_hi_ref[...]) + _dot(h_lo, wr_hi_ref[...]) + _dot(h_hi, wr_lo_ref[...])
              + rb_ref[...])

    lane = _lane_iota(logits.shape).astype(F32)
    is_grp = (lane >= N_EXPERTS) & (lane < N_EXPERTS + N_GROUPS)
    gl = jnp.where(is_grp, logits, NEG_INF)
    ge = jnp.exp(gl - jnp.max(gl, axis=1, keepdims=True))
    gp = ge / jnp.sum(ge, axis=1, keepdims=True)
    g_w = jnp.max(gp, axis=1, keepdims=True)
    big = float(4 * LANES)
    g_first = jnp.min(jnp.where(is_grp & (gp == g_w), lane, big), axis=1, keepdims=True)
    grp_lo = (g_first - N_EXPERTS) * EXPERTS_PER_GROUP

    in_grp = (lane >= grp_lo) & (lane < grp_lo + EXPERTS_PER_GROUP)
    el = jnp.where(in_grp, logits, NEG_INF)
    ee = jnp.exp(el - jnp.max(el, axis=1, keepdims=True))
    ep = ee / jnp.sum(ee, axis=1, keepdims=True)
    w1 = jnp.max(ep, axis=1, keepdims=True)
    i1 = jnp.min(jnp.where(in_grp & (ep == w1), lane, big), axis=1, keepdims=True)
    rest = jnp.where(in_grp & (lane != i1), ep, -1.0)
    w2 = jnp.max(rest, axis=1, keepdims=True)
    i2 = jnp.min(jnp.where(rest == w2, lane, big), axis=1, keepdims=True)
    den = w1 + w2
    comb_ref[...] = jnp.where(lane == i1, g_w * (w1 / den), jnp.where(lane == i2, g_w * (w2 / den), 0.0))


def _post(o_nsa, y_a, mg, x2, w_nsa, w_mix, g2, wr_hi, wr_lo, rb):
    t = x2.shape[0]
    tm = TM_POST
    row = lambda i: (i, 0)
    const = lambda i: (0, 0)
    return pl.pallas_call(
        _post_kernel,
        grid=(t // tm,),
        in_specs=[
            pl.BlockSpec((tm, Q_DIM), row),
            pl.BlockSpec((tm, D_MODEL), row),
            pl.BlockSpec((tm, 2 * D_MODEL), row),
            pl.BlockSpec((tm, D_MODEL), row),
            pl.BlockSpec((Q_DIM, D_MODEL), const),
            pl.BlockSpec((D_MODEL, D_MODEL), const),
            pl.BlockSpec((1, D_MODEL), const),
            pl.BlockSpec((D_MODEL, LANES), const),
            pl.BlockSpec((D_MODEL, LANES), const),
            pl.BlockSpec((1, LANES), const),
        ],
        out_specs=(
            pl.BlockSpec((tm, D_MODEL), row),
            pl.BlockSpec((tm, D_MODEL), row),
            pl.BlockSpec((tm, LANES), row),
        ),
        out_shape=(
            jax.ShapeDtypeStruct((t, D_MODEL), F32),
            jax.ShapeDtypeStruct((t, D_MODEL), BF16),
            jax.ShapeDtypeStruct((t, LANES), F32),
        ),
        compiler_params=pltpu.CompilerParams(
            dimension_semantics=("parallel",), vmem_limit_bytes=VMEM_LIMIT),
        name="post",
    )(o_nsa, y_a, mg, x2, w_nsa, w_mix, g2, wr_hi, wr_lo, rb)


def _moe_kernel(h_ref, comb_ref, x1_ref, wg_ref, wu_ref, wd_ref, gf_ref, o_ref, acc_s):
    e = pl.program_id(1)

    @pl.when(e == 0)
    def _():
        acc_s[...] = jnp.zeros_like(acc_s)

    h = h_ref[...]
    act = jax.nn.silu(_dot(h, wg_ref[0])) * _dot(h, wu_ref[0])
    y = _dot(act.astype(BF16), wd_ref[0])
    comb = comb_ref[...]
    weight = jnp.sum(jnp.where(_lane_iota(comb.shape) == e, comb, 0.0), axis=1, keepdims=True)
    acc_s[...] += weight * y

    @pl.when(e == N_EXPERTS - 1)
    def _():
        x = x1_ref[...] + acc_s[...]
        o_ref[...] = x * lax.rsqrt(jnp.mean(x * x, axis=-1, keepdims=True) + EPS) * gf_ref[...]


def _moe(h2, comb, x1, wg, wu, wd, gf):
    t = h2.shape[0]
    tm = TM_MOE
    row = lambda i, e: (i, 0)
    return pl.pallas_call(
        _moe_kernel,
        grid=(t // tm, N_EXPERTS),
        in_specs=[
            pl.BlockSpec((tm, D_MODEL), row),
            pl.BlockSpec((tm, LANES), row),
            pl.BlockSpec((tm, D_MODEL), row),
            pl.BlockSpec((1, D_MODEL, D_EXPERT), lambda i, e: (e, 0, 0)),
            pl.BlockSpec((1, D_MODEL, D_EXPERT), lambda i, e: (e, 0, 0)),
            pl.BlockSpec((1, D_EXPERT, D_MODEL), lambda i, e: (e, 0, 0)),
            pl.BlockSpec((1, D_MODEL), lambda i, e: (0, 0)),
        ],
        out_specs=pl.BlockSpec((tm, D_MODEL), row),
        out_shape=jax.ShapeDtypeStruct((t, D_MODEL), F32),
        scratch_shapes=[pltpu.VMEM((tm, D_MODEL), F32)],
        compiler_params=pltpu.CompilerParams(
            dimension_semantics=("parallel", "arbitrary"), vmem_limit_bytes=VMEM_LIMIT),
        name="moe",
    )(h2, comb, x1, wg, wu, wd, gf)


def _split_w_in(w):
    sizes = [D_RNN, D_RNN, Q_DIM] + [KV_DIM] * 6 + [3 * N_HEADS, 2 * D_MODEL]
    pts = np.cumsum(sizes)[:-1]
    return jnp.split(w, [int(p) for p in pts], axis=-1)


def _proj_weight(w_in):
    xr, gr, q, kc, vc, ksl, vsl, kw, vw, nsa_g, merge_g = _split_w_in(w_in)
    ng = nsa_g.reshape(D_MODEL, 3, N_KV_GROUPS, HEADS_PER_GROUP).transpose(0, 2, 1, 3)
    ng = ng.reshape(D_MODEL, N_KV_GROUPS, 3 * HEADS_PER_GROUP)
    ng = jnp.pad(ng, ((0, 0), (0, 0), (0, LANES - 3 * HEADS_PER_GROUP))).reshape(D_MODEL, N_KV_GROUPS * LANES)
    return jnp.concatenate([xr, gr, q, merge_g, kc, vc, ksl, vsl, kw, vw, ng], axis=1).astype(BF16)


def _rope_tables(pos, width):
    inv_freq = ROPE_THETA ** (-(jnp.arange(0, HEAD_DIM, 2, dtype=F32) / HEAD_DIM))
    ang = pos.astype(F32)[:, None] * inv_freq[None, :]
    cos, sin = jnp.cos(ang), jnp.sin(ang)
    reps = width // HEAD_DIM
    return jnp.tile(jnp.concatenate([cos, cos], axis=1), (1, reps)), jnp.tile(jnp.concatenate([-sin, sin], axis=1), (1, reps))


def _block_diag(w, per):
    nb, d = w.shape[0], w.shape[1]
    w = w.reshape(nb // per, per, d, d)
    eye = jnp.eye(per, dtype=w.dtype)
    return jnp.einsum('npij,pq->npiqj', w, eye).reshape(nb // per, per * d, per * d)


def _cmp_to_sel():
    n_sel_pad = LANES
    c0 = np.arange(N_CMP_PAD) * CMP_STRIDE
    s0 = np.arange(n_sel_pad) * SEL_BLOCK
    ov = np.minimum(c0[:, None] + CMP_BLOCK, s0[None, :] + SEL_BLOCK) - np.maximum(c0[:, None], s0[None, :])
    m = np.clip(ov, 0, None) / CMP_BLOCK
    m[:, SEL_BLOCK // 2:] = 0.0
    m[N_CMP_PAD - 1, :] = 0.0
    return m.astype(np.float32)


def kernel(x, norm1_g, w_in, conv_w, conv_b, lru_wa, lru_ba, lru_wi, lru_bi, lru_lambda, w_rnn_out, cmpk_pos, cmpk_w1, cmpk_b1, cmpk_w2, cmpk_b2, cmpv_pos, cmpv_w1, cmpv_b1, cmpv_w2, cmpv_b2, w_nsa_out, w_mix_out, norm2_g, router_group_w, router_group_b, router_expert_w, router_expert_b, expert_w_gate, expert_w_up, expert_w_down, final_norm_g):
    batch, seq, _ = x.shape
    t = batch * seq
    assert w_in.shape[0] == 1, "the final norm is fused into the expert kernel: single layer only"
    x2 = x.reshape(t, D_MODEL)

    cos, sin = _rope_tables(jnp.arange(seq), LANES)
    cmp_ends = jnp.arange(N_CMP_PAD) * CMP_STRIDE + (CMP_BLOCK - 1)
    ccos, csin = _rope_tables(cmp_ends, LANES)
    reps = TR_CMP // N_CMP_PAD
    cmp_cos = jnp.stack([jnp.tile(ccos, (reps, 1)), jnp.ones((TR_CMP, LANES), F32)])
    cmp_sin = jnp.stack([jnp.tile(csin, (reps, 1)), jnp.zeros((TR_CMP, LANES), F32)])
    key_blk = np.arange(seq)[:, None] // SEL_BLOCK
    e_mat = jnp.asarray((key_blk == np.arange(LANES)[None, :]).astype(np.float32), dtype=BF16)
    cs_mat = jnp.asarray(_cmp_to_sel(), dtype=BF16)

    l = 0
    xg, q, mg, kcv, ksd, vsd, kwd, vwd, ng = _inproj(
        x2, norm1_g[l][None, :], _proj_weight(w_in[l]), cos, sin, seq)

    y_a = _rnn(xg, conv_w[l], conv_b[l][None, :],
               _block_diag(lru_wa[l], 4).astype(BF16), lru_ba[l][None, :],
               _block_diag(lru_wi[l], 4).astype(BF16), lru_bi[l][None, :],
               lru_lambda[l][None, :], w_rnn_out[l].astype(BF16), batch, seq)

    n_rows = seq // CMP_STRIDE
    x_blk = kcv.reshape(batch, n_rows, CMP_STRIDE, 2, N_KV_GROUPS, HEAD_DIM)
    x_blk = x_blk.transpose(3, 0, 4, 1, 2, 5).reshape(2, batch * N_KV_GROUPS * n_rows, CMP_STRIDE * HEAD_DIM)
    half = CMP_STRIDE * HEAD_DIM
    pos = jnp.stack([cmpk_pos[l].reshape(2, half), cmpv_pos[l].reshape(2, half)])
    w1 = jnp.stack([cmpk_w1[l], cmpv_w1[l]]).astype(BF16)
    b1 = jnp.stack([cmpk_b1[l], cmpv_b1[l]])[:, None, :]
    w2 = jnp.stack([cmpk_w2[l], cmpv_w2[l]])
    w2 = jnp.concatenate([w2, w2], axis=2).astype(BF16)
    b2 = jnp.stack([cmpk_b2[l], cmpv_b2[l]])
    b2 = jnp.concatenate([b2, b2], axis=1)[:, None, :]
    kvc = _compress(x_blk, pos, w1, b1, w2, b2, cmp_cos, cmp_sin)

    bias, o_cmp = _select(q, kvc, ng, cs_mat, batch, seq)
    o_nsa = _attn(q, bias, o_cmp, ksd, vsd, kwd, vwd, ng, e_mat, batch, seq)

    wr = jnp.concatenate([
        router_expert_w[l].transpose(1, 0, 2).reshape(D_MODEL, N_EXPERTS),
        router_group_w[l],
        jnp.zeros((D_MODEL, LANES - N_EXPERTS - N_GROUPS), F32)], axis=1)
    wr_hi = wr.astype(BF16)
    wr_lo = (wr - wr_hi.astype(F32)).astype(BF16)
    rb = jnp.concatenate([router_expert_b[l].reshape(N_EXPERTS), router_group_b[l],
                          jnp.zeros((LANES - N_EXPERTS - N_GROUPS,), F32)])[None, :]
    x1, h2, comb = _post(o_nsa, y_a, mg, x2, w_nsa_out[l].astype(BF16), w_mix_out[l].astype(BF16),
                         norm2_g[l][None, :], wr_hi, wr_lo, rb)

    out = _moe(h2, comb, x1, expert_w_gate[l].astype(BF16), expert_w_up[l].astype(BF16),
               expert_w_down[l].astype(BF16), final_norm_g[None, :])
    return out.reshape(batch, seq, D_MODEL)
```

```python
import functools

import numpy as np
import jax
import jax.numpy as jnp
from jax import lax
from jax.experimental import pallas as pl
from jax.experimental.pallas import tpu as pltpu

D_MODEL = 1024
D_RNN = 1024
RNN_BLOCKS = 16
RNN_BLOCK_DIM = D_RNN // RNN_BLOCKS
CONV_WIDTH = 4
LRU_C = 8.0
N_HEADS = 16
HEAD_DIM = 64
HALF_DIM = HEAD_DIM // 2
N_KV_GROUPS = 4
HEADS_PER_GROUP = N_HEADS // N_KV_GROUPS
Q_DIM = N_HEADS * HEAD_DIM
KV_DIM = N_KV_GROUPS * HEAD_DIM
CMP_BLOCK = 32
CMP_STRIDE = 16
CMP_HIDDEN = 256
SEL_BLOCK = 64
SEL_TOPN = 16
WINDOW = 512
ROPE_THETA = 10000.0
FORCE_BONUS = 1e4
NEG_INF = -1e30
N_GROUPS = 4
EXPERTS_PER_GROUP = 4
N_EXPERTS = N_GROUPS * EXPERTS_PER_GROUP
D_EXPERT = 512
EPS = 1e-6

LANES = 128
SUBLANES = 8
VMEM_LIMIT = 56 * 1024 * 1024

BF16 = jnp.bfloat16
F32 = jnp.float32

COL_XR = 0
COL_GR = COL_XR + D_RNN
COL_Q = COL_GR + D_RNN
COL_MG = COL_Q + Q_DIM
COL_KCV = COL_MG + 2 * D_MODEL
COL_KS = COL_KCV + 2 * KV_DIM
COL_VS = COL_KS + KV_DIM
COL_KW = COL_VS + KV_DIM
COL_VW = COL_KW + KV_DIM
COL_NG = COL_VW + KV_DIM
N_PROJ = COL_NG + N_KV_GROUPS * LANES

TM_PROJ = 256
TS_RNN = 512
TR_CMP = 256
TQ = 256
TQ_SELECT = 512
TQ_SELECT_TILE = 128
TK = 128
CK_SEL = 512
TM_POST = 512
TM_MOE = 1024
N_CMP_PAD = 128


def _dot(a, b):
    return jnp.dot(a, b, preferred_element_type=F32)


def _dot_t(a, b):
    return lax.dot_general(a, b, (((1,), (1,)), ((), ())), preferred_element_type=F32)


def _lane_iota(shape):
    return lax.broadcasted_iota(jnp.int32, shape, len(shape) - 1)


def _row_iota(shape):
    return lax.broadcasted_iota(jnp.int32, shape, 0)


def _rope(x, cos, sin_signed):
    width = x.shape[-1]
    reps = width // cos.shape[-1]
    if reps > 1:
        cos = jnp.concatenate([cos] * reps, axis=1)
        sin_signed = jnp.concatenate([sin_signed] * reps, axis=1)
    first_half = (_lane_iota(x.shape) & (HEAD_DIM - 1)) < HALF_DIM
    partner = jnp.where(first_half, pltpu.roll(x, width - HALF_DIM, 1), pltpu.roll(x, HALF_DIM, 1))
    return x * cos + partner * sin_signed


def _spread_heads(x, fill=None):
    out = []
    low = _lane_iota((x.shape[0], LANES)) < HEAD_DIM
    for c in range(x.shape[1] // LANES):
        xc = x[:, c * LANES:(c + 1) * LANES]
        rolled = pltpu.roll(xc, HEAD_DIM, 1)
        out.append(jnp.where(low, xc, rolled if fill is None else fill))
        out.append(jnp.where(low, rolled, xc if fill is None else fill))
    return jnp.concatenate(out, axis=1)


def _inproj_kernel(x_ref, g_ref, w_ref, cos_ref, sin_ref,
                   xg_ref, q_ref, mg_ref, kcv_ref, ks_ref, vs_ref, kw_ref, vw_ref, ng_ref):
    x = x_ref[...]
    y = x * lax.rsqrt(jnp.mean(x * x, axis=-1, keepdims=True) + EPS)
    h = (y * g_ref[...]).astype(BF16)
    cos = cos_ref[...]
    sin = sin_ref[...]

    def mm(lo, width):
        return _dot(h, w_ref[:, lo:lo + width])

    xg_ref[:, 0:D_RNN] = mm(COL_XR, D_RNN)
    xg_ref[:, D_RNN:2 * D_RNN] = mm(COL_GR, D_RNN)
    q_ref[...] = (_rope(mm(COL_Q, Q_DIM), cos, sin) * (HEAD_DIM ** -0.5)).astype(BF16)
    mg_ref[:, 0:D_MODEL] = jax.nn.sigmoid(mm(COL_MG, D_MODEL))
    mg_ref[:, D_MODEL:2 * D_MODEL] = jax.nn.sigmoid(mm(COL_MG + D_MODEL, D_MODEL))
    kcv_ref[...] = mm(COL_KCV, 2 * KV_DIM)
    ks_ref[...] = _spread_heads(_rope(mm(COL_KS, KV_DIM), cos, sin)).astype(BF16)
    vs_ref[...] = _spread_heads(mm(COL_VS, KV_DIM), 1.0).astype(BF16)
    kw_ref[...] = _spread_heads(_rope(mm(COL_KW, KV_DIM), cos, sin)).astype(BF16)
    vw_ref[...] = _spread_heads(mm(COL_VW, KV_DIM), 1.0).astype(BF16)
    ng_ref[...] = jax.nn.sigmoid(mm(COL_NG, N_KV_GROUPS * LANES))


def _inproj(x2, norm_g, w_proj, cos, sin, seq):
    t = x2.shape[0]
    tm = TM_PROJ
    pos_blocks = seq // tm
    row = lambda i: (i, 0)
    const = lambda i: (0, 0)
    out_shape = (
        jax.ShapeDtypeStruct((t, 2 * D_RNN), F32),
        jax.ShapeDtypeStruct((t, Q_DIM), BF16),
        jax.ShapeDtypeStruct((t, 2 * D_MODEL), F32),
        jax.ShapeDtypeStruct((t, 2 * KV_DIM), F32),
        jax.ShapeDtypeStruct((t, N_KV_GROUPS * LANES), BF16),
        jax.ShapeDtypeStruct((t, N_KV_GROUPS * LANES), BF16),
        jax.ShapeDtypeStruct((t, N_KV_GROUPS * LANES), BF16),
        jax.ShapeDtypeStruct((t, N_KV_GROUPS * LANES), BF16),
        jax.ShapeDtypeStruct((t, N_KV_GROUPS * LANES), F32),
    )
    return pl.pallas_call(
        _inproj_kernel,
        grid=(t // tm,),
        in_specs=[
            pl.BlockSpec((tm, D_MODEL), row),
            pl.BlockSpec((1, D_MODEL), const),
            pl.BlockSpec((D_MODEL, N_PROJ), const, pipeline_mode=pl.Buffered(1)),
            pl.BlockSpec((tm, LANES), lambda i: (i % pos_blocks, 0)),
            pl.BlockSpec((tm, LANES), lambda i: (i % pos_blocks, 0)),
        ],
        out_specs=tuple(pl.BlockSpec((tm, s.shape[1]), row) for s in out_shape),
        out_shape=out_shape,
        compiler_params=pltpu.CompilerParams(
            dimension_semantics=("parallel",), vmem_limit_bytes=VMEM_LIMIT),
        name="inproj",
    )(x2, norm_g, w_proj, cos, sin)


def _rnn_kernel(xr_ref, gr_ref, cw_ref, cb_ref, wa_ref, ba_ref, wi_ref, bi_ref, lam_ref, wo_ref,
                y_ref, tail_s, carry_s, a_s, u_s, h_s):
    ts = xr_ref.shape[0]

    @pl.when(pl.program_id(1) == 0)
    def _():
        tail_s[...] = jnp.zeros_like(tail_s)
        carry_s[...] = jnp.zeros_like(carry_s)

    x = xr_ref[...]
    xext = jnp.concatenate([tail_s[...], x], axis=0)
    tail_s[...] = x[ts - SUBLANES:ts, :]
    conv = cb_ref[...]
    for k in range(CONV_WIDTH):
        back = CONV_WIDTH - 1 - k
        shifted = xext if back == 0 else pltpu.roll(xext, back, 0)
        conv = conv + cw_ref[k:k + 1, :] * shifted[SUBLANES:SUBLANES + ts, :]

    cb16 = conv.astype(BF16)
    blk = wa_ref.shape[1]
    r_pre = jnp.concatenate(
        [_dot(cb16[:, j * blk:(j + 1) * blk], wa_ref[j]) for j in range(D_RNN // blk)], axis=1)
    i_pre = jnp.concatenate(
        [_dot(cb16[:, j * blk:(j + 1) * blk], wi_ref[j]) for j in range(D_RNN // blk)], axis=1)
    r = jax.nn.sigmoid(r_pre + ba_ref[...])
    gate_i = jax.nn.sigmoid(i_pre + bi_ref[...])
    neg_lam = -lam_ref[...]
    softplus = jnp.maximum(neg_lam, 0.0) + jnp.log1p(jnp.exp(-jnp.abs(neg_lam)))
    log_a = (-LRU_C) * r * softplus
    a = jnp.exp(log_a)
    a_s[...] = a
    u_s[...] = jnp.sqrt(-jnp.tanh(log_a) * (a * a + 1.0)) * (gate_i * conv)

    row = _row_iota((SUBLANES, D_RNN))

    def body(k, carry):
        off = pl.multiple_of(k * SUBLANES, SUBLANES)
        a = a_s[pl.ds(off, SUBLANES), :]
        b = u_s[pl.ds(off, SUBLANES), :]
        for sh in (1, 2, 4):
            keep = row >= sh
            a_prev = jnp.where(keep, pltpu.roll(a, sh, 0), 1.0)
            b_prev = jnp.where(keep, pltpu.roll(b, sh, 0), 0.0)
            b = a * b_prev + b
            a = a * a_prev
        h = a * carry + b
        h_s[pl.ds(off, SUBLANES), :] = h
        return jnp.broadcast_to(h[SUBLANES - 1:SUBLANES, :], (SUBLANES, D_RNN))

    carry_s[...] = lax.fori_loop(0, ts // SUBLANES, body, carry_s[...])
    gated = (jax.nn.gelu(gr_ref[...]) * h_s[...]).astype(BF16)
    y_ref[...] = _dot(gated, wo_ref[...])


def _rnn(xg, conv_w, conv_b, wa_bd, ba, wi_bd, bi, lam, w_out, batch, seq):
    t = xg.shape[0]
    ts = TS_RNN
    nt = seq // ts
    const2 = lambda b, s: (0, 0)
    const3 = lambda b, s: (0, 0, 0)
    nblk, blk = wa_bd.shape[0], wa_bd.shape[1]
    return pl.pallas_call(
        _rnn_kernel,
        grid=(batch, nt),
        in_specs=[
            pl.BlockSpec((ts, D_RNN), lambda b, s: (b * nt + s, 0)),
            pl.BlockSpec((ts, D_RNN), lambda b, s: (b * nt + s, 1)),
            pl.BlockSpec((CONV_WIDTH, D_RNN), const2),
            pl.BlockSpec((1, D_RNN), const2),
            pl.BlockSpec((nblk, blk, blk), const3),
            pl.BlockSpec((1, D_RNN), const2),
            pl.BlockSpec((nblk, blk, blk), const3),
            pl.BlockSpec((1, D_RNN), const2),
            pl.BlockSpec((1, D_RNN), const2),
            pl.BlockSpec((D_RNN, D_MODEL), const2),
        ],
        out_specs=pl.BlockSpec((ts, D_MODEL), lambda b, s: (b * nt + s, 0)),
        out_shape=jax.ShapeDtypeStruct((t, D_MODEL), F32),
        scratch_shapes=[
            pltpu.VMEM((SUBLANES, D_RNN), F32),
            pltpu.VMEM((SUBLANES, D_RNN), F32),
            pltpu.VMEM((ts, D_RNN), F32),
            pltpu.VMEM((ts, D_RNN), F32),
            pltpu.VMEM((ts, D_RNN), F32),
        ],
        compiler_params=pltpu.CompilerParams(
            dimension_semantics=("parallel", "arbitrary"), vmem_limit_bytes=VMEM_LIMIT),
        name="rnn",
    )(xg, xg, conv_w, conv_b, wa_bd, ba, wi_bd, bi, lam, w_out)


def _compress_kernel(x_ref, pos_ref, w1_ref, b1_ref, w2_ref, b2_ref, cos_ref, sin_ref, o_ref):
    n_rows = x_ref.shape[0] // CMP_STRIDE
    strided = [x_ref[pl.ds(r, n_rows, stride=CMP_STRIDE), :] for r in range(CMP_STRIDE)]
    low = _lane_iota((n_rows, LANES)) < HEAD_DIM
    groups = []
    for g in range(2):
        tiles = []
        for j in range(CMP_STRIDE // 2):
            even = strided[2 * j]
            odd = strided[2 * j + 1]
            if g == 0:
                tiles.append(jnp.where(low, even, pltpu.roll(odd, HEAD_DIM, 1)))
            else:
                tiles.append(jnp.where(low, pltpu.roll(even, HEAD_DIM, 1), odd))
        groups.append(jnp.concatenate(tiles, axis=1))
    x = jnp.concatenate(groups, axis=0)
    tr = x.shape[0]
    half = x.shape[1]
    first = _dot((x + pos_ref[0, 0:1, :]).astype(BF16), w1_ref[0, 0:half, :])
    second = _dot((x + pos_ref[0, 1:2, :]).astype(BF16), w1_ref[0, half:2 * half, :])
    hid = jax.nn.gelu(first + pltpu.roll(second, tr - 1, 0) + b1_ref[0])
    out = _dot(hid.astype(BF16), w2_ref[0]) + b2_ref[0]
    o_ref[0] = _rope(out, cos_ref[0], sin_ref[0]).astype(BF16)


def _compress(kcv, pos, w1, b1, w2, b2, cos, sin, batch, seq):
    pairs = KV_DIM // LANES
    tr = TR_CMP
    rows = batch * pairs * tr
    sel = lambda k, r: (k, 0, 0)
    return pl.pallas_call(
        _compress_kernel,
        grid=(2, batch * pairs),
        in_specs=[
            pl.BlockSpec((seq, LANES), lambda k, r: (r // pairs, k * pairs + r % pairs)),
            pl.BlockSpec((1, 2, pos.shape[2]), sel),
            pl.BlockSpec((1,) + w1.shape[1:], sel),
            pl.BlockSpec((1, 1, CMP_HIDDEN), sel),
            pl.BlockSpec((1,) + w2.shape[1:], sel),
            pl.BlockSpec((1, 1, LANES), sel),
            pl.BlockSpec((1, tr, LANES), sel),
            pl.BlockSpec((1, tr, LANES), sel),
        ],
        out_specs=pl.BlockSpec((1, tr, LANES), lambda k, r: (k, r, 0)),
        out_shape=jax.ShapeDtypeStruct((2, rows, LANES), BF16),
        compiler_params=pltpu.CompilerParams(
            dimension_semantics=("parallel", "parallel"), vmem_limit_bytes=VMEM_LIMIT),
        name="compress",
    )(kcv, pos, w1, b1, w2, b2, cos, sin)


def _lane_tile_max(s):
    tiles = [s[:, c * LANES:(c + 1) * LANES] for c in range(s.shape[1] // LANES)]
    while len(tiles) > 1:
        tiles = [jnp.maximum(tiles[k], tiles[k + 1]) if k + 1 < len(tiles) else tiles[k]
                 for k in range(0, len(tiles), 2)]
    return tiles[0]


def _stack_heads(q):
    tq = q.shape[0]
    low = _lane_iota((tq, LANES)) < HEAD_DIM
    zero = jnp.zeros((tq, LANES), BF16)
    heads = []
    for hh in range(HEADS_PER_GROUP):
        pair = q[:, (hh // 2) * LANES:(hh // 2 + 1) * LANES]
        heads.append(jnp.where(low if hh % 2 == 0 else jnp.logical_not(low), pair, zero))
    return jnp.concatenate(heads, axis=0)


def _select_kernel(q_ref, kc_ref, vc_ref, ng_ref, m_ref, bias_ref, ocmp_ref):
    tq = TQ_SELECT_TILE
    for sub in range(q_ref.shape[0] // tq):
        bias, o_cmp = _select_tile(pl.program_id(2) * (q_ref.shape[0] // tq) + sub,
                                   q_ref[sub * tq:(sub + 1) * tq, :], kc_ref, vc_ref,
                                   ng_ref[sub * tq:(sub + 1) * tq, :], m_ref)
        bias_ref[sub * tq:(sub + 1) * tq, :] = bias
        ocmp_ref[sub * tq:(sub + 1) * tq, :] = o_cmp


def _select_tile(i, q, kc_ref, vc_ref, ng, m_ref):
    tq = q.shape[0]
    rows = HEADS_PER_GROUP * tq
    low = _lane_iota((tq, LANES)) < HEAD_DIM
    q4 = _stack_heads(q)
    lane4 = _lane_iota((rows, LANES))
    qpos4 = i * tq + (_row_iota((rows, LANES)) & (tq - 1))

    sc = _dot_t(q4, kc_ref[0])
    sc = jnp.where(lane4 * CMP_STRIDE + (CMP_BLOCK - 1) <= qpos4, sc, NEG_INF)
    mc = jnp.max(sc, axis=1, keepdims=True)
    pc = jnp.exp(sc - mc)
    pc = pc / jnp.sum(pc, axis=1, keepdims=True)
    pc = jnp.where(qpos4 >= CMP_BLOCK - 1, pc, 0.0)
    o_cmp = _dot(pc.astype(BF16), vc_ref[0])

    psum = pc[0:tq] + pc[tq:2 * tq] + pc[2 * tq:3 * tq] + pc[3 * tq:4 * tq]
    p_hi = psum.astype(BF16)
    rem = psum - p_hi.astype(F32)
    p_mid = rem.astype(BF16)
    p_lo = (rem - p_mid.astype(F32)).astype(BF16)
    cs = m_ref[...]
    p_slc = _dot(p_hi, cs) + _dot(p_mid, cs) + _dot(p_lo, cs)

    n_sel = SEL_BLOCK // 2
    p_slc_t = p_slc.T[0:n_sel, :]
    blk = _row_iota((n_sel, tq))
    tblk = (i * tq + _lane_iota((n_sel, tq))) >> 6
    forced = (blk == 0) | (blk == tblk) | (blk == tblk - 1)
    score = jnp.where(blk <= tblk, p_slc_t + jnp.where(forced, FORCE_BONUS, 0.0), -1.0)
    rank = jnp.zeros((n_sel, tq), F32)
    for j in range(n_sel):
        sj = score[j:j + 1, :]
        beats = (sj > score) | ((sj == score) & (blk > j))
        rank = rank + jnp.where(beats, 1.0, 0.0)
    bias_t = jnp.where(rank < SEL_TOPN, 0.0, NEG_INF)
    bias = jnp.concatenate([bias_t, jnp.zeros((LANES - n_sel, tq), F32)], axis=0).T.astype(BF16)

    gated = [ng[:, hh:hh + 1] * o_cmp[hh * tq:(hh + 1) * tq] for hh in range(HEADS_PER_GROUP)]
    o_pairs = jnp.concatenate([jnp.where(low, gated[0], gated[1]), jnp.where(low, gated[2], gated[3])], axis=1)
    return bias, o_pairs


def _attn_kernel(*refs):
    n_full = pl.program_id(2) // (CK_SEL // TQ)
    for n_chunks in range(1, refs[3].shape[0] // CK_SEL + 1):
        pl.when(n_full == n_chunks - 1)(functools.partial(_attn_step, n_chunks, *refs))


def _attn_step(n_chunks, q_ref, bias_ref, ocmp_ref, ks_ref, vs_ref, kw_ref, vw_ref, ng_ref, e_ref, o_ref):
    i = pl.program_id(2)
    tq = q_ref.shape[0]
    rows = HEADS_PER_GROUP * tq
    ck = CK_SEL
    low = _lane_iota((tq, LANES)) < HEAD_DIM
    q4 = _stack_heads(q_ref[...])

    n_sub = tq // TK
    back = WINDOW // TK
    k_tiles, v_tiles = [], []
    for jj in range(back + n_sub):
        j = i * n_sub - back + jj
        off = pl.multiple_of(jnp.maximum(j, 0) * TK, TK)
        k_tiles.append(kw_ref[pl.ds(off, TK), :])
        v_tiles.append(vw_ref[pl.ds(off, TK), :])
    sw = _dot_t(q4, jnp.concatenate(k_tiles, axis=0))
    on_or_after = _row_iota((TK, TK)) >= _lane_iota((TK, TK))
    sw_tiles = []
    for jj in range(back + n_sub):
        blocks = []
        for blk_i in range(rows // TK):
            piece = sw[blk_i * TK:(blk_i + 1) * TK, jj * TK:(jj + 1) * TK]
            tiles_back = blk_i % n_sub + back - jj
            if tiles_back == 0:
                piece = jnp.where(on_or_after, piece, NEG_INF)
            elif tiles_back == back:
                piece = jnp.where(on_or_after, NEG_INF, piece)
            elif tiles_back < 0 or tiles_back > back:
                piece = jnp.full((TK, TK), NEG_INF, F32)
            blocks.append(piece)
        tile = jnp.concatenate(blocks, axis=0)
        if jj < back:
            tile = jnp.where(i * n_sub - back + jj >= 0, tile, NEG_INF)
        sw_tiles.append(tile)
    sw = jnp.concatenate(sw_tiles, axis=1)
    m_win = jnp.max(_lane_tile_max(sw), axis=1, keepdims=True)
    acc_win = _dot(jnp.exp(sw - m_win).astype(BF16), jnp.concatenate(v_tiles, axis=0))

    qa = jnp.concatenate([q4, jnp.concatenate([bias_ref[...]] * HEADS_PER_GROUP, axis=0)], axis=1)
    rel_q = i * tq - (n_chunks - 1) * ck + (_row_iota((rows, ck)) & (tq - 1))
    m_run = None
    acc_sel = None
    for c in range(n_chunks):
        ka = jnp.concatenate([ks_ref[c * ck:(c + 1) * ck, :], e_ref[c * ck:(c + 1) * ck, :]], axis=1)
        s = _dot_t(qa, ka)
        if c == n_chunks - 1:
            s = jnp.where(_lane_iota((rows, ck)) <= rel_q, s, NEG_INF)
        m_new = jnp.max(_lane_tile_max(s), axis=1, keepdims=True)
        if c > 0:
            m_new = jnp.maximum(m_run, m_new)
            acc_sel = acc_sel * jnp.exp(m_run - m_new)
        part = _dot(jnp.exp(s - m_new).astype(BF16), vs_ref[c * ck:(c + 1) * ck, :])
        acc_sel = part if c == 0 else acc_sel + part
        m_run = m_new

    ng = ng_ref[...]
    outs = []
    for hh in range(HEADS_PER_GROUP):
        r0 = hh * tq
        col = lambda br: ng[:, br * HEADS_PER_GROUP + hh:br * HEADS_PER_GROUP + hh + 1]
        parts = []
        for acc in (acc_sel[r0:r0 + tq], acc_win[r0:r0 + tq]):
            swapped = pltpu.roll(acc, HEAD_DIM, 1)
            parts.append(acc / swapped if hh % 2 == 0 else swapped / acc)
        outs.append(col(1) * parts[0] + col(2) * parts[1])
    o_ref[...] = (ocmp_ref[...] + jnp.concatenate(
        [jnp.where(low, outs[0], outs[1]), jnp.where(low, outs[2], outs[3])], axis=1)).astype(o_ref.dtype)


def _select(q, kvc, ng, cs_mat, batch, seq):
    t = q.shape[0]
    tq = TQ_SELECT
    nq = seq // tq
    qrow = lambda b, g, i: (b * nq + i, g)
    return pl.pallas_call(
        _select_kernel,
        grid=(batch, N_KV_GROUPS, nq),
        in_specs=[
            pl.BlockSpec((tq, HEADS_PER_GROUP * HEAD_DIM), qrow),
            pl.BlockSpec((1, N_CMP_PAD, LANES), lambda b, g, i: (0, b * N_KV_GROUPS + g, 0)),
            pl.BlockSpec((1, N_CMP_PAD, LANES), lambda b, g, i: (1, b * N_KV_GROUPS + g, 0)),
            pl.BlockSpec((tq, LANES), qrow),
            pl.BlockSpec((N_CMP_PAD, LANES), lambda b, g, i: (0, 0)),
        ],
        out_specs=(
            pl.BlockSpec((tq, LANES), qrow),
            pl.BlockSpec((tq, HEADS_PER_GROUP * HEAD_DIM), qrow),
        ),
        out_shape=(
            jax.ShapeDtypeStruct((t, N_KV_GROUPS * LANES), BF16),
            jax.ShapeDtypeStruct((t, Q_DIM), F32),
        ),
        compiler_params=pltpu.CompilerParams(
            dimension_semantics=("parallel", "parallel", "parallel"), vmem_limit_bytes=VMEM_LIMIT),
        name="select",
    )(q, kvc, kvc, ng, cs_mat)


def _attn(q, bias, o_cmp, ksd, vsd, kwd, vwd, ng, e_mat, batch, seq):
    t = q.shape[0]
    tq = TQ
    nq = seq // tq
    qrow = lambda b, g, i: (b * nq + i, g)
    kv = lambda b, g, i: (b, g)
    return pl.pallas_call(
        _attn_kernel,
        grid=(batch, N_KV_GROUPS, nq),
        in_specs=[
            pl.BlockSpec((tq, HEADS_PER_GROUP * HEAD_DIM), qrow),
            pl.BlockSpec((tq, LANES), qrow),
            pl.BlockSpec((tq, HEADS_PER_GROUP * HEAD_DIM), qrow),
            pl.BlockSpec((seq, LANES), kv),
            pl.BlockSpec((seq, LANES), kv),
            pl.BlockSpec((seq, LANES), kv),
            pl.BlockSpec((seq, LANES), kv),
            pl.BlockSpec((tq, LANES), qrow),
            pl.BlockSpec((seq, LANES), lambda b, g, i: (0, 0)),
        ],
        out_specs=pl.BlockSpec((tq, HEADS_PER_GROUP * HEAD_DIM), qrow),
        out_shape=jax.ShapeDtypeStruct((t, Q_DIM), BF16),
        compiler_params=pltpu.CompilerParams(
            dimension_semantics=("parallel", "parallel", "arbitrary"), vmem_limit_bytes=VMEM_LIMIT),
        name="attn",
    )(q, bias, o_cmp, ksd, vsd, kwd, vwd, ng, e_mat)


def _post_kernel(o_ref, ya_ref, mg_ref, x_ref, wn_ref, wm_ref, g2_ref, wr_hi_ref, wr_lo_ref, rb_ref,
                 x1_ref, h2_ref, comb_ref):
    y_b = _dot(o_ref[...], wn_ref[...])
    mixed = mg_ref[:, 0:D_MODEL] * ya_ref[...] + mg_ref[:, D_MODEL:2 * D_MODEL] * y_b
    x1 = x_ref[...] + _dot(mixed.astype(BF16), wm_ref[...])
    x1_ref[...] = x1
    h2 = x1 * lax.rsqrt(jnp.mean(x1 * x1, axis=-1, keepdims=True) + EPS) * g2_ref[...]
    h_hi = h2.astype(BF16)
    h2_ref[...] = h_hi
    h_lo = (h2 - h_hi.astype(F32)).astype(BF16)
    logits = (_dot(h_hi, wr_hi_ref[...]) + _dot(h_lo, wr_hi_ref[...]) + _dot(h_hi, wr_lo_ref[...])
              + rb_ref[...])

    lane = _lane_iota(logits.shape).astype(F32)
    is_grp = (lane >= N_EXPERTS) & (lane < N_EXPERTS + N_GROUPS)
    gl = jnp.where(is_grp, logits, NEG_INF)
    ge = jnp.exp(gl - jnp.max(gl, axis=1, keepdims=True))
    gp = ge / jnp.sum(ge, axis=1, keepdims=True)
    g_w = jnp.max(gp, axis=1, keepdims=True)
    big = float(4 * LANES)
    g_first = jnp.min(jnp.where(is_grp & (gp == g_w), lane, big), axis=1, keepdims=True)
    grp_lo = (g_first - N_EXPERTS) * EXPERTS_PER_GROUP

    in_grp = (lane >= grp_lo) & (lane < grp_lo + EXPERTS_PER_GROUP)
    el = jnp.where(in_grp, logits, NEG_INF)
    ee = jnp.exp(el - jnp.max(el, axis=1, keepdims=True))
    ep = ee / jnp.sum(ee, axis=1, keepdims=True)
    w1 = jnp.max(ep, axis=1, keepdims=True)
    i1 = jnp.min(jnp.where(in_grp & (ep == w1), lane, big), axis=1, keepdims=True)
    rest = jnp.where(in_grp & (lane != i1), ep, -1.0)
    w2 = jnp.max(rest, axis=1, keepdims=True)
    i2 = jnp.min(jnp.where(rest == w2, lane, big), axis=1, keepdims=True)
    den = w1 + w2
    comb_ref[...] = jnp.where(lane == i1, g_w * (w1 / den), jnp.where(lane == i2, g_w * (w2 / den), 0.0))


def _post(o_nsa, y_a, mg, x2, w_nsa, w_mix, g2, wr_hi, wr_lo, rb):
    t = x2.shape[0]
    tm = TM_POST
    row = lambda i: (i, 0)
    const = lambda i: (0, 0)
    return pl.pallas_call(
        _post_kernel,
        grid=(t // tm,),
        in_specs=[
            pl.BlockSpec((tm, Q_DIM), row),
            pl.BlockSpec((tm, D_MODEL), row),
            pl.BlockSpec((tm, 2 * D_MODEL), row),
            pl.BlockSpec((tm, D_MODEL), row),
            pl.BlockSpec((Q_DIM, D_MODEL), const),
            pl.BlockSpec((D_MODEL, D_MODEL), const),
            pl.BlockSpec((1, D_MODEL), const),
            pl.BlockSpec((D_MODEL, LANES), const),
            pl.BlockSpec((D_MODEL, LANES), const),
            pl.BlockSpec((1, LANES), const),
        ],
        out_specs=(
            pl.BlockSpec((tm, D_MODEL), row),
            pl.BlockSpec((tm, D_MODEL), row),
            pl.BlockSpec((tm, LANES), row),
        ),
        out_shape=(
            jax.ShapeDtypeStruct((t, D_MODEL), F32),
            jax.ShapeDtypeStruct((t, D_MODEL), BF16),
            jax.ShapeDtypeStruct((t, LANES), F32),
        ),
        compiler_params=pltpu.CompilerParams(
            dimension_semantics=("parallel",), vmem_limit_bytes=VMEM_LIMIT),
        name="post",
    )(o_nsa, y_a, mg, x2, w_nsa, w_mix, g2, wr_hi, wr_lo, rb)


def _moe_kernel(h_ref, comb_ref, x1_ref, wg_ref, wu_ref, wd_ref, gf_ref, o_ref, acc_s):
    e = pl.program_id(1)

    @pl.when(e == 0)
    def _():
        acc_s[...] = jnp.zeros_like(acc_s)

    h = h_ref[...]
    act = jax.nn.silu(_dot(h, wg_ref[0].astype(BF16))) * _dot(h, wu_ref[0].astype(BF16))
    y = _dot(act.astype(BF16), wd_ref[0].astype(BF16))
    comb = comb_ref[...]
    weight = jnp.sum(jnp.where(_lane_iota(comb.shape) == e, comb, 0.0), axis=1, keepdims=True)
    acc_s[...] += weight * y

    @pl.when(e == N_EXPERTS - 1)
    def _():
        x = x1_ref[...] + acc_s[...]
        o_ref[...] = x * lax.rsqrt(jnp.mean(x * x, axis=-1, keepdims=True) + EPS) * gf_ref[...]


def _moe(h2, comb, x1, wg, wu, wd, gf):
    t = h2.shape[0]
    tm = TM_MOE
    row = lambda i, e: (i, 0)
    return pl.pallas_call(
        _moe_kernel,
        grid=(t // tm, N_EXPERTS),
        in_specs=[
            pl.BlockSpec((tm, D_MODEL), row),
            pl.BlockSpec((tm, LANES), row),
            pl.BlockSpec((tm, D_MODEL), row),
            pl.BlockSpec((1, D_MODEL, D_EXPERT), lambda i, e: (e, 0, 0)),
            pl.BlockSpec((1, D_MODEL, D_EXPERT), lambda i, e: (e, 0, 0)),
            pl.BlockSpec((1, D_EXPERT, D_MODEL), lambda i, e: (e, 0, 0)),
            pl.BlockSpec((1, D_MODEL), lambda i, e: (0, 0)),
        ],
        out_specs=pl.BlockSpec((tm, D_MODEL), row),
        out_shape=jax.ShapeDtypeStruct((t, D_MODEL), F32),
        scratch_shapes=[pltpu.VMEM((tm, D_MODEL), F32)],
        compiler_params=pltpu.CompilerParams(
            dimension_semantics=("parallel", "arbitrary"), vmem_limit_bytes=VMEM_LIMIT),
        name="moe",
    )(h2, comb, x1, wg, wu, wd, gf)


def _split_w_in(w):
    sizes = [D_RNN, D_RNN, Q_DIM] + [KV_DIM] * 6 + [3 * N_HEADS, 2 * D_MODEL]
    pts = np.cumsum(sizes)[:-1]
    return jnp.split(w, [int(p) for p in pts], axis=-1)


def _proj_weight(w_in):
    xr, gr, q, kc, vc, ksl, vsl, kw, vw, nsa_g, merge_g = _split_w_in(w_in)
    ng = nsa_g.reshape(D_MODEL, 3, N_KV_GROUPS, HEADS_PER_GROUP).transpose(0, 2, 1, 3)
    ng = ng.reshape(D_MODEL, N_KV_GROUPS, 3 * HEADS_PER_GROUP)
    ng = jnp.pad(ng, ((0, 0), (0, 0), (0, LANES - 3 * HEADS_PER_GROUP))).reshape(D_MODEL, N_KV_GROUPS * LANES)
    return jnp.concatenate([xr, gr, q, merge_g, kc, vc, ksl, vsl, kw, vw, ng], axis=1).astype(BF16)


def _rope_tables(pos, width):
    inv_freq = ROPE_THETA ** (-(jnp.arange(0, HEAD_DIM, 2, dtype=F32) / HEAD_DIM))
    ang = pos.astype(F32)[:, None] * inv_freq[None, :]
    cos, sin = jnp.cos(ang), jnp.sin(ang)
    reps = width // HEAD_DIM
    return jnp.tile(jnp.concatenate([cos, cos], axis=1), (1, reps)), jnp.tile(jnp.concatenate([-sin, sin], axis=1), (1, reps))


def _block_diag(w, per):
    nb, d = w.shape[0], w.shape[1]
    w = w.reshape(nb // per, per, d, d)
    eye = jnp.eye(per, dtype=w.dtype)
    return jnp.einsum('npij,pq->npiqj', w, eye).reshape(nb // per, per * d, per * d)


def _cmp_to_sel():
    n_sel_pad = LANES
    c0 = np.arange(N_CMP_PAD) * CMP_STRIDE
    s0 = np.arange(n_sel_pad) * SEL_BLOCK
    ov = np.minimum(c0[:, None] + CMP_BLOCK, s0[None, :] + SEL_BLOCK) - np.maximum(c0[:, None], s0[None, :])
    m = np.clip(ov, 0, None) / CMP_BLOCK
    m[:, SEL_BLOCK // 2:] = 0.0
    m[N_CMP_PAD - 1, :] = 0.0
    return m.astype(np.float32)


def kernel(x, norm1_g, w_in, conv_w, conv_b, lru_wa, lru_ba, lru_wi, lru_bi, lru_lambda, w_rnn_out, cmpk_pos, cmpk_w1, cmpk_b1, cmpk_w2, cmpk_b2, cmpv_pos, cmpv_w1, cmpv_b1, cmpv_w2, cmpv_b2, w_nsa_out, w_mix_out, norm2_g, router_group_w, router_group_b, router_expert_w, router_expert_b, expert_w_gate, expert_w_up, expert_w_down, final_norm_g):
    batch, seq, _ = x.shape
    t = batch * seq
    assert w_in.shape[0] == 1, "the final norm is fused into the expert kernel: single layer only"
    x2 = x.reshape(t, D_MODEL)

    cos, sin = _rope_tables(jnp.arange(seq), LANES)
    cmp_ends = jnp.arange(N_CMP_PAD) * CMP_STRIDE + (CMP_BLOCK - 1)
    ccos, csin = _rope_tables(cmp_ends, LANES)
    reps = TR_CMP // N_CMP_PAD
    cmp_cos = jnp.stack([jnp.tile(ccos, (reps, 1)), jnp.ones((TR_CMP, LANES), F32)])
    cmp_sin = jnp.stack([jnp.tile(csin, (reps, 1)), jnp.zeros((TR_CMP, LANES), F32)])
    key_blk = np.arange(seq)[:, None] // SEL_BLOCK
    e_mat = jnp.asarray((key_blk == np.arange(LANES)[None, :]).astype(np.float32), dtype=BF16)
    cs_mat = jnp.asarray(_cmp_to_sel(), dtype=BF16)

    l = 0
    xg, q, mg, kcv, ksd, vsd, kwd, vwd, ng = _inproj(
        x2, norm1_g[l][None, :], _proj_weight(w_in[l]), cos, sin, seq)

    y_a = _rnn(xg, conv_w[l], conv_b[l][None, :],
               _block_diag(lru_wa[l], 4).astype(BF16), lru_ba[l][None, :],
               _block_diag(lru_wi[l], 4).astype(BF16), lru_bi[l][None, :],
               lru_lambda[l][None, :], w_rnn_out[l].astype(BF16), batch, seq)

    half = CMP_STRIDE * HEAD_DIM
    pos = jnp.stack([cmpk_pos[l].reshape(2, half), cmpv_pos[l].reshape(2, half)])
    w1 = jnp.stack([cmpk_w1[l], cmpv_w1[l]]).astype(BF16)
    b1 = jnp.stack([cmpk_b1[l], cmpv_b1[l]])[:, None, :]
    w2 = jnp.stack([cmpk_w2[l], cmpv_w2[l]])
    w2 = jnp.concatenate([w2, w2], axis=2).astype(BF16)
    b2 = jnp.stack([cmpk_b2[l], cmpv_b2[l]])
    b2 = jnp.concatenate([b2, b2], axis=1)[:, None, :]
    kvc = _compress(kcv, pos, w1, b1, w2, b2, cmp_cos, cmp_sin, batch, seq)

    bias, o_cmp = _select(q, kvc, ng, cs_mat, batch, seq)
    o_nsa = _attn(q, bias, o_cmp, ksd, vsd, kwd, vwd, ng, e_mat, batch, seq)

    wr = jnp.concatenate([
        router_expert_w[l].transpose(1, 0, 2).reshape(D_MODEL, N_EXPERTS),
        router_group_w[l],
        jnp.zeros((D_MODEL, LANES - N_EXPERTS - N_GROUPS), F32)], axis=1)
    wr_hi = wr.astype(BF16)
    wr_lo = (wr - wr_hi.astype(F32)).astype(BF16)
    rb = jnp.concatenate([router_expert_b[l].reshape(N_EXPERTS), router_group_b[l],
                          jnp.zeros((LANES - N_EXPERTS - N_GROUPS,), F32)])[None, :]
    x1, h2, comb = _post(o_nsa, y_a, mg, x2, w_nsa_out[l].astype(BF16), w_mix_out[l].astype(BF16),
                         norm2_g[l][None, :], wr_hi, wr_lo, rb)

    out = _moe(h2, comb, x1, expert_w_gate[l], expert_w_up[l], expert_w_down[l], final_norm_g[None, :])
    return out.reshape(batch, seq, D_MODEL)
```

```python
import functools

import numpy as np
import jax
import jax.numpy as jnp
from jax import lax
from jax.experimental import pallas as pl
from jax.experimental.pallas import tpu as pltpu

D_MODEL = 1024
D_RNN = 1024
RNN_BLOCKS = 16
RNN_BLOCK_DIM = D_RNN // RNN_BLOCKS
CONV_WIDTH = 4
LRU_C = 8.0
N_HEADS = 16
HEAD_DIM = 64
HALF_DIM = HEAD_DIM // 2
N_KV_GROUPS = 4
HEADS_PER_GROUP = N_HEADS // N_KV_GROUPS
Q_DIM = N_HEADS * HEAD_DIM
KV_DIM = N_KV_GROUPS * HEAD_DIM
CMP_BLOCK = 32
CMP_STRIDE = 16
CMP_HIDDEN = 256
SEL_BLOCK = 64
SEL_TOPN = 16
WINDOW = 512
ROPE_THETA = 10000.0
FORCE_BONUS = 1e4
NEG_INF = -1e30
N_GROUPS = 4
EXPERTS_PER_GROUP = 4
N_EXPERTS = N_GROUPS * EXPERTS_PER_GROUP
D_EXPERT = 512
EPS = 1e-6

LANES = 128
SUBLANES = 8
VMEM_LIMIT = 56 * 1024 * 1024

BF16 = jnp.bfloat16
F32 = jnp.float32

COL_XR = 0
COL_GR = COL_XR + D_RNN
COL_Q = COL_GR + D_RNN
COL_MG = COL_Q + Q_DIM
COL_KCV = COL_MG + 2 * D_MODEL
COL_KS = COL_KCV + 2 * KV_DIM
COL_VS = COL_KS + KV_DIM
COL_KW = COL_VS + KV_DIM
COL_VW = COL_KW + KV_DIM
COL_NG = COL_VW + KV_DIM
N_PROJ = COL_NG + N_KV_GROUPS * LANES

TM_PROJ = 256
TS_RNN = 512
TR_CMP = 256
TQ = 256
TQ_SELECT = 512
TQ_SELECT_TILE = 128
TK = 128
CK_SEL = 512
TM_POST = 512
TM_PLAN = 1024
TM_ROWS = 512
TM_EXP = 1024
ROW_COPY_UNROLL = 8
D_AUG = D_MODEL + LANES
GROUP_LANE = N_EXPERTS
N_CMP_PAD = 128


def _dot(a, b):
    return jnp.dot(a, b, preferred_element_type=F32)


def _dot_t(a, b):
    return lax.dot_general(a, b, (((1,), (1,)), ((), ())), preferred_element_type=F32)


def _lane_iota(shape):
    return lax.broadcasted_iota(jnp.int32, shape, len(shape) - 1)


def _row_iota(shape):
    return lax.broadcasted_iota(jnp.int32, shape, 0)


def _rope(x, cos, sin_signed):
    width = x.shape[-1]
    reps = width // cos.shape[-1]
    if reps > 1:
        cos = jnp.concatenate([cos] * reps, axis=1)
        sin_signed = jnp.concatenate([sin_signed] * reps, axis=1)
    first_half = (_lane_iota(x.shape) & (HEAD_DIM - 1)) < HALF_DIM
    partner = jnp.where(first_half, pltpu.roll(x, width - HALF_DIM, 1), pltpu.roll(x, HALF_DIM, 1))
    return x * cos + partner * sin_signed


def _spread_heads(x, fill=None):
    out = []
    low = _lane_iota((x.shape[0], LANES)) < HEAD_DIM
    for c in range(x.shape[1] // LANES):
        xc = x[:, c * LANES:(c + 1) * LANES]
        rolled = pltpu.roll(xc, HEAD_DIM, 1)
        out.append(jnp.where(low, xc, rolled if fill is None else fill))
        out.append(jnp.where(low, rolled, xc if fill is None else fill))
    return jnp.concatenate(out, axis=1)


def _inproj_kernel(x_ref, g_ref, w_ref, cos_ref, sin_ref,
                   xg_ref, q_ref, mg_ref, kcv_ref, ks_ref, vs_ref, kw_ref, vw_ref, ng_ref):
    x = x_ref[...]
    y = x * lax.rsqrt(jnp.mean(x * x, axis=-1, keepdims=True) + EPS)
    h = (y * g_ref[...]).astype(BF16)
    cos = cos_ref[...]
    sin = sin_ref[...]

    def mm(lo, width):
        return _dot(h, w_ref[:, lo:lo + width])

    xg_ref[:, 0:D_RNN] = mm(COL_XR, D_RNN)
    xg_ref[:, D_RNN:2 * D_RNN] = mm(COL_GR, D_RNN)
    q_ref[...] = (_rope(mm(COL_Q, Q_DIM), cos, sin) * (HEAD_DIM ** -0.5)).astype(BF16)
    mg_ref[:, 0:D_MODEL] = jax.nn.sigmoid(mm(COL_MG, D_MODEL))
    mg_ref[:, D_MODEL:2 * D_MODEL] = jax.nn.sigmoid(mm(COL_MG + D_MODEL, D_MODEL))
    kcv_ref[...] = mm(COL_KCV, 2 * KV_DIM)
    ks_ref[...] = _spread_heads(_rope(mm(COL_KS, KV_DIM), cos, sin)).astype(BF16)
    vs_ref[...] = _spread_heads(mm(COL_VS, KV_DIM), 1.0).astype(BF16)
    kw_ref[...] = _spread_heads(_rope(mm(COL_KW, KV_DIM), cos, sin)).astype(BF16)
    vw_ref[...] = _spread_heads(mm(COL_VW, KV_DIM), 1.0).astype(BF16)
    ng_ref[...] = jax.nn.sigmoid(mm(COL_NG, N_KV_GROUPS * LANES))


def _inproj(x2, norm_g, w_proj, cos, sin, seq):
    t = x2.shape[0]
    tm = TM_PROJ
    pos_blocks = seq // tm
    row = lambda i: (i, 0)
    const = lambda i: (0, 0)
    out_shape = (
        jax.ShapeDtypeStruct((t, 2 * D_RNN), F32),
        jax.ShapeDtypeStruct((t, Q_DIM), BF16),
        jax.ShapeDtypeStruct((t, 2 * D_MODEL), F32),
        jax.ShapeDtypeStruct((t, 2 * KV_DIM), F32),
        jax.ShapeDtypeStruct((t, N_KV_GROUPS * LANES), BF16),
        jax.ShapeDtypeStruct((t, N_KV_GROUPS * LANES), BF16),
        jax.ShapeDtypeStruct((t, N_KV_GROUPS * LANES), BF16),
        jax.ShapeDtypeStruct((t, N_KV_GROUPS * LANES), BF16),
        jax.ShapeDtypeStruct((t, N_KV_GROUPS * LANES), F32),
    )
    return pl.pallas_call(
        _inproj_kernel,
        grid=(t // tm,),
        in_specs=[
            pl.BlockSpec((tm, D_MODEL), row),
            pl.BlockSpec((1, D_MODEL), const),
            pl.BlockSpec((D_MODEL, N_PROJ), const, pipeline_mode=pl.Buffered(1)),
            pl.BlockSpec((tm, LANES), lambda i: (i % pos_blocks, 0)),
            pl.BlockSpec((tm, LANES), lambda i: (i % pos_blocks, 0)),
        ],
        out_specs=tuple(pl.BlockSpec((tm, s.shape[1]), row) for s in out_shape),
        out_shape=out_shape,
        compiler_params=pltpu.CompilerParams(
            dimension_semantics=("parallel",), vmem_limit_bytes=VMEM_LIMIT),
        name="inproj",
    )(x2, norm_g, w_proj, cos, sin)


def _rnn_kernel(xr_ref, gr_ref, cw_ref, cb_ref, wa_ref, ba_ref, wi_ref, bi_ref, lam_ref, wo_ref,
                y_ref, tail_s, carry_s, a_s, u_s, h_s):
    ts = xr_ref.shape[0]

    @pl.when(pl.program_id(1) == 0)
    def _():
        tail_s[...] = jnp.zeros_like(tail_s)
        carry_s[...] = jnp.zeros_like(carry_s)

    x = xr_ref[...]
    xext = jnp.concatenate([tail_s[...], x], axis=0)
    tail_s[...] = x[ts - SUBLANES:ts, :]
    conv = cb_ref[...]
    for k in range(CONV_WIDTH):
        back = CONV_WIDTH - 1 - k
        shifted = xext if back == 0 else pltpu.roll(xext, back, 0)
        conv = conv + cw_ref[k:k + 1, :] * shifted[SUBLANES:SUBLANES + ts, :]

    cb16 = conv.astype(BF16)
    blk = wa_ref.shape[1]
    r_pre = jnp.concatenate(
        [_dot(cb16[:, j * blk:(j + 1) * blk], wa_ref[j]) for j in range(D_RNN // blk)], axis=1)
    i_pre = jnp.concatenate(
        [_dot(cb16[:, j * blk:(j + 1) * blk], wi_ref[j]) for j in range(D_RNN // blk)], axis=1)
    r = jax.nn.sigmoid(r_pre + ba_ref[...])
    gate_i = jax.nn.sigmoid(i_pre + bi_ref[...])
    neg_lam = -lam_ref[...]
    softplus = jnp.maximum(neg_lam, 0.0) + jnp.log1p(jnp.exp(-jnp.abs(neg_lam)))
    log_a = (-LRU_C) * r * softplus
    a = jnp.exp(log_a)
    a_s[...] = a
    u_s[...] = jnp.sqrt(-jnp.tanh(log_a) * (a * a + 1.0)) * (gate_i * conv)

    row = _row_iota((SUBLANES, D_RNN))

    def body(k, carry):
        off = pl.multiple_of(k * SUBLANES, SUBLANES)
        a = a_s[pl.ds(off, SUBLANES), :]
        b = u_s[pl.ds(off, SUBLANES), :]
        for sh in (1, 2, 4):
            keep = row >= sh
            a_prev = jnp.where(keep, pltpu.roll(a, sh, 0), 1.0)
            b_prev = jnp.where(keep, pltpu.roll(b, sh, 0), 0.0)
            b = a * b_prev + b
            a = a * a_prev
        h = a * carry + b
        h_s[pl.ds(off, SUBLANES), :] = h
        return jnp.broadcast_to(h[SUBLANES - 1:SUBLANES, :], (SUBLANES, D_RNN))

    carry_s[...] = lax.fori_loop(0, ts // SUBLANES, body, carry_s[...])
    gated = (jax.nn.gelu(gr_ref[...]) * h_s[...]).astype(BF16)
    y_ref[...] = _dot(gated, wo_ref[...])


def _rnn(xg, conv_w, conv_b, wa_bd, ba, wi_bd, bi, lam, w_out, batch, seq):
    t = xg.shape[0]
    ts = TS_RNN
    nt = seq // ts
    const2 = lambda b, s: (0, 0)
    const3 = lambda b, s: (0, 0, 0)
    nblk, blk = wa_bd.shape[0], wa_bd.shape[1]
    return pl.pallas_call(
        _rnn_kernel,
        grid=(batch, nt),
        in_specs=[
            pl.BlockSpec((ts, D_RNN), lambda b, s: (b * nt + s, 0)),
            pl.BlockSpec((ts, D_RNN), lambda b, s: (b * nt + s, 1)),
            pl.BlockSpec((CONV_WIDTH, D_RNN), const2),
            pl.BlockSpec((1, D_RNN), const2),
            pl.BlockSpec((nblk, blk, blk), const3),
            pl.BlockSpec((1, D_RNN), const2),
            pl.BlockSpec((nblk, blk, blk), const3),
            pl.BlockSpec((1, D_RNN), const2),
            pl.BlockSpec((1, D_RNN), const2),
            pl.BlockSpec((D_RNN, D_MODEL), const2),
        ],
        out_specs=pl.BlockSpec((ts, D_MODEL), lambda b, s: (b * nt + s, 0)),
        out_shape=jax.ShapeDtypeStruct((t, D_MODEL), F32),
        scratch_shapes=[
            pltpu.VMEM((SUBLANES, D_RNN), F32),
            pltpu.VMEM((SUBLANES, D_RNN), F32),
            pltpu.VMEM((ts, D_RNN), F32),
            pltpu.VMEM((ts, D_RNN), F32),
            pltpu.VMEM((ts, D_RNN), F32),
        ],
        compiler_params=pltpu.CompilerParams(
            dimension_semantics=("parallel", "arbitrary"), vmem_limit_bytes=VMEM_LIMIT),
        name="rnn",
    )(xg, xg, conv_w, conv_b, wa_bd, ba, wi_bd, bi, lam, w_out)


def _compress_kernel(x_ref, pos_ref, w1_ref, b1_ref, w2_ref, b2_ref, cos_ref, sin_ref, o_ref):
    n_rows = x_ref.shape[0] // CMP_STRIDE
    strided = [x_ref[pl.ds(r, n_rows, stride=CMP_STRIDE), :] for r in range(CMP_STRIDE)]
    low = _lane_iota((n_rows, LANES)) < HEAD_DIM
    groups = []
    for g in range(2):
        tiles = []
        for j in range(CMP_STRIDE // 2):
            even = strided[2 * j]
            odd = strided[2 * j + 1]
            if g == 0:
                tiles.append(jnp.where(low, even, pltpu.roll(odd, HEAD_DIM, 1)))
            else:
                tiles.append(jnp.where(low, pltpu.roll(even, HEAD_DIM, 1), odd))
        groups.append(jnp.concatenate(tiles, axis=1))
    x = jnp.concatenate(groups, axis=0)
    tr = x.shape[0]
    half = x.shape[1]
    first = _dot((x + pos_ref[0, 0:1, :]).astype(BF16), w1_ref[0, 0:half, :])
    second = _dot((x + pos_ref[0, 1:2, :]).astype(BF16), w1_ref[0, half:2 * half, :])
    hid = jax.nn.gelu(first + pltpu.roll(second, tr - 1, 0) + b1_ref[0])
    out = _dot(hid.astype(BF16), w2_ref[0]) + b2_ref[0]
    o_ref[0] = _rope(out, cos_ref[0], sin_ref[0]).astype(BF16)


def _compress(kcv, pos, w1, b1, w2, b2, cos, sin, batch, seq):
    pairs = KV_DIM // LANES
    tr = TR_CMP
    rows = batch * pairs * tr
    sel = lambda k, r: (k, 0, 0)
    return pl.pallas_call(
        _compress_kernel,
        grid=(2, batch * pairs),
        in_specs=[
            pl.BlockSpec((seq, LANES), lambda k, r: (r // pairs, k * pairs + r % pairs)),
            pl.BlockSpec((1, 2, pos.shape[2]), sel),
            pl.BlockSpec((1,) + w1.shape[1:], sel),
            pl.BlockSpec((1, 1, CMP_HIDDEN), sel),
            pl.BlockSpec((1,) + w2.shape[1:], sel),
            pl.BlockSpec((1, 1, LANES), sel),
            pl.BlockSpec((1, tr, LANES), sel),
            pl.BlockSpec((1, tr, LANES), sel),
        ],
        out_specs=pl.BlockSpec((1, tr, LANES), lambda k, r: (k, r, 0)),
        out_shape=jax.ShapeDtypeStruct((2, rows, LANES), BF16),
        compiler_params=pltpu.CompilerParams(
            dimension_semantics=("parallel", "parallel"), vmem_limit_bytes=VMEM_LIMIT),
        name="compress",
    )(kcv, pos, w1, b1, w2, b2, cos, sin)


def _lane_tile_max(s):
    tiles = [s[:, c * LANES:(c + 1) * LANES] for c in range(s.shape[1] // LANES)]
    while len(tiles) > 1:
        tiles = [jnp.maximum(tiles[k], tiles[k + 1]) if k + 1 < len(tiles) else tiles[k]
                 for k in range(0, len(tiles), 2)]
    return tiles[0]


def _stack_heads(q):
    tq = q.shape[0]
    low = _lane_iota((tq, LANES)) < HEAD_DIM
    zero = jnp.zeros((tq, LANES), BF16)
    heads = []
    for hh in range(HEADS_PER_GROUP):
        pair = q[:, (hh // 2) * LANES:(hh // 2 + 1) * LANES]
        heads.append(jnp.where(low if hh % 2 == 0 else jnp.logical_not(low), pair, zero))
    return jnp.concatenate(heads, axis=0)


def _select_kernel(q_ref, kc_ref, vc_ref, ng_ref, m_ref, bias_ref, ocmp_ref):
    tq = TQ_SELECT_TILE
    for sub in range(q_ref.shape[0] // tq):
        bias, o_cmp = _select_tile(pl.program_id(2) * (q_ref.shape[0] // tq) + sub,
                                   q_ref[sub * tq:(sub + 1) * tq, :], kc_ref, vc_ref,
                                   ng_ref[sub * tq:(sub + 1) * tq, :], m_ref)
        bias_ref[sub * tq:(sub + 1) * tq, :] = bias
        ocmp_ref[sub * tq:(sub + 1) * tq, :] = o_cmp


def _select_tile(i, q, kc_ref, vc_ref, ng, m_ref):
    tq = q.shape[0]
    rows = HEADS_PER_GROUP * tq
    low = _lane_iota((tq, LANES)) < HEAD_DIM
    q4 = _stack_heads(q)
    lane4 = _lane_iota((rows, LANES))
    qpos4 = i * tq + (_row_iota((rows, LANES)) & (tq - 1))

    sc = _dot_t(q4, kc_ref[0])
    sc = jnp.where(lane4 * CMP_STRIDE + (CMP_BLOCK - 1) <= qpos4, sc, NEG_INF)
    mc = jnp.max(sc, axis=1, keepdims=True)
    pc = jnp.exp(sc - mc)
    pc = pc / jnp.sum(pc, axis=1, keepdims=True)
    pc = jnp.where(qpos4 >= CMP_BLOCK - 1, pc, 0.0)
    o_cmp = _dot(pc.astype(BF16), vc_ref[0])

    psum = pc[0:tq] + pc[tq:2 * tq] + pc[2 * tq:3 * tq] + pc[3 * tq:4 * tq]
    p_hi = psum.astype(BF16)
    rem = psum - p_hi.astype(F32)
    p_mid = rem.astype(BF16)
    p_lo = (rem - p_mid.astype(F32)).astype(BF16)
    cs = m_ref[...]
    p_slc = _dot(p_hi, cs) + _dot(p_mid, cs) + _dot(p_lo, cs)

    n_sel = SEL_BLOCK // 2
    p_slc_t = p_slc.T[0:n_sel, :]
    blk = _row_iota((n_sel, tq))
    tblk = (i * tq + _lane_iota((n_sel, tq))) >> 6
    forced = (blk == 0) | (blk == tblk) | (blk == tblk - 1)
    score = jnp.where(blk <= tblk, p_slc_t + jnp.where(forced, FORCE_BONUS, 0.0), -1.0)
    rank = jnp.zeros((n_sel, tq), F32)
    for j in range(n_sel):
        sj = score[j:j + 1, :]
        beats = (sj > score) | ((sj == score) & (blk > j))
        rank = rank + jnp.where(beats, 1.0, 0.0)
    bias_t = jnp.where(rank < SEL_TOPN, 0.0, NEG_INF)
    bias = jnp.concatenate([bias_t, jnp.zeros((LANES - n_sel, tq), F32)], axis=0).T.astype(BF16)

    gated = [ng[:, hh:hh + 1] * o_cmp[hh * tq:(hh + 1) * tq] for hh in range(HEADS_PER_GROUP)]
    o_pairs = jnp.concatenate([jnp.where(low, gated[0], gated[1]), jnp.where(low, gated[2], gated[3])], axis=1)
    return bias, o_pairs


def _attn_kernel(*refs):
    n_full = pl.program_id(2) // (CK_SEL // TQ)
    for n_chunks in range(1, refs[3].shape[0] // CK_SEL + 1):
        pl.when(n_full == n_chunks - 1)(functools.partial(_attn_step, n_chunks, *refs))


def _attn_step(n_chunks, q_ref, bias_ref, ocmp_ref, ks_ref, vs_ref, kw_ref, vw_ref, ng_ref, e_ref, o_ref):
    i = pl.program_id(2)
    tq = q_ref.shape[0]
    rows = HEADS_PER_GROUP * tq
    ck = CK_SEL
    low = _lane_iota((tq, LANES)) < HEAD_DIM
    q4 = _stack_heads(q_ref[...])

    n_sub = tq // TK
    back = WINDOW // TK
    k_tiles, v_tiles = [], []
    for jj in range(back + n_sub):
        j = i * n_sub - back + jj
        off = pl.multiple_of(jnp.maximum(j, 0) * TK, TK)
        k_tiles.append(kw_ref[pl.ds(off, TK), :])
        v_tiles.append(vw_ref[pl.ds(off, TK), :])
    sw = _dot_t(q4, jnp.concatenate(k_tiles, axis=0))
    on_or_after = _row_iota((TK, TK)) >= _lane_iota((TK, TK))
    sw_tiles = []
    for jj in range(back + n_sub):
        blocks = []
        for blk_i in range(rows // TK):
            piece = sw[blk_i * TK:(blk_i + 1) * TK, jj * TK:(jj + 1) * TK]
            tiles_back = blk_i % n_sub + back - jj
            if tiles_back == 0:
                piece = jnp.where(on_or_after, piece, NEG_INF)
            elif tiles_back == back:
                piece = jnp.where(on_or_after, NEG_INF, piece)
            elif tiles_back < 0 or tiles_back > back:
                piece = jnp.full((TK, TK), NEG_INF, F32)
            blocks.append(piece)
        tile = jnp.concatenate(blocks, axis=0)
        if jj < back:
            tile = jnp.where(i * n_sub - back + jj >= 0, tile, NEG_INF)
        sw_tiles.append(tile)
    sw = jnp.concatenate(sw_tiles, axis=1)
    m_win = jnp.max(_lane_tile_max(sw), axis=1, keepdims=True)
    acc_win = _dot(jnp.exp(sw - m_win).astype(BF16), jnp.concatenate(v_tiles, axis=0))

    qa = jnp.concatenate([q4, jnp.concatenate([bias_ref[...]] * HEADS_PER_GROUP, axis=0)], axis=1)
    rel_q = i * tq - (n_chunks - 1) * ck + (_row_iota((rows, ck)) & (tq - 1))
    m_run = None
    acc_sel = None
    for c in range(n_chunks):
        ka = jnp.concatenate([ks_ref[c * ck:(c + 1) * ck, :], e_ref[c * ck:(c + 1) * ck, :]], axis=1)
        s = _dot_t(qa, ka)
        if c == n_chunks - 1:
            s = jnp.where(_lane_iota((rows, ck)) <= rel_q, s, NEG_INF)
        m_new = jnp.max(_lane_tile_max(s), axis=1, keepdims=True)
        if c > 0:
            m_new = jnp.maximum(m_run, m_new)
            acc_sel = acc_sel * jnp.exp(m_run - m_new)
        part = _dot(jnp.exp(s - m_new).astype(BF16), vs_ref[c * ck:(c + 1) * ck, :])
        acc_sel = part if c == 0 else acc_sel + part
        m_run = m_new

    ng = ng_ref[...]
    outs = []
    for hh in range(HEADS_PER_GROUP):
        r0 = hh * tq
        col = lambda br: ng[:, br * HEADS_PER_GROUP + hh:br * HEADS_PER_GROUP + hh + 1]
        parts = []
        for acc in (acc_sel[r0:r0 + tq], acc_win[r0:r0 + tq]):
            swapped = pltpu.roll(acc, HEAD_DIM, 1)
            parts.append(acc / swapped if hh % 2 == 0 else swapped / acc)
        outs.append(col(1) * parts[0] + col(2) * parts[1])
    o_ref[...] = (ocmp_ref[...] + jnp.concatenate(
        [jnp.where(low, outs[0], outs[1]), jnp.where(low, outs[2], outs[3])], axis=1)).astype(o_ref.dtype)


def _select(q, kvc, ng, cs_mat, batch, seq):
    t = q.shape[0]
    tq = TQ_SELECT
    nq = seq // tq
    qrow = lambda b, g, i: (b * nq + i, g)
    return pl.pallas_call(
        _select_kernel,
        grid=(batch, N_KV_GROUPS, nq),
        in_specs=[
            pl.BlockSpec((tq, HEADS_PER_GROUP * HEAD_DIM), qrow),
            pl.BlockSpec((1, N_CMP_PAD, LANES), lambda b, g, i: (0, b * N_KV_GROUPS + g, 0)),
            pl.BlockSpec((1, N_CMP_PAD, LANES), lambda b, g, i: (1, b * N_KV_GROUPS + g, 0)),
            pl.BlockSpec((tq, LANES), qrow),
            pl.BlockSpec((N_CMP_PAD, LANES), lambda b, g, i: (0, 0)),
        ],
        out_specs=(
            pl.BlockSpec((tq, LANES), qrow),
            pl.BlockSpec((tq, HEADS_PER_GROUP * HEAD_DIM), qrow),
        ),
        out_shape=(
            jax.ShapeDtypeStruct((t, N_KV_GROUPS * LANES), BF16),
            jax.ShapeDtypeStruct((t, Q_DIM), F32),
        ),
        compiler_params=pltpu.CompilerParams(
            dimension_semantics=("parallel", "parallel", "parallel"), vmem_limit_bytes=VMEM_LIMIT),
        name="select",
    )(q, kvc, kvc, ng, cs_mat)


def _attn(q, bias, o_cmp, ksd, vsd, kwd, vwd, ng, e_mat, batch, seq):
    t = q.shape[0]
    tq = TQ
    nq = seq // tq
    qrow = lambda b, g, i: (b * nq + i, g)
    kv = lambda b, g, i: (b, g)
    return pl.pallas_call(
        _attn_kernel,
        grid=(batch, N_KV_GROUPS, nq),
        in_specs=[
            pl.BlockSpec((tq, HEADS_PER_GROUP * HEAD_DIM), qrow),
            pl.BlockSpec((tq, LANES), qrow),
            pl.BlockSpec((tq, HEADS_PER_GROUP * HEAD_DIM), qrow),
            pl.BlockSpec((seq, LANES), kv),
            pl.BlockSpec((seq, LANES), kv),
            pl.BlockSpec((seq, LANES), kv),
            pl.BlockSpec((seq, LANES), kv),
            pl.BlockSpec((tq, LANES), qrow),
            pl.BlockSpec((seq, LANES), lambda b, g, i: (0, 0)),
        ],
        out_specs=pl.BlockSpec((tq, HEADS_PER_GROUP * HEAD_DIM), qrow),
        out_shape=jax.ShapeDtypeStruct((t, Q_DIM), BF16),
        compiler_params=pltpu.CompilerParams(
            dimension_semantics=("parallel", "parallel", "arbitrary"), vmem_limit_bytes=VMEM_LIMIT),
        name="attn",
    )(q, bias, o_cmp, ksd, vsd, kwd, vwd, ng, e_mat)


def _post_kernel(o_ref, ya_ref, mg_ref, x_ref, wn_ref, wm_ref, g2_ref, wr_hi_ref, wr_lo_ref, rb_ref,
                 xa_ref):
    y_b = _dot(o_ref[...], wn_ref[...])
    mixed = mg_ref[:, 0:D_MODEL] * ya_ref[...] + mg_ref[:, D_MODEL:2 * D_MODEL] * y_b
    x1 = x_ref[...] + _dot(mixed.astype(BF16), wm_ref[...])
    xa_ref[:, 0:D_MODEL] = x1
    h2 = x1 * lax.rsqrt(jnp.mean(x1 * x1, axis=-1, keepdims=True) + EPS) * g2_ref[...]
    h_hi = h2.astype(BF16)
    h_lo = (h2 - h_hi.astype(F32)).astype(BF16)
    logits = (_dot(h_hi, wr_hi_ref[...]) + _dot(h_lo, wr_hi_ref[...]) + _dot(h_hi, wr_lo_ref[...])
              + rb_ref[...])

    lane = _lane_iota(logits.shape).astype(F32)
    is_grp = (lane >= N_EXPERTS) & (lane < N_EXPERTS + N_GROUPS)
    gl = jnp.where(is_grp, logits, NEG_INF)
    ge = jnp.exp(gl - jnp.max(gl, axis=1, keepdims=True))
    gp = ge / jnp.sum(ge, axis=1, keepdims=True)
    g_w = jnp.max(gp, axis=1, keepdims=True)
    big = float(4 * LANES)
    g_first = jnp.min(jnp.where(is_grp & (gp == g_w), lane, big), axis=1, keepdims=True)
    grp_lo = (g_first - N_EXPERTS) * EXPERTS_PER_GROUP

    in_grp = (lane >= grp_lo) & (lane < grp_lo + EXPERTS_PER_GROUP)
    el = jnp.where(in_grp, logits, NEG_INF)
    ee = jnp.exp(el - jnp.max(el, axis=1, keepdims=True))
    ep = ee / jnp.sum(ee, axis=1, keepdims=True)
    w1 = jnp.max(ep, axis=1, keepdims=True)
    i1 = jnp.min(jnp.where(in_grp & (ep == w1), lane, big), axis=1, keepdims=True)
    rest = jnp.where(in_grp & (lane != i1), ep, -1.0)
    w2 = jnp.max(rest, axis=1, keepdims=True)
    i2 = jnp.min(jnp.where(rest == w2, lane, big), axis=1, keepdims=True)
    den = w1 + w2
    comb = jnp.where(lane == i1, g_w * (w1 / den), jnp.where(lane == i2, g_w * (w2 / den), 0.0))
    xa_ref[:, D_MODEL:D_MODEL + LANES] = jnp.where(lane == GROUP_LANE, g_first - N_EXPERTS, comb)


def _post(o_nsa, y_a, mg, x2, w_nsa, w_mix, g2, wr_hi, wr_lo, rb):
    t = x2.shape[0]
    tm = TM_POST
    row = lambda i: (i, 0)
    const = lambda i: (0, 0)
    return pl.pallas_call(
        _post_kernel,
        grid=(t // tm,),
        in_specs=[
            pl.BlockSpec((tm, Q_DIM), row),
            pl.BlockSpec((tm, D_MODEL), row),
            pl.BlockSpec((tm, 2 * D_MODEL), row),
            pl.BlockSpec((tm, D_MODEL), row),
            pl.BlockSpec((Q_DIM, D_MODEL), const),
            pl.BlockSpec((D_MODEL, D_MODEL), const),
            pl.BlockSpec((1, D_MODEL), const),
            pl.BlockSpec((D_MODEL, LANES), const),
            pl.BlockSpec((D_MODEL, LANES), const),
            pl.BlockSpec((1, LANES), const),
        ],
        out_specs=pl.BlockSpec((tm, D_AUG), row),
        out_shape=jax.ShapeDtypeStruct((t, D_AUG), F32),
        compiler_params=pltpu.CompilerParams(
            dimension_semantics=("parallel",), vmem_limit_bytes=VMEM_LIMIT),
        name="post",
    )(o_nsa, y_a, mg, x2, w_nsa, w_mix, g2, wr_hi, wr_lo, rb)


def _plan_kernel(rec_ref, tri_ref, info_ref, counts_ref, carry_s):
    @pl.when(pl.program_id(0) == 0)
    def _():
        carry_s[...] = jnp.zeros_like(carry_s)

    rec = rec_ref[...]
    lane = _lane_iota(rec.shape)
    gid = rec[:, GROUP_LANE:GROUP_LANE + 1]
    onehot = jnp.where(lane.astype(F32) == gid, 1.0, 0.0)
    before = _dot(tri_ref[...], onehot.astype(BF16)) + carry_s[0:1, :]
    rank = jnp.sum(onehot * before, axis=1, keepdims=True)
    info_ref[...] = jnp.where(lane == 0, rank, jnp.where(lane == 1, gid, 0.0))
    carry_s[...] = carry_s[...] + jnp.sum(onehot, axis=0, keepdims=True)
    counts_ref[...] = carry_s[...]


def _plan(xa, tri):
    t = xa.shape[0]
    tm = TM_PLAN
    return pl.pallas_call(
        _plan_kernel,
        grid=(t // tm,),
        in_specs=[
            pl.BlockSpec((tm, LANES), lambda i: (i, D_MODEL // LANES)),
            pl.BlockSpec((tm, tm), lambda i: (0, 0)),
        ],
        out_specs=(
            pl.BlockSpec((tm, LANES), lambda i: (i, 0)),
            pl.BlockSpec((SUBLANES, LANES), lambda i: (0, 0)),
        ),
        out_shape=(
            jax.ShapeDtypeStruct((t, LANES), F32),
            jax.ShapeDtypeStruct((SUBLANES, LANES), F32),
        ),
        scratch_shapes=[pltpu.VMEM((SUBLANES, LANES), F32)],
        compiler_params=pltpu.CompilerParams(
            dimension_semantics=("arbitrary",), vmem_limit_bytes=VMEM_LIMIT),
        name="plan",
    )(xa, tri)


def _row_copies(n_rows, make_copy):
    def start(r, carry):
        make_copy(r).start()
        return carry

    def wait(r, carry):
        make_copy(0).wait()
        return carry

    lax.fori_loop(0, n_rows, start, 0, unroll=ROW_COPY_UNROLL)
    lax.fori_loop(0, n_rows, wait, 0, unroll=ROW_COPY_UNROLL)


def _dispatch_kernel(gid_ref, rank_ref, base_ref, xa_ref, zero_hbm, xs_hbm, sem):
    del zero_hbm
    first = pl.program_id(0) * xa_ref.shape[0]

    def copy(r):
        slot = base_ref[gid_ref[first + r]] + rank_ref[first + r]
        return pltpu.make_async_copy(xa_ref.at[pl.ds(r, 1), :], xs_hbm.at[pl.ds(slot, 1), :], sem)

    _row_copies(xa_ref.shape[0], copy)


def _dispatch(gid, rank, base, xa, n_sorted):
    t = xa.shape[0]
    tm = TM_ROWS
    zeros = jnp.zeros((n_sorted, D_AUG), F32)
    return pl.pallas_call(
        _dispatch_kernel,
        grid_spec=pltpu.PrefetchScalarGridSpec(
            num_scalar_prefetch=3,
            grid=(t // tm,),
            in_specs=[
                pl.BlockSpec((tm, D_AUG), lambda i, *_: (i, 0)),
                pl.BlockSpec(memory_space=pl.ANY),
            ],
            out_specs=pl.BlockSpec(memory_space=pl.ANY),
            scratch_shapes=[pltpu.SemaphoreType.DMA],
        ),
        out_shape=jax.ShapeDtypeStruct((n_sorted, D_AUG), F32),
        input_output_aliases={4: 0},
        compiler_params=pltpu.CompilerParams(
            dimension_semantics=("arbitrary",), vmem_limit_bytes=VMEM_LIMIT),
        name="dispatch",
    )(gid, rank, base, xa, zeros)


def _experts_kernel(widx_ref, used_ref, tgrp_ref, xs_ref, wg_ref, wu_ref, wd_ref, g2_ref, gf_ref,
                    fs_ref, h_s, y_s):
    del widx_ref
    j = pl.program_id(0)
    e = pl.program_id(1)
    used = used_ref[j] == 1

    @pl.when(used & (e == 0))
    def _():
        x1 = xs_ref[:, 0:D_MODEL]
        h2 = x1 * lax.rsqrt(jnp.mean(x1 * x1, axis=-1, keepdims=True) + EPS) * g2_ref[...]
        h_s[...] = h2.astype(BF16)
        y_s[...] = jnp.zeros_like(y_s)

    @pl.when(used)
    def _():
        h = h_s[...]
        act = jax.nn.silu(_dot(h, wg_ref[0].astype(BF16))) * _dot(h, wu_ref[0].astype(BF16))
        y = _dot(act.astype(BF16), wd_ref[0].astype(BF16))
        rec = xs_ref[:, D_MODEL:D_AUG]
        col = tgrp_ref[j] * EXPERTS_PER_GROUP + e
        weight = jnp.sum(jnp.where(_lane_iota(rec.shape) == col, rec, 0.0), axis=1, keepdims=True)
        y_s[...] += weight * y

    @pl.when(used & (e == EXPERTS_PER_GROUP - 1))
    def _():
        x = xs_ref[:, 0:D_MODEL] + y_s[...]
        fs_ref[...] = x * lax.rsqrt(jnp.mean(x * x, axis=-1, keepdims=True) + EPS) * gf_ref[...]

    @pl.when(jnp.logical_not(used) & (e == EXPERTS_PER_GROUP - 1))
    def _():
        fs_ref[...] = jnp.zeros_like(fs_ref)


def _experts(widx, used, tgrp, xs, wg, wu, wd, g2, gf):
    n_sorted = xs.shape[0]
    tm = TM_EXP
    row = lambda j, e, *_: (j, 0)
    wsel = lambda j, e, widx, used, tgrp: (widx[j * EXPERTS_PER_GROUP + e], 0, 0)
    const = lambda j, e, *_: (0, 0)
    return pl.pallas_call(
        _experts_kernel,
        grid_spec=pltpu.PrefetchScalarGridSpec(
            num_scalar_prefetch=3,
            grid=(n_sorted // tm, EXPERTS_PER_GROUP),
            in_specs=[
                pl.BlockSpec((tm, D_AUG), row),
                pl.BlockSpec((1, D_MODEL, D_EXPERT), wsel),
                pl.BlockSpec((1, D_MODEL, D_EXPERT), wsel),
                pl.BlockSpec((1, D_EXPERT, D_MODEL), wsel),
                pl.BlockSpec((1, D_MODEL), const),
                pl.BlockSpec((1, D_MODEL), const),
            ],
            out_specs=pl.BlockSpec((tm, D_MODEL), row),
            scratch_shapes=[pltpu.VMEM((tm, D_MODEL), BF16), pltpu.VMEM((tm, D_MODEL), F32)],
        ),
        out_shape=jax.ShapeDtypeStruct((n_sorted, D_MODEL), F32),
        compiler_params=pltpu.CompilerParams(
            dimension_semantics=("arbitrary", "arbitrary"), vmem_limit_bytes=VMEM_LIMIT),
        name="experts",
    )(widx, used, tgrp, xs, wg, wu, wd, g2, gf)


def _combine_kernel(gid_ref, rank_ref, base_ref, fs_hbm, o_ref, sem):
    first = pl.program_id(0) * o_ref.shape[0]

    def copy(r):
        slot = base_ref[gid_ref[first + r]] + rank_ref[first + r]
        return pltpu.make_async_copy(fs_hbm.at[pl.ds(slot, 1), :], o_ref.at[pl.ds(r, 1), :], sem)

    _row_copies(o_ref.shape[0], copy)


def _combine(gid, rank, base, fs, t):
    tm = TM_ROWS
    return pl.pallas_call(
        _combine_kernel,
        grid_spec=pltpu.PrefetchScalarGridSpec(
            num_scalar_prefetch=3,
            grid=(t // tm,),
            in_specs=[pl.BlockSpec(memory_space=pl.ANY)],
            out_specs=pl.BlockSpec((tm, D_MODEL), lambda i, *_: (i, 0)),
            scratch_shapes=[pltpu.SemaphoreType.DMA],
        ),
        out_shape=jax.ShapeDtypeStruct((t, D_MODEL), F32),
        compiler_params=pltpu.CompilerParams(
            dimension_semantics=("arbitrary",), vmem_limit_bytes=VMEM_LIMIT),
        name="combine",
    )(gid, rank, base, fs)


def _moe_sorted(xa, wg, wu, wd, g2, gf):
    t = xa.shape[0]
    tri = jnp.asarray(np.tril(np.ones((TM_PLAN, TM_PLAN), np.float32), -1), dtype=BF16)
    info, counts = _plan(xa, tri)
    rank = info[:, 0].astype(jnp.int32)
    gid = info[:, 1].astype(jnp.int32)
    counts = counts[0, :N_GROUPS].astype(jnp.int32)

    n_tiles = t // TM_EXP + N_GROUPS
    tiles_g = (counts + TM_EXP - 1) // TM_EXP
    tile_end = jnp.cumsum(tiles_g)
    base = jnp.pad((tile_end - tiles_g) * TM_EXP, (0, SUBLANES - N_GROUPS))
    tile_ids = jnp.arange(n_tiles, dtype=jnp.int32)
    used = (tile_ids < tile_end[-1]).astype(jnp.int32)
    last = jnp.maximum(tile_end[-1] - 1, 0)
    tgrp = jnp.sum((jnp.minimum(tile_ids, last)[:, None] >= tile_end[None, :]).astype(jnp.int32), axis=1)
    step_e = jnp.arange(EXPERTS_PER_GROUP, dtype=jnp.int32)[None, :]
    widx = jnp.where(used[:, None] == 1, tgrp[:, None] * EXPERTS_PER_GROUP + step_e,
                     tgrp[:, None] * EXPERTS_PER_GROUP + EXPERTS_PER_GROUP - 1).reshape(-1)

    xs = _dispatch(gid, rank, base, xa, n_tiles * TM_EXP)
    fs = _experts(widx, used, tgrp, xs, wg, wu, wd, g2, gf)
    return _combine(gid, rank, base, fs, t)


def _split_w_in(w):
    sizes = [D_RNN, D_RNN, Q_DIM] + [KV_DIM] * 6 + [3 * N_HEADS, 2 * D_MODEL]
    pts = np.cumsum(sizes)[:-1]
    return jnp.split(w, [int(p) for p in pts], axis=-1)


def _proj_weight(w_in):
    xr, gr, q, kc, vc, ksl, vsl, kw, vw, nsa_g, merge_g = _split_w_in(w_in)
    ng = nsa_g.reshape(D_MODEL, 3, N_KV_GROUPS, HEADS_PER_GROUP).transpose(0, 2, 1, 3)
    ng = ng.reshape(D_MODEL, N_KV_GROUPS, 3 * HEADS_PER_GROUP)
    ng = jnp.pad(ng, ((0, 0), (0, 0), (0, LANES - 3 * HEADS_PER_GROUP))).reshape(D_MODEL, N_KV_GROUPS * LANES)
    return jnp.concatenate([xr, gr, q, merge_g, kc, vc, ksl, vsl, kw, vw, ng], axis=1).astype(BF16)


def _rope_tables(pos, width):
    inv_freq = ROPE_THETA ** (-(jnp.arange(0, HEAD_DIM, 2, dtype=F32) / HEAD_DIM))
    ang = pos.astype(F32)[:, None] * inv_freq[None, :]
    cos, sin = jnp.cos(ang), jnp.sin(ang)
    reps = width // HEAD_DIM
    return jnp.tile(jnp.concatenate([cos, cos], axis=1), (1, reps)), jnp.tile(jnp.concatenate([-sin, sin], axis=1), (1, reps))


def _block_diag(w, per):
    nb, d = w.shape[0], w.shape[1]
    w = w.reshape(nb // per, per, d, d)
    eye = jnp.eye(per, dtype=w.dtype)
    return jnp.einsum('npij,pq->npiqj', w, eye).reshape(nb // per, per * d, per * d)


def _cmp_to_sel():
    n_sel_pad = LANES
    c0 = np.arange(N_CMP_PAD) * CMP_STRIDE
    s0 = np.arange(n_sel_pad) * SEL_BLOCK
    ov = np.minimum(c0[:, None] + CMP_BLOCK, s0[None, :] + SEL_BLOCK) - np.maximum(c0[:, None], s0[None, :])
    m = np.clip(ov, 0, None) / CMP_BLOCK
    m[:, SEL_BLOCK // 2:] = 0.0
    m[N_CMP_PAD - 1, :] = 0.0
    return m.astype(np.float32)


def kernel(x, norm1_g, w_in, conv_w, conv_b, lru_wa, lru_ba, lru_wi, lru_bi, lru_lambda, w_rnn_out, cmpk_pos, cmpk_w1, cmpk_b1, cmpk_w2, cmpk_b2, cmpv_pos, cmpv_w1, cmpv_b1, cmpv_w2, cmpv_b2, w_nsa_out, w_mix_out, norm2_g, router_group_w, router_group_b, router_expert_w, router_expert_b, expert_w_gate, expert_w_up, expert_w_down, final_norm_g):
    batch, seq, _ = x.shape
    t = batch * seq
    assert w_in.shape[0] == 1, "the final norm is fused into the expert kernel: single layer only"
    x2 = x.reshape(t, D_MODEL)

    cos, sin = _rope_tables(jnp.arange(seq), LANES)
    cmp_ends = jnp.arange(N_CMP_PAD) * CMP_STRIDE + (CMP_BLOCK - 1)
    ccos, csin = _rope_tables(cmp_ends, LANES)
    reps = TR_CMP // N_CMP_PAD
    cmp_cos = jnp.stack([jnp.tile(ccos, (reps, 1)), jnp.ones((TR_CMP, LANES), F32)])
    cmp_sin = jnp.stack([jnp.tile(csin, (reps, 1)), jnp.zeros((TR_CMP, LANES), F32)])
    key_blk = np.arange(seq)[:, None] // SEL_BLOCK
    e_mat = jnp.asarray((key_blk == np.arange(LANES)[None, :]).astype(np.float32), dtype=BF16)
    cs_mat = jnp.asarray(_cmp_to_sel(), dtype=BF16)

    l = 0
    xg, q, mg, kcv, ksd, vsd, kwd, vwd, ng = _inproj(
        x2, norm1_g[l][None, :], _proj_weight(w_in[l]), cos, sin, seq)

    y_a = _rnn(xg, conv_w[l], conv_b[l][None, :],
               _block_diag(lru_wa[l], 4).astype(BF16), lru_ba[l][None, :],
               _block_diag(lru_wi[l], 4).astype(BF16), lru_bi[l][None, :],
               lru_lambda[l][None, :], w_rnn_out[l].astype(BF16), batch, seq)

    half = CMP_STRIDE * HEAD_DIM
    pos = jnp.stack([cmpk_pos[l].reshape(2, half), cmpv_pos[l].reshape(2, half)])
    w1 = jnp.stack([cmpk_w1[l], cmpv_w1[l]]).astype(BF16)
    b1 = jnp.stack([cmpk_b1[l], cmpv_b1[l]])[:, None, :]
    w2 = jnp.stack([cmpk_w2[l], cmpv_w2[l]])
    w2 = jnp.concatenate([w2, w2], axis=2).astype(BF16)
    b2 = jnp.stack([cmpk_b2[l], cmpv_b2[l]])
    b2 = jnp.concatenate([b2, b2], axis=1)[:, None, :]
    kvc = _compress(kcv, pos, w1, b1, w2, b2, cmp_cos, cmp_sin, batch, seq)

    bias, o_cmp = _select(q, kvc, ng, cs_mat, batch, seq)
    o_nsa = _attn(q, bias, o_cmp, ksd, vsd, kwd, vwd, ng, e_mat, batch, seq)

    wr = jnp.concatenate([
        router_expert_w[l].transpose(1, 0, 2).reshape(D_MODEL, N_EXPERTS),
        router_group_w[l],
        jnp.zeros((D_MODEL, LANES - N_EXPERTS - N_GROUPS), F32)], axis=1)
    wr_hi = wr.astype(BF16)
    wr_lo = (wr - wr_hi.astype(F32)).astype(BF16)
    rb = jnp.concatenate([router_expert_b[l].reshape(N_EXPERTS), router_group_b[l],
                          jnp.zeros((LANES - N_EXPERTS - N_GROUPS,), F32)])[None, :]
    xa = _post(o_nsa, y_a, mg, x2, w_nsa_out[l].astype(BF16), w_mix_out[l].astype(BF16),
               norm2_g[l][None, :], wr_hi, wr_lo, rb)

    out = _moe_sorted(xa, expert_w_gate[l], expert_w_up[l], expert_w_down[l],
                      norm2_g[l][None, :], final_norm_g[None, :])
    return out.reshape(batch, seq, D_MODEL)
```

```python
import functools

import numpy as np
import jax
import jax.numpy as jnp
from jax import lax
from jax.experimental import pallas as pl
from jax.experimental.pallas import tpu as pltpu

D_MODEL = 1024
D_RNN = 1024
RNN_BLOCKS = 16
RNN_BLOCK_DIM = D_RNN // RNN_BLOCKS
CONV_WIDTH = 4
LRU_C = 8.0
N_HEADS = 16
HEAD_DIM = 64
HALF_DIM = HEAD_DIM // 2
N_KV_GROUPS = 4
HEADS_PER_GROUP = N_HEADS // N_KV_GROUPS
Q_DIM = N_HEADS * HEAD_DIM
KV_DIM = N_KV_GROUPS * HEAD_DIM
CMP_BLOCK = 32
CMP_STRIDE = 16
CMP_HIDDEN = 256
SEL_BLOCK = 64
SEL_TOPN = 16
WINDOW = 512
ROPE_THETA = 10000.0
FORCE_BONUS = 1e4
NEG_INF = -1e30
N_GROUPS = 4
EXPERTS_PER_GROUP = 4
N_EXPERTS = N_GROUPS * EXPERTS_PER_GROUP
D_EXPERT = 512
EPS = 1e-6

LANES = 128
SUBLANES = 8
VMEM_LIMIT = 56 * 1024 * 1024

BF16 = jnp.bfloat16
F32 = jnp.float32

COL_Q = 0
COL_KCV = COL_Q + Q_DIM
COL_KS = COL_KCV + 2 * KV_DIM
COL_VS = COL_KS + KV_DIM
COL_KW = COL_VS + KV_DIM
COL_VW = COL_KW + KV_DIM
COL_NG = COL_VW + KV_DIM
N_PROJ = COL_NG + N_KV_GROUPS * LANES

TM_PROJ = 512
TS_RNN = 512
TR_CMP = 256
TQ = 256
TQ_SELECT = 512
TQ_SELECT_TILE = 128
TK = 128
CK_SEL = 512
TM_POST = 512
TM_PLAN = 1024
TM_ROWS = 512
TM_EXP = 1024
ROW_COPY_UNROLL = 8
D_AUG = D_MODEL + LANES
GROUP_LANE = N_EXPERTS
N_CMP_PAD = 128


def _dot(a, b):
    return jnp.dot(a, b, preferred_element_type=F32)


def _dot_t(a, b):
    return lax.dot_general(a, b, (((1,), (1,)), ((), ())), preferred_element_type=F32)


def _lane_iota(shape):
    return lax.broadcasted_iota(jnp.int32, shape, len(shape) - 1)


def _row_iota(shape):
    return lax.broadcasted_iota(jnp.int32, shape, 0)


def _rope(x, cos, sin_signed):
    width = x.shape[-1]
    reps = width // cos.shape[-1]
    if reps > 1:
        cos = jnp.concatenate([cos] * reps, axis=1)
        sin_signed = jnp.concatenate([sin_signed] * reps, axis=1)
    first_half = (_lane_iota(x.shape) & (HEAD_DIM - 1)) < HALF_DIM
    partner = jnp.where(first_half, pltpu.roll(x, width - HALF_DIM, 1), pltpu.roll(x, HALF_DIM, 1))
    return x * cos + partner * sin_signed


def _spread_heads(x, fill=None):
    out = []
    low = _lane_iota((x.shape[0], LANES)) < HEAD_DIM
    for c in range(x.shape[1] // LANES):
        xc = x[:, c * LANES:(c + 1) * LANES]
        rolled = pltpu.roll(xc, HEAD_DIM, 1)
        out.append(jnp.where(low, xc, rolled if fill is None else fill))
        out.append(jnp.where(low, rolled, xc if fill is None else fill))
    return jnp.concatenate(out, axis=1)


def _sigmoid(x):
    return 0.5 * (jnp.tanh(0.5 * x) + 1.0)


def _normed(x, g):
    return (x * lax.rsqrt(jnp.mean(x * x, axis=-1, keepdims=True) + EPS) * g).astype(BF16)


def _inproj_kernel(x_ref, g_ref, w_ref, cos_ref, sin_ref,
                   q_ref, kcv_ref, ks_ref, vs_ref, kw_ref, vw_ref, ng_ref):
    h = _normed(x_ref[...], g_ref[...])
    cos = cos_ref[...]
    sin = sin_ref[...]

    def mm(lo, width):
        return _dot(h, w_ref[:, lo:lo + width])

    q_ref[...] = (_rope(mm(COL_Q, Q_DIM), cos, sin) * (HEAD_DIM ** -0.5)).astype(BF16)
    kcv_ref[...] = mm(COL_KCV, 2 * KV_DIM)
    ks_ref[...] = _spread_heads(_rope(mm(COL_KS, KV_DIM), cos, sin)).astype(BF16)
    vs_ref[...] = _spread_heads(mm(COL_VS, KV_DIM), 1.0).astype(BF16)
    kw_ref[...] = _spread_heads(_rope(mm(COL_KW, KV_DIM), cos, sin)).astype(BF16)
    vw_ref[...] = _spread_heads(mm(COL_VW, KV_DIM), 1.0).astype(BF16)
    ng_ref[...] = jax.nn.sigmoid(mm(COL_NG, N_KV_GROUPS * LANES))


def _inproj(x2, norm_g, w_proj, cos, sin, seq):
    t = x2.shape[0]
    tm = TM_PROJ
    pos_blocks = seq // tm
    row = lambda i: (i, 0)
    const = lambda i: (0, 0)
    out_shape = (
        jax.ShapeDtypeStruct((t, Q_DIM), BF16),
        jax.ShapeDtypeStruct((t, 2 * KV_DIM), F32),
        jax.ShapeDtypeStruct((t, N_KV_GROUPS * LANES), BF16),
        jax.ShapeDtypeStruct((t, N_KV_GROUPS * LANES), BF16),
        jax.ShapeDtypeStruct((t, N_KV_GROUPS * LANES), BF16),
        jax.ShapeDtypeStruct((t, N_KV_GROUPS * LANES), BF16),
        jax.ShapeDtypeStruct((t, N_KV_GROUPS * LANES), F32),
    )
    return pl.pallas_call(
        _inproj_kernel,
        grid=(t // tm,),
        in_specs=[
            pl.BlockSpec((tm, D_MODEL), row),
            pl.BlockSpec((1, D_MODEL), const),
            pl.BlockSpec((D_MODEL, N_PROJ), const, pipeline_mode=pl.Buffered(1)),
            pl.BlockSpec((tm, LANES), lambda i: (i % pos_blocks, 0)),
            pl.BlockSpec((tm, LANES), lambda i: (i % pos_blocks, 0)),
        ],
        out_specs=tuple(pl.BlockSpec((tm, s.shape[1]), row) for s in out_shape),
        out_shape=out_shape,
        compiler_params=pltpu.CompilerParams(
            dimension_semantics=("parallel",), vmem_limit_bytes=VMEM_LIMIT),
        name="inproj",
    )(x2, norm_g, w_proj, cos, sin)


def _rnn_kernel(xin_ref, g1_ref, wx_ref, cw_ref, cb_ref, wa_ref, ba_ref, wi_ref, bi_ref, lam_ref, wo_ref,
                y_ref, tail_s, carry_s, a_s, u_s, h_s):
    ts = xin_ref.shape[0]

    @pl.when(pl.program_id(1) == 0)
    def _():
        tail_s[...] = jnp.zeros_like(tail_s)
        carry_s[...] = jnp.zeros_like(carry_s)

    hn = _normed(xin_ref[...], g1_ref[...])
    x = _dot(hn, wx_ref[:, 0:D_RNN])
    gate_pre = _dot(hn, wx_ref[:, D_RNN:2 * D_RNN])
    xext = jnp.concatenate([tail_s[...], x], axis=0)
    tail_s[...] = x[ts - SUBLANES:ts, :]
    conv = cb_ref[...]
    for k in range(CONV_WIDTH):
        back = CONV_WIDTH - 1 - k
        shifted = xext if back == 0 else pltpu.roll(xext, back, 0)
        conv = conv + cw_ref[k:k + 1, :] * shifted[SUBLANES:SUBLANES + ts, :]

    cb16 = conv.astype(BF16)
    blk = wa_ref.shape[1]
    r_pre = jnp.concatenate(
        [_dot(cb16[:, j * blk:(j + 1) * blk], wa_ref[j]) for j in range(D_RNN // blk)], axis=1)
    i_pre = jnp.concatenate(
        [_dot(cb16[:, j * blk:(j + 1) * blk], wi_ref[j]) for j in range(D_RNN // blk)], axis=1)
    r = _sigmoid(r_pre + ba_ref[...])
    gate_i = _sigmoid(i_pre + bi_ref[...])
    neg_lam = -lam_ref[...]
    softplus = jnp.maximum(neg_lam, 0.0) + jnp.log1p(jnp.exp(-jnp.abs(neg_lam)))
    log_a = (-LRU_C) * r * softplus
    a = jnp.exp(log_a)
    a_s[...] = a
    u_s[...] = jnp.sqrt(-jnp.tanh(log_a) * (a * a + 1.0)) * (gate_i * conv)

    row = _row_iota((SUBLANES, D_RNN))

    def body(k, carry):
        off = pl.multiple_of(k * SUBLANES, SUBLANES)
        a = a_s[pl.ds(off, SUBLANES), :]
        b = u_s[pl.ds(off, SUBLANES), :]
        for sh in (1, 2, 4):
            keep = row >= sh
            a_prev = jnp.where(keep, pltpu.roll(a, sh, 0), 1.0)
            b_prev = jnp.where(keep, pltpu.roll(b, sh, 0), 0.0)
            b = a * b_prev + b
            a = a * a_prev
        h = a * carry + b
        h_s[pl.ds(off, SUBLANES), :] = h
        return jnp.broadcast_to(h[SUBLANES - 1:SUBLANES, :], (SUBLANES, D_RNN))

    carry_s[...] = lax.fori_loop(0, ts // SUBLANES, body, carry_s[...])
    gated = (jax.nn.gelu(gate_pre) * h_s[...]).astype(BF16)
    y_ref[...] = _dot(gated, wo_ref[...])


def _rnn(x2, norm_g, w_xg, conv_w, conv_b, wa_bd, ba, wi_bd, bi, lam, w_out, batch, seq):
    t = x2.shape[0]
    ts = TS_RNN
    nt = seq // ts
    const2 = lambda b, s: (0, 0)
    const3 = lambda b, s: (0, 0, 0)
    nblk, blk = wa_bd.shape[0], wa_bd.shape[1]
    return pl.pallas_call(
        _rnn_kernel,
        grid=(batch, nt),
        in_specs=[
            pl.BlockSpec((ts, D_MODEL), lambda b, s: (b * nt + s, 0)),
            pl.BlockSpec((1, D_MODEL), const2),
            pl.BlockSpec((D_MODEL, 2 * D_RNN), const2),
            pl.BlockSpec((CONV_WIDTH, D_RNN), const2),
            pl.BlockSpec((1, D_RNN), const2),
            pl.BlockSpec((nblk, blk, blk), const3),
            pl.BlockSpec((1, D_RNN), const2),
            pl.BlockSpec((nblk, blk, blk), const3),
            pl.BlockSpec((1, D_RNN), const2),
            pl.BlockSpec((1, D_RNN), const2),
            pl.BlockSpec((D_RNN, D_MODEL), const2),
        ],
        out_specs=pl.BlockSpec((ts, D_MODEL), lambda b, s: (b * nt + s, 0)),
        out_shape=jax.ShapeDtypeStruct((t, D_MODEL), F32),
        scratch_shapes=[
            pltpu.VMEM((SUBLANES, D_RNN), F32),
            pltpu.VMEM((SUBLANES, D_RNN), F32),
            pltpu.VMEM((ts, D_RNN), F32),
            pltpu.VMEM((ts, D_RNN), F32),
            pltpu.VMEM((ts, D_RNN), F32),
        ],
        compiler_params=pltpu.CompilerParams(
            dimension_semantics=("parallel", "arbitrary"), vmem_limit_bytes=VMEM_LIMIT),
        name="rnn",
    )(x2, norm_g, w_xg, conv_w, conv_b, wa_bd, ba, wi_bd, bi, lam, w_out)


def _compress_kernel(x_ref, pos_ref, w1_ref, b1_ref, w2_ref, b2_ref, cos_ref, sin_ref, o_ref):
    n_rows = x_ref.shape[0] // CMP_STRIDE
    strided = [x_ref[pl.ds(r, n_rows, stride=CMP_STRIDE), :] for r in range(CMP_STRIDE)]
    low = _lane_iota((n_rows, LANES)) < HEAD_DIM
    groups = []
    for g in range(2):
        tiles = []
        for j in range(CMP_STRIDE // 2):
            even = strided[2 * j]
            odd = strided[2 * j + 1]
            if g == 0:
                tiles.append(jnp.where(low, even, pltpu.roll(odd, HEAD_DIM, 1)))
            else:
                tiles.append(jnp.where(low, pltpu.roll(even, HEAD_DIM, 1), odd))
        groups.append(jnp.concatenate(tiles, axis=1))
    x = jnp.concatenate(groups, axis=0)
    tr = x.shape[0]
    half = x.shape[1]
    first = _dot((x + pos_ref[0, 0:1, :]).astype(BF16), w1_ref[0, 0:half, :])
    second = _dot((x + pos_ref[0, 1:2, :]).astype(BF16), w1_ref[0, half:2 * half, :])
    hid = jax.nn.gelu(first + pltpu.roll(second, tr - 1, 0) + b1_ref[0])
    out = _dot(hid.astype(BF16), w2_ref[0]) + b2_ref[0]
    o_ref[0] = _rope(out, cos_ref[0], sin_ref[0]).astype(BF16)


def _compress(kcv, pos, w1, b1, w2, b2, cos, sin, batch, seq):
    pairs = KV_DIM // LANES
    tr = TR_CMP
    rows = batch * pairs * tr
    sel = lambda k, r: (k, 0, 0)
    return pl.pallas_call(
        _compress_kernel,
        grid=(2, batch * pairs),
        in_specs=[
            pl.BlockSpec((seq, LANES), lambda k, r: (r // pairs, k * pairs + r % pairs)),
            pl.BlockSpec((1, 2, pos.shape[2]), sel),
            pl.BlockSpec((1,) + w1.shape[1:], sel),
            pl.BlockSpec((1, 1, CMP_HIDDEN), sel),
            pl.BlockSpec((1,) + w2.shape[1:], sel),
            pl.BlockSpec((1, 1, LANES), sel),
            pl.BlockSpec((1, tr, LANES), sel),
            pl.BlockSpec((1, tr, LANES), sel),
        ],
        out_specs=pl.BlockSpec((1, tr, LANES), lambda k, r: (k, r, 0)),
        out_shape=jax.ShapeDtypeStruct((2, rows, LANES), BF16),
        compiler_params=pltpu.CompilerParams(
            dimension_semantics=("parallel", "parallel"), vmem_limit_bytes=VMEM_LIMIT),
        name="compress",
    )(kcv, pos, w1, b1, w2, b2, cos, sin)


def _lane_tile_max(s):
    tiles = [s[:, c * LANES:(c + 1) * LANES] for c in range(s.shape[1] // LANES)]
    while len(tiles) > 1:
        tiles = [jnp.maximum(tiles[k], tiles[k + 1]) if k + 1 < len(tiles) else tiles[k]
                 for k in range(0, len(tiles), 2)]
    return tiles[0]


def _stack_heads(q):
    tq = q.shape[0]
    low = _lane_iota((tq, LANES)) < HEAD_DIM
    zero = jnp.zeros((tq, LANES), BF16)
    heads = []
    for hh in range(HEADS_PER_GROUP):
        pair = q[:, (hh // 2) * LANES:(hh // 2 + 1) * LANES]
        heads.append(jnp.where(low if hh % 2 == 0 else jnp.logical_not(low), pair, zero))
    return jnp.concatenate(heads, axis=0)


def _select_tile(i, q, kc_ref, vc_ref, ng, m_ref):
    tq = q.shape[0]
    rows = HEADS_PER_GROUP * tq
    low = _lane_iota((tq, LANES)) < HEAD_DIM
    q4 = _stack_heads(q)
    lane4 = _lane_iota((rows, LANES))
    qpos4 = i * tq + (_row_iota((rows, LANES)) & (tq - 1))

    sc = _dot_t(q4, kc_ref[0])
    sc = jnp.where(lane4 * CMP_STRIDE + (CMP_BLOCK - 1) <= qpos4, sc, NEG_INF)
    mc = jnp.max(sc, axis=1, keepdims=True)
    pc = jnp.exp(sc - mc)
    pc = pc / jnp.sum(pc, axis=1, keepdims=True)
    pc = jnp.where(qpos4 >= CMP_BLOCK - 1, pc, 0.0)
    o_cmp = _dot(pc.astype(BF16), vc_ref[0])

    psum = pc[0:tq] + pc[tq:2 * tq] + pc[2 * tq:3 * tq] + pc[3 * tq:4 * tq]
    p_hi = psum.astype(BF16)
    rem = psum - p_hi.astype(F32)
    p_mid = rem.astype(BF16)
    p_lo = (rem - p_mid.astype(F32)).astype(BF16)
    cs = m_ref[...]
    p_slc = _dot(p_hi, cs) + _dot(p_mid, cs) + _dot(p_lo, cs)

    n_sel = SEL_BLOCK // 2
    p_slc_t = p_slc.T[0:n_sel, :]
    blk = _row_iota((n_sel, tq))
    tblk = (i * tq + _lane_iota((n_sel, tq))) >> 6
    forced = (blk == 0) | (blk == tblk) | (blk == tblk - 1)
    score = jnp.where(blk <= tblk, p_slc_t + jnp.where(forced, FORCE_BONUS, 0.0), -1.0)
    rank = jnp.zeros((n_sel, tq), F32)
    for j in range(n_sel):
        sj = score[j:j + 1, :]
        beats = (sj > score) | ((sj == score) & (blk > j))
        rank = rank + jnp.where(beats, 1.0, 0.0)
    bias_t = jnp.where(rank < SEL_TOPN, 0.0, NEG_INF)
    bias = jnp.concatenate([bias_t, jnp.zeros((LANES - n_sel, tq), F32)], axis=0).T.astype(BF16)

    gated = [ng[:, hh:hh + 1] * o_cmp[hh * tq:(hh + 1) * tq] for hh in range(HEADS_PER_GROUP)]
    o_pairs = jnp.concatenate([jnp.where(low, gated[0], gated[1]), jnp.where(low, gated[2], gated[3])], axis=1)
    return bias, o_pairs


def _attn_kernel(*refs):
    n_full = pl.program_id(2) // (CK_SEL // TQ)
    for n_chunks in range(1, refs[3].shape[0] // CK_SEL + 1):
        pl.when(n_full == n_chunks - 1)(functools.partial(_attn_step, n_chunks, *refs))


def _attn_step(n_chunks, q_ref, kc_ref, vc_ref, ks_ref, vs_ref, kw_ref, vw_ref, ng_ref, e_ref, m_ref, o_ref):
    i = pl.program_id(2)
    tq = q_ref.shape[0]
    rows = HEADS_PER_GROUP * tq
    ck = CK_SEL
    low = _lane_iota((tq, LANES)) < HEAD_DIM
    q4 = _stack_heads(q_ref[...])

    st = TQ_SELECT_TILE
    picks = [_select_tile(i * (tq // st) + sub, q_ref[sub * st:(sub + 1) * st, :], kc_ref, vc_ref,
                          ng_ref[sub * st:(sub + 1) * st, :], m_ref) for sub in range(tq // st)]
    bias = jnp.concatenate([p[0] for p in picks], axis=0)
    o_cmp = jnp.concatenate([p[1] for p in picks], axis=0)

    n_sub = tq // TK
    back = WINDOW // TK
    k_tiles, v_tiles = [], []
    for jj in range(back + n_sub):
        j = i * n_sub - back + jj
        off = pl.multiple_of(jnp.maximum(j, 0) * TK, TK)
        k_tiles.append(kw_ref[pl.ds(off, TK), :])
        v_tiles.append(vw_ref[pl.ds(off, TK), :])
    sw = _dot_t(q4, jnp.concatenate(k_tiles, axis=0))
    on_or_after = _row_iota((TK, TK)) >= _lane_iota((TK, TK))
    sw_tiles = []
    for jj in range(back + n_sub):
        blocks = []
        for blk_i in range(rows // TK):
            piece = sw[blk_i * TK:(blk_i + 1) * TK, jj * TK:(jj + 1) * TK]
            tiles_back = blk_i % n_sub + back - jj
            if tiles_back == 0:
                piece = jnp.where(on_or_after, piece, NEG_INF)
            elif tiles_back == back:
                piece = jnp.where(on_or_after, NEG_INF, piece)
            elif tiles_back < 0 or tiles_back > back:
                piece = jnp.full((TK, TK), NEG_INF, F32)
            blocks.append(piece)
        tile = jnp.concatenate(blocks, axis=0)
        if jj < back:
            tile = jnp.where(i * n_sub - back + jj >= 0, tile, NEG_INF)
        sw_tiles.append(tile)
    sw = jnp.concatenate(sw_tiles, axis=1)
    m_win = jnp.max(_lane_tile_max(sw), axis=1, keepdims=True)
    acc_win = _dot(jnp.exp(sw - m_win).astype(BF16), jnp.concatenate(v_tiles, axis=0))

    qa = jnp.concatenate([q4, jnp.concatenate([bias] * HEADS_PER_GROUP, axis=0)], axis=1)
    rel_q = i * tq - (n_chunks - 1) * ck + (_row_iota((rows, ck)) & (tq - 1))
    m_run = None
    acc_sel = None
    for c in range(n_chunks):
        ka = jnp.concatenate([ks_ref[c * ck:(c + 1) * ck, :], e_ref[c * ck:(c + 1) * ck, :]], axis=1)
        s = _dot_t(qa, ka)
        if c == n_chunks - 1:
            s = jnp.where(_lane_iota((rows, ck)) <= rel_q, s, NEG_INF)
        m_new = jnp.max(_lane_tile_max(s), axis=1, keepdims=True)
        if c > 0:
            m_new = jnp.maximum(m_run, m_new)
            acc_sel = acc_sel * jnp.exp(m_run - m_new)
        part = _dot(jnp.exp(s - m_new).astype(BF16), vs_ref[c * ck:(c + 1) * ck, :])
        acc_sel = part if c == 0 else acc_sel + part
        m_run = m_new

    ng = ng_ref[...]
    outs = []
    for hh in range(HEADS_PER_GROUP):
        r0 = hh * tq
        col = lambda br: ng[:, br * HEADS_PER_GROUP + hh:br * HEADS_PER_GROUP + hh + 1]
        parts = []
        for acc in (acc_sel[r0:r0 + tq], acc_win[r0:r0 + tq]):
            swapped = pltpu.roll(acc, HEAD_DIM, 1)
            parts.append(acc / swapped if hh % 2 == 0 else swapped / acc)
        outs.append(col(1) * parts[0] + col(2) * parts[1])
    o_ref[...] = (o_cmp + jnp.concatenate(
        [jnp.where(low, outs[0], outs[1]), jnp.where(low, outs[2], outs[3])], axis=1)).astype(o_ref.dtype)


def _attn(q, kvc, ksd, vsd, kwd, vwd, ng, e_mat, cs_mat, batch, seq):
    t = q.shape[0]
    tq = TQ
    nq = seq // tq
    qrow = lambda b, g, i: (b * nq + i, g)
    kv = lambda b, g, i: (b, g)
    const = lambda b, g, i: (0, 0)
    return pl.pallas_call(
        _attn_kernel,
        grid=(batch, N_KV_GROUPS, nq),
        in_specs=[
            pl.BlockSpec((tq, HEADS_PER_GROUP * HEAD_DIM), qrow),
            pl.BlockSpec((1, N_CMP_PAD, LANES), lambda b, g, i: (0, b * N_KV_GROUPS + g, 0)),
            pl.BlockSpec((1, N_CMP_PAD, LANES), lambda b, g, i: (1, b * N_KV_GROUPS + g, 0)),
            pl.BlockSpec((seq, LANES), kv),
            pl.BlockSpec((seq, LANES), kv),
            pl.BlockSpec((seq, LANES), kv),
            pl.BlockSpec((seq, LANES), kv),
            pl.BlockSpec((tq, LANES), qrow),
            pl.BlockSpec((seq, LANES), const),
            pl.BlockSpec((N_CMP_PAD, LANES), const),
        ],
        out_specs=pl.BlockSpec((tq, HEADS_PER_GROUP * HEAD_DIM), qrow),
        out_shape=jax.ShapeDtypeStruct((t, Q_DIM), BF16),
        compiler_params=pltpu.CompilerParams(
            dimension_semantics=("parallel", "parallel", "arbitrary"), vmem_limit_bytes=VMEM_LIMIT),
        name="attn",
    )(q, kvc, kvc, ksd, vsd, kwd, vwd, ng, e_mat, cs_mat)


def _post_kernel(o_ref, ya_ref, x_ref, g1_ref, wg_ref, wn_ref, wm_ref, g2_ref, wr_hi_ref, wr_lo_ref, rb_ref,
                 xa_ref):
    hn = _normed(x_ref[...], g1_ref[...])
    gate_a = jax.nn.sigmoid(_dot(hn, wg_ref[:, 0:D_MODEL]))
    gate_b = jax.nn.sigmoid(_dot(hn, wg_ref[:, D_MODEL:2 * D_MODEL]))
    y_b = _dot(o_ref[...], wn_ref[...])
    mixed = gate_a * ya_ref[...] + gate_b * y_b
    x1 = x_ref[...] + _dot(mixed.astype(BF16), wm_ref[...])
    xa_ref[:, 0:D_MODEL] = x1
    h2 = x1 * lax.rsqrt(jnp.mean(x1 * x1, axis=-1, keepdims=True) + EPS) * g2_ref[...]
    h_hi = h2.astype(BF16)
    h_lo = (h2 - h_hi.astype(F32)).astype(BF16)
    logits = (_dot(h_hi, wr_hi_ref[...]) + _dot(h_lo, wr_hi_ref[...]) + _dot(h_hi, wr_lo_ref[...])
              + rb_ref[...])

    lane = _lane_iota(logits.shape).astype(F32)
    is_grp = (lane >= N_EXPERTS) & (lane < N_EXPERTS + N_GROUPS)
    gl = jnp.where(is_grp, logits, NEG_INF)
    ge = jnp.exp(gl - jnp.max(gl, axis=1, keepdims=True))
    gp = ge / jnp.sum(ge, axis=1, keepdims=True)
    g_w = jnp.max(gp, axis=1, keepdims=True)
    big = float(4 * LANES)
    g_first = jnp.min(jnp.where(is_grp & (gp == g_w), lane, big), axis=1, keepdims=True)
    grp_lo = (g_first - N_EXPERTS) * EXPERTS_PER_GROUP

    in_grp = (lane >= grp_lo) & (lane < grp_lo + EXPERTS_PER_GROUP)
    el = jnp.where(in_grp, logits, NEG_INF)
    ee = jnp.exp(el - jnp.max(el, axis=1, keepdims=True))
    ep = ee / jnp.sum(ee, axis=1, keepdims=True)
    w1 = jnp.max(ep, axis=1, keepdims=True)
    i1 = jnp.min(jnp.where(in_grp & (ep == w1), lane, big), axis=1, keepdims=True)
    rest = jnp.where(in_grp & (lane != i1), ep, -1.0)
    w2 = jnp.max(rest, axis=1, keepdims=True)
    i2 = jnp.min(jnp.where(rest == w2, lane, big), axis=1, keepdims=True)
    den = w1 + w2
    comb = jnp.where(lane == i1, g_w * (w1 / den), jnp.where(lane == i2, g_w * (w2 / den), 0.0))
    xa_ref[:, D_MODEL:D_MODEL + LANES] = jnp.where(lane == GROUP_LANE, g_first - N_EXPERTS, comb)


def _post(o_nsa, y_a, x2, g1, w_mg, w_nsa, w_mix, g2, wr_hi, wr_lo, rb):
    t = x2.shape[0]
    tm = TM_POST
    row = lambda i: (i, 0)
    const = lambda i: (0, 0)
    return pl.pallas_call(
        _post_kernel,
        grid=(t // tm,),
        in_specs=[
            pl.BlockSpec((tm, Q_DIM), row),
            pl.BlockSpec((tm, D_MODEL), row),
            pl.BlockSpec((tm, D_MODEL), row),
            pl.BlockSpec((1, D_MODEL), const),
            pl.BlockSpec((D_MODEL, 2 * D_MODEL), const),
            pl.BlockSpec((Q_DIM, D_MODEL), const),
            pl.BlockSpec((D_MODEL, D_MODEL), const),
            pl.BlockSpec((1, D_MODEL), const),
            pl.BlockSpec((D_MODEL, LANES), const),
            pl.BlockSpec((D_MODEL, LANES), const),
            pl.BlockSpec((1, LANES), const),
        ],
        out_specs=pl.BlockSpec((tm, D_AUG), row),
        out_shape=jax.ShapeDtypeStruct((t, D_AUG), F32),
        compiler_params=pltpu.CompilerParams(
            dimension_semantics=("parallel",), vmem_limit_bytes=VMEM_LIMIT),
        name="post",
    )(o_nsa, y_a, x2, g1, w_mg, w_nsa, w_mix, g2, wr_hi, wr_lo, rb)


def _plan_kernel(rec_ref, tri_ref, info_ref, counts_ref, carry_s):
    @pl.when(pl.program_id(0) == 0)
    def _():
        carry_s[...] = jnp.zeros_like(carry_s)

    rec = rec_ref[...]
    lane = _lane_iota(rec.shape)
    gid = rec[:, GROUP_LANE:GROUP_LANE + 1]
    onehot = jnp.where(lane.astype(F32) == gid, 1.0, 0.0)
    before = _dot(tri_ref[...], onehot.astype(BF16)) + carry_s[0:1, :]
    rank = jnp.sum(onehot * before, axis=1, keepdims=True)
    info_ref[...] = jnp.where(lane == 0, rank, jnp.where(lane == 1, gid, 0.0))
    carry_s[...] = carry_s[...] + jnp.sum(onehot, axis=0, keepdims=True)
    counts_ref[...] = carry_s[...]


def _plan(xa, tri):
    t = xa.shape[0]
    tm = TM_PLAN
    return pl.pallas_call(
        _plan_kernel,
        grid=(t // tm,),
        in_specs=[
            pl.BlockSpec((tm, LANES), lambda i: (i, D_MODEL // LANES)),
            pl.BlockSpec((tm, tm), lambda i: (0, 0)),
        ],
        out_specs=(
            pl.BlockSpec((tm, LANES), lambda i: (i, 0)),
            pl.BlockSpec((SUBLANES, LANES), lambda i: (0, 0)),
        ),
        out_shape=(
            jax.ShapeDtypeStruct((t, LANES), F32),
            jax.ShapeDtypeStruct((SUBLANES, LANES), F32),
        ),
        scratch_shapes=[pltpu.VMEM((SUBLANES, LANES), F32)],
        compiler_params=pltpu.CompilerParams(
            dimension_semantics=("arbitrary",), vmem_limit_bytes=VMEM_LIMIT),
        name="plan",
    )(xa, tri)


def _row_copies(n_rows, make_copy):
    def start(r, carry):
        make_copy(r).start()
        return carry

    def wait(r, carry):
        make_copy(0).wait()
        return carry

    lax.fori_loop(0, n_rows, start, 0, unroll=ROW_COPY_UNROLL)
    lax.fori_loop(0, n_rows, wait, 0, unroll=ROW_COPY_UNROLL)


def _dispatch_kernel(gid_ref, rank_ref, base_ref, xa_ref, zero_hbm, xs_hbm, sem):
    del zero_hbm
    first = pl.program_id(0) * xa_ref.shape[0]

    def copy(r):
        slot = base_ref[gid_ref[first + r]] + rank_ref[first + r]
        return pltpu.make_async_copy(xa_ref.at[pl.ds(r, 1), :], xs_hbm.at[pl.ds(slot, 1), :], sem)

    _row_copies(xa_ref.shape[0], copy)


def _dispatch(gid, rank, base, xa, n_sorted):
    t = xa.shape[0]
    tm = TM_ROWS
    zeros = jnp.zeros((n_sorted, D_AUG), F32)
    return pl.pallas_call(
        _dispatch_kernel,
        grid_spec=pltpu.PrefetchScalarGridSpec(
            num_scalar_prefetch=3,
            grid=(t // tm,),
            in_specs=[
                pl.BlockSpec((tm, D_AUG), lambda i, *_: (i, 0)),
                pl.BlockSpec(memory_space=pl.ANY),
            ],
            out_specs=pl.BlockSpec(memory_space=pl.ANY),
            scratch_shapes=[pltpu.SemaphoreType.DMA],
        ),
        out_shape=jax.ShapeDtypeStruct((n_sorted, D_AUG), F32),
        input_output_aliases={4: 0},
        compiler_params=pltpu.CompilerParams(
            dimension_semantics=("arbitrary",), vmem_limit_bytes=VMEM_LIMIT),
        name="dispatch",
    )(gid, rank, base, xa, zeros)


def _experts_kernel(widx_ref, used_ref, tgrp_ref, xs_ref, wg_ref, wu_ref, wd_ref, g2_ref, gf_ref,
                    fs_ref, h_s, y_s):
    del widx_ref
    j = pl.program_id(0)
    e = pl.program_id(1)
    used = used_ref[j] == 1

    @pl.when(used & (e == 0))
    def _():
        x1 = xs_ref[:, 0:D_MODEL]
        h2 = x1 * lax.rsqrt(jnp.mean(x1 * x1, axis=-1, keepdims=True) + EPS) * g2_ref[...]
        h_s[...] = h2.astype(BF16)
        y_s[...] = jnp.zeros_like(y_s)

    @pl.when(used)
    def _():
        h = h_s[...]
        act = jax.nn.silu(_dot(h, wg_ref[0].astype(BF16))) * _dot(h, wu_ref[0].astype(BF16))
        y = _dot(act.astype(BF16), wd_ref[0].astype(BF16))
        rec = xs_ref[:, D_MODEL:D_AUG]
        col = tgrp_ref[j] * EXPERTS_PER_GROUP + e
        weight = jnp.sum(jnp.where(_lane_iota(rec.shape) == col, rec, 0.0), axis=1, keepdims=True)
        y_s[...] += weight * y

    @pl.when(used & (e == EXPERTS_PER_GROUP - 1))
    def _():
        x = xs_ref[:, 0:D_MODEL] + y_s[...]
        fs_ref[...] = x * lax.rsqrt(jnp.mean(x * x, axis=-1, keepdims=True) + EPS) * gf_ref[...]

    @pl.when(jnp.logical_not(used) & (e == EXPERTS_PER_GROUP - 1))
    def _():
        fs_ref[...] = jnp.zeros_like(fs_ref)


def _experts(widx, used, tgrp, xs, wg, wu, wd, g2, gf):
    n_sorted = xs.shape[0]
    tm = TM_EXP
    row = lambda j, e, *_: (j, 0)
    wsel = lambda j, e, widx, used, tgrp: (widx[j * EXPERTS_PER_GROUP + e], 0, 0)
    const = lambda j, e, *_: (0, 0)
    return pl.pallas_call(
        _experts_kernel,
        grid_spec=pltpu.PrefetchScalarGridSpec(
            num_scalar_prefetch=3,
            grid=(n_sorted // tm, EXPERTS_PER_GROUP),
            in_specs=[
                pl.BlockSpec((tm, D_AUG), row),
                pl.BlockSpec((1, D_MODEL, D_EXPERT), wsel),
                pl.BlockSpec((1, D_MODEL, D_EXPERT), wsel),
                pl.BlockSpec((1, D_EXPERT, D_MODEL), wsel),
                pl.BlockSpec((1, D_MODEL), const),
                pl.BlockSpec((1, D_MODEL), const),
            ],
            out_specs=pl.BlockSpec((tm, D_MODEL), row),
            scratch_shapes=[pltpu.VMEM((tm, D_MODEL), BF16), pltpu.VMEM((tm, D_MODEL), F32)],
        ),
        out_shape=jax.ShapeDtypeStruct((n_sorted, D_MODEL), F32),
        compiler_params=pltpu.CompilerParams(
            dimension_semantics=("arbitrary", "arbitrary"), vmem_limit_bytes=VMEM_LIMIT),
        name="experts",
    )(widx, used, tgrp, xs, wg, wu, wd, g2, gf)


def _combine_kernel(gid_ref, rank_ref, base_ref, fs_hbm, o_ref, sem):
    first = pl.program_id(0) * o_ref.shape[0]

    def copy(r):
        slot = base_ref[gid_ref[first + r]] + rank_ref[first + r]
        return pltpu.make_async_copy(fs_hbm.at[pl.ds(slot, 1), :], o_ref.at[pl.ds(r, 1), :], sem)

    _row_copies(o_ref.shape[0], copy)


def _combine(gid, rank, base, fs, t):
    tm = TM_ROWS
    return pl.pallas_call(
        _combine_kernel,
        grid_spec=pltpu.PrefetchScalarGridSpec(
            num_scalar_prefetch=3,
            grid=(t // tm,),
            in_specs=[pl.BlockSpec(memory_space=pl.ANY)],
            out_specs=pl.BlockSpec((tm, D_MODEL), lambda i, *_: (i, 0)),
            scratch_shapes=[pltpu.SemaphoreType.DMA],
        ),
        out_shape=jax.ShapeDtypeStruct((t, D_MODEL), F32),
        compiler_params=pltpu.CompilerParams(
            dimension_semantics=("arbitrary",), vmem_limit_bytes=VMEM_LIMIT),
        name="combine",
    )(gid, rank, base, fs)


def _moe_sorted(xa, wg, wu, wd, g2, gf):
    t = xa.shape[0]
    tri = jnp.asarray(np.tril(np.ones((TM_PLAN, TM_PLAN), np.float32), -1), dtype=BF16)
    info, counts = _plan(xa, tri)
    rank = info[:, 0].astype(jnp.int32)
    gid = info[:, 1].astype(jnp.int32)
    counts = counts[0, :N_GROUPS].astype(jnp.int32)

    n_tiles = t // TM_EXP + N_GROUPS
    tiles_g = (counts + TM_EXP - 1) // TM_EXP
    tile_end = jnp.cumsum(tiles_g)
    base = jnp.pad((tile_end - tiles_g) * TM_EXP, (0, SUBLANES - N_GROUPS))
    tile_ids = jnp.arange(n_tiles, dtype=jnp.int32)
    used = (tile_ids < tile_end[-1]).astype(jnp.int32)
    last = jnp.maximum(tile_end[-1] - 1, 0)
    tgrp = jnp.sum((jnp.minimum(tile_ids, last)[:, None] >= tile_end[None, :]).astype(jnp.int32), axis=1)
    step_e = jnp.arange(EXPERTS_PER_GROUP, dtype=jnp.int32)[None, :]
    widx = jnp.where(used[:, None] == 1, tgrp[:, None] * EXPERTS_PER_GROUP + step_e,
                     tgrp[:, None] * EXPERTS_PER_GROUP + EXPERTS_PER_GROUP - 1).reshape(-1)

    xs = _dispatch(gid, rank, base, xa, n_tiles * TM_EXP)
    fs = _experts(widx, used, tgrp, xs, wg, wu, wd, g2, gf)
    return _combine(gid, rank, base, fs, t)


def _split_w_in(w):
    sizes = [D_RNN, D_RNN, Q_DIM] + [KV_DIM] * 6 + [3 * N_HEADS, 2 * D_MODEL]
    pts = np.cumsum(sizes)[:-1]
    return jnp.split(w, [int(p) for p in pts], axis=-1)


def _proj_weights(w_in):
    xr, gr, q, kc, vc, ksl, vsl, kw, vw, nsa_g, merge_g = _split_w_in(w_in)
    ng = nsa_g.reshape(D_MODEL, 3, N_KV_GROUPS, HEADS_PER_GROUP).transpose(0, 2, 1, 3)
    ng = ng.reshape(D_MODEL, N_KV_GROUPS, 3 * HEADS_PER_GROUP)
    ng = jnp.pad(ng, ((0, 0), (0, 0), (0, LANES - 3 * HEADS_PER_GROUP))).reshape(D_MODEL, N_KV_GROUPS * LANES)
    w_attn = jnp.concatenate([q, kc, vc, ksl, vsl, kw, vw, ng], axis=1)
    return (jnp.concatenate([xr, gr], axis=1).astype(BF16), w_attn.astype(BF16), merge_g.astype(BF16))


def _rope_tables(pos, width):
    inv_freq = ROPE_THETA ** (-(jnp.arange(0, HEAD_DIM, 2, dtype=F32) / HEAD_DIM))
    ang = pos.astype(F32)[:, None] * inv_freq[None, :]
    cos, sin = jnp.cos(ang), jnp.sin(ang)
    reps = width // HEAD_DIM
    return jnp.tile(jnp.concatenate([cos, cos], axis=1), (1, reps)), jnp.tile(jnp.concatenate([-sin, sin], axis=1), (1, reps))


def _block_diag(w, per):
    nb, d = w.shape[0], w.shape[1]
    w = w.reshape(nb // per, per, d, d)
    eye = jnp.eye(per, dtype=w.dtype)
    return jnp.einsum('npij,pq->npiqj', w, eye).reshape(nb // per, per * d, per * d)


def _cmp_to_sel():
    n_sel_pad = LANES
    c0 = np.arange(N_CMP_PAD) * CMP_STRIDE
    s0 = np.arange(n_sel_pad) * SEL_BLOCK
    ov = np.minimum(c0[:, None] + CMP_BLOCK, s0[None, :] + SEL_BLOCK) - np.maximum(c0[:, None], s0[None, :])
    m = np.clip(ov, 0, None) / CMP_BLOCK
    m[:, SEL_BLOCK // 2:] = 0.0
    m[N_CMP_PAD - 1, :] = 0.0
    return m.astype(np.float32)


def kernel(x, norm1_g, w_in, conv_w, conv_b, lru_wa, lru_ba, lru_wi, lru_bi, lru_lambda, w_rnn_out, cmpk_pos, cmpk_w1, cmpk_b1, cmpk_w2, cmpk_b2, cmpv_pos, cmpv_w1, cmpv_b1, cmpv_w2, cmpv_b2, w_nsa_out, w_mix_out, norm2_g, router_group_w, router_group_b, router_expert_w, router_expert_b, expert_w_gate, expert_w_up, expert_w_down, final_norm_g):
    batch, seq, _ = x.shape
    t = batch * seq
    assert w_in.shape[0] == 1, "the final norm is fused into the expert kernel: single layer only"
    x2 = x.reshape(t, D_MODEL)

    cos, sin = _rope_tables(jnp.arange(seq), LANES)
    cmp_ends = jnp.arange(N_CMP_PAD) * CMP_STRIDE + (CMP_BLOCK - 1)
    ccos, csin = _rope_tables(cmp_ends, LANES)
    reps = TR_CMP // N_CMP_PAD
    cmp_cos = jnp.stack([jnp.tile(ccos, (reps, 1)), jnp.ones((TR_CMP, LANES), F32)])
    cmp_sin = jnp.stack([jnp.tile(csin, (reps, 1)), jnp.zeros((TR_CMP, LANES), F32)])
    key_blk = np.arange(seq)[:, None] // SEL_BLOCK
    e_mat = jnp.asarray((key_blk == np.arange(LANES)[None, :]).astype(np.float32), dtype=BF16)
    cs_mat = jnp.asarray(_cmp_to_sel(), dtype=BF16)

    l = 0
    w_xg, w_attn, w_mg = _proj_weights(w_in[l])
    g1 = norm1_g[l][None, :]
    q, kcv, ksd, vsd, kwd, vwd, ng = _inproj(x2, g1, w_attn, cos, sin, seq)

    y_a = _rnn(x2, g1, w_xg, conv_w[l], conv_b[l][None, :],
               _block_diag(lru_wa[l], 4).astype(BF16), lru_ba[l][None, :],
               _block_diag(lru_wi[l], 4).astype(BF16), lru_bi[l][None, :],
               lru_lambda[l][None, :], w_rnn_out[l].astype(BF16), batch, seq)

    half = CMP_STRIDE * HEAD_DIM
    pos = jnp.stack([cmpk_pos[l].reshape(2, half), cmpv_pos[l].reshape(2, half)])
    w1 = jnp.stack([cmpk_w1[l], cmpv_w1[l]]).astype(BF16)
    b1 = jnp.stack([cmpk_b1[l], cmpv_b1[l]])[:, None, :]
    w2 = jnp.stack([cmpk_w2[l], cmpv_w2[l]])
    w2 = jnp.concatenate([w2, w2], axis=2).astype(BF16)
    b2 = jnp.stack([cmpk_b2[l], cmpv_b2[l]])
    b2 = jnp.concatenate([b2, b2], axis=1)[:, None, :]
    kvc = _compress(kcv, pos, w1, b1, w2, b2, cmp_cos, cmp_sin, batch, seq)

    o_nsa = _attn(q, kvc, ksd, vsd, kwd, vwd, ng, e_mat, cs_mat, batch, seq)

    wr = jnp.concatenate([
        router_expert_w[l].transpose(1, 0, 2).reshape(D_MODEL, N_EXPERTS),
        router_group_w[l],
        jnp.zeros((D_MODEL, LANES - N_EXPERTS - N_GROUPS), F32)], axis=1)
    wr_hi = wr.astype(BF16)
    wr_lo = (wr - wr_hi.astype(F32)).astype(BF16)
    rb = jnp.concatenate([router_expert_b[l].reshape(N_EXPERTS), router_group_b[l],
                          jnp.zeros((LANES - N_EXPERTS - N_GROUPS,), F32)])[None, :]
    xa = _post(o_nsa, y_a, x2, g1, w_mg, w_nsa_out[l].astype(BF16), w_mix_out[l].astype(BF16),
               norm2_g[l][None, :], wr_hi, wr_lo, rb)

    out = _moe_sorted(xa, expert_w_gate[l], expert_w_up[l], expert_w_down[l],
                      norm2_g[l][None, :], final_norm_g[None, :])
    return out.reshape(batch, seq, D_MODEL)
```

```python
import functools

import numpy as np
import jax
import jax.numpy as jnp
from jax import lax
from jax.experimental import pallas as pl
from jax.experimental.pallas import tpu as pltpu

D_MODEL = 1024
D_RNN = 1024
RNN_BLOCKS = 16
RNN_BLOCK_DIM = D_RNN // RNN_BLOCKS
CONV_WIDTH = 4
LRU_C = 8.0
N_HEADS = 16
HEAD_DIM = 64
HALF_DIM = HEAD_DIM // 2
N_KV_GROUPS = 4
HEADS_PER_GROUP = N_HEADS // N_KV_GROUPS
Q_DIM = N_HEADS * HEAD_DIM
KV_DIM = N_KV_GROUPS * HEAD_DIM
CMP_BLOCK = 32
CMP_STRIDE = 16
CMP_HIDDEN = 256
SEL_BLOCK = 64
SEL_TOPN = 16
WINDOW = 512
ROPE_THETA = 10000.0
FORCE_BONUS = 1e4
NEG_INF = -1e30
N_GROUPS = 4
EXPERTS_PER_GROUP = 4
N_EXPERTS = N_GROUPS * EXPERTS_PER_GROUP
D_EXPERT = 512
EPS = 1e-6

Q_SCALE = HEAD_DIM ** -0.5 * float(np.log2(np.e))

LANES = 128
SUBLANES = 8
VMEM_LIMIT = 56 * 1024 * 1024

BF16 = jnp.bfloat16
F32 = jnp.float32

COL_Q = 0
COL_KCV = COL_Q + Q_DIM
COL_KS = COL_KCV + 2 * KV_DIM
COL_VS = COL_KS + KV_DIM
COL_KW = COL_VS + KV_DIM
COL_VW = COL_KW + KV_DIM
COL_NG = COL_VW + KV_DIM
N_PROJ = COL_NG + N_KV_GROUPS * LANES

TM_PROJ = 512
TS_RNN = 512
TR_CMP = 256
TQ = 256
TQ_SELECT = 512
TQ_SELECT_TILE = 128
TK = 128
CK_SEL = 512
TM_POST = 512
TM_PLAN = 1024
TM_ROWS = 512
TM_EXP = 1024
ROW_COPY_UNROLL = 8
D_AUG = D_MODEL + LANES
GROUP_LANE = N_EXPERTS
N_CMP_PAD = 128


def _dot(a, b):
    return jnp.dot(a, b, preferred_element_type=F32)


def _dot_t(a, b):
    return lax.dot_general(a, b, (((1,), (1,)), ((), ())), preferred_element_type=F32)


def _lane_iota(shape):
    return lax.broadcasted_iota(jnp.int32, shape, len(shape) - 1)


def _row_iota(shape):
    return lax.broadcasted_iota(jnp.int32, shape, 0)


def _rope(x, cos, sin_signed):
    width = x.shape[-1]
    reps = width // cos.shape[-1]
    if reps > 1:
        cos = jnp.concatenate([cos] * reps, axis=1)
        sin_signed = jnp.concatenate([sin_signed] * reps, axis=1)
    first_half = (_lane_iota(x.shape) & (HEAD_DIM - 1)) < HALF_DIM
    partner = jnp.where(first_half, pltpu.roll(x, width - HALF_DIM, 1), pltpu.roll(x, HALF_DIM, 1))
    return x * cos + partner * sin_signed


def _spread_heads(x, fill=None):
    out = []
    low = _lane_iota((x.shape[0], LANES)) < HEAD_DIM
    for c in range(x.shape[1] // LANES):
        xc = x[:, c * LANES:(c + 1) * LANES]
        rolled = pltpu.roll(xc, HEAD_DIM, 1)
        out.append(jnp.where(low, xc, rolled if fill is None else fill))
        out.append(jnp.where(low, rolled, xc if fill is None else fill))
    return jnp.concatenate(out, axis=1)


def _sigmoid(x):
    return 0.5 * (jnp.tanh(0.5 * x) + 1.0)


def _normed(x, g):
    return (x * lax.rsqrt(jnp.mean(x * x, axis=-1, keepdims=True) + EPS) * g).astype(BF16)


def _inproj_kernel(x_ref, g_ref, w_ref, cos_ref, sin_ref,
                   q_ref, kcv_ref, ks_ref, vs_ref, kw_ref, vw_ref, ng_ref):
    h = _normed(x_ref[...], g_ref[...])
    cos = cos_ref[...]
    sin = sin_ref[...]

    def mm(lo, width):
        return _dot(h, w_ref[:, lo:lo + width])

    q_ref[...] = (_rope(mm(COL_Q, Q_DIM), cos, sin) * Q_SCALE).astype(BF16)
    kcv_ref[...] = mm(COL_KCV, 2 * KV_DIM)
    ks_ref[...] = _spread_heads(_rope(mm(COL_KS, KV_DIM), cos, sin)).astype(BF16)
    vs_ref[...] = _spread_heads(mm(COL_VS, KV_DIM), 1.0).astype(BF16)
    kw_ref[...] = _spread_heads(_rope(mm(COL_KW, KV_DIM), cos, sin)).astype(BF16)
    vw_ref[...] = _spread_heads(mm(COL_VW, KV_DIM), 1.0).astype(BF16)
    ng_ref[...] = jax.nn.sigmoid(mm(COL_NG, N_KV_GROUPS * LANES))


def _inproj(x2, norm_g, w_proj, cos, sin, seq):
    t = x2.shape[0]
    tm = TM_PROJ
    pos_blocks = seq // tm
    row = lambda i: (i, 0)
    const = lambda i: (0, 0)
    out_shape = (
        jax.ShapeDtypeStruct((t, Q_DIM), BF16),
        jax.ShapeDtypeStruct((t, 2 * KV_DIM), F32),
        jax.ShapeDtypeStruct((t, N_KV_GROUPS * LANES), BF16),
        jax.ShapeDtypeStruct((t, N_KV_GROUPS * LANES), BF16),
        jax.ShapeDtypeStruct((t, N_KV_GROUPS * LANES), BF16),
        jax.ShapeDtypeStruct((t, N_KV_GROUPS * LANES), BF16),
        jax.ShapeDtypeStruct((t, N_KV_GROUPS * LANES), F32),
    )
    return pl.pallas_call(
        _inproj_kernel,
        grid=(t // tm,),
        in_specs=[
            pl.BlockSpec((tm, D_MODEL), row),
            pl.BlockSpec((1, D_MODEL), const),
            pl.BlockSpec((D_MODEL, N_PROJ), const, pipeline_mode=pl.Buffered(1)),
            pl.BlockSpec((tm, LANES), lambda i: (i % pos_blocks, 0)),
            pl.BlockSpec((tm, LANES), lambda i: (i % pos_blocks, 0)),
        ],
        out_specs=tuple(pl.BlockSpec((tm, s.shape[1]), row) for s in out_shape),
        out_shape=out_shape,
        compiler_params=pltpu.CompilerParams(
            dimension_semantics=("parallel",), vmem_limit_bytes=VMEM_LIMIT),
        name="inproj",
    )(x2, norm_g, w_proj, cos, sin)


def _rnn_kernel(xin_ref, g1_ref, wx_ref, cw_ref, cb_ref, wa_ref, ba_ref, wi_ref, bi_ref, lam_ref, wo_ref,
                y_ref, tail_s, carry_s, a_s, u_s, h_s):
    ts = xin_ref.shape[0]

    @pl.when(pl.program_id(1) == 0)
    def _():
        tail_s[...] = jnp.zeros_like(tail_s)
        carry_s[...] = jnp.zeros_like(carry_s)

    hn = _normed(xin_ref[...], g1_ref[...])
    x = _dot(hn, wx_ref[:, 0:D_RNN])
    gate_pre = _dot(hn, wx_ref[:, D_RNN:2 * D_RNN])
    xext = jnp.concatenate([tail_s[...], x], axis=0)
    tail_s[...] = x[ts - SUBLANES:ts, :]
    conv = cb_ref[...]
    for k in range(CONV_WIDTH):
        back = CONV_WIDTH - 1 - k
        shifted = xext if back == 0 else pltpu.roll(xext, back, 0)
        conv = conv + cw_ref[k:k + 1, :] * shifted[SUBLANES:SUBLANES + ts, :]

    cb16 = conv.astype(BF16)
    blk = wa_ref.shape[1]
    r_pre = jnp.concatenate(
        [_dot(cb16[:, j * blk:(j + 1) * blk], wa_ref[j]) for j in range(D_RNN // blk)], axis=1)
    i_pre = jnp.concatenate(
        [_dot(cb16[:, j * blk:(j + 1) * blk], wi_ref[j]) for j in range(D_RNN // blk)], axis=1)
    r = _sigmoid(r_pre + ba_ref[...])
    gate_i = _sigmoid(i_pre + bi_ref[...])
    neg_lam = -lam_ref[...]
    softplus = jnp.maximum(neg_lam, 0.0) + jnp.log1p(jnp.exp(-jnp.abs(neg_lam)))
    log_a = (-LRU_C) * r * softplus
    a = jnp.exp(log_a)
    a_s[...] = a
    u_s[...] = jnp.sqrt(-jnp.tanh(log_a) * (a * a + 1.0)) * (gate_i * conv)

    row = _row_iota((SUBLANES, D_RNN))

    def body(k, carry):
        off = pl.multiple_of(k * SUBLANES, SUBLANES)
        a = a_s[pl.ds(off, SUBLANES), :]
        b = u_s[pl.ds(off, SUBLANES), :]
        for sh in (1, 2, 4):
            keep = row >= sh
            a_prev = jnp.where(keep, pltpu.roll(a, sh, 0), 1.0)
            b_prev = jnp.where(keep, pltpu.roll(b, sh, 0), 0.0)
            b = a * b_prev + b
            a = a * a_prev
        h = a * carry + b
        h_s[pl.ds(off, SUBLANES), :] = h
        return jnp.broadcast_to(h[SUBLANES - 1:SUBLANES, :], (SUBLANES, D_RNN))

    carry_s[...] = lax.fori_loop(0, ts // SUBLANES, body, carry_s[...])
    gated = (jax.nn.gelu(gate_pre) * h_s[...]).astype(BF16)
    y_ref[...] = _dot(gated, wo_ref[...])


def _rnn(x2, norm_g, w_xg, conv_w, conv_b, wa_bd, ba, wi_bd, bi, lam, w_out, batch, seq):
    t = x2.shape[0]
    ts = TS_RNN
    nt = seq // ts
    const2 = lambda b, s: (0, 0)
    const3 = lambda b, s: (0, 0, 0)
    nblk, blk = wa_bd.shape[0], wa_bd.shape[1]
    return pl.pallas_call(
        _rnn_kernel,
        grid=(batch, nt),
        in_specs=[
            pl.BlockSpec((ts, D_MODEL), lambda b, s: (b * nt + s, 0)),
            pl.BlockSpec((1, D_MODEL), const2),
            pl.BlockSpec((D_MODEL, 2 * D_RNN), const2),
            pl.BlockSpec((CONV_WIDTH, D_RNN), const2),
            pl.BlockSpec((1, D_RNN), const2),
            pl.BlockSpec((nblk, blk, blk), const3),
            pl.BlockSpec((1, D_RNN), const2),
            pl.BlockSpec((nblk, blk, blk), const3),
            pl.BlockSpec((1, D_RNN), const2),
            pl.BlockSpec((1, D_RNN), const2),
            pl.BlockSpec((D_RNN, D_MODEL), const2),
        ],
        out_specs=pl.BlockSpec((ts, D_MODEL), lambda b, s: (b * nt + s, 0)),
        out_shape=jax.ShapeDtypeStruct((t, D_MODEL), F32),
        scratch_shapes=[
            pltpu.VMEM((SUBLANES, D_RNN), F32),
            pltpu.VMEM((SUBLANES, D_RNN), F32),
            pltpu.VMEM((ts, D_RNN), F32),
            pltpu.VMEM((ts, D_RNN), F32),
            pltpu.VMEM((ts, D_RNN), F32),
        ],
        compiler_params=pltpu.CompilerParams(
            dimension_semantics=("parallel", "arbitrary"), vmem_limit_bytes=VMEM_LIMIT),
        name="rnn",
    )(x2, norm_g, w_xg, conv_w, conv_b, wa_bd, ba, wi_bd, bi, lam, w_out)


def _compress_kernel(x_ref, pos_ref, w1_ref, b1_ref, w2_ref, b2_ref, cos_ref, sin_ref, o_ref):
    n_rows = x_ref.shape[0] // CMP_STRIDE
    strided = [x_ref[pl.ds(r, n_rows, stride=CMP_STRIDE), :] for r in range(CMP_STRIDE)]
    low = _lane_iota((n_rows, LANES)) < HEAD_DIM
    groups = []
    for g in range(2):
        tiles = []
        for j in range(CMP_STRIDE // 2):
            even = strided[2 * j]
            odd = strided[2 * j + 1]
            if g == 0:
                tiles.append(jnp.where(low, even, pltpu.roll(odd, HEAD_DIM, 1)))
            else:
                tiles.append(jnp.where(low, pltpu.roll(even, HEAD_DIM, 1), odd))
        groups.append(jnp.concatenate(tiles, axis=1))
    x = jnp.concatenate(groups, axis=0)
    tr = x.shape[0]
    half = x.shape[1]
    first = _dot((x + pos_ref[0, 0:1, :]).astype(BF16), w1_ref[0, 0:half, :])
    second = _dot((x + pos_ref[0, 1:2, :]).astype(BF16), w1_ref[0, half:2 * half, :])
    hid = jax.nn.gelu(first + pltpu.roll(second, tr - 1, 0) + b1_ref[0])
    out = _dot(hid.astype(BF16), w2_ref[0]) + b2_ref[0]
    o_ref[0] = _rope(out, cos_ref[0], sin_ref[0]).astype(BF16)


def _compress(kcv, pos, w1, b1, w2, b2, cos, sin, batch, seq):
    pairs = KV_DIM // LANES
    tr = TR_CMP
    rows = batch * pairs * tr
    sel = lambda k, r: (k, 0, 0)
    return pl.pallas_call(
        _compress_kernel,
        grid=(2, batch * pairs),
        in_specs=[
            pl.BlockSpec((seq, LANES), lambda k, r: (r // pairs, k * pairs + r % pairs)),
            pl.BlockSpec((1, 2, pos.shape[2]), sel),
            pl.BlockSpec((1,) + w1.shape[1:], sel),
            pl.BlockSpec((1, 1, CMP_HIDDEN), sel),
            pl.BlockSpec((1,) + w2.shape[1:], sel),
            pl.BlockSpec((1, 1, LANES), sel),
            pl.BlockSpec((1, tr, LANES), sel),
            pl.BlockSpec((1, tr, LANES), sel),
        ],
        out_specs=pl.BlockSpec((1, tr, LANES), lambda k, r: (k, r, 0)),
        out_shape=jax.ShapeDtypeStruct((2, rows, LANES), BF16),
        compiler_params=pltpu.CompilerParams(
            dimension_semantics=("parallel", "parallel"), vmem_limit_bytes=VMEM_LIMIT),
        name="compress",
    )(kcv, pos, w1, b1, w2, b2, cos, sin)


def _lane_tile_max(s):
    tiles = [s[:, c * LANES:(c + 1) * LANES] for c in range(s.shape[1] // LANES)]
    while len(tiles) > 1:
        tiles = [jnp.maximum(tiles[k], tiles[k + 1]) if k + 1 < len(tiles) else tiles[k]
                 for k in range(0, len(tiles), 2)]
    return tiles[0]


def _stack_heads(q):
    tq = q.shape[0]
    low = _lane_iota((tq, LANES)) < HEAD_DIM
    zero = jnp.zeros((tq, LANES), BF16)
    heads = []
    for hh in range(HEADS_PER_GROUP):
        pair = q[:, (hh // 2) * LANES:(hh // 2 + 1) * LANES]
        heads.append(jnp.where(low if hh % 2 == 0 else jnp.logical_not(low), pair, zero))
    return jnp.concatenate(heads, axis=0)


def _select_tile(i, q, kc_ref, vc_ref, ng, m_ref):
    tq = q.shape[0]
    rows = HEADS_PER_GROUP * tq
    low = _lane_iota((tq, LANES)) < HEAD_DIM
    q4 = _stack_heads(q)
    lane4 = _lane_iota((rows, LANES))
    qpos4 = i * tq + (_row_iota((rows, LANES)) & (tq - 1))

    sc = _dot_t(q4, kc_ref[0])
    sc = jnp.where(lane4 * CMP_STRIDE + (CMP_BLOCK - 1) <= qpos4, sc, NEG_INF)
    mc = jnp.max(sc, axis=1, keepdims=True)
    pc = jnp.exp2(sc - mc)
    pc = pc / jnp.sum(pc, axis=1, keepdims=True)
    pc = jnp.where(qpos4 >= CMP_BLOCK - 1, pc, 0.0)
    o_cmp = _dot(pc.astype(BF16), vc_ref[0])

    psum = pc[0:tq] + pc[tq:2 * tq] + pc[2 * tq:3 * tq] + pc[3 * tq:4 * tq]
    p_hi = psum.astype(BF16)
    rem = psum - p_hi.astype(F32)
    p_mid = rem.astype(BF16)
    p_lo = (rem - p_mid.astype(F32)).astype(BF16)
    cs = m_ref[...]
    p_slc = _dot(p_hi, cs) + _dot(p_mid, cs) + _dot(p_lo, cs)

    n_sel = SEL_BLOCK // 2
    p_slc_t = p_slc.T[0:n_sel, :]
    blk = _row_iota((n_sel, tq))
    tblk = (i * tq + _lane_iota((n_sel, tq))) >> 6
    forced = (blk == 0) | (blk == tblk) | (blk == tblk - 1)
    score = jnp.where(blk <= tblk, p_slc_t + jnp.where(forced, FORCE_BONUS, 0.0), -1.0)
    rank = jnp.zeros((n_sel, tq), F32)
    for j in range(n_sel):
        sj = score[j:j + 1, :]
        beats = (sj > score) | ((sj == score) & (blk > j))
        rank = rank + jnp.where(beats, 1.0, 0.0)
    bias_t = jnp.where(rank < SEL_TOPN, 0.0, NEG_INF)
    bias = jnp.concatenate([bias_t, jnp.zeros((LANES - n_sel, tq), F32)], axis=0).T.astype(BF16)

    gated = [ng[:, hh:hh + 1] * o_cmp[hh * tq:(hh + 1) * tq] for hh in range(HEADS_PER_GROUP)]
    o_pairs = jnp.concatenate([jnp.where(low, gated[0], gated[1]), jnp.where(low, gated[2], gated[3])], axis=1)
    return bias, o_pairs


def _attn_kernel(*refs):
    n_full = pl.program_id(2) // (CK_SEL // TQ)
    for n_chunks in range(1, refs[3].shape[0] // CK_SEL + 1):
        pl.when(n_full == n_chunks - 1)(functools.partial(_attn_step, n_chunks, *refs))


def _attn_step(n_chunks, q_ref, kc_ref, vc_ref, ks_ref, vs_ref, kw_ref, vw_ref, ng_ref, e_ref, m_ref, o_ref):
    i = pl.program_id(2)
    tq = q_ref.shape[0]
    rows = HEADS_PER_GROUP * tq
    ck = CK_SEL
    low = _lane_iota((tq, LANES)) < HEAD_DIM
    q4 = _stack_heads(q_ref[...])

    st = TQ_SELECT_TILE
    picks = [_select_tile(i * (tq // st) + sub, q_ref[sub * st:(sub + 1) * st, :], kc_ref, vc_ref,
                          ng_ref[sub * st:(sub + 1) * st, :], m_ref) for sub in range(tq // st)]
    bias = jnp.concatenate([p[0] for p in picks], axis=0)
    o_cmp = jnp.concatenate([p[1] for p in picks], axis=0)

    n_sub = tq // TK
    back = WINDOW // TK
    k_tiles, v_tiles = [], []
    for jj in range(back + n_sub):
        j = i * n_sub - back + jj
        off = pl.multiple_of(jnp.maximum(j, 0) * TK, TK)
        k_tiles.append(kw_ref[pl.ds(off, TK), :])
        v_tiles.append(vw_ref[pl.ds(off, TK), :])
    sw = _dot_t(q4, jnp.concatenate(k_tiles, axis=0))
    on_or_after = _row_iota((TK, TK)) >= _lane_iota((TK, TK))
    sw_tiles = []
    for jj in range(back + n_sub):
        blocks = []
        for blk_i in range(rows // TK):
            piece = sw[blk_i * TK:(blk_i + 1) * TK, jj * TK:(jj + 1) * TK]
            tiles_back = blk_i % n_sub + back - jj
            if tiles_back == 0:
                piece = jnp.where(on_or_after, piece, NEG_INF)
            elif tiles_back == back:
                piece = jnp.where(on_or_after, NEG_INF, piece)
            elif tiles_back < 0 or tiles_back > back:
                piece = jnp.full((TK, TK), NEG_INF, F32)
            blocks.append(piece)
        tile = jnp.concatenate(blocks, axis=0)
        if jj < back:
            tile = jnp.where(i * n_sub - back + jj >= 0, tile, NEG_INF)
        sw_tiles.append(tile)
    sw = jnp.concatenate(sw_tiles, axis=1)
    m_win = jnp.max(_lane_tile_max(sw), axis=1, keepdims=True)
    acc_win = _dot(jnp.exp2(sw - m_win).astype(BF16), jnp.concatenate(v_tiles, axis=0))

    qa = jnp.concatenate([q4, jnp.concatenate([bias] * HEADS_PER_GROUP, axis=0)], axis=1)
    rel_q = i * tq - (n_chunks - 1) * ck + (_row_iota((rows, ck)) & (tq - 1))
    m_run = None
    acc_sel = None
    for c in range(n_chunks):
        ka = jnp.concatenate([ks_ref[c * ck:(c + 1) * ck, :], e_ref[c * ck:(c + 1) * ck, :]], axis=1)
        s = _dot_t(qa, ka)
        if c == n_chunks - 1:
            s = jnp.where(_lane_iota((rows, ck)) <= rel_q, s, NEG_INF)
        m_new = jnp.max(_lane_tile_max(s), axis=1, keepdims=True)
        if c > 0:
            m_new = jnp.maximum(m_run, m_new)
            acc_sel = acc_sel * jnp.exp2(m_run - m_new)
        part = _dot(jnp.exp2(s - m_new).astype(BF16), vs_ref[c * ck:(c + 1) * ck, :])
        acc_sel = part if c == 0 else acc_sel + part
        m_run = m_new

    ng = ng_ref[...]
    outs = []
    for hh in range(HEADS_PER_GROUP):
        r0 = hh * tq
        col = lambda br: ng[:, br * HEADS_PER_GROUP + hh:br * HEADS_PER_GROUP + hh + 1]
        parts = []
        for acc in (acc_sel[r0:r0 + tq], acc_win[r0:r0 + tq]):
            swapped = pltpu.roll(acc, HEAD_DIM, 1)
            parts.append(acc / swapped if hh % 2 == 0 else swapped / acc)
        outs.append(col(1) * parts[0] + col(2) * parts[1])
    o_ref[...] = (o_cmp + jnp.concatenate(
        [jnp.where(low, outs[0], outs[1]), jnp.where(low, outs[2], outs[3])], axis=1)).astype(o_ref.dtype)


def _attn(q, kvc, ksd, vsd, kwd, vwd, ng, e_mat, cs_mat, batch, seq):
    t = q.shape[0]
    tq = TQ
    nq = seq // tq
    qrow = lambda b, g, i: (b * nq + i, g)
    kv = lambda b, g, i: (b, g)
    const = lambda b, g, i: (0, 0)
    return pl.pallas_call(
        _attn_kernel,
        grid=(batch, N_KV_GROUPS, nq),
        in_specs=[
            pl.BlockSpec((tq, HEADS_PER_GROUP * HEAD_DIM), qrow),
            pl.BlockSpec((1, N_CMP_PAD, LANES), lambda b, g, i: (0, b * N_KV_GROUPS + g, 0)),
            pl.BlockSpec((1, N_CMP_PAD, LANES), lambda b, g, i: (1, b * N_KV_GROUPS + g, 0)),
            pl.BlockSpec((seq, LANES), kv),
            pl.BlockSpec((seq, LANES), kv),
            pl.BlockSpec((seq, LANES), kv),
            pl.BlockSpec((seq, LANES), kv),
            pl.BlockSpec((tq, LANES), qrow),
            pl.BlockSpec((seq, LANES), const),
            pl.BlockSpec((N_CMP_PAD, LANES), const),
        ],
        out_specs=pl.BlockSpec((tq, HEADS_PER_GROUP * HEAD_DIM), qrow),
        out_shape=jax.ShapeDtypeStruct((t, Q_DIM), BF16),
        compiler_params=pltpu.CompilerParams(
            dimension_semantics=("parallel", "parallel", "arbitrary"), vmem_limit_bytes=VMEM_LIMIT),
        name="attn",
    )(q, kvc, kvc, ksd, vsd, kwd, vwd, ng, e_mat, cs_mat)


def _post_kernel(o_ref, ya_ref, x_ref, g1_ref, wg_ref, wn_ref, wm_ref, g2_ref, wr_hi_ref, wr_lo_ref, rb_ref,
                 xa_ref):
    hn = _normed(x_ref[...], g1_ref[...])
    gate_a = jax.nn.sigmoid(_dot(hn, wg_ref[:, 0:D_MODEL]))
    gate_b = jax.nn.sigmoid(_dot(hn, wg_ref[:, D_MODEL:2 * D_MODEL]))
    y_b = _dot(o_ref[...], wn_ref[...])
    mixed = gate_a * ya_ref[...] + gate_b * y_b
    x1 = x_ref[...] + _dot(mixed.astype(BF16), wm_ref[...])
    xa_ref[:, 0:D_MODEL] = x1
    h2 = x1 * lax.rsqrt(jnp.mean(x1 * x1, axis=-1, keepdims=True) + EPS) * g2_ref[...]
    h_hi = h2.astype(BF16)
    h_lo = (h2 - h_hi.astype(F32)).astype(BF16)
    logits = (_dot(h_hi, wr_hi_ref[...]) + _dot(h_lo, wr_hi_ref[...]) + _dot(h_hi, wr_lo_ref[...])
              + rb_ref[...])

    lane = _lane_iota(logits.shape).astype(F32)
    is_grp = (lane >= N_EXPERTS) & (lane < N_EXPERTS + N_GROUPS)
    gl = jnp.where(is_grp, logits, NEG_INF)
    ge = jnp.exp(gl - jnp.max(gl, axis=1, keepdims=True))
    gp = ge / jnp.sum(ge, axis=1, keepdims=True)
    g_w = jnp.max(gp, axis=1, keepdims=True)
    big = float(4 * LANES)
    g_first = jnp.min(jnp.where(is_grp & (gp == g_w), lane, big), axis=1, keepdims=True)
    grp_lo = (g_first - N_EXPERTS) * EXPERTS_PER_GROUP

    in_grp = (lane >= grp_lo) & (lane < grp_lo + EXPERTS_PER_GROUP)
    el = jnp.where(in_grp, logits, NEG_INF)
    ee = jnp.exp(el - jnp.max(el, axis=1, keepdims=True))
    ep = ee / jnp.sum(ee, axis=1, keepdims=True)
    w1 = jnp.max(ep, axis=1, keepdims=True)
    i1 = jnp.min(jnp.where(in_grp & (ep == w1), lane, big), axis=1, keepdims=True)
    rest = jnp.where(in_grp & (lane != i1), ep, -1.0)
    w2 = jnp.max(rest, axis=1, keepdims=True)
    i2 = jnp.min(jnp.where(rest == w2, lane, big), axis=1, keepdims=True)
    den = w1 + w2
    comb = jnp.where(lane == i1, g_w * (w1 / den), jnp.where(lane == i2, g_w * (w2 / den), 0.0))
    xa_ref[:, D_MODEL:D_MODEL + LANES] = jnp.where(lane == GROUP_LANE, g_first - N_EXPERTS, comb)


def _post(o_nsa, y_a, x2, g1, w_mg, w_nsa, w_mix, g2, wr_hi, wr_lo, rb):
    t = x2.shape[0]
    tm = TM_POST
    row = lambda i: (i, 0)
    const = lambda i: (0, 0)
    return pl.pallas_call(
        _post_kernel,
        grid=(t // tm,),
        in_specs=[
            pl.BlockSpec((tm, Q_DIM), row),
            pl.BlockSpec((tm, D_MODEL), row),
            pl.BlockSpec((tm, D_MODEL), row),
            pl.BlockSpec((1, D_MODEL), const),
            pl.BlockSpec((D_MODEL, 2 * D_MODEL), const),
            pl.BlockSpec((Q_DIM, D_MODEL), const),
            pl.BlockSpec((D_MODEL, D_MODEL), const),
            pl.BlockSpec((1, D_MODEL), const),
            pl.BlockSpec((D_MODEL, LANES), const),
            pl.BlockSpec((D_MODEL, LANES), const),
            pl.BlockSpec((1, LANES), const),
        ],
        out_specs=pl.BlockSpec((tm, D_AUG), row),
        out_shape=jax.ShapeDtypeStruct((t, D_AUG), F32),
        compiler_params=pltpu.CompilerParams(
            dimension_semantics=("parallel",), vmem_limit_bytes=VMEM_LIMIT),
        name="post",
    )(o_nsa, y_a, x2, g1, w_mg, w_nsa, w_mix, g2, wr_hi, wr_lo, rb)


def _plan_kernel(rec_ref, tri_ref, info_ref, counts_ref, carry_s):
    @pl.when(pl.program_id(0) == 0)
    def _():
        carry_s[...] = jnp.zeros_like(carry_s)

    rec = rec_ref[...]
    lane = _lane_iota(rec.shape)
    gid = rec[:, GROUP_LANE:GROUP_LANE + 1]
    onehot = jnp.where(lane.astype(F32) == gid, 1.0, 0.0)
    before = _dot(tri_ref[...], onehot.astype(BF16)) + carry_s[0:1, :]
    rank = jnp.sum(onehot * before, axis=1, keepdims=True)
    info_ref[...] = jnp.where(lane == 0, rank, jnp.where(lane == 1, gid, 0.0))
    carry_s[...] = carry_s[...] + jnp.sum(onehot, axis=0, keepdims=True)
    counts_ref[...] = carry_s[...]


def _plan(xa, tri):
    t = xa.shape[0]
    tm = TM_PLAN
    return pl.pallas_call(
        _plan_kernel,
        grid=(t // tm,),
        in_specs=[
            pl.BlockSpec((tm, LANES), lambda i: (i, D_MODEL // LANES)),
            pl.BlockSpec((tm, tm), lambda i: (0, 0)),
        ],
        out_specs=(
            pl.BlockSpec((tm, LANES), lambda i: (i, 0)),
            pl.BlockSpec((SUBLANES, LANES), lambda i: (0, 0)),
        ),
        out_shape=(
            jax.ShapeDtypeStruct((t, LANES), F32),
            jax.ShapeDtypeStruct((SUBLANES, LANES), F32),
        ),
        scratch_shapes=[pltpu.VMEM((SUBLANES, LANES), F32)],
        compiler_params=pltpu.CompilerParams(
            dimension_semantics=("arbitrary",), vmem_limit_bytes=VMEM_LIMIT),
        name="plan",
    )(xa, tri)


def _row_copies(n_rows, make_copy):
    def start(g, carry):
        r0 = pl.multiple_of(g * ROW_COPY_UNROLL, ROW_COPY_UNROLL)
        for k in range(ROW_COPY_UNROLL):
            make_copy(r0, k).start()
        return carry

    def wait(g, carry):
        for _ in range(ROW_COPY_UNROLL):
            make_copy(0, 0).wait()
        return carry

    lax.fori_loop(0, n_rows // ROW_COPY_UNROLL, start, 0)
    lax.fori_loop(0, n_rows // ROW_COPY_UNROLL, wait, 0)


def _dispatch_kernel(slot_ref, xa_ref, zero_hbm, xs_hbm, sem):
    del zero_hbm
    first = pl.program_id(0) * xa_ref.shape[0]

    def copy(r0, k):
        slot = slot_ref[first + r0 + k]
        return pltpu.make_async_copy(xa_ref.at[pl.ds(r0 + k, 1), :], xs_hbm.at[pl.ds(slot, 1), :], sem)

    _row_copies(xa_ref.shape[0], copy)


def _dispatch(slot, xa, n_sorted):
    t = xa.shape[0]
    tm = TM_ROWS
    zeros = jnp.zeros((n_sorted, D_AUG), F32)
    return pl.pallas_call(
        _dispatch_kernel,
        grid_spec=pltpu.PrefetchScalarGridSpec(
            num_scalar_prefetch=1,
            grid=(t // tm,),
            in_specs=[
                pl.BlockSpec((tm, D_AUG), lambda i, *_: (i, 0)),
                pl.BlockSpec(memory_space=pl.ANY),
            ],
            out_specs=pl.BlockSpec(memory_space=pl.ANY),
            scratch_shapes=[pltpu.SemaphoreType.DMA],
        ),
        out_shape=jax.ShapeDtypeStruct((n_sorted, D_AUG), F32),
        input_output_aliases={2: 0},
        compiler_params=pltpu.CompilerParams(
            dimension_semantics=("arbitrary",), vmem_limit_bytes=VMEM_LIMIT),
        name="dispatch",
    )(slot, xa, zeros)


def _experts_kernel(widx_ref, used_ref, tgrp_ref, xs_ref, wg_ref, wu_ref, wd_ref, g2_ref, gf_ref,
                    fs_ref, h_s, y_s):
    del widx_ref
    j = pl.program_id(0)
    e = pl.program_id(1)
    used = used_ref[j] == 1

    @pl.when(used & (e == 0))
    def _():
        x1 = xs_ref[:, 0:D_MODEL]
        h2 = x1 * lax.rsqrt(jnp.mean(x1 * x1, axis=-1, keepdims=True) + EPS) * g2_ref[...]
        h_s[...] = h2.astype(BF16)
        y_s[...] = jnp.zeros_like(y_s)

    @pl.when(used)
    def _():
        h = h_s[...]
        act = jax.nn.silu(_dot(h, wg_ref[0].astype(BF16))) * _dot(h, wu_ref[0].astype(BF16))
        y = _dot(act.astype(BF16), wd_ref[0].astype(BF16))
        rec = xs_ref[:, D_MODEL:D_AUG]
        col = tgrp_ref[j] * EXPERTS_PER_GROUP + e
        weight = jnp.sum(jnp.where(_lane_iota(rec.shape) == col, rec, 0.0), axis=1, keepdims=True)
        y_s[...] += weight * y

    @pl.when(used & (e == EXPERTS_PER_GROUP - 1))
    def _():
        x = xs_ref[:, 0:D_MODEL] + y_s[...]
        fs_ref[...] = x * lax.rsqrt(jnp.mean(x * x, axis=-1, keepdims=True) + EPS) * gf_ref[...]

    @pl.when(jnp.logical_not(used) & (e == EXPERTS_PER_GROUP - 1))
    def _():
        fs_ref[...] = jnp.zeros_like(fs_ref)


def _experts(widx, used, tgrp, xs, wg, wu, wd, g2, gf):
    n_sorted = xs.shape[0]
    tm = TM_EXP
    row = lambda j, e, *_: (j, 0)
    wsel = lambda j, e, widx, used, tgrp: (widx[j * EXPERTS_PER_GROUP + e], 0, 0)
    const = lambda j, e, *_: (0, 0)
    return pl.pallas_call(
        _experts_kernel,
        grid_spec=pltpu.PrefetchScalarGridSpec(
            num_scalar_prefetch=3,
            grid=(n_sorted // tm, EXPERTS_PER_GROUP),
            in_specs=[
                pl.BlockSpec((tm, D_AUG), row),
                pl.BlockSpec((1, D_MODEL, D_EXPERT), wsel),
                pl.BlockSpec((1, D_MODEL, D_EXPERT), wsel),
                pl.BlockSpec((1, D_EXPERT, D_MODEL), wsel),
                pl.BlockSpec((1, D_MODEL), const),
                pl.BlockSpec((1, D_MODEL), const),
            ],
            out_specs=pl.BlockSpec((tm, D_MODEL), row),
            scratch_shapes=[pltpu.VMEM((tm, D_MODEL), BF16), pltpu.VMEM((tm, D_MODEL), F32)],
        ),
        out_shape=jax.ShapeDtypeStruct((n_sorted, D_MODEL), F32),
        compiler_params=pltpu.CompilerParams(
            dimension_semantics=("arbitrary", "arbitrary"), vmem_limit_bytes=VMEM_LIMIT),
        name="experts",
    )(widx, used, tgrp, xs, wg, wu, wd, g2, gf)


def _combine_kernel(slot_ref, fs_hbm, o_ref, sem):
    first = pl.program_id(0) * o_ref.shape[0]

    def copy(r0, k):
        slot = slot_ref[first + r0 + k]
        return pltpu.make_async_copy(fs_hbm.at[pl.ds(slot, 1), :], o_ref.at[pl.ds(r0 + k, 1), :], sem)

    _row_copies(o_ref.shape[0], copy)


def _combine(slot, fs, t):
    tm = TM_ROWS
    return pl.pallas_call(
        _combine_kernel,
        grid_spec=pltpu.PrefetchScalarGridSpec(
            num_scalar_prefetch=1,
            grid=(t // tm,),
            in_specs=[pl.BlockSpec(memory_space=pl.ANY)],
            out_specs=pl.BlockSpec((tm, D_MODEL), lambda i, *_: (i, 0)),
            scratch_shapes=[pltpu.SemaphoreType.DMA],
        ),
        out_shape=jax.ShapeDtypeStruct((t, D_MODEL), F32),
        compiler_params=pltpu.CompilerParams(
            dimension_semantics=("arbitrary",), vmem_limit_bytes=VMEM_LIMIT),
        name="combine",
    )(slot, fs)


def _moe_sorted(xa, wg, wu, wd, g2, gf):
    t = xa.shape[0]
    tri = jnp.asarray(np.tril(np.ones((TM_PLAN, TM_PLAN), np.float32), -1), dtype=BF16)
    info, counts = _plan(xa, tri)
    rank = info[:, 0].astype(jnp.int32)
    gid = info[:, 1].astype(jnp.int32)
    counts = counts[0, :N_GROUPS].astype(jnp.int32)

    n_tiles = t // TM_EXP + N_GROUPS
    tiles_g = (counts + TM_EXP - 1) // TM_EXP
    tile_end = jnp.cumsum(tiles_g)
    base = (tile_end - tiles_g) * TM_EXP
    groups = jnp.arange(N_GROUPS, dtype=jnp.int32)
    slot = jnp.sum(jnp.where(gid[:, None] == groups[None, :], base[None, :], 0), axis=1) + rank
    tile_ids = jnp.arange(n_tiles, dtype=jnp.int32)
    used = (tile_ids < tile_end[-1]).astype(jnp.int32)
    last = jnp.maximum(tile_end[-1] - 1, 0)
    tgrp = jnp.sum((jnp.minimum(tile_ids, last)[:, None] >= tile_end[None, :]).astype(jnp.int32), axis=1)
    step_e = jnp.arange(EXPERTS_PER_GROUP, dtype=jnp.int32)[None, :]
    widx = jnp.where(used[:, None] == 1, tgrp[:, None] * EXPERTS_PER_GROUP + step_e,
                     tgrp[:, None] * EXPERTS_PER_GROUP + EXPERTS_PER_GROUP - 1).reshape(-1)

    xs = _dispatch(slot, xa, n_tiles * TM_EXP)
    fs = _experts(widx, used, tgrp, xs, wg, wu, wd, g2, gf)
    return _combine(slot, fs, t)


def _split_w_in(w):
    sizes = [D_RNN, D_RNN, Q_DIM] + [KV_DIM] * 6 + [3 * N_HEADS, 2 * D_MODEL]
    pts = np.cumsum(sizes)[:-1]
    return jnp.split(w, [int(p) for p in pts], axis=-1)


def _proj_weights(w_in):
    xr, gr, q, kc, vc, ksl, vsl, kw, vw, nsa_g, merge_g = _split_w_in(w_in)
    ng = nsa_g.reshape(D_MODEL, 3, N_KV_GROUPS, HEADS_PER_GROUP).transpose(0, 2, 1, 3)
    ng = ng.reshape(D_MODEL, N_KV_GROUPS, 3 * HEADS_PER_GROUP)
    ng = jnp.pad(ng, ((0, 0), (0, 0), (0, LANES - 3 * HEADS_PER_GROUP))).reshape(D_MODEL, N_KV_GROUPS * LANES)
    w_attn = jnp.concatenate([q, kc, vc, ksl, vsl, kw, vw, ng], axis=1)
    return (jnp.concatenate([xr, gr], axis=1).astype(BF16), w_attn.astype(BF16), merge_g.astype(BF16))


def _rope_tables(pos, width):
    inv_freq = ROPE_THETA ** (-(jnp.arange(0, HEAD_DIM, 2, dtype=F32) / HEAD_DIM))
    ang = pos.astype(F32)[:, None] * inv_freq[None, :]
    cos, sin = jnp.cos(ang), jnp.sin(ang)
    reps = width // HEAD_DIM
    return jnp.tile(jnp.concatenate([cos, cos], axis=1), (1, reps)), jnp.tile(jnp.concatenate([-sin, sin], axis=1), (1, reps))


def _block_diag(w, per):
    nb, d = w.shape[0], w.shape[1]
    w = w.reshape(nb // per, per, d, d)
    eye = jnp.eye(per, dtype=w.dtype)
    return jnp.einsum('npij,pq->npiqj', w, eye).reshape(nb // per, per * d, per * d)


def _cmp_to_sel():
    n_sel_pad = LANES
    c0 = np.arange(N_CMP_PAD) * CMP_STRIDE
    s0 = np.arange(n_sel_pad) * SEL_BLOCK
    ov = np.minimum(c0[:, None] + CMP_BLOCK, s0[None, :] + SEL_BLOCK) - np.maximum(c0[:, None], s0[None, :])
    m = np.clip(ov, 0, None) / CMP_BLOCK
    m[:, SEL_BLOCK // 2:] = 0.0
    m[N_CMP_PAD - 1, :] = 0.0
    return m.astype(np.float32)


def kernel(x, norm1_g, w_in, conv_w, conv_b, lru_wa, lru_ba, lru_wi, lru_bi, lru_lambda, w_rnn_out, cmpk_pos, cmpk_w1, cmpk_b1, cmpk_w2, cmpk_b2, cmpv_pos, cmpv_w1, cmpv_b1, cmpv_w2, cmpv_b2, w_nsa_out, w_mix_out, norm2_g, router_group_w, router_group_b, router_expert_w, router_expert_b, expert_w_gate, expert_w_up, expert_w_down, final_norm_g):
    batch, seq, _ = x.shape
    t = batch * seq
    assert w_in.shape[0] == 1, "the final norm is fused into the expert kernel: single layer only"
    x2 = x.reshape(t, D_MODEL)

    cos, sin = _rope_tables(jnp.arange(seq), LANES)
    cmp_ends = jnp.arange(N_CMP_PAD) * CMP_STRIDE + (CMP_BLOCK - 1)
    ccos, csin = _rope_tables(cmp_ends, LANES)
    reps = TR_CMP // N_CMP_PAD
    cmp_cos = jnp.stack([jnp.tile(ccos, (reps, 1)), jnp.ones((TR_CMP, LANES), F32)])
    cmp_sin = jnp.stack([jnp.tile(csin, (reps, 1)), jnp.zeros((TR_CMP, LANES), F32)])
    key_blk = np.arange(seq)[:, None] // SEL_BLOCK
    e_mat = jnp.asarray((key_blk == np.arange(LANES)[None, :]).astype(np.float32), dtype=BF16)
    cs_mat = jnp.asarray(_cmp_to_sel(), dtype=BF16)

    l = 0
    w_xg, w_attn, w_mg = _proj_weights(w_in[l])
    g1 = norm1_g[l][None, :]
    q, kcv, ksd, vsd, kwd, vwd, ng = _inproj(x2, g1, w_attn, cos, sin, seq)

    y_a = _rnn(x2, g1, w_xg, conv_w[l], conv_b[l][None, :],
               _block_diag(lru_wa[l], 4).astype(BF16), lru_ba[l][None, :],
               _block_diag(lru_wi[l], 4).astype(BF16), lru_bi[l][None, :],
               lru_lambda[l][None, :], w_rnn_out[l].astype(BF16), batch, seq)

    half = CMP_STRIDE * HEAD_DIM
    pos = jnp.stack([cmpk_pos[l].reshape(2, half), cmpv_pos[l].reshape(2, half)])
    w1 = jnp.stack([cmpk_w1[l], cmpv_w1[l]]).astype(BF16)
    b1 = jnp.stack([cmpk_b1[l], cmpv_b1[l]])[:, None, :]
    w2 = jnp.stack([cmpk_w2[l], cmpv_w2[l]])
    w2 = jnp.concatenate([w2, w2], axis=2).astype(BF16)
    b2 = jnp.stack([cmpk_b2[l], cmpv_b2[l]])
    b2 = jnp.concatenate([b2, b2], axis=1)[:, None, :]
    kvc = _compress(kcv, pos, w1, b1, w2, b2, cmp_cos, cmp_sin, batch, seq)

    o_nsa = _attn(q, kvc, ksd, vsd, kwd, vwd, ng, e_mat, cs_mat, batch, seq)

    wr = jnp.concatenate([
        router_expert_w[l].transpose(1, 0, 2).reshape(D_MODEL, N_EXPERTS),
        router_group_w[l],
        jnp.zeros((D_MODEL, LANES - N_EXPERTS - N_GROUPS), F32)], axis=1)
    wr_hi = wr.astype(BF16)
    wr_lo = (wr - wr_hi.astype(F32)).astype(BF16)
    rb = jnp.concatenate([router_expert_b[l].reshape(N_EXPERTS), router_group_b[l],
                          jnp.zeros((LANES - N_EXPERTS - N_GROUPS,), F32)])[None, :]
    xa = _post(o_nsa, y_a, x2, g1, w_mg, w_nsa_out[l].astype(BF16), w_mix_out[l].astype(BF16),
               norm2_g[l][None, :], wr_hi, wr_lo, rb)

    out = _moe_sorted(xa, expert_w_gate[l], expert_w_up[l], expert_w_down[l],
                      norm2_g[l][None, :], final_norm_g[None, :])
    return out.reshape(batch, seq, D_MODEL)
```

```python
import functools

import numpy as np
import jax
import jax.numpy as jnp
from jax import lax
from jax.experimental import pallas as pl
from jax.experimental.pallas import tpu as pltpu

D_MODEL = 1024
D_RNN = 1024
RNN_BLOCKS = 16
RNN_BLOCK_DIM = D_RNN // RNN_BLOCKS
CONV_WIDTH = 4
LRU_C = 8.0
N_HEADS = 16
HEAD_DIM = 64
HALF_DIM = HEAD_DIM // 2
N_KV_GROUPS = 4
HEADS_PER_GROUP = N_HEADS // N_KV_GROUPS
Q_DIM = N_HEADS * HEAD_DIM
KV_DIM = N_KV_GROUPS * HEAD_DIM
CMP_BLOCK = 32
CMP_STRIDE = 16
CMP_HIDDEN = 256
SEL_BLOCK = 64
SEL_TOPN = 16
WINDOW = 512
ROPE_THETA = 10000.0
FORCE_BONUS = 1e4
NEG_INF = -1e30
N_GROUPS = 4
EXPERTS_PER_GROUP = 4
N_EXPERTS = N_GROUPS * EXPERTS_PER_GROUP
D_EXPERT = 512
EPS = 1e-6

Q_SCALE = HEAD_DIM ** -0.5 * float(np.log2(np.e))

LANES = 128
SUBLANES = 8
VMEM_LIMIT = 56 * 1024 * 1024

BF16 = jnp.bfloat16
F32 = jnp.float32

COL_Q = 0
COL_KCV = COL_Q + Q_DIM
COL_KS = COL_KCV + 2 * KV_DIM
COL_VS = COL_KS + KV_DIM
COL_KW = COL_VS + KV_DIM
COL_VW = COL_KW + KV_DIM
COL_NG = COL_VW + KV_DIM
N_PROJ = COL_NG + N_KV_GROUPS * LANES

TM_PROJ = 512
TS_RNN = 512
TR_CMP = 256
TQ = 256
TQ_SELECT_TILE = 256
TK = 128
CK_SEL = 512
TM_POST = 512
TM_PLAN = 1024
TM_ROWS = 512
TM_EXP = 1024
ROW_COPY_UNROLL = 8
D_AUG = D_MODEL + LANES
GROUP_LANE = N_EXPERTS
N_CMP_PAD = 128


def _dot(a, b):
    return jnp.dot(a, b, preferred_element_type=F32)


def _dot_t(a, b):
    return lax.dot_general(a, b, (((1,), (1,)), ((), ())), preferred_element_type=F32)


def _lane_iota(shape):
    return lax.broadcasted_iota(jnp.int32, shape, len(shape) - 1)


def _row_iota(shape):
    return lax.broadcasted_iota(jnp.int32, shape, 0)


def _rope(x, cos, sin_signed):
    width = x.shape[-1]
    reps = width // cos.shape[-1]
    if reps > 1:
        cos = jnp.concatenate([cos] * reps, axis=1)
        sin_signed = jnp.concatenate([sin_signed] * reps, axis=1)
    first_half = (_lane_iota(x.shape) & (HEAD_DIM - 1)) < HALF_DIM
    partner = jnp.where(first_half, pltpu.roll(x, width - HALF_DIM, 1), pltpu.roll(x, HALF_DIM, 1))
    return x * cos + partner * sin_signed


def _spread_heads(x, fill=None):
    out = []
    low = _lane_iota((x.shape[0], LANES)) < HEAD_DIM
    for c in range(x.shape[1] // LANES):
        xc = x[:, c * LANES:(c + 1) * LANES]
        rolled = pltpu.roll(xc, HEAD_DIM, 1)
        out.append(jnp.where(low, xc, rolled if fill is None else fill))
        out.append(jnp.where(low, rolled, xc if fill is None else fill))
    return jnp.concatenate(out, axis=1)


def _sigmoid(x):
    return 0.5 * (jnp.tanh(0.5 * x) + 1.0)


def _rms_scale(x):
    sq = x * x
    part = sq[:, 0:LANES]
    for c in range(1, x.shape[1] // LANES):
        part = part + sq[:, c * LANES:(c + 1) * LANES]
    return lax.rsqrt(jnp.sum(part, axis=1, keepdims=True) * (1.0 / x.shape[1]) + EPS)


def _normed(x, g):
    return (x * _rms_scale(x) * g).astype(BF16)


def _inproj_kernel(x_ref, g_ref, w_ref, cos_ref, sin_ref,
                   q_ref, kcv_ref, ks_ref, vs_ref, kw_ref, vw_ref, ng_ref):
    h = _normed(x_ref[...], g_ref[...])
    cos = cos_ref[...]
    sin = sin_ref[...]

    def mm(lo, width):
        return _dot(h, w_ref[:, lo:lo + width])

    q_ref[...] = (_rope(mm(COL_Q, Q_DIM), cos, sin) * Q_SCALE).astype(BF16)
    kcv_ref[...] = mm(COL_KCV, 2 * KV_DIM)
    ks_ref[...] = _spread_heads(_rope(mm(COL_KS, KV_DIM), cos, sin)).astype(BF16)
    vs_ref[...] = _spread_heads(mm(COL_VS, KV_DIM), 1.0).astype(BF16)
    kw_ref[...] = _spread_heads(_rope(mm(COL_KW, KV_DIM), cos, sin)).astype(BF16)
    vw_ref[...] = _spread_heads(mm(COL_VW, KV_DIM), 1.0).astype(BF16)
    ng_ref[...] = jax.nn.sigmoid(mm(COL_NG, N_KV_GROUPS * LANES))


def _inproj(x2, norm_g, w_proj, cos, sin, seq):
    t = x2.shape[0]
    tm = TM_PROJ
    pos_blocks = seq // tm
    row = lambda i: (i, 0)
    const = lambda i: (0, 0)
    out_shape = (
        jax.ShapeDtypeStruct((t, Q_DIM), BF16),
        jax.ShapeDtypeStruct((t, 2 * KV_DIM), F32),
        jax.ShapeDtypeStruct((t, N_KV_GROUPS * LANES), BF16),
        jax.ShapeDtypeStruct((t, N_KV_GROUPS * LANES), BF16),
        jax.ShapeDtypeStruct((t, N_KV_GROUPS * LANES), BF16),
        jax.ShapeDtypeStruct((t, N_KV_GROUPS * LANES), BF16),
        jax.ShapeDtypeStruct((t, N_KV_GROUPS * LANES), F32),
    )
    return pl.pallas_call(
        _inproj_kernel,
        grid=(t // tm,),
        in_specs=[
            pl.BlockSpec((tm, D_MODEL), row),
            pl.BlockSpec((1, D_MODEL), const),
            pl.BlockSpec((D_MODEL, N_PROJ), const, pipeline_mode=pl.Buffered(1)),
            pl.BlockSpec((tm, LANES), lambda i: (i % pos_blocks, 0)),
            pl.BlockSpec((tm, LANES), lambda i: (i % pos_blocks, 0)),
        ],
        out_specs=tuple(pl.BlockSpec((tm, s.shape[1]), row) for s in out_shape),
        out_shape=out_shape,
        compiler_params=pltpu.CompilerParams(
            dimension_semantics=("parallel",), vmem_limit_bytes=VMEM_LIMIT),
        name="inproj",
    )(x2, norm_g, w_proj, cos, sin)


def _rnn_kernel(xin_ref, g1_ref, wx_ref, cw_ref, cb_ref, wa_ref, ba_ref, wi_ref, bi_ref, lam_ref, wo_ref,
                y_ref, tail_s, carry_s, a_s, u_s, h_s):
    ts = xin_ref.shape[0]

    @pl.when(pl.program_id(1) == 0)
    def _():
        tail_s[...] = jnp.zeros_like(tail_s)
        carry_s[...] = jnp.zeros_like(carry_s)

    hn = _normed(xin_ref[...], g1_ref[...])
    x = _dot(hn, wx_ref[:, 0:D_RNN])
    gate_pre = _dot(hn, wx_ref[:, D_RNN:2 * D_RNN])
    xext = jnp.concatenate([tail_s[...], x], axis=0)
    tail_s[...] = x[ts - SUBLANES:ts, :]
    conv = cb_ref[...]
    for k in range(CONV_WIDTH):
        back = CONV_WIDTH - 1 - k
        shifted = xext if back == 0 else pltpu.roll(xext, back, 0)
        conv = conv + cw_ref[k:k + 1, :] * shifted[SUBLANES:SUBLANES + ts, :]

    cb16 = conv.astype(BF16)
    blk = wa_ref.shape[1]
    r_pre = jnp.concatenate(
        [_dot(cb16[:, j * blk:(j + 1) * blk], wa_ref[j]) for j in range(D_RNN // blk)], axis=1)
    i_pre = jnp.concatenate(
        [_dot(cb16[:, j * blk:(j + 1) * blk], wi_ref[j]) for j in range(D_RNN // blk)], axis=1)
    r = _sigmoid(r_pre + ba_ref[...])
    gate_i = _sigmoid(i_pre + bi_ref[...])
    neg_lam = -lam_ref[...]
    softplus = jnp.maximum(neg_lam, 0.0) + jnp.log1p(jnp.exp(-jnp.abs(neg_lam)))
    log_a = (-LRU_C) * r * softplus
    a = jnp.exp(log_a)
    a_s[...] = a
    u_s[...] = jnp.sqrt(-jnp.tanh(log_a) * (a * a + 1.0)) * (gate_i * conv)

    row = _row_iota((SUBLANES, D_RNN))

    def body(k, carry):
        off = pl.multiple_of(k * SUBLANES, SUBLANES)
        a = a_s[pl.ds(off, SUBLANES), :]
        b = u_s[pl.ds(off, SUBLANES), :]
        for sh in (1, 2, 4):
            keep = row >= sh
            a_prev = jnp.where(keep, pltpu.roll(a, sh, 0), 1.0)
            b_prev = jnp.where(keep, pltpu.roll(b, sh, 0), 0.0)
            b = a * b_prev + b
            a = a * a_prev
        h = a * carry + b
        h_s[pl.ds(off, SUBLANES), :] = h
        return jnp.broadcast_to(h[SUBLANES - 1:SUBLANES, :], (SUBLANES, D_RNN))

    carry_s[...] = lax.fori_loop(0, ts // SUBLANES, body, carry_s[...])
    gated = (jax.nn.gelu(gate_pre) * h_s[...]).astype(BF16)
    y_ref[...] = _dot(gated, wo_ref[...])


def _rnn(x2, norm_g, w_xg, conv_w, conv_b, wa_bd, ba, wi_bd, bi, lam, w_out, batch, seq):
    t = x2.shape[0]
    ts = TS_RNN
    nt = seq // ts
    const2 = lambda b, s: (0, 0)
    const3 = lambda b, s: (0, 0, 0)
    nblk, blk = wa_bd.shape[0], wa_bd.shape[1]
    return pl.pallas_call(
        _rnn_kernel,
        grid=(batch, nt),
        in_specs=[
            pl.BlockSpec((ts, D_MODEL), lambda b, s: (b * nt + s, 0)),
            pl.BlockSpec((1, D_MODEL), const2),
            pl.BlockSpec((D_MODEL, 2 * D_RNN), const2),
            pl.BlockSpec((CONV_WIDTH, D_RNN), const2),
            pl.BlockSpec((1, D_RNN), const2),
            pl.BlockSpec((nblk, blk, blk), const3),
            pl.BlockSpec((1, D_RNN), const2),
            pl.BlockSpec((nblk, blk, blk), const3),
            pl.BlockSpec((1, D_RNN), const2),
            pl.BlockSpec((1, D_RNN), const2),
            pl.BlockSpec((D_RNN, D_MODEL), const2),
        ],
        out_specs=pl.BlockSpec((ts, D_MODEL), lambda b, s: (b * nt + s, 0)),
        out_shape=jax.ShapeDtypeStruct((t, D_MODEL), F32),
        scratch_shapes=[
            pltpu.VMEM((SUBLANES, D_RNN), F32),
            pltpu.VMEM((SUBLANES, D_RNN), F32),
            pltpu.VMEM((ts, D_RNN), F32),
            pltpu.VMEM((ts, D_RNN), F32),
            pltpu.VMEM((ts, D_RNN), F32),
        ],
        compiler_params=pltpu.CompilerParams(
            dimension_semantics=("parallel", "arbitrary"), vmem_limit_bytes=VMEM_LIMIT),
        name="rnn",
    )(x2, norm_g, w_xg, conv_w, conv_b, wa_bd, ba, wi_bd, bi, lam, w_out)


def _compress_kernel(x_ref, pos_ref, w1_ref, b1_ref, w2_ref, b2_ref, cos_ref, sin_ref, o_ref):
    n_rows = x_ref.shape[0] // CMP_STRIDE
    strided = [x_ref[pl.ds(r, n_rows, stride=CMP_STRIDE), :] for r in range(CMP_STRIDE)]
    low = _lane_iota((n_rows, LANES)) < HEAD_DIM
    groups = []
    for g in range(2):
        tiles = []
        for j in range(CMP_STRIDE // 2):
            even = strided[2 * j]
            odd = strided[2 * j + 1]
            if g == 0:
                tiles.append(jnp.where(low, even, pltpu.roll(odd, HEAD_DIM, 1)))
            else:
                tiles.append(jnp.where(low, pltpu.roll(even, HEAD_DIM, 1), odd))
        groups.append(jnp.concatenate(tiles, axis=1))
    x = jnp.concatenate(groups, axis=0)
    tr = x.shape[0]
    half = x.shape[1]
    first = _dot((x + pos_ref[0, 0:1, :]).astype(BF16), w1_ref[0, 0:half, :])
    second = _dot((x + pos_ref[0, 1:2, :]).astype(BF16), w1_ref[0, half:2 * half, :])
    hid = jax.nn.gelu(first + pltpu.roll(second, tr - 1, 0) + b1_ref[0])
    out = _dot(hid.astype(BF16), w2_ref[0]) + b2_ref[0]
    o_ref[0] = _rope(out, cos_ref[0], sin_ref[0]).astype(BF16)


def _compress(kcv, pos, w1, b1, w2, b2, cos, sin, batch, seq):
    pairs = KV_DIM // LANES
    tr = TR_CMP
    rows = batch * pairs * tr
    sel = lambda k, r: (k, 0, 0)
    return pl.pallas_call(
        _compress_kernel,
        grid=(2, batch * pairs),
        in_specs=[
            pl.BlockSpec((seq, LANES), lambda k, r: (r // pairs, k * pairs + r % pairs)),
            pl.BlockSpec((1, 2, pos.shape[2]), sel),
            pl.BlockSpec((1,) + w1.shape[1:], sel),
            pl.BlockSpec((1, 1, CMP_HIDDEN), sel),
            pl.BlockSpec((1,) + w2.shape[1:], sel),
            pl.BlockSpec((1, 1, LANES), sel),
            pl.BlockSpec((1, tr, LANES), sel),
            pl.BlockSpec((1, tr, LANES), sel),
        ],
        out_specs=pl.BlockSpec((1, tr, LANES), lambda k, r: (k, r, 0)),
        out_shape=jax.ShapeDtypeStruct((2, rows, LANES), BF16),
        compiler_params=pltpu.CompilerParams(
            dimension_semantics=("parallel", "parallel"), vmem_limit_bytes=VMEM_LIMIT),
        name="compress",
    )(kcv, pos, w1, b1, w2, b2, cos, sin)


def _lane_tile_max(s):
    tiles = [s[:, c * LANES:(c + 1) * LANES] for c in range(s.shape[1] // LANES)]
    while len(tiles) > 1:
        tiles = [jnp.maximum(tiles[k], tiles[k + 1]) if k + 1 < len(tiles) else tiles[k]
                 for k in range(0, len(tiles), 2)]
    return tiles[0]


def _stack_heads(q):
    tq = q.shape[0]
    low = _lane_iota((tq, LANES)) < HEAD_DIM
    zero = jnp.zeros((tq, LANES), BF16)
    heads = []
    for hh in range(HEADS_PER_GROUP):
        pair = q[:, (hh // 2) * LANES:(hh // 2 + 1) * LANES]
        heads.append(jnp.where(low if hh % 2 == 0 else jnp.logical_not(low), pair, zero))
    return jnp.concatenate(heads, axis=0)


def _select_tile(i, q, kc_ref, vc_ref, ng, m_ref):
    tq = q.shape[0]
    rows = HEADS_PER_GROUP * tq
    low = _lane_iota((tq, LANES)) < HEAD_DIM
    q4 = _stack_heads(q)
    lane4 = _lane_iota((rows, LANES))
    qpos4 = i * tq + (_row_iota((rows, LANES)) & (tq - 1))

    sc = _dot_t(q4, kc_ref[0])
    sc = jnp.where(lane4 * CMP_STRIDE + (CMP_BLOCK - 1) <= qpos4, sc, NEG_INF)
    mc = jnp.max(sc, axis=1, keepdims=True)
    pc = jnp.exp2(sc - mc)
    pc = pc / jnp.sum(pc, axis=1, keepdims=True)
    pc = jnp.where(qpos4 >= CMP_BLOCK - 1, pc, 0.0)
    o_cmp = _dot(pc.astype(BF16), vc_ref[0])

    psum = pc[0:tq] + pc[tq:2 * tq] + pc[2 * tq:3 * tq] + pc[3 * tq:4 * tq]
    p_hi = psum.astype(BF16)
    rem = psum - p_hi.astype(F32)
    p_mid = rem.astype(BF16)
    p_lo = (rem - p_mid.astype(F32)).astype(BF16)
    cs = m_ref[...]
    p_slc = _dot(p_hi, cs) + _dot(p_mid, cs) + _dot(p_lo, cs)

    n_sel = SEL_BLOCK // 2
    p_slc_t = p_slc.T[0:n_sel, :]
    blk = _row_iota((n_sel, tq))
    tblk = (i * tq + _lane_iota((n_sel, tq))) >> 6
    forced = (blk == 0) | (blk == tblk) | (blk == tblk - 1)
    score = jnp.where(blk <= tblk, p_slc_t + jnp.where(forced, FORCE_BONUS, 0.0), -1.0)
    rank = jnp.zeros((n_sel, tq), F32)
    for j in range(n_sel):
        sj = score[j:j + 1, :]
        beats = (sj > score) | ((sj == score) & (blk > j))
        rank = rank + jnp.where(beats, 1.0, 0.0)
    bias_t = jnp.where(rank < SEL_TOPN, 0.0, NEG_INF)
    bias = jnp.concatenate([bias_t, jnp.zeros((LANES - n_sel, tq), F32)], axis=0).T.astype(BF16)

    gated = [ng[:, hh:hh + 1] * o_cmp[hh * tq:(hh + 1) * tq] for hh in range(HEADS_PER_GROUP)]
    o_pairs = jnp.concatenate([jnp.where(low, gated[0], gated[1]), jnp.where(low, gated[2], gated[3])], axis=1)
    return bias, o_pairs


def _attn_kernel(*refs):
    n_full = pl.program_id(2) // (CK_SEL // TQ)
    for n_chunks in range(1, refs[3].shape[0] // CK_SEL + 1):
        pl.when(n_full == n_chunks - 1)(functools.partial(_attn_step, n_chunks, *refs))


def _attn_step(n_chunks, q_ref, kc_ref, vc_ref, ks_ref, vs_ref, kw_ref, vw_ref, ng_ref, e_ref, m_ref, o_ref):
    i = pl.program_id(2)
    tq = q_ref.shape[0]
    rows = HEADS_PER_GROUP * tq
    ck = CK_SEL
    low = _lane_iota((tq, LANES)) < HEAD_DIM
    q4 = _stack_heads(q_ref[...])

    st = TQ_SELECT_TILE
    picks = [_select_tile(i * (tq // st) + sub, q_ref[sub * st:(sub + 1) * st, :], kc_ref, vc_ref,
                          ng_ref[sub * st:(sub + 1) * st, :], m_ref) for sub in range(tq // st)]
    bias = jnp.concatenate([p[0] for p in picks], axis=0)
    o_cmp = jnp.concatenate([p[1] for p in picks], axis=0)

    n_sub = tq // TK
    back = WINDOW // TK
    k_tiles, v_tiles = [], []
    for jj in range(back + n_sub):
        j = i * n_sub - back + jj
        off = pl.multiple_of(jnp.maximum(j, 0) * TK, TK)
        k_tiles.append(kw_ref[pl.ds(off, TK), :])
        v_tiles.append(vw_ref[pl.ds(off, TK), :])
    sw = _dot_t(q4, jnp.concatenate(k_tiles, axis=0))
    on_or_after = _row_iota((TK, TK)) >= _lane_iota((TK, TK))
    sw_tiles = []
    for jj in range(back + n_sub):
        blocks = []
        for blk_i in range(rows // TK):
            piece = sw[blk_i * TK:(blk_i + 1) * TK, jj * TK:(jj + 1) * TK]
            tiles_back = blk_i % n_sub + back - jj
            if tiles_back == 0:
                piece = jnp.where(on_or_after, piece, NEG_INF)
            elif tiles_back == back:
                piece = jnp.where(on_or_after, NEG_INF, piece)
            elif tiles_back < 0 or tiles_back > back:
                piece = jnp.full((TK, TK), NEG_INF, F32)
            blocks.append(piece)
        tile = jnp.concatenate(blocks, axis=0)
        if jj < back:
            tile = jnp.where(i * n_sub - back + jj >= 0, tile, NEG_INF)
        sw_tiles.append(tile)
    sw = jnp.concatenate(sw_tiles, axis=1)
    m_win = jnp.max(_lane_tile_max(sw), axis=1, keepdims=True)
    acc_win = _dot(jnp.exp2(sw - m_win).astype(BF16), jnp.concatenate(v_tiles, axis=0))

    qa = jnp.concatenate([q4, jnp.concatenate([bias] * HEADS_PER_GROUP, axis=0)], axis=1)
    rel_q = i * tq - (n_chunks - 1) * ck + (_row_iota((rows, ck)) & (tq - 1))
    m_run = None
    acc_sel = None
    for c in range(n_chunks):
        ka = jnp.concatenate([ks_ref[c * ck:(c + 1) * ck, :], e_ref[c * ck:(c + 1) * ck, :]], axis=1)
        s = _dot_t(qa, ka)
        if c == n_chunks - 1:
            s = jnp.where(_lane_iota((rows, ck)) <= rel_q, s, NEG_INF)
        m_new = jnp.max(_lane_tile_max(s), axis=1, keepdims=True)
        if c > 0:
            m_new = jnp.maximum(m_run, m_new)
            acc_sel = acc_sel * jnp.exp2(m_run - m_new)
        part = _dot(jnp.exp2(s - m_new).astype(BF16), vs_ref[c * ck:(c + 1) * ck, :])
        acc_sel = part if c == 0 else acc_sel + part
        m_run = m_new

    ng = ng_ref[...]
    outs = []
    for hh in range(HEADS_PER_GROUP):
        r0 = hh * tq
        col = lambda br: ng[:, br * HEADS_PER_GROUP + hh:br * HEADS_PER_GROUP + hh + 1]
        parts = []
        for acc in (acc_sel[r0:r0 + tq], acc_win[r0:r0 + tq]):
            swapped = pltpu.roll(acc, HEAD_DIM, 1)
            parts.append(acc / swapped if hh % 2 == 0 else swapped / acc)
        outs.append(col(1) * parts[0] + col(2) * parts[1])
    o_ref[...] = (o_cmp + jnp.concatenate(
        [jnp.where(low, outs[0], outs[1]), jnp.where(low, outs[2], outs[3])], axis=1)).astype(o_ref.dtype)


def _attn(q, kvc, ksd, vsd, kwd, vwd, ng, e_mat, cs_mat, batch, seq):
    t = q.shape[0]
    tq = TQ
    nq = seq // tq
    qrow = lambda b, g, i: (b * nq + i, g)
    kv = lambda b, g, i: (b, g)
    const = lambda b, g, i: (0, 0)
    return pl.pallas_call(
        _attn_kernel,
        grid=(batch, N_KV_GROUPS, nq),
        in_specs=[
            pl.BlockSpec((tq, HEADS_PER_GROUP * HEAD_DIM), qrow),
            pl.BlockSpec((1, N_CMP_PAD, LANES), lambda b, g, i: (0, b * N_KV_GROUPS + g, 0)),
            pl.BlockSpec((1, N_CMP_PAD, LANES), lambda b, g, i: (1, b * N_KV_GROUPS + g, 0)),
            pl.BlockSpec((seq, LANES), kv),
            pl.BlockSpec((seq, LANES), kv),
            pl.BlockSpec((seq, LANES), kv),
            pl.BlockSpec((seq, LANES), kv),
            pl.BlockSpec((tq, LANES), qrow),
            pl.BlockSpec((seq, LANES), const),
            pl.BlockSpec((N_CMP_PAD, LANES), const),
        ],
        out_specs=pl.BlockSpec((tq, HEADS_PER_GROUP * HEAD_DIM), qrow),
        out_shape=jax.ShapeDtypeStruct((t, Q_DIM), BF16),
        compiler_params=pltpu.CompilerParams(
            dimension_semantics=("parallel", "parallel", "arbitrary"), vmem_limit_bytes=VMEM_LIMIT),
        name="attn",
    )(q, kvc, kvc, ksd, vsd, kwd, vwd, ng, e_mat, cs_mat)


def _post_kernel(o_ref, ya_ref, x_ref, g1_ref, wg_ref, wn_ref, wm_ref, g2_ref, wr_hi_ref, wr_lo_ref, rb_ref,
                 xa_ref):
    hn = _normed(x_ref[...], g1_ref[...])
    gate_a = jax.nn.sigmoid(_dot(hn, wg_ref[:, 0:D_MODEL]))
    gate_b = jax.nn.sigmoid(_dot(hn, wg_ref[:, D_MODEL:2 * D_MODEL]))
    y_b = _dot(o_ref[...], wn_ref[...])
    mixed = gate_a * ya_ref[...] + gate_b * y_b
    x1 = x_ref[...] + _dot(mixed.astype(BF16), wm_ref[...])
    xa_ref[:, 0:D_MODEL] = x1
    h2 = x1 * _rms_scale(x1) * g2_ref[...]
    h_hi = h2.astype(BF16)
    h_lo = (h2 - h_hi.astype(F32)).astype(BF16)
    logits = (_dot(h_hi, wr_hi_ref[...]) + _dot(h_lo, wr_hi_ref[...]) + _dot(h_hi, wr_lo_ref[...])
              + rb_ref[...])

    lane = _lane_iota(logits.shape).astype(F32)
    is_grp = (lane >= N_EXPERTS) & (lane < N_EXPERTS + N_GROUPS)
    gl = jnp.where(is_grp, logits, NEG_INF)
    ge = jnp.exp(gl - jnp.max(gl, axis=1, keepdims=True))
    gp = ge / jnp.sum(ge, axis=1, keepdims=True)
    g_w = jnp.max(gp, axis=1, keepdims=True)
    big = float(4 * LANES)
    g_first = jnp.min(jnp.where(is_grp & (gp == g_w), lane, big), axis=1, keepdims=True)
    grp_lo = (g_first - N_EXPERTS) * EXPERTS_PER_GROUP

    in_grp = (lane >= grp_lo) & (lane < grp_lo + EXPERTS_PER_GROUP)
    el = jnp.where(in_grp, logits, NEG_INF)
    ee = jnp.exp(el - jnp.max(el, axis=1, keepdims=True))
    ep = ee / jnp.sum(ee, axis=1, keepdims=True)
    w1 = jnp.max(ep, axis=1, keepdims=True)
    i1 = jnp.min(jnp.where(in_grp & (ep == w1), lane, big), axis=1, keepdims=True)
    rest = jnp.where(in_grp & (lane != i1), ep, -1.0)
    w2 = jnp.max(rest, axis=1, keepdims=True)
    i2 = jnp.min(jnp.where(rest == w2, lane, big), axis=1, keepdims=True)
    den = w1 + w2
    comb = jnp.where(lane == i1, g_w * (w1 / den), jnp.where(lane == i2, g_w * (w2 / den), 0.0))
    xa_ref[:, D_MODEL:D_MODEL + LANES] = jnp.where(lane == GROUP_LANE, g_first - N_EXPERTS, comb)


def _post(o_nsa, y_a, x2, g1, w_mg, w_nsa, w_mix, g2, wr_hi, wr_lo, rb):
    t = x2.shape[0]
    tm = TM_POST
    row = lambda i: (i, 0)
    const = lambda i: (0, 0)
    return pl.pallas_call(
        _post_kernel,
        grid=(t // tm,),
        in_specs=[
            pl.BlockSpec((tm, Q_DIM), row),
            pl.BlockSpec((tm, D_MODEL), row),
            pl.BlockSpec((tm, D_MODEL), row),
            pl.BlockSpec((1, D_MODEL), const),
            pl.BlockSpec((D_MODEL, 2 * D_MODEL), const),
            pl.BlockSpec((Q_DIM, D_MODEL), const),
            pl.BlockSpec((D_MODEL, D_MODEL), const),
            pl.BlockSpec((1, D_MODEL), const),
            pl.BlockSpec((D_MODEL, LANES), const),
            pl.BlockSpec((D_MODEL, LANES), const),
            pl.BlockSpec((1, LANES), const),
        ],
        out_specs=pl.BlockSpec((tm, D_AUG), row),
        out_shape=jax.ShapeDtypeStruct((t, D_AUG), F32),
        compiler_params=pltpu.CompilerParams(
            dimension_semantics=("parallel",), vmem_limit_bytes=VMEM_LIMIT),
        name="post",
    )(o_nsa, y_a, x2, g1, w_mg, w_nsa, w_mix, g2, wr_hi, wr_lo, rb)


def _plan_kernel(rec_ref, tri_ref, info_ref, counts_ref, carry_s):
    @pl.when(pl.program_id(0) == 0)
    def _():
        carry_s[...] = jnp.zeros_like(carry_s)

    rec = rec_ref[...]
    lane = _lane_iota(rec.shape)
    gid = rec[:, GROUP_LANE:GROUP_LANE + 1]
    onehot = jnp.where(lane.astype(F32) == gid, 1.0, 0.0)
    before = _dot(tri_ref[...], onehot.astype(BF16)) + carry_s[0:1, :]
    rank = jnp.sum(onehot * before, axis=1, keepdims=True)
    info_ref[...] = jnp.where(lane == 0, rank, jnp.where(lane == 1, gid, 0.0))
    carry_s[...] = carry_s[...] + jnp.sum(onehot, axis=0, keepdims=True)
    counts_ref[...] = carry_s[...]


def _plan(xa, tri):
    t = xa.shape[0]
    tm = TM_PLAN
    return pl.pallas_call(
        _plan_kernel,
        grid=(t // tm,),
        in_specs=[
            pl.BlockSpec((tm, LANES), lambda i: (i, D_MODEL // LANES)),
            pl.BlockSpec((tm, tm), lambda i: (0, 0)),
        ],
        out_specs=(
            pl.BlockSpec((tm, LANES), lambda i: (i, 0)),
            pl.BlockSpec((SUBLANES, LANES), lambda i: (0, 0)),
        ),
        out_shape=(
            jax.ShapeDtypeStruct((t, LANES), F32),
            jax.ShapeDtypeStruct((SUBLANES, LANES), F32),
        ),
        scratch_shapes=[pltpu.VMEM((SUBLANES, LANES), F32)],
        compiler_params=pltpu.CompilerParams(
            dimension_semantics=("arbitrary",), vmem_limit_bytes=VMEM_LIMIT),
        name="plan",
    )(xa, tri)


def _row_copies(n_rows, make_copy):
    def start(g, carry):
        r0 = pl.multiple_of(g * ROW_COPY_UNROLL, ROW_COPY_UNROLL)
        for k in range(ROW_COPY_UNROLL):
            make_copy(r0, k).start()
        return carry

    def wait(g, carry):
        for _ in range(ROW_COPY_UNROLL):
            make_copy(0, 0).wait()
        return carry

    lax.fori_loop(0, n_rows // ROW_COPY_UNROLL, start, 0)
    lax.fori_loop(0, n_rows // ROW_COPY_UNROLL, wait, 0)


def _dispatch_kernel(slot_ref, xa_ref, zero_hbm, xs_hbm, sem):
    del zero_hbm
    first = pl.program_id(0) * xa_ref.shape[0]

    def copy(r0, k):
        slot = slot_ref[first + r0 + k]
        return pltpu.make_async_copy(xa_ref.at[pl.ds(r0 + k, 1), :], xs_hbm.at[pl.ds(slot, 1), :], sem)

    _row_copies(xa_ref.shape[0], copy)


def _dispatch(slot, xa, n_sorted):
    t = xa.shape[0]
    tm = TM_ROWS
    zeros = jnp.zeros((n_sorted, D_AUG), F32)
    return pl.pallas_call(
        _dispatch_kernel,
        grid_spec=pltpu.PrefetchScalarGridSpec(
            num_scalar_prefetch=1,
            grid=(t // tm,),
            in_specs=[
                pl.BlockSpec((tm, D_AUG), lambda i, *_: (i, 0)),
                pl.BlockSpec(memory_space=pl.ANY),
            ],
            out_specs=pl.BlockSpec(memory_space=pl.ANY),
            scratch_shapes=[pltpu.SemaphoreType.DMA],
        ),
        out_shape=jax.ShapeDtypeStruct((n_sorted, D_AUG), F32),
        input_output_aliases={2: 0},
        compiler_params=pltpu.CompilerParams(
            dimension_semantics=("arbitrary",), vmem_limit_bytes=VMEM_LIMIT),
        name="dispatch",
    )(slot, xa, zeros)


def _experts_kernel(widx_ref, used_ref, tgrp_ref, xs_ref, wg_ref, wu_ref, wd_ref, g2_ref, gf_ref,
                    fs_ref, h_s, y_s):
    del widx_ref
    j = pl.program_id(0)
    e = pl.program_id(1)
    used = used_ref[j] == 1

    @pl.when(used & (e == 0))
    def _():
        x1 = xs_ref[:, 0:D_MODEL]
        h_s[...] = _normed(x1, g2_ref[...])
        y_s[...] = jnp.zeros_like(y_s)

    @pl.when(used)
    def _():
        h = h_s[...]
        act = jax.nn.silu(_dot(h, wg_ref[0].astype(BF16))) * _dot(h, wu_ref[0].astype(BF16))
        y = _dot(act.astype(BF16), wd_ref[0].astype(BF16))
        rec = xs_ref[:, D_MODEL:D_AUG]
        col = tgrp_ref[j] * EXPERTS_PER_GROUP + e
        weight = jnp.sum(jnp.where(_lane_iota(rec.shape) == col, rec, 0.0), axis=1, keepdims=True)
        y_s[...] += weight * y

    @pl.when(used & (e == EXPERTS_PER_GROUP - 1))
    def _():
        x = xs_ref[:, 0:D_MODEL] + y_s[...]
        fs_ref[...] = x * _rms_scale(x) * gf_ref[...]

    @pl.when(jnp.logical_not(used) & (e == EXPERTS_PER_GROUP - 1))
    def _():
        fs_ref[...] = jnp.zeros_like(fs_ref)


def _experts(widx, used, tgrp, xs, wg, wu, wd, g2, gf):
    n_sorted = xs.shape[0]
    tm = TM_EXP
    row = lambda j, e, *_: (j, 0)
    wsel = lambda j, e, widx, used, tgrp: (widx[j * EXPERTS_PER_GROUP + e], 0, 0)
    const = lambda j, e, *_: (0, 0)
    return pl.pallas_call(
        _experts_kernel,
        grid_spec=pltpu.PrefetchScalarGridSpec(
            num_scalar_prefetch=3,
            grid=(n_sorted // tm, EXPERTS_PER_GROUP),
            in_specs=[
                pl.BlockSpec((tm, D_AUG), row),
                pl.BlockSpec((1, D_MODEL, D_EXPERT), wsel),
                pl.BlockSpec((1, D_MODEL, D_EXPERT), wsel),
                pl.BlockSpec((1, D_EXPERT, D_MODEL), wsel),
                pl.BlockSpec((1, D_MODEL), const),
                pl.BlockSpec((1, D_MODEL), const),
            ],
            out_specs=pl.BlockSpec((tm, D_MODEL), row),
            scratch_shapes=[pltpu.VMEM((tm, D_MODEL), BF16), pltpu.VMEM((tm, D_MODEL), F32)],
        ),
        out_shape=jax.ShapeDtypeStruct((n_sorted, D_MODEL), F32),
        compiler_params=pltpu.CompilerParams(
            dimension_semantics=("arbitrary", "arbitrary"), vmem_limit_bytes=VMEM_LIMIT),
        name="experts",
    )(widx, used, tgrp, xs, wg, wu, wd, g2, gf)


def _combine_kernel(slot_ref, fs_hbm, o_ref, sem):
    first = pl.program_id(0) * o_ref.shape[0]

    def copy(r0, k):
        slot = slot_ref[first + r0 + k]
        return pltpu.make_async_copy(fs_hbm.at[pl.ds(slot, 1), :], o_ref.at[pl.ds(r0 + k, 1), :], sem)

    _row_copies(o_ref.shape[0], copy)


def _combine(slot, fs, t):
    tm = TM_ROWS
    return pl.pallas_call(
        _combine_kernel,
        grid_spec=pltpu.PrefetchScalarGridSpec(
            num_scalar_prefetch=1,
            grid=(t // tm,),
            in_specs=[pl.BlockSpec(memory_space=pl.ANY)],
            out_specs=pl.BlockSpec((tm, D_MODEL), lambda i, *_: (i, 0)),
            scratch_shapes=[pltpu.SemaphoreType.DMA],
        ),
        out_shape=jax.ShapeDtypeStruct((t, D_MODEL), F32),
        compiler_params=pltpu.CompilerParams(
            dimension_semantics=("arbitrary",), vmem_limit_bytes=VMEM_LIMIT),
        name="combine",
    )(slot, fs)


def _moe_sorted(xa, wg, wu, wd, g2, gf):
    t = xa.shape[0]
    tri = jnp.asarray(np.tril(np.ones((TM_PLAN, TM_PLAN), np.float32), -1), dtype=BF16)
    info, counts = _plan(xa, tri)
    rank = info[:, 0].astype(jnp.int32)
    gid = info[:, 1].astype(jnp.int32)
    counts = counts[0, :N_GROUPS].astype(jnp.int32)

    n_tiles = t // TM_EXP + N_GROUPS
    tiles_g = (counts + TM_EXP - 1) // TM_EXP
    tile_end = jnp.cumsum(tiles_g)
    base = (tile_end - tiles_g) * TM_EXP
    groups = jnp.arange(N_GROUPS, dtype=jnp.int32)
    slot = jnp.sum(jnp.where(gid[:, None] == groups[None, :], base[None, :], 0), axis=1) + rank
    tile_ids = jnp.arange(n_tiles, dtype=jnp.int32)
    used = (tile_ids < tile_end[-1]).astype(jnp.int32)
    last = jnp.maximum(tile_end[-1] - 1, 0)
    tgrp = jnp.sum((jnp.minimum(tile_ids, last)[:, None] >= tile_end[None, :]).astype(jnp.int32), axis=1)
    step_e = jnp.arange(EXPERTS_PER_GROUP, dtype=jnp.int32)[None, :]
    widx = jnp.where(used[:, None] == 1, tgrp[:, None] * EXPERTS_PER_GROUP + step_e,
                     tgrp[:, None] * EXPERTS_PER_GROUP + EXPERTS_PER_GROUP - 1).reshape(-1)

    xs = _dispatch(slot, xa, n_tiles * TM_EXP)
    fs = _experts(widx, used, tgrp, xs, wg, wu, wd, g2, gf)
    return _combine(slot, fs, t)


def _split_w_in(w):
    sizes = [D_RNN, D_RNN, Q_DIM] + [KV_DIM] * 6 + [3 * N_HEADS, 2 * D_MODEL]
    pts = np.cumsum(sizes)[:-1]
    return jnp.split(w, [int(p) for p in pts], axis=-1)


def _proj_weights(w_in):
    xr, gr, q, kc, vc, ksl, vsl, kw, vw, nsa_g, merge_g = _split_w_in(w_in)
    ng = nsa_g.reshape(D_MODEL, 3, N_KV_GROUPS, HEADS_PER_GROUP).transpose(0, 2, 1, 3)
    ng = ng.reshape(D_MODEL, N_KV_GROUPS, 3 * HEADS_PER_GROUP)
    ng = jnp.pad(ng, ((0, 0), (0, 0), (0, LANES - 3 * HEADS_PER_GROUP))).reshape(D_MODEL, N_KV_GROUPS * LANES)
    w_attn = jnp.concatenate([q, kc, vc, ksl, vsl, kw, vw, ng], axis=1)
    return (jnp.concatenate([xr, gr], axis=1).astype(BF16), w_attn.astype(BF16), merge_g.astype(BF16))


def _rope_tables(pos, width):
    inv_freq = ROPE_THETA ** (-(jnp.arange(0, HEAD_DIM, 2, dtype=F32) / HEAD_DIM))
    ang = pos.astype(F32)[:, None] * inv_freq[None, :]
    cos, sin = jnp.cos(ang), jnp.sin(ang)
    reps = width // HEAD_DIM
    return jnp.tile(jnp.concatenate([cos, cos], axis=1), (1, reps)), jnp.tile(jnp.concatenate([-sin, sin], axis=1), (1, reps))


def _block_diag(w, per):
    nb, d = w.shape[0], w.shape[1]
    w = w.reshape(nb // per, per, d, d)
    eye = jnp.eye(per, dtype=w.dtype)
    return jnp.einsum('npij,pq->npiqj', w, eye).reshape(nb // per, per * d, per * d)


def _cmp_to_sel():
    n_sel_pad = LANES
    c0 = np.arange(N_CMP_PAD) * CMP_STRIDE
    s0 = np.arange(n_sel_pad) * SEL_BLOCK
    ov = np.minimum(c0[:, None] + CMP_BLOCK, s0[None, :] + SEL_BLOCK) - np.maximum(c0[:, None], s0[None, :])
    m = np.clip(ov, 0, None) / CMP_BLOCK
    m[:, SEL_BLOCK // 2:] = 0.0
    m[N_CMP_PAD - 1, :] = 0.0
    return m.astype(np.float32)


def kernel(x, norm1_g, w_in, conv_w, conv_b, lru_wa, lru_ba, lru_wi, lru_bi, lru_lambda, w_rnn_out, cmpk_pos, cmpk_w1, cmpk_b1, cmpk_w2, cmpk_b2, cmpv_pos, cmpv_w1, cmpv_b1, cmpv_w2, cmpv_b2, w_nsa_out, w_mix_out, norm2_g, router_group_w, router_group_b, router_expert_w, router_expert_b, expert_w_gate, expert_w_up, expert_w_down, final_norm_g):
    batch, seq, _ = x.shape
    t = batch * seq
    assert w_in.shape[0] == 1, "the final norm is fused into the expert kernel: single layer only"
    x2 = x.reshape(t, D_MODEL)

    cos, sin = _rope_tables(jnp.arange(seq), LANES)
    cmp_ends = jnp.arange(N_CMP_PAD) * CMP_STRIDE + (CMP_BLOCK - 1)
    ccos, csin = _rope_tables(cmp_ends, LANES)
    reps = TR_CMP // N_CMP_PAD
    cmp_cos = jnp.stack([jnp.tile(ccos, (reps, 1)), jnp.ones((TR_CMP, LANES), F32)])
    cmp_sin = jnp.stack([jnp.tile(csin, (reps, 1)), jnp.zeros((TR_CMP, LANES), F32)])
    key_blk = np.arange(seq)[:, None] // SEL_BLOCK
    e_mat = jnp.asarray((key_blk == np.arange(LANES)[None, :]).astype(np.float32), dtype=BF16)
    cs_mat = jnp.asarray(_cmp_to_sel(), dtype=BF16)

    l = 0
    w_xg, w_attn, w_mg = _proj_weights(w_in[l])
    g1 = norm1_g[l][None, :]
    q, kcv, ksd, vsd, kwd, vwd, ng = _inproj(x2, g1, w_attn, cos, sin, seq)

    y_a = _rnn(x2, g1, w_xg, conv_w[l], conv_b[l][None, :],
               _block_diag(lru_wa[l], 4).astype(BF16), lru_ba[l][None, :],
               _block_diag(lru_wi[l], 4).astype(BF16), lru_bi[l][None, :],
               lru_lambda[l][None, :], w_rnn_out[l].astype(BF16), batch, seq)

    half = CMP_STRIDE * HEAD_DIM
    pos = jnp.stack([cmpk_pos[l].reshape(2, half), cmpv_pos[l].reshape(2, half)])
    w1 = jnp.stack([cmpk_w1[l], cmpv_w1[l]]).astype(BF16)
    b1 = jnp.stack([cmpk_b1[l], cmpv_b1[l]])[:, None, :]
    w2 = jnp.stack([cmpk_w2[l], cmpv_w2[l]])
    w2 = jnp.concatenate([w2, w2], axis=2).astype(BF16)
    b2 = jnp.stack([cmpk_b2[l], cmpv_b2[l]])
    b2 = jnp.concatenate([b2, b2], axis=1)[:, None, :]
    kvc = _compress(kcv, pos, w1, b1, w2, b2, cmp_cos, cmp_sin, batch, seq)

    o_nsa = _attn(q, kvc, ksd, vsd, kwd, vwd, ng, e_mat, cs_mat, batch, seq)

    wr = jnp.concatenate([
        router_expert_w[l].transpose(1, 0, 2).reshape(D_MODEL, N_EXPERTS),
        router_group_w[l],
        jnp.zeros((D_MODEL, LANES - N_EXPERTS - N_GROUPS), F32)], axis=1)
    wr_hi = wr.astype(BF16)
    wr_lo = (wr - wr_hi.astype(F32)).astype(BF16)
    rb = jnp.concatenate([router_expert_b[l].reshape(N_EXPERTS), router_group_b[l],
                          jnp.zeros((LANES - N_EXPERTS - N_GROUPS,), F32)])[None, :]
    xa = _post(o_nsa, y_a, x2, g1, w_mg, w_nsa_out[l].astype(BF16), w_mix_out[l].astype(BF16),
               norm2_g[l][None, :], wr_hi, wr_lo, rb)

    out = _moe_sorted(xa, expert_w_gate[l], expert_w_up[l], expert_w_down[l],
                      norm2_g[l][None, :], final_norm_g[None, :])
    return out.reshape(batch, seq, D_MODEL)
```

```python
import functools

import numpy as np
import jax
import jax.numpy as jnp
from jax import lax
from jax.experimental import pallas as pl
from jax.experimental.pallas import tpu as pltpu

D_MODEL = 1024
D_RNN = 1024
RNN_BLOCKS = 16
RNN_BLOCK_DIM = D_RNN // RNN_BLOCKS
CONV_WIDTH = 4
LRU_C = 8.0
N_HEADS = 16
HEAD_DIM = 64
HALF_DIM = HEAD_DIM // 2
N_KV_GROUPS = 4
HEADS_PER_GROUP = N_HEADS // N_KV_GROUPS
Q_DIM = N_HEADS * HEAD_DIM
KV_DIM = N_KV_GROUPS * HEAD_DIM
CMP_BLOCK = 32
CMP_STRIDE = 16
CMP_HIDDEN = 256
SEL_BLOCK = 64
SEL_TOPN = 16
WINDOW = 512
ROPE_THETA = 10000.0
FORCE_BONUS = 1e4
NEG_INF = -1e30
N_GROUPS = 4
EXPERTS_PER_GROUP = 4
N_EXPERTS = N_GROUPS * EXPERTS_PER_GROUP
D_EXPERT = 512
EPS = 1e-6

Q_SCALE = HEAD_DIM ** -0.5 * float(np.log2(np.e))

LANES = 128
SUBLANES = 8
VMEM_LIMIT = 56 * 1024 * 1024

BF16 = jnp.bfloat16
F32 = jnp.float32

COL_Q = 0
COL_KCV = COL_Q + Q_DIM
COL_KS = COL_KCV + 2 * KV_DIM
COL_VS = COL_KS + KV_DIM
COL_KW = COL_VS + KV_DIM
COL_VW = COL_KW + KV_DIM
COL_NG = COL_VW + KV_DIM
N_PROJ = COL_NG + N_KV_GROUPS * LANES

TM_PROJ = 512
TS_RNN = 1024
TR_CMP = 256
TQ = 256
TQ_SELECT_TILE = 256
TK = 128
CK_SEL = 512
TM_POST = 512
TM_PLAN = 1024
TM_ROWS = 2048
TM_EXP = 1024
ROW_COPY_UNROLL = 8
D_AUG = D_MODEL + LANES
GROUP_LANE = N_EXPERTS
N_CMP_PAD = 128


def _dot(a, b):
    return jnp.dot(a, b, preferred_element_type=F32)


def _dot_t(a, b):
    return lax.dot_general(a, b, (((1,), (1,)), ((), ())), preferred_element_type=F32)


def _lane_iota(shape):
    return lax.broadcasted_iota(jnp.int32, shape, len(shape) - 1)


def _row_iota(shape):
    return lax.broadcasted_iota(jnp.int32, shape, 0)


def _rope(x, cos, sin_signed):
    width = x.shape[-1]
    reps = width // cos.shape[-1]
    if reps > 1:
        cos = jnp.concatenate([cos] * reps, axis=1)
        sin_signed = jnp.concatenate([sin_signed] * reps, axis=1)
    first_half = (_lane_iota(x.shape) & (HEAD_DIM - 1)) < HALF_DIM
    partner = jnp.where(first_half, pltpu.roll(x, width - HALF_DIM, 1), pltpu.roll(x, HALF_DIM, 1))
    return x * cos + partner * sin_signed


def _spread_heads(x, fill=None):
    out = []
    low = _lane_iota((x.shape[0], LANES)) < HEAD_DIM
    for c in range(x.shape[1] // LANES):
        xc = x[:, c * LANES:(c + 1) * LANES]
        rolled = pltpu.roll(xc, HEAD_DIM, 1)
        out.append(jnp.where(low, xc, rolled if fill is None else fill))
        out.append(jnp.where(low, rolled, xc if fill is None else fill))
    return jnp.concatenate(out, axis=1)


def _sigmoid(x):
    return 0.5 * (jnp.tanh(0.5 * x) + 1.0)


def _rms_scale(x):
    sq = x * x
    part = sq[:, 0:LANES]
    for c in range(1, x.shape[1] // LANES):
        part = part + sq[:, c * LANES:(c + 1) * LANES]
    return lax.rsqrt(jnp.sum(part, axis=1, keepdims=True) * (1.0 / x.shape[1]) + EPS)


def _normed(x, g):
    return (x * _rms_scale(x) * g).astype(BF16)


def _inproj_kernel(x_ref, g_ref, w_ref, cos_ref, sin_ref,
                   q_ref, kcv_ref, ks_ref, vs_ref, kw_ref, vw_ref, ng_ref):
    h = _normed(x_ref[...], g_ref[...])
    cos = cos_ref[...]
    sin = sin_ref[...]

    def mm(lo, width):
        return _dot(h, w_ref[:, lo:lo + width])

    q_ref[...] = (_rope(mm(COL_Q, Q_DIM), cos, sin) * Q_SCALE).astype(BF16)
    kcv_ref[...] = mm(COL_KCV, 2 * KV_DIM)
    ks_ref[...] = _spread_heads(_rope(mm(COL_KS, KV_DIM), cos, sin)).astype(BF16)
    vs_ref[...] = _spread_heads(mm(COL_VS, KV_DIM), 1.0).astype(BF16)
    kw_ref[...] = _spread_heads(_rope(mm(COL_KW, KV_DIM), cos, sin)).astype(BF16)
    vw_ref[...] = _spread_heads(mm(COL_VW, KV_DIM), 1.0).astype(BF16)
    ng_ref[...] = jax.nn.sigmoid(mm(COL_NG, N_KV_GROUPS * LANES))


def _inproj(x2, norm_g, w_proj, cos, sin, seq):
    t = x2.shape[0]
    tm = TM_PROJ
    pos_blocks = seq // tm
    row = lambda i: (i, 0)
    const = lambda i: (0, 0)
    out_shape = (
        jax.ShapeDtypeStruct((t, Q_DIM), BF16),
        jax.ShapeDtypeStruct((t, 2 * KV_DIM), F32),
        jax.ShapeDtypeStruct((t, N_KV_GROUPS * LANES), BF16),
        jax.ShapeDtypeStruct((t, N_KV_GROUPS * LANES), BF16),
        jax.ShapeDtypeStruct((t, N_KV_GROUPS * LANES), BF16),
        jax.ShapeDtypeStruct((t, N_KV_GROUPS * LANES), BF16),
        jax.ShapeDtypeStruct((t, N_KV_GROUPS * LANES), F32),
    )
    return pl.pallas_call(
        _inproj_kernel,
        grid=(t // tm,),
        in_specs=[
            pl.BlockSpec((tm, D_MODEL), row),
            pl.BlockSpec((1, D_MODEL), const),
            pl.BlockSpec((D_MODEL, N_PROJ), const, pipeline_mode=pl.Buffered(1)),
            pl.BlockSpec((tm, LANES), lambda i: (i % pos_blocks, 0)),
            pl.BlockSpec((tm, LANES), lambda i: (i % pos_blocks, 0)),
        ],
        out_specs=tuple(pl.BlockSpec((tm, s.shape[1]), row) for s in out_shape),
        out_shape=out_shape,
        compiler_params=pltpu.CompilerParams(
            dimension_semantics=("parallel",), vmem_limit_bytes=VMEM_LIMIT),
        name="inproj",
    )(x2, norm_g, w_proj, cos, sin)


def _rnn_kernel(xin_ref, g1_ref, wx_ref, cw_ref, cb_ref, wa_ref, ba_ref, wi_ref, bi_ref, lam_ref, wo_ref,
                y_ref, tail_s, carry_s, a_s, u_s, h_s):
    ts = xin_ref.shape[0]

    @pl.when(pl.program_id(1) == 0)
    def _():
        tail_s[...] = jnp.zeros_like(tail_s)
        carry_s[...] = jnp.zeros_like(carry_s)

    hn = _normed(xin_ref[...], g1_ref[...])
    x = _dot(hn, wx_ref[:, 0:D_RNN])
    gate_pre = _dot(hn, wx_ref[:, D_RNN:2 * D_RNN])
    xext = jnp.concatenate([tail_s[...], x], axis=0)
    tail_s[...] = x[ts - SUBLANES:ts, :]
    conv = cb_ref[...]
    for k in range(CONV_WIDTH):
        back = CONV_WIDTH - 1 - k
        shifted = xext if back == 0 else pltpu.roll(xext, back, 0)
        conv = conv + cw_ref[k:k + 1, :] * shifted[SUBLANES:SUBLANES + ts, :]

    cb16 = conv.astype(BF16)
    blk = wa_ref.shape[1]
    r_pre = jnp.concatenate(
        [_dot(cb16[:, j * blk:(j + 1) * blk], wa_ref[j]) for j in range(D_RNN // blk)], axis=1)
    i_pre = jnp.concatenate(
        [_dot(cb16[:, j * blk:(j + 1) * blk], wi_ref[j]) for j in range(D_RNN // blk)], axis=1)
    r = _sigmoid(r_pre + ba_ref[...])
    gate_i = _sigmoid(i_pre + bi_ref[...])
    neg_lam = -lam_ref[...]
    softplus = jnp.maximum(neg_lam, 0.0) + jnp.log1p(jnp.exp(-jnp.abs(neg_lam)))
    log_a = r * ((-LRU_C) * softplus)
    a = jnp.exp(log_a)
    a_s[...] = a
    var = -jnp.tanh(log_a) * (a * a + 1.0)
    u_s[...] = jnp.where(var > 0.0, var * lax.rsqrt(var), 0.0) * (gate_i * conv)

    row = _row_iota((SUBLANES, D_RNN))

    def body(k, carry):
        off = pl.multiple_of(k * SUBLANES, SUBLANES)
        a = a_s[pl.ds(off, SUBLANES), :]
        b = u_s[pl.ds(off, SUBLANES), :]
        for sh in (1, 2, 4):
            keep = row >= sh
            a_prev = jnp.where(keep, pltpu.roll(a, sh, 0), 1.0)
            b_prev = jnp.where(keep, pltpu.roll(b, sh, 0), 0.0)
            b = a * b_prev + b
            a = a * a_prev
        h = a * carry + b
        h_s[pl.ds(off, SUBLANES), :] = h
        return jnp.broadcast_to(h[SUBLANES - 1:SUBLANES, :], (SUBLANES, D_RNN))

    carry_s[...] = lax.fori_loop(0, ts // SUBLANES, body, carry_s[...])
    gated = (jax.nn.gelu(gate_pre) * h_s[...]).astype(BF16)
    y_ref[...] = _dot(gated, wo_ref[...])


def _rnn(x2, norm_g, w_xg, conv_w, conv_b, wa_bd, ba, wi_bd, bi, lam, w_out, batch, seq):
    t = x2.shape[0]
    ts = TS_RNN
    nt = seq // ts
    const2 = lambda b, s: (0, 0)
    const3 = lambda b, s: (0, 0, 0)
    nblk, blk = wa_bd.shape[0], wa_bd.shape[1]
    return pl.pallas_call(
        _rnn_kernel,
        grid=(batch, nt),
        in_specs=[
            pl.BlockSpec((ts, D_MODEL), lambda b, s: (b * nt + s, 0)),
            pl.BlockSpec((1, D_MODEL), const2),
            pl.BlockSpec((D_MODEL, 2 * D_RNN), const2),
            pl.BlockSpec((CONV_WIDTH, D_RNN), const2),
            pl.BlockSpec((1, D_RNN), const2),
            pl.BlockSpec((nblk, blk, blk), const3),
            pl.BlockSpec((1, D_RNN), const2),
            pl.BlockSpec((nblk, blk, blk), const3),
            pl.BlockSpec((1, D_RNN), const2),
            pl.BlockSpec((1, D_RNN), const2),
            pl.BlockSpec((D_RNN, D_MODEL), const2),
        ],
        out_specs=pl.BlockSpec((ts, D_MODEL), lambda b, s: (b * nt + s, 0)),
        out_shape=jax.ShapeDtypeStruct((t, D_MODEL), F32),
        scratch_shapes=[
            pltpu.VMEM((SUBLANES, D_RNN), F32),
            pltpu.VMEM((SUBLANES, D_RNN), F32),
            pltpu.VMEM((ts, D_RNN), F32),
            pltpu.VMEM((ts, D_RNN), F32),
            pltpu.VMEM((ts, D_RNN), F32),
        ],
        compiler_params=pltpu.CompilerParams(
            dimension_semantics=("parallel", "arbitrary"), vmem_limit_bytes=VMEM_LIMIT),
        name="rnn",
    )(x2, norm_g, w_xg, conv_w, conv_b, wa_bd, ba, wi_bd, bi, lam, w_out)


def _compress_kernel(x_ref, pos_ref, w1_ref, b1_ref, w2_ref, b2_ref, cos_ref, sin_ref, o_ref):
    n_rows = x_ref.shape[0] // CMP_STRIDE
    strided = [x_ref[pl.ds(r, n_rows, stride=CMP_STRIDE), :] for r in range(CMP_STRIDE)]
    low = _lane_iota((n_rows, LANES)) < HEAD_DIM
    groups = []
    for g in range(2):
        tiles = []
        for j in range(CMP_STRIDE // 2):
            even = strided[2 * j]
            odd = strided[2 * j + 1]
            if g == 0:
                tiles.append(jnp.where(low, even, pltpu.roll(odd, HEAD_DIM, 1)))
            else:
                tiles.append(jnp.where(low, pltpu.roll(even, HEAD_DIM, 1), odd))
        groups.append(jnp.concatenate(tiles, axis=1))
    x = jnp.concatenate(groups, axis=0)
    tr = x.shape[0]
    half = x.shape[1]
    first = _dot((x + pos_ref[0, 0:1, :]).astype(BF16), w1_ref[0, 0:half, :])
    second = _dot((x + pos_ref[0, 1:2, :]).astype(BF16), w1_ref[0, half:2 * half, :])
    hid = jax.nn.gelu(first + pltpu.roll(second, tr - 1, 0) + b1_ref[0])
    out = _dot(hid.astype(BF16), w2_ref[0]) + b2_ref[0]
    o_ref[0] = _rope(out, cos_ref[0], sin_ref[0]).astype(BF16)


def _compress(kcv, pos, w1, b1, w2, b2, cos, sin, batch, seq):
    pairs = KV_DIM // LANES
    tr = TR_CMP
    rows = batch * pairs * tr
    sel = lambda k, r: (k, 0, 0)
    return pl.pallas_call(
        _compress_kernel,
        grid=(2, batch * pairs),
        in_specs=[
            pl.BlockSpec((seq, LANES), lambda k, r: (r // pairs, k * pairs + r % pairs)),
            pl.BlockSpec((1, 2, pos.shape[2]), sel),
            pl.BlockSpec((1,) + w1.shape[1:], sel),
            pl.BlockSpec((1, 1, CMP_HIDDEN), sel),
            pl.BlockSpec((1,) + w2.shape[1:], sel),
            pl.BlockSpec((1, 1, LANES), sel),
            pl.BlockSpec((1, tr, LANES), sel),
            pl.BlockSpec((1, tr, LANES), sel),
        ],
        out_specs=pl.BlockSpec((1, tr, LANES), lambda k, r: (k, r, 0)),
        out_shape=jax.ShapeDtypeStruct((2, rows, LANES), BF16),
        compiler_params=pltpu.CompilerParams(
            dimension_semantics=("parallel", "parallel"), vmem_limit_bytes=VMEM_LIMIT),
        name="compress",
    )(kcv, pos, w1, b1, w2, b2, cos, sin)


def _lane_tile_max(s):
    tiles = [s[:, c * LANES:(c + 1) * LANES] for c in range(s.shape[1] // LANES)]
    while len(tiles) > 1:
        tiles = [jnp.maximum(tiles[k], tiles[k + 1]) if k + 1 < len(tiles) else tiles[k]
                 for k in range(0, len(tiles), 2)]
    return tiles[0]


def _stack_heads(q):
    tq = q.shape[0]
    low = _lane_iota((tq, LANES)) < HEAD_DIM
    zero = jnp.zeros((tq, LANES), BF16)
    heads = []
    for hh in range(HEADS_PER_GROUP):
        pair = q[:, (hh // 2) * LANES:(hh // 2 + 1) * LANES]
        heads.append(jnp.where(low if hh % 2 == 0 else jnp.logical_not(low), pair, zero))
    return jnp.concatenate(heads, axis=0)


def _select_tile(i, q, kc_ref, vc_ref, ng, m_ref):
    tq = q.shape[0]
    rows = HEADS_PER_GROUP * tq
    low = _lane_iota((tq, LANES)) < HEAD_DIM
    q4 = _stack_heads(q)
    lane4 = _lane_iota((rows, LANES))
    qpos4 = i * tq + (_row_iota((rows, LANES)) & (tq - 1))

    sc = _dot_t(q4, kc_ref[0])
    sc = jnp.where(lane4 * CMP_STRIDE + (CMP_BLOCK - 1) <= qpos4, sc, NEG_INF)
    mc = jnp.max(sc, axis=1, keepdims=True)
    pc = jnp.exp2(sc - mc)
    pc = pc / jnp.sum(pc, axis=1, keepdims=True)
    pc = jnp.where(qpos4 >= CMP_BLOCK - 1, pc, 0.0)
    o_cmp = _dot(pc.astype(BF16), vc_ref[0])

    psum = pc[0:tq] + pc[tq:2 * tq] + pc[2 * tq:3 * tq] + pc[3 * tq:4 * tq]
    p_hi = psum.astype(BF16)
    rem = psum - p_hi.astype(F32)
    p_mid = rem.astype(BF16)
    p_lo = (rem - p_mid.astype(F32)).astype(BF16)
    cs = m_ref[...]
    p_slc = _dot(p_hi, cs) + _dot(p_mid, cs) + _dot(p_lo, cs)

    n_sel = SEL_BLOCK // 2
    p_slc_t = p_slc.T[0:n_sel, :]
    blk = _row_iota((n_sel, tq))
    tblk = (i * tq + _lane_iota((n_sel, tq))) >> 6
    forced = (blk == 0) | (blk == tblk) | (blk == tblk - 1)
    score = jnp.where(blk <= tblk, p_slc_t + jnp.where(forced, FORCE_BONUS, 0.0), -1.0)
    rank = jnp.zeros((n_sel, tq), F32)
    for j in range(n_sel):
        sj = score[j:j + 1, :]
        beats = (sj > score) | ((sj == score) & (blk > j))
        rank = rank + jnp.where(beats, 1.0, 0.0)
    bias_t = jnp.where(rank < SEL_TOPN, 0.0, NEG_INF)
    bias = jnp.concatenate([bias_t, jnp.zeros((LANES - n_sel, tq), F32)], axis=0).T.astype(BF16)

    gated = [ng[:, hh:hh + 1] * o_cmp[hh * tq:(hh + 1) * tq] for hh in range(HEADS_PER_GROUP)]
    o_pairs = jnp.concatenate([jnp.where(low, gated[0], gated[1]), jnp.where(low, gated[2], gated[3])], axis=1)
    return bias, o_pairs


def _attn_kernel(*refs):
    n_full = pl.program_id(2) // (CK_SEL // TQ)
    for n_chunks in range(1, refs[3].shape[0] // CK_SEL + 1):
        pl.when(n_full == n_chunks - 1)(functools.partial(_attn_step, n_chunks, *refs))


def _attn_step(n_chunks, q_ref, kc_ref, vc_ref, ks_ref, vs_ref, kw_ref, vw_ref, ng_ref, e_ref, m_ref, o_ref):
    i = pl.program_id(2)
    tq = q_ref.shape[0]
    rows = HEADS_PER_GROUP * tq
    ck = CK_SEL
    low = _lane_iota((tq, LANES)) < HEAD_DIM
    q4 = _stack_heads(q_ref[...])

    st = TQ_SELECT_TILE
    picks = [_select_tile(i * (tq // st) + sub, q_ref[sub * st:(sub + 1) * st, :], kc_ref, vc_ref,
                          ng_ref[sub * st:(sub + 1) * st, :], m_ref) for sub in range(tq // st)]
    bias = jnp.concatenate([p[0] for p in picks], axis=0)
    o_cmp = jnp.concatenate([p[1] for p in picks], axis=0)

    n_sub = tq // TK
    back = WINDOW // TK
    k_tiles, v_tiles = [], []
    for jj in range(back + n_sub):
        j = i * n_sub - back + jj
        off = pl.multiple_of(jnp.maximum(j, 0) * TK, TK)
        k_tiles.append(kw_ref[pl.ds(off, TK), :])
        v_tiles.append(vw_ref[pl.ds(off, TK), :])
    sw = _dot_t(q4, jnp.concatenate(k_tiles, axis=0))
    on_or_after = _row_iota((TK, TK)) >= _lane_iota((TK, TK))
    sw_tiles = []
    for jj in range(back + n_sub):
        blocks = []
        for blk_i in range(rows // TK):
            piece = sw[blk_i * TK:(blk_i + 1) * TK, jj * TK:(jj + 1) * TK]
            tiles_back = blk_i % n_sub + back - jj
            if tiles_back == 0:
                piece = jnp.where(on_or_after, piece, NEG_INF)
            elif tiles_back == back:
                piece = jnp.where(on_or_after, NEG_INF, piece)
            elif tiles_back < 0 or tiles_back > back:
                piece = jnp.full((TK, TK), NEG_INF, F32)
            blocks.append(piece)
        tile = jnp.concatenate(blocks, axis=0)
        if jj < back:
            tile = jnp.where(i * n_sub - back + jj >= 0, tile, NEG_INF)
        sw_tiles.append(tile)
    sw = jnp.concatenate(sw_tiles, axis=1)
    m_win = jnp.max(_lane_tile_max(sw), axis=1, keepdims=True)
    acc_win = _dot(jnp.exp2(sw - m_win).astype(BF16), jnp.concatenate(v_tiles, axis=0))

    qa = jnp.concatenate([q4, jnp.concatenate([bias] * HEADS_PER_GROUP, axis=0)], axis=1)
    rel_q = i * tq - (n_chunks - 1) * ck + (_row_iota((rows, ck)) & (tq - 1))
    m_run = None
    acc_sel = None
    for c in range(n_chunks):
        ka = jnp.concatenate([ks_ref[c * ck:(c + 1) * ck, :], e_ref[c * ck:(c + 1) * ck, :]], axis=1)
        s = _dot_t(qa, ka)
        if c == n_chunks - 1:
            s = jnp.where(_lane_iota((rows, ck)) <= rel_q, s, NEG_INF)
        m_new = jnp.max(_lane_tile_max(s), axis=1, keepdims=True)
        if c > 0:
            m_new = jnp.maximum(m_run, m_new)
            acc_sel = acc_sel * jnp.exp2(m_run - m_new)
        part = _dot(jnp.exp2(s - m_new).astype(BF16), vs_ref[c * ck:(c + 1) * ck, :])
        acc_sel = part if c == 0 else acc_sel + part
        m_run = m_new

    ng = ng_ref[...]
    outs = []
    for hh in range(HEADS_PER_GROUP):
        r0 = hh * tq
        col = lambda br: ng[:, br * HEADS_PER_GROUP + hh:br * HEADS_PER_GROUP + hh + 1]
        parts = []
        for acc in (acc_sel[r0:r0 + tq], acc_win[r0:r0 + tq]):
            swapped = pltpu.roll(acc, HEAD_DIM, 1)
            parts.append(acc / swapped if hh % 2 == 0 else swapped / acc)
        outs.append(col(1) * parts[0] + col(2) * parts[1])
    o_ref[...] = (o_cmp + jnp.concatenate(
        [jnp.where(low, outs[0], outs[1]), jnp.where(low, outs[2], outs[3])], axis=1)).astype(o_ref.dtype)


def _attn(q, kvc, ksd, vsd, kwd, vwd, ng, e_mat, cs_mat, batch, seq):
    t = q.shape[0]
    tq = TQ
    nq = seq // tq
    qrow = lambda b, g, i: (b * nq + i, g)
    kv = lambda b, g, i: (b, g)
    const = lambda b, g, i: (0, 0)
    return pl.pallas_call(
        _attn_kernel,
        grid=(batch, N_KV_GROUPS, nq),
        in_specs=[
            pl.BlockSpec((tq, HEADS_PER_GROUP * HEAD_DIM), qrow),
            pl.BlockSpec((1, N_CMP_PAD, LANES), lambda b, g, i: (0, b * N_KV_GROUPS + g, 0)),
            pl.BlockSpec((1, N_CMP_PAD, LANES), lambda b, g, i: (1, b * N_KV_GROUPS + g, 0)),
            pl.BlockSpec((seq, LANES), kv),
            pl.BlockSpec((seq, LANES), kv),
            pl.BlockSpec((seq, LANES), kv),
            pl.BlockSpec((seq, LANES), kv),
            pl.BlockSpec((tq, LANES), qrow),
            pl.BlockSpec((seq, LANES), const),
            pl.BlockSpec((N_CMP_PAD, LANES), const),
        ],
        out_specs=pl.BlockSpec((tq, HEADS_PER_GROUP * HEAD_DIM), qrow),
        out_shape=jax.ShapeDtypeStruct((t, Q_DIM), BF16),
        compiler_params=pltpu.CompilerParams(
            dimension_semantics=("parallel", "parallel", "arbitrary"), vmem_limit_bytes=VMEM_LIMIT),
        name="attn",
    )(q, kvc, kvc, ksd, vsd, kwd, vwd, ng, e_mat, cs_mat)


def _post_kernel(o_ref, ya_ref, x_ref, g1_ref, wg_ref, wn_ref, wm_ref, g2_ref, wr_ref, rb_ref,
                 xa_ref):
    hn = _normed(x_ref[...], g1_ref[...])
    gate_a = jax.nn.sigmoid(_dot(hn, wg_ref[:, 0:D_MODEL]))
    gate_b = jax.nn.sigmoid(_dot(hn, wg_ref[:, D_MODEL:2 * D_MODEL]))
    y_b = _dot(o_ref[...], wn_ref[...])
    mixed = gate_a * ya_ref[...] + gate_b * y_b
    x1 = x_ref[...] + _dot(mixed.astype(BF16), wm_ref[...])
    xa_ref[:, 0:D_MODEL] = x1
    h2 = x1 * _rms_scale(x1) * g2_ref[...]
    h_hi = h2.astype(BF16)
    h_lo = (h2 - h_hi.astype(F32)).astype(BF16)
    both = _dot(h_hi, wr_ref[...])
    logits = both[:, 0:LANES] + both[:, LANES:2 * LANES] + _dot(h_lo, wr_ref[:, 0:LANES]) + rb_ref[...]

    lane = _lane_iota(logits.shape).astype(F32)
    is_grp = (lane >= N_EXPERTS) & (lane < N_EXPERTS + N_GROUPS)
    gl = jnp.where(is_grp, logits, NEG_INF)
    ge = jnp.exp(gl - jnp.max(gl, axis=1, keepdims=True))
    gp = ge / jnp.sum(ge, axis=1, keepdims=True)
    g_w = jnp.max(gp, axis=1, keepdims=True)
    big = float(4 * LANES)
    g_first = jnp.min(jnp.where(is_grp & (gp == g_w), lane, big), axis=1, keepdims=True)
    grp_lo = (g_first - N_EXPERTS) * EXPERTS_PER_GROUP

    in_grp = (lane >= grp_lo) & (lane < grp_lo + EXPERTS_PER_GROUP)
    el = jnp.where(in_grp, logits, NEG_INF)
    ee = jnp.exp(el - jnp.max(el, axis=1, keepdims=True))
    ep = ee / jnp.sum(ee, axis=1, keepdims=True)
    w1 = jnp.max(ep, axis=1, keepdims=True)
    i1 = jnp.min(jnp.where(in_grp & (ep == w1), lane, big), axis=1, keepdims=True)
    rest = jnp.where(in_grp & (lane != i1), ep, -1.0)
    w2 = jnp.max(rest, axis=1, keepdims=True)
    i2 = jnp.min(jnp.where(rest == w2, lane, big), axis=1, keepdims=True)
    den = w1 + w2
    comb = jnp.where(lane == i1, g_w * (w1 / den), jnp.where(lane == i2, g_w * (w2 / den), 0.0))
    xa_ref[:, D_MODEL:D_MODEL + LANES] = jnp.where(lane == GROUP_LANE, g_first - N_EXPERTS, comb)


def _post(o_nsa, y_a, x2, g1, w_mg, w_nsa, w_mix, g2, wr, rb):
    t = x2.shape[0]
    tm = TM_POST
    row = lambda i: (i, 0)
    const = lambda i: (0, 0)
    return pl.pallas_call(
        _post_kernel,
        grid=(t // tm,),
        in_specs=[
            pl.BlockSpec((tm, Q_DIM), row),
            pl.BlockSpec((tm, D_MODEL), row),
            pl.BlockSpec((tm, D_MODEL), row),
            pl.BlockSpec((1, D_MODEL), const),
            pl.BlockSpec((D_MODEL, 2 * D_MODEL), const),
            pl.BlockSpec((Q_DIM, D_MODEL), const),
            pl.BlockSpec((D_MODEL, D_MODEL), const),
            pl.BlockSpec((1, D_MODEL), const),
            pl.BlockSpec((D_MODEL, 2 * LANES), const),
            pl.BlockSpec((1, LANES), const),
        ],
        out_specs=pl.BlockSpec((tm, D_AUG), row),
        out_shape=jax.ShapeDtypeStruct((t, D_AUG), F32),
        compiler_params=pltpu.CompilerParams(
            dimension_semantics=("parallel",), vmem_limit_bytes=VMEM_LIMIT),
        name="post",
    )(o_nsa, y_a, x2, g1, w_mg, w_nsa, w_mix, g2, wr, rb)


def _plan_kernel(rec_ref, tri_ref, info_ref, counts_ref, carry_s):
    @pl.when(pl.program_id(0) == 0)
    def _():
        carry_s[...] = jnp.zeros_like(carry_s)

    rec = rec_ref[...]
    lane = _lane_iota(rec.shape)
    gid = rec[:, GROUP_LANE:GROUP_LANE + 1]
    onehot = jnp.where(lane.astype(F32) == gid, 1.0, 0.0)
    before = _dot(tri_ref[...], onehot.astype(BF16)) + carry_s[0:1, :]
    rank = jnp.sum(onehot * before, axis=1, keepdims=True)
    info_ref[...] = jnp.where(lane == 0, rank, jnp.where(lane == 1, gid, 0.0))
    carry_s[...] = carry_s[...] + jnp.sum(onehot, axis=0, keepdims=True)
    counts_ref[...] = carry_s[...]


def _plan(xa, tri):
    t = xa.shape[0]
    tm = TM_PLAN
    return pl.pallas_call(
        _plan_kernel,
        grid=(t // tm,),
        in_specs=[
            pl.BlockSpec((tm, LANES), lambda i: (i, D_MODEL // LANES)),
            pl.BlockSpec((tm, tm), lambda i: (0, 0)),
        ],
        out_specs=(
            pl.BlockSpec((tm, LANES), lambda i: (i, 0)),
            pl.BlockSpec((SUBLANES, LANES), lambda i: (0, 0)),
        ),
        out_shape=(
            jax.ShapeDtypeStruct((t, LANES), F32),
            jax.ShapeDtypeStruct((SUBLANES, LANES), F32),
        ),
        scratch_shapes=[pltpu.VMEM((SUBLANES, LANES), F32)],
        compiler_params=pltpu.CompilerParams(
            dimension_semantics=("arbitrary",), vmem_limit_bytes=VMEM_LIMIT),
        name="plan",
    )(xa, tri)


def _row_copies(n_rows, make_copy):
    def start(g, carry):
        r0 = pl.multiple_of(g * ROW_COPY_UNROLL, ROW_COPY_UNROLL)
        for k in range(ROW_COPY_UNROLL):
            make_copy(r0, k).start()
        return carry

    def wait(g, carry):
        for _ in range(ROW_COPY_UNROLL):
            make_copy(0, 0).wait()
        return carry

    lax.fori_loop(0, n_rows // ROW_COPY_UNROLL, start, 0)
    lax.fori_loop(0, n_rows // ROW_COPY_UNROLL, wait, 0)


def _dispatch_kernel(slot_ref, xa_ref, zero_hbm, xs_hbm, sem):
    del zero_hbm
    first = pl.program_id(0) * xa_ref.shape[0]

    def copy(r0, k):
        slot = slot_ref[first + r0 + k]
        return pltpu.make_async_copy(xa_ref.at[pl.ds(r0 + k, 1), :], xs_hbm.at[pl.ds(slot, 1), :], sem)

    _row_copies(xa_ref.shape[0], copy)


def _dispatch(slot, xa, n_sorted):
    t = xa.shape[0]
    tm = TM_ROWS
    zeros = jnp.zeros((n_sorted, D_AUG), F32)
    return pl.pallas_call(
        _dispatch_kernel,
        grid_spec=pltpu.PrefetchScalarGridSpec(
            num_scalar_prefetch=1,
            grid=(t // tm,),
            in_specs=[
                pl.BlockSpec((tm, D_AUG), lambda i, *_: (i, 0)),
                pl.BlockSpec(memory_space=pl.ANY),
            ],
            out_specs=pl.BlockSpec(memory_space=pl.ANY),
            scratch_shapes=[pltpu.SemaphoreType.DMA],
        ),
        out_shape=jax.ShapeDtypeStruct((n_sorted, D_AUG), F32),
        input_output_aliases={2: 0},
        compiler_params=pltpu.CompilerParams(
            dimension_semantics=("arbitrary",), vmem_limit_bytes=VMEM_LIMIT),
        name="dispatch",
    )(slot, xa, zeros)


def _experts_kernel(widx_ref, used_ref, tgrp_ref, xs_ref, wg_ref, wu_ref, wd_ref, g2_ref, gf_ref,
                    fs_ref, h_s, y_s):
    del widx_ref
    j = pl.program_id(0)
    e = pl.program_id(1)
    used = used_ref[j] == 1

    @pl.when(used & (e == 0))
    def _():
        x1 = xs_ref[:, 0:D_MODEL]
        h_s[...] = _normed(x1, g2_ref[...])
        y_s[...] = jnp.zeros_like(y_s)

    @pl.when(used)
    def _():
        h = h_s[...]
        act = jax.nn.silu(_dot(h, wg_ref[0].astype(BF16))) * _dot(h, wu_ref[0].astype(BF16))
        y = _dot(act.astype(BF16), wd_ref[0].astype(BF16))
        rec = xs_ref[:, D_MODEL:D_AUG]
        col = tgrp_ref[j] * EXPERTS_PER_GROUP + e
        weight = jnp.sum(jnp.where(_lane_iota(rec.shape) == col, rec, 0.0), axis=1, keepdims=True)
        y_s[...] += weight * y

    @pl.when(used & (e == EXPERTS_PER_GROUP - 1))
    def _():
        x = xs_ref[:, 0:D_MODEL] + y_s[...]
        fs_ref[...] = x * _rms_scale(x) * gf_ref[...]

    @pl.when(jnp.logical_not(used) & (e == EXPERTS_PER_GROUP - 1))
    def _():
        fs_ref[...] = jnp.zeros_like(fs_ref)


def _experts(widx, used, tgrp, xs, wg, wu, wd, g2, gf):
    n_sorted = xs.shape[0]
    tm = TM_EXP
    row = lambda j, e, *_: (j, 0)
    wsel = lambda j, e, widx, used, tgrp: (widx[j * EXPERTS_PER_GROUP + e], 0, 0)
    const = lambda j, e, *_: (0, 0)
    return pl.pallas_call(
        _experts_kernel,
        grid_spec=pltpu.PrefetchScalarGridSpec(
            num_scalar_prefetch=3,
            grid=(n_sorted // tm, EXPERTS_PER_GROUP),
            in_specs=[
                pl.BlockSpec((tm, D_AUG), row),
                pl.BlockSpec((1, D_MODEL, D_EXPERT), wsel),
                pl.BlockSpec((1, D_MODEL, D_EXPERT), wsel),
                pl.BlockSpec((1, D_EXPERT, D_MODEL), wsel),
                pl.BlockSpec((1, D_MODEL), const),
                pl.BlockSpec((1, D_MODEL), const),
            ],
            out_specs=pl.BlockSpec((tm, D_MODEL), row),
            scratch_shapes=[pltpu.VMEM((tm, D_MODEL), BF16), pltpu.VMEM((tm, D_MODEL), F32)],
        ),
        out_shape=jax.ShapeDtypeStruct((n_sorted, D_MODEL), F32),
        compiler_params=pltpu.CompilerParams(
            dimension_semantics=("arbitrary", "arbitrary"), vmem_limit_bytes=VMEM_LIMIT),
        name="experts",
    )(widx, used, tgrp, xs, wg, wu, wd, g2, gf)


def _combine_kernel(slot_ref, fs_hbm, o_ref, sem):
    first = pl.program_id(0) * o_ref.shape[0]

    def copy(r0, k):
        slot = slot_ref[first + r0 + k]
        return pltpu.make_async_copy(fs_hbm.at[pl.ds(slot, 1), :], o_ref.at[pl.ds(r0 + k, 1), :], sem)

    _row_copies(o_ref.shape[0], copy)


def _combine(slot, fs, t):
    tm = TM_ROWS
    return pl.pallas_call(
        _combine_kernel,
        grid_spec=pltpu.PrefetchScalarGridSpec(
            num_scalar_prefetch=1,
            grid=(t // tm,),
            in_specs=[pl.BlockSpec(memory_space=pl.ANY)],
            out_specs=pl.BlockSpec((tm, D_MODEL), lambda i, *_: (i, 0)),
            scratch_shapes=[pltpu.SemaphoreType.DMA],
        ),
        out_shape=jax.ShapeDtypeStruct((t, D_MODEL), F32),
        compiler_params=pltpu.CompilerParams(
            dimension_semantics=("arbitrary",), vmem_limit_bytes=VMEM_LIMIT),
        name="combine",
    )(slot, fs)


def _moe_sorted(xa, wg, wu, wd, g2, gf):
    t = xa.shape[0]
    tri = jnp.asarray(np.tril(np.ones((TM_PLAN, TM_PLAN), np.float32), -1), dtype=BF16)
    info, counts = _plan(xa, tri)
    rank = info[:, 0].astype(jnp.int32)
    gid = info[:, 1].astype(jnp.int32)
    counts = counts[0, :N_GROUPS].astype(jnp.int32)

    n_tiles = t // TM_EXP + N_GROUPS
    tiles_g = (counts + TM_EXP - 1) // TM_EXP
    tile_end = jnp.cumsum(tiles_g)
    base = (tile_end - tiles_g) * TM_EXP
    groups = jnp.arange(N_GROUPS, dtype=jnp.int32)
    slot = jnp.sum(jnp.where(gid[:, None] == groups[None, :], base[None, :], 0), axis=1) + rank
    tile_ids = jnp.arange(n_tiles, dtype=jnp.int32)
    used = (tile_ids < tile_end[-1]).astype(jnp.int32)
    last = jnp.maximum(tile_end[-1] - 1, 0)
    tgrp = jnp.sum((jnp.minimum(tile_ids, last)[:, None] >= tile_end[None, :]).astype(jnp.int32), axis=1)
    step_e = jnp.arange(EXPERTS_PER_GROUP, dtype=jnp.int32)[None, :]
    widx = jnp.where(used[:, None] == 1, tgrp[:, None] * EXPERTS_PER_GROUP + step_e,
                     tgrp[:, None] * EXPERTS_PER_GROUP + EXPERTS_PER_GROUP - 1).reshape(-1)

    xs = _dispatch(slot, xa, n_tiles * TM_EXP)
    fs = _experts(widx, used, tgrp, xs, wg, wu, wd, g2, gf)
    return _combine(slot, fs, t)


def _split_w_in(w):
    sizes = [D_RNN, D_RNN, Q_DIM] + [KV_DIM] * 6 + [3 * N_HEADS, 2 * D_MODEL]
    pts = np.cumsum(sizes)[:-1]
    return jnp.split(w, [int(p) for p in pts], axis=-1)


def _proj_weights(w_in):
    xr, gr, q, kc, vc, ksl, vsl, kw, vw, nsa_g, merge_g = _split_w_in(w_in)
    ng = nsa_g.reshape(D_MODEL, 3, N_KV_GROUPS, HEADS_PER_GROUP).transpose(0, 2, 1, 3)
    ng = ng.reshape(D_MODEL, N_KV_GROUPS, 3 * HEADS_PER_GROUP)
    ng = jnp.pad(ng, ((0, 0), (0, 0), (0, LANES - 3 * HEADS_PER_GROUP))).reshape(D_MODEL, N_KV_GROUPS * LANES)
    w_attn = jnp.concatenate([q, kc, vc, ksl, vsl, kw, vw, ng], axis=1)
    return (jnp.concatenate([xr, gr], axis=1).astype(BF16), w_attn.astype(BF16), merge_g.astype(BF16))


def _rope_tables(pos, width):
    inv_freq = ROPE_THETA ** (-(jnp.arange(0, HEAD_DIM, 2, dtype=F32) / HEAD_DIM))
    ang = pos.astype(F32)[:, None] * inv_freq[None, :]
    cos, sin = jnp.cos(ang), jnp.sin(ang)
    reps = width // HEAD_DIM
    return jnp.tile(jnp.concatenate([cos, cos], axis=1), (1, reps)), jnp.tile(jnp.concatenate([-sin, sin], axis=1), (1, reps))


def _block_diag(w, per):
    nb, d = w.shape[0], w.shape[1]
    w = w.reshape(nb // per, per, d, d)
    eye = jnp.eye(per, dtype=w.dtype)
    return jnp.einsum('npij,pq->npiqj', w, eye).reshape(nb // per, per * d, per * d)


def _cmp_to_sel():
    n_sel_pad = LANES
    c0 = np.arange(N_CMP_PAD) * CMP_STRIDE
    s0 = np.arange(n_sel_pad) * SEL_BLOCK
    ov = np.minimum(c0[:, None] + CMP_BLOCK, s0[None, :] + SEL_BLOCK) - np.maximum(c0[:, None], s0[None, :])
    m = np.clip(ov, 0, None) / CMP_BLOCK
    m[:, SEL_BLOCK // 2:] = 0.0
    m[N_CMP_PAD - 1, :] = 0.0
    return m.astype(np.float32)


def kernel(x, norm1_g, w_in, conv_w, conv_b, lru_wa, lru_ba, lru_wi, lru_bi, lru_lambda, w_rnn_out, cmpk_pos, cmpk_w1, cmpk_b1, cmpk_w2, cmpk_b2, cmpv_pos, cmpv_w1, cmpv_b1, cmpv_w2, cmpv_b2, w_nsa_out, w_mix_out, norm2_g, router_group_w, router_group_b, router_expert_w, router_expert_b, expert_w_gate, expert_w_up, expert_w_down, final_norm_g):
    batch, seq, _ = x.shape
    t = batch * seq
    assert w_in.shape[0] == 1, "the final norm is fused into the expert kernel: single layer only"
    x2 = x.reshape(t, D_MODEL)

    cos, sin = _rope_tables(jnp.arange(seq), LANES)
    cmp_ends = jnp.arange(N_CMP_PAD) * CMP_STRIDE + (CMP_BLOCK - 1)
    ccos, csin = _rope_tables(cmp_ends, LANES)
    reps = TR_CMP // N_CMP_PAD
    cmp_cos = jnp.stack([jnp.tile(ccos, (reps, 1)), jnp.ones((TR_CMP, LANES), F32)])
    cmp_sin = jnp.stack([jnp.tile(csin, (reps, 1)), jnp.zeros((TR_CMP, LANES), F32)])
    key_blk = np.arange(seq)[:, None] // SEL_BLOCK
    e_mat = jnp.asarray((key_blk == np.arange(LANES)[None, :]).astype(np.float32), dtype=BF16)
    cs_mat = jnp.asarray(_cmp_to_sel(), dtype=BF16)

    l = 0
    w_xg, w_attn, w_mg = _proj_weights(w_in[l])
    g1 = norm1_g[l][None, :]
    q, kcv, ksd, vsd, kwd, vwd, ng = _inproj(x2, g1, w_attn, cos, sin, seq)

    y_a = _rnn(x2, g1, w_xg, conv_w[l], conv_b[l][None, :],
               _block_diag(lru_wa[l], 4).astype(BF16), lru_ba[l][None, :],
               _block_diag(lru_wi[l], 4).astype(BF16), lru_bi[l][None, :],
               lru_lambda[l][None, :], w_rnn_out[l].astype(BF16), batch, seq)

    half = CMP_STRIDE * HEAD_DIM
    pos = jnp.stack([cmpk_pos[l].reshape(2, half), cmpv_pos[l].reshape(2, half)])
    w1 = jnp.stack([cmpk_w1[l], cmpv_w1[l]]).astype(BF16)
    b1 = jnp.stack([cmpk_b1[l], cmpv_b1[l]])[:, None, :]
    w2 = jnp.stack([cmpk_w2[l], cmpv_w2[l]])
    w2 = jnp.concatenate([w2, w2], axis=2).astype(BF16)
    b2 = jnp.stack([cmpk_b2[l], cmpv_b2[l]])
    b2 = jnp.concatenate([b2, b2], axis=1)[:, None, :]
    kvc = _compress(kcv, pos, w1, b1, w2, b2, cmp_cos, cmp_sin, batch, seq)

    o_nsa = _attn(q, kvc, ksd, vsd, kwd, vwd, ng, e_mat, cs_mat, batch, seq)

    wr = jnp.concatenate([
        router_expert_w[l].transpose(1, 0, 2).reshape(D_MODEL, N_EXPERTS),
        router_group_w[l],
        jnp.zeros((D_MODEL, LANES - N_EXPERTS - N_GROUPS), F32)], axis=1)
    wr_hi = wr.astype(BF16)
    wr_lo = (wr - wr_hi.astype(F32)).astype(BF16)
    rb = jnp.concatenate([router_expert_b[l].reshape(N_EXPERTS), router_group_b[l],
                          jnp.zeros((LANES - N_EXPERTS - N_GROUPS,), F32)])[None, :]
    xa = _post(o_nsa, y_a, x2, g1, w_mg, w_nsa_out[l].astype(BF16), w_mix_out[l].astype(BF16),
               norm2_g[l][None, :], jnp.concatenate([wr_hi, wr_lo], axis=1), rb)

    out = _moe_sorted(xa, expert_w_gate[l], expert_w_up[l], expert_w_down[l],
                      norm2_g[l][None, :], final_norm_g[None, :])
    return out.reshape(batch, seq, D_MODEL)
```

```python
import functools

import numpy as np
import jax
import jax.numpy as jnp
from jax import lax
from jax.experimental import pallas as pl
from jax.experimental.pallas import tpu as pltpu

D_MODEL = 1024
D_RNN = 1024
RNN_BLOCKS = 16
RNN_BLOCK_DIM = D_RNN // RNN_BLOCKS
CONV_WIDTH = 4
LRU_C = 8.0
N_HEADS = 16
HEAD_DIM = 64
HALF_DIM = HEAD_DIM // 2
N_KV_GROUPS = 4
HEADS_PER_GROUP = N_HEADS // N_KV_GROUPS
Q_DIM = N_HEADS * HEAD_DIM
KV_DIM = N_KV_GROUPS * HEAD_DIM
CMP_BLOCK = 32
CMP_STRIDE = 16
CMP_HIDDEN = 256
SEL_BLOCK = 64
SEL_TOPN = 16
WINDOW = 512
ROPE_THETA = 10000.0
FORCE_BONUS = 1e4
NEG_INF = -1e30
N_GROUPS = 4
EXPERTS_PER_GROUP = 4
N_EXPERTS = N_GROUPS * EXPERTS_PER_GROUP
D_EXPERT = 512
EPS = 1e-6

Q_SCALE = HEAD_DIM ** -0.5 * float(np.log2(np.e))

LANES = 128
SUBLANES = 8
VMEM_LIMIT = 56 * 1024 * 1024

BF16 = jnp.bfloat16
F32 = jnp.float32

COL_Q = 0
COL_KCV = COL_Q + Q_DIM
COL_KS = COL_KCV + 2 * KV_DIM
COL_VS = COL_KS + KV_DIM
COL_KW = COL_VS + KV_DIM
COL_VW = COL_KW + KV_DIM
COL_NG = COL_VW + KV_DIM
N_PROJ = COL_NG + N_KV_GROUPS * LANES

TM_PROJ = 1024
TS_RNN = 1024
TR_CMP = 256
TQ = 256
TQ_SELECT_TILE = 256
TK = 128
CK_SEL = 512
TM_POST = 1024
TM_PLAN = 1024
TM_ROWS = 2048
TM_EXP = 1024
ROW_COPY_UNROLL = 16
D_AUG = D_MODEL + LANES
GROUP_LANE = N_EXPERTS
N_CMP_PAD = 128


def _dot(a, b):
    return jnp.dot(a, b, preferred_element_type=F32)


def _dot_t(a, b):
    return lax.dot_general(a, b, (((1,), (1,)), ((), ())), preferred_element_type=F32)


def _lane_iota(shape):
    return lax.broadcasted_iota(jnp.int32, shape, len(shape) - 1)


def _row_iota(shape):
    return lax.broadcasted_iota(jnp.int32, shape, 0)


def _rope(x, cos, sin_signed):
    width = x.shape[-1]
    reps = width // cos.shape[-1]
    if reps > 1:
        cos = jnp.concatenate([cos] * reps, axis=1)
        sin_signed = jnp.concatenate([sin_signed] * reps, axis=1)
    first_half = (_lane_iota(x.shape) & (HEAD_DIM - 1)) < HALF_DIM
    partner = jnp.where(first_half, pltpu.roll(x, width - HALF_DIM, 1), pltpu.roll(x, HALF_DIM, 1))
    return x * cos + partner * sin_signed


def _spread_heads(x, fill=None):
    out = []
    low = _lane_iota((x.shape[0], LANES)) < HEAD_DIM
    for c in range(x.shape[1] // LANES):
        xc = x[:, c * LANES:(c + 1) * LANES]
        rolled = pltpu.roll(xc, HEAD_DIM, 1)
        out.append(jnp.where(low, xc, rolled if fill is None else fill))
        out.append(jnp.where(low, rolled, xc if fill is None else fill))
    return jnp.concatenate(out, axis=1)


def _sigmoid(x):
    return 0.5 * (jnp.tanh(0.5 * x) + 1.0)


def _rms_scale(x):
    sq = x * x
    part = sq[:, 0:LANES]
    for c in range(1, x.shape[1] // LANES):
        part = part + sq[:, c * LANES:(c + 1) * LANES]
    return lax.rsqrt(jnp.sum(part, axis=1, keepdims=True) * (1.0 / x.shape[1]) + EPS)


def _normed(x, g):
    return (x * _rms_scale(x) * g).astype(BF16)


def _inproj_kernel(x_ref, g_ref, w_ref, cos_ref, sin_ref,
                   q_ref, kcv_ref, ks_ref, vs_ref, kw_ref, vw_ref, ng_ref):
    h = _normed(x_ref[...], g_ref[...])
    cos = cos_ref[...]
    sin = sin_ref[...]

    def mm(lo, width):
        return _dot(h, w_ref[:, lo:lo + width])

    q_ref[...] = (_rope(mm(COL_Q, Q_DIM), cos, sin) * Q_SCALE).astype(BF16)
    kcv_ref[...] = mm(COL_KCV, 2 * KV_DIM)
    ks_ref[...] = _spread_heads(_rope(mm(COL_KS, KV_DIM), cos, sin)).astype(BF16)
    vs_ref[...] = _spread_heads(mm(COL_VS, KV_DIM), 1.0).astype(BF16)
    kw_ref[...] = _spread_heads(_rope(mm(COL_KW, KV_DIM), cos, sin)).astype(BF16)
    vw_ref[...] = _spread_heads(mm(COL_VW, KV_DIM), 1.0).astype(BF16)
    ng_ref[...] = jax.nn.sigmoid(mm(COL_NG, N_KV_GROUPS * LANES))


def _inproj(x2, norm_g, w_proj, cos, sin, seq):
    t = x2.shape[0]
    tm = TM_PROJ
    pos_blocks = seq // tm
    row = lambda i: (i, 0)
    const = lambda i: (0, 0)
    out_shape = (
        jax.ShapeDtypeStruct((t, Q_DIM), BF16),
        jax.ShapeDtypeStruct((t, 2 * KV_DIM), F32),
        jax.ShapeDtypeStruct((t, N_KV_GROUPS * LANES), BF16),
        jax.ShapeDtypeStruct((t, N_KV_GROUPS * LANES), BF16),
        jax.ShapeDtypeStruct((t, N_KV_GROUPS * LANES), BF16),
        jax.ShapeDtypeStruct((t, N_KV_GROUPS * LANES), BF16),
        jax.ShapeDtypeStruct((t, N_KV_GROUPS * LANES), F32),
    )
    return pl.pallas_call(
        _inproj_kernel,
        grid=(t // tm,),
        in_specs=[
            pl.BlockSpec((tm, D_MODEL), row),
            pl.BlockSpec((1, D_MODEL), const),
            pl.BlockSpec((D_MODEL, N_PROJ), const, pipeline_mode=pl.Buffered(1)),
            pl.BlockSpec((tm, LANES), lambda i: (i % pos_blocks, 0)),
            pl.BlockSpec((tm, LANES), lambda i: (i % pos_blocks, 0)),
        ],
        out_specs=tuple(pl.BlockSpec((tm, s.shape[1]), row) for s in out_shape),
        out_shape=out_shape,
        compiler_params=pltpu.CompilerParams(
            dimension_semantics=("parallel",), vmem_limit_bytes=VMEM_LIMIT),
        name="inproj",
    )(x2, norm_g, w_proj, cos, sin)


def _rnn_kernel(xin_ref, g1_ref, wx_ref, cw_ref, cb_ref, wa_ref, ba_ref, wi_ref, bi_ref, lam_ref, wo_ref,
                y_ref, tail_s, carry_s, a_s, u_s, h_s):
    ts = xin_ref.shape[0]

    @pl.when(pl.program_id(1) == 0)
    def _():
        tail_s[...] = jnp.zeros_like(tail_s)
        carry_s[...] = jnp.zeros_like(carry_s)

    hn = _normed(xin_ref[...], g1_ref[...])
    x = _dot(hn, wx_ref[:, 0:D_RNN])
    gate_pre = _dot(hn, wx_ref[:, D_RNN:2 * D_RNN])
    xext = jnp.concatenate([tail_s[...], x], axis=0)
    tail_s[...] = x[ts - SUBLANES:ts, :]
    conv = cb_ref[...]
    for k in range(CONV_WIDTH):
        back = CONV_WIDTH - 1 - k
        shifted = xext if back == 0 else pltpu.roll(xext, back, 0)
        conv = conv + cw_ref[k:k + 1, :] * shifted[SUBLANES:SUBLANES + ts, :]

    cb16 = conv.astype(BF16)
    blk = wa_ref.shape[1]
    r_pre = jnp.concatenate(
        [_dot(cb16[:, j * blk:(j + 1) * blk], wa_ref[j]) for j in range(D_RNN // blk)], axis=1)
    i_pre = jnp.concatenate(
        [_dot(cb16[:, j * blk:(j + 1) * blk], wi_ref[j]) for j in range(D_RNN // blk)], axis=1)
    r = _sigmoid(r_pre + ba_ref[...])
    gate_i = _sigmoid(i_pre + bi_ref[...])
    neg_lam = -lam_ref[...]
    softplus = jnp.maximum(neg_lam, 0.0) + jnp.log1p(jnp.exp(-jnp.abs(neg_lam)))
    log_a = r * ((-LRU_C) * softplus)
    a = jnp.exp(log_a)
    a_s[...] = a
    var = -jnp.tanh(log_a) * (a * a + 1.0)
    u_s[...] = jnp.where(var > 0.0, var * lax.rsqrt(var), 0.0) * (gate_i * conv)

    row = _row_iota((SUBLANES, D_RNN))

    def body(k, carry):
        off = pl.multiple_of(k * SUBLANES, SUBLANES)
        a = a_s[pl.ds(off, SUBLANES), :]
        b = u_s[pl.ds(off, SUBLANES), :]
        for sh in (1, 2, 4):
            keep = row >= sh
            a_prev = jnp.where(keep, pltpu.roll(a, sh, 0), 1.0)
            b_prev = jnp.where(keep, pltpu.roll(b, sh, 0), 0.0)
            b = a * b_prev + b
            a = a * a_prev
        h = a * carry + b
        h_s[pl.ds(off, SUBLANES), :] = h
        return jnp.broadcast_to(h[SUBLANES - 1:SUBLANES, :], (SUBLANES, D_RNN))

    carry_s[...] = lax.fori_loop(0, ts // SUBLANES, body, carry_s[...])
    gated = (jax.nn.gelu(gate_pre) * h_s[...]).astype(BF16)
    y_ref[...] = _dot(gated, wo_ref[...])


def _rnn(x2, norm_g, w_xg, conv_w, conv_b, wa_bd, ba, wi_bd, bi, lam, w_out, batch, seq):
    t = x2.shape[0]
    ts = TS_RNN
    nt = seq // ts
    const2 = lambda b, s: (0, 0)
    const3 = lambda b, s: (0, 0, 0)
    nblk, blk = wa_bd.shape[0], wa_bd.shape[1]
    return pl.pallas_call(
        _rnn_kernel,
        grid=(batch, nt),
        in_specs=[
            pl.BlockSpec((ts, D_MODEL), lambda b, s: (b * nt + s, 0)),
            pl.BlockSpec((1, D_MODEL), const2),
            pl.BlockSpec((D_MODEL, 2 * D_RNN), const2),
            pl.BlockSpec((CONV_WIDTH, D_RNN), const2),
            pl.BlockSpec((1, D_RNN), const2),
            pl.BlockSpec((nblk, blk, blk), const3),
            pl.BlockSpec((1, D_RNN), const2),
            pl.BlockSpec((nblk, blk, blk), const3),
            pl.BlockSpec((1, D_RNN), const2),
            pl.BlockSpec((1, D_RNN), const2),
            pl.BlockSpec((D_RNN, D_MODEL), const2),
        ],
        out_specs=pl.BlockSpec((ts, D_MODEL), lambda b, s: (b * nt + s, 0)),
        out_shape=jax.ShapeDtypeStruct((t, D_MODEL), F32),
        scratch_shapes=[
            pltpu.VMEM((SUBLANES, D_RNN), F32),
            pltpu.VMEM((SUBLANES, D_RNN), F32),
            pltpu.VMEM((ts, D_RNN), F32),
            pltpu.VMEM((ts, D_RNN), F32),
            pltpu.VMEM((ts, D_RNN), F32),
        ],
        compiler_params=pltpu.CompilerParams(
            dimension_semantics=("parallel", "arbitrary"), vmem_limit_bytes=VMEM_LIMIT),
        name="rnn",
    )(x2, norm_g, w_xg, conv_w, conv_b, wa_bd, ba, wi_bd, bi, lam, w_out)


def _compress_kernel(x_ref, pos_ref, w1_ref, b1_ref, w2_ref, b2_ref, cos_ref, sin_ref, o_ref):
    n_rows = x_ref.shape[0] // CMP_STRIDE
    strided = [x_ref[pl.ds(r, n_rows, stride=CMP_STRIDE), :] for r in range(CMP_STRIDE)]
    low = _lane_iota((n_rows, LANES)) < HEAD_DIM
    groups = []
    for g in range(2):
        tiles = []
        for j in range(CMP_STRIDE // 2):
            even = strided[2 * j]
            odd = strided[2 * j + 1]
            if g == 0:
                tiles.append(jnp.where(low, even, pltpu.roll(odd, HEAD_DIM, 1)))
            else:
                tiles.append(jnp.where(low, pltpu.roll(even, HEAD_DIM, 1), odd))
        groups.append(jnp.concatenate(tiles, axis=1))
    x = jnp.concatenate(groups, axis=0)
    tr = x.shape[0]
    half = x.shape[1]
    first = _dot((x + pos_ref[0, 0:1, :]).astype(BF16), w1_ref[0, 0:half, :])
    second = _dot((x + pos_ref[0, 1:2, :]).astype(BF16), w1_ref[0, half:2 * half, :])
    hid = jax.nn.gelu(first + pltpu.roll(second, tr - 1, 0) + b1_ref[0])
    out = _dot(hid.astype(BF16), w2_ref[0]) + b2_ref[0]
    o_ref[0] = _rope(out, cos_ref[0], sin_ref[0]).astype(BF16)


def _compress(kcv, pos, w1, b1, w2, b2, cos, sin, batch, seq):
    pairs = KV_DIM // LANES
    tr = TR_CMP
    rows = batch * pairs * tr
    sel = lambda k, r: (k, 0, 0)
    return pl.pallas_call(
        _compress_kernel,
        grid=(2, batch * pairs),
        in_specs=[
            pl.BlockSpec((seq, LANES), lambda k, r: (r // pairs, k * pairs + r % pairs)),
            pl.BlockSpec((1, 2, pos.shape[2]), sel),
            pl.BlockSpec((1,) + w1.shape[1:], sel),
            pl.BlockSpec((1, 1, CMP_HIDDEN), sel),
            pl.BlockSpec((1,) + w2.shape[1:], sel),
            pl.BlockSpec((1, 1, LANES), sel),
            pl.BlockSpec((1, tr, LANES), sel),
            pl.BlockSpec((1, tr, LANES), sel),
        ],
        out_specs=pl.BlockSpec((1, tr, LANES), lambda k, r: (k, r, 0)),
        out_shape=jax.ShapeDtypeStruct((2, rows, LANES), BF16),
        compiler_params=pltpu.CompilerParams(
            dimension_semantics=("parallel", "parallel"), vmem_limit_bytes=VMEM_LIMIT),
        name="compress",
    )(kcv, pos, w1, b1, w2, b2, cos, sin)


def _lane_tile_max(s):
    tiles = [s[:, c * LANES:(c + 1) * LANES] for c in range(s.shape[1] // LANES)]
    while len(tiles) > 1:
        tiles = [jnp.maximum(tiles[k], tiles[k + 1]) if k + 1 < len(tiles) else tiles[k]
                 for k in range(0, len(tiles), 2)]
    return tiles[0]


def _stack_heads(q):
    tq = q.shape[0]
    low = _lane_iota((tq, LANES)) < HEAD_DIM
    zero = jnp.zeros((tq, LANES), BF16)
    heads = []
    for hh in range(HEADS_PER_GROUP):
        pair = q[:, (hh // 2) * LANES:(hh // 2 + 1) * LANES]
        heads.append(jnp.where(low if hh % 2 == 0 else jnp.logical_not(low), pair, zero))
    return jnp.concatenate(heads, axis=0)


def _select_tile(i, q, kc_ref, vc_ref, ng, m_ref):
    tq = q.shape[0]
    rows = HEADS_PER_GROUP * tq
    low = _lane_iota((tq, LANES)) < HEAD_DIM
    q4 = _stack_heads(q)
    lane4 = _lane_iota((rows, LANES))
    qpos4 = i * tq + (_row_iota((rows, LANES)) & (tq - 1))

    sc = _dot_t(q4, kc_ref[0])
    sc = jnp.where(lane4 * CMP_STRIDE + (CMP_BLOCK - 1) <= qpos4, sc, NEG_INF)
    mc = jnp.max(sc, axis=1, keepdims=True)
    pc = jnp.exp2(sc - mc)
    pc = pc / jnp.sum(pc, axis=1, keepdims=True)
    pc = jnp.where(qpos4 >= CMP_BLOCK - 1, pc, 0.0)
    o_cmp = _dot(pc.astype(BF16), vc_ref[0])

    psum = pc[0:tq] + pc[tq:2 * tq] + pc[2 * tq:3 * tq] + pc[3 * tq:4 * tq]
    p_hi = psum.astype(BF16)
    rem = psum - p_hi.astype(F32)
    p_mid = rem.astype(BF16)
    p_lo = (rem - p_mid.astype(F32)).astype(BF16)
    cs = m_ref[...]
    p_slc = _dot(p_hi, cs) + _dot(p_mid, cs) + _dot(p_lo, cs)

    n_sel = SEL_BLOCK // 2
    p_slc_t = p_slc.T[0:n_sel, :]
    blk = _row_iota((n_sel, tq))
    tblk = (i * tq + _lane_iota((n_sel, tq))) >> 6
    forced = (blk == 0) | (blk == tblk) | (blk == tblk - 1)
    score = jnp.where(blk <= tblk, p_slc_t + jnp.where(forced, FORCE_BONUS, 0.0), -1.0)
    rank = jnp.zeros((n_sel, tq), F32)
    for j in range(n_sel):
        sj = score[j:j + 1, :]
        beats = (sj > score) | ((sj == score) & (blk > j))
        rank = rank + jnp.where(beats, 1.0, 0.0)
    bias_t = jnp.where(rank < SEL_TOPN, 0.0, NEG_INF)
    bias = jnp.concatenate([bias_t, jnp.zeros((LANES - n_sel, tq), F32)], axis=0).T.astype(BF16)

    gated = [ng[:, hh:hh + 1] * o_cmp[hh * tq:(hh + 1) * tq] for hh in range(HEADS_PER_GROUP)]
    o_pairs = jnp.concatenate([jnp.where(low, gated[0], gated[1]), jnp.where(low, gated[2], gated[3])], axis=1)
    return bias, o_pairs


def _attn_kernel(*refs):
    n_full = pl.program_id(2) // (CK_SEL // TQ)
    for n_chunks in range(1, refs[3].shape[0] // CK_SEL + 1):
        pl.when(n_full == n_chunks - 1)(functools.partial(_attn_step, n_chunks, *refs))


def _attn_step(n_chunks, q_ref, kc_ref, vc_ref, ks_ref, vs_ref, kw_ref, vw_ref, ng_ref, e_ref, m_ref, o_ref):
    i = pl.program_id(2)
    tq = q_ref.shape[0]
    rows = HEADS_PER_GROUP * tq
    ck = CK_SEL
    low = _lane_iota((tq, LANES)) < HEAD_DIM
    q4 = _stack_heads(q_ref[...])

    st = TQ_SELECT_TILE
    picks = [_select_tile(i * (tq // st) + sub, q_ref[sub * st:(sub + 1) * st, :], kc_ref, vc_ref,
                          ng_ref[sub * st:(sub + 1) * st, :], m_ref) for sub in range(tq // st)]
    bias = jnp.concatenate([p[0] for p in picks], axis=0)
    o_cmp = jnp.concatenate([p[1] for p in picks], axis=0)

    n_sub = tq // TK
    back = WINDOW // TK
    k_tiles, v_tiles = [], []
    for jj in range(back + n_sub):
        j = i * n_sub - back + jj
        off = pl.multiple_of(jnp.maximum(j, 0) * TK, TK)
        k_tiles.append(kw_ref[pl.ds(off, TK), :])
        v_tiles.append(vw_ref[pl.ds(off, TK), :])
    sw = _dot_t(q4, jnp.concatenate(k_tiles, axis=0))
    on_or_after = _row_iota((TK, TK)) >= _lane_iota((TK, TK))
    sw_tiles = []
    for jj in range(back + n_sub):
        blocks = []
        for blk_i in range(rows // TK):
            piece = sw[blk_i * TK:(blk_i + 1) * TK, jj * TK:(jj + 1) * TK]
            tiles_back = blk_i % n_sub + back - jj
            if tiles_back == 0:
                piece = jnp.where(on_or_after, piece, NEG_INF)
            elif tiles_back == back:
                piece = jnp.where(on_or_after, NEG_INF, piece)
            elif tiles_back < 0 or tiles_back > back:
                piece = jnp.full((TK, TK), NEG_INF, F32)
            blocks.append(piece)
        tile = jnp.concatenate(blocks, axis=0)
        if jj < back:
            tile = jnp.where(i * n_sub - back + jj >= 0, tile, NEG_INF)
        sw_tiles.append(tile)
    sw = jnp.concatenate(sw_tiles, axis=1)
    m_win = jnp.max(_lane_tile_max(sw), axis=1, keepdims=True)
    acc_win = _dot(jnp.exp2(sw - m_win).astype(BF16), jnp.concatenate(v_tiles, axis=0))

    qa = jnp.concatenate([q4, jnp.concatenate([bias] * HEADS_PER_GROUP, axis=0)], axis=1)
    rel_q = i * tq - (n_chunks - 1) * ck + (_row_iota((rows, ck)) & (tq - 1))
    m_run = None
    acc_sel = None
    for c in range(n_chunks):
        ka = jnp.concatenate([ks_ref[c * ck:(c + 1) * ck, :], e_ref[c * ck:(c + 1) * ck, :]], axis=1)
        s = _dot_t(qa, ka)
        if c == n_chunks - 1:
            s = jnp.where(_lane_iota((rows, ck)) <= rel_q, s, NEG_INF)
        m_new = jnp.max(_lane_tile_max(s), axis=1, keepdims=True)
        if c > 0:
            m_new = jnp.maximum(m_run, m_new)
            acc_sel = acc_sel * jnp.exp2(m_run - m_new)
        part = _dot(jnp.exp2(s - m_new).astype(BF16), vs_ref[c * ck:(c + 1) * ck, :])
        acc_sel = part if c == 0 else acc_sel + part
        m_run = m_new

    ng = ng_ref[...]
    outs = []
    for hh in range(HEADS_PER_GROUP):
        r0 = hh * tq
        col = lambda br: ng[:, br * HEADS_PER_GROUP + hh:br * HEADS_PER_GROUP + hh + 1]
        parts = []
        for acc in (acc_sel[r0:r0 + tq], acc_win[r0:r0 + tq]):
            swapped = pltpu.roll(acc, HEAD_DIM, 1)
            parts.append(acc / swapped if hh % 2 == 0 else swapped / acc)
        outs.append(col(1) * parts[0] + col(2) * parts[1])
    o_ref[...] = (o_cmp + jnp.concatenate(
        [jnp.where(low, outs[0], outs[1]), jnp.where(low, outs[2], outs[3])], axis=1)).astype(o_ref.dtype)


def _attn(q, kvc, ksd, vsd, kwd, vwd, ng, e_mat, cs_mat, batch, seq):
    t = q.shape[0]
    tq = TQ
    nq = seq // tq
    qrow = lambda b, g, i: (b * nq + i, g)
    kv = lambda b, g, i: (b, g)
    const = lambda b, g, i: (0, 0)
    return pl.pallas_call(
        _attn_kernel,
        grid=(batch, N_KV_GROUPS, nq),
        in_specs=[
            pl.BlockSpec((tq, HEADS_PER_GROUP * HEAD_DIM), qrow),
            pl.BlockSpec((1, N_CMP_PAD, LANES), lambda b, g, i: (0, b * N_KV_GROUPS + g, 0)),
            pl.BlockSpec((1, N_CMP_PAD, LANES), lambda b, g, i: (1, b * N_KV_GROUPS + g, 0)),
            pl.BlockSpec((seq, LANES), kv),
            pl.BlockSpec((seq, LANES), kv),
            pl.BlockSpec((seq, LANES), kv),
            pl.BlockSpec((seq, LANES), kv),
            pl.BlockSpec((tq, LANES), qrow),
            pl.BlockSpec((seq, LANES), const),
            pl.BlockSpec((N_CMP_PAD, LANES), const),
        ],
        out_specs=pl.BlockSpec((tq, HEADS_PER_GROUP * HEAD_DIM), qrow),
        out_shape=jax.ShapeDtypeStruct((t, Q_DIM), BF16),
        compiler_params=pltpu.CompilerParams(
            dimension_semantics=("parallel", "parallel", "arbitrary"), vmem_limit_bytes=VMEM_LIMIT),
        name="attn",
    )(q, kvc, kvc, ksd, vsd, kwd, vwd, ng, e_mat, cs_mat)


def _post_kernel(o_ref, ya_ref, x_ref, g1_ref, wg_ref, wn_ref, wm_ref, g2_ref, wr_ref, rb_ref,
                 xa_ref):
    hn = _normed(x_ref[...], g1_ref[...])
    gate_a = jax.nn.sigmoid(_dot(hn, wg_ref[:, 0:D_MODEL]))
    gate_b = jax.nn.sigmoid(_dot(hn, wg_ref[:, D_MODEL:2 * D_MODEL]))
    y_b = _dot(o_ref[...], wn_ref[...])
    mixed = gate_a * ya_ref[...] + gate_b * y_b
    x1 = x_ref[...] + _dot(mixed.astype(BF16), wm_ref[...])
    xa_ref[:, 0:D_MODEL] = x1
    h2 = x1 * _rms_scale(x1) * g2_ref[...]
    h_hi = h2.astype(BF16)
    h_lo = (h2 - h_hi.astype(F32)).astype(BF16)
    both = _dot(h_hi, wr_ref[...])
    logits = both[:, 0:LANES] + both[:, LANES:2 * LANES] + _dot(h_lo, wr_ref[:, 0:LANES]) + rb_ref[...]

    lane = _lane_iota(logits.shape).astype(F32)
    is_grp = (lane >= N_EXPERTS) & (lane < N_EXPERTS + N_GROUPS)
    gl = jnp.where(is_grp, logits, NEG_INF)
    ge = jnp.exp(gl - jnp.max(gl, axis=1, keepdims=True))
    gp = ge / jnp.sum(ge, axis=1, keepdims=True)
    g_w = jnp.max(gp, axis=1, keepdims=True)
    big = float(4 * LANES)
    g_first = jnp.min(jnp.where(is_grp & (gp == g_w), lane, big), axis=1, keepdims=True)
    grp_lo = (g_first - N_EXPERTS) * EXPERTS_PER_GROUP

    in_grp = (lane >= grp_lo) & (lane < grp_lo + EXPERTS_PER_GROUP)
    el = jnp.where(in_grp, logits, NEG_INF)
    ee = jnp.exp(el - jnp.max(el, axis=1, keepdims=True))
    ep = ee / jnp.sum(ee, axis=1, keepdims=True)
    w1 = jnp.max(ep, axis=1, keepdims=True)
    i1 = jnp.min(jnp.where(in_grp & (ep == w1), lane, big), axis=1, keepdims=True)
    rest = jnp.where(in_grp & (lane != i1), ep, -1.0)
    w2 = jnp.max(rest, axis=1, keepdims=True)
    i2 = jnp.min(jnp.where(rest == w2, lane, big), axis=1, keepdims=True)
    den = w1 + w2
    comb = jnp.where(lane == i1, g_w * (w1 / den), jnp.where(lane == i2, g_w * (w2 / den), 0.0))
    xa_ref[:, D_MODEL:D_MODEL + LANES] = jnp.where(lane == GROUP_LANE, g_first - N_EXPERTS, comb)


def _post(o_nsa, y_a, x2, g1, w_mg, w_nsa, w_mix, g2, wr, rb):
    t = x2.shape[0]
    tm = TM_POST
    row = lambda i: (i, 0)
    const = lambda i: (0, 0)
    return pl.pallas_call(
        _post_kernel,
        grid=(t // tm,),
        in_specs=[
            pl.BlockSpec((tm, Q_DIM), row),
            pl.BlockSpec((tm, D_MODEL), row),
            pl.BlockSpec((tm, D_MODEL), row),
            pl.BlockSpec((1, D_MODEL), const),
            pl.BlockSpec((D_MODEL, 2 * D_MODEL), const),
            pl.BlockSpec((Q_DIM, D_MODEL), const),
            pl.BlockSpec((D_MODEL, D_MODEL), const),
            pl.BlockSpec((1, D_MODEL), const),
            pl.BlockSpec((D_MODEL, 2 * LANES), const),
            pl.BlockSpec((1, LANES), const),
        ],
        out_specs=pl.BlockSpec((tm, D_AUG), row),
        out_shape=jax.ShapeDtypeStruct((t, D_AUG), F32),
        compiler_params=pltpu.CompilerParams(
            dimension_semantics=("parallel",), vmem_limit_bytes=VMEM_LIMIT),
        name="post",
    )(o_nsa, y_a, x2, g1, w_mg, w_nsa, w_mix, g2, wr, rb)


def _plan_kernel(rec_ref, tri_ref, info_ref, counts_ref, carry_s):
    @pl.when(pl.program_id(0) == 0)
    def _():
        carry_s[...] = jnp.zeros_like(carry_s)

    rec = rec_ref[...]
    lane = _lane_iota(rec.shape)
    gid = rec[:, GROUP_LANE:GROUP_LANE + 1]
    onehot = jnp.where(lane.astype(F32) == gid, 1.0, 0.0)
    before = _dot(tri_ref[...], onehot.astype(BF16)) + carry_s[0:1, :]
    rank = jnp.sum(onehot * before, axis=1, keepdims=True)
    info_ref[...] = jnp.where(lane == 0, rank, jnp.where(lane == 1, gid, 0.0))
    carry_s[...] = carry_s[...] + jnp.sum(onehot, axis=0, keepdims=True)
    counts_ref[...] = carry_s[...]


def _plan(xa, tri):
    t = xa.shape[0]
    tm = TM_PLAN
    return pl.pallas_call(
        _plan_kernel,
        grid=(t // tm,),
        in_specs=[
            pl.BlockSpec((tm, LANES), lambda i: (i, D_MODEL // LANES)),
            pl.BlockSpec((tm, tm), lambda i: (0, 0)),
        ],
        out_specs=(
            pl.BlockSpec((tm, LANES), lambda i: (i, 0)),
            pl.BlockSpec((SUBLANES, LANES), lambda i: (0, 0)),
        ),
        out_shape=(
            jax.ShapeDtypeStruct((t, LANES), F32),
            jax.ShapeDtypeStruct((SUBLANES, LANES), F32),
        ),
        scratch_shapes=[pltpu.VMEM((SUBLANES, LANES), F32)],
        compiler_params=pltpu.CompilerParams(
            dimension_semantics=("arbitrary",), vmem_limit_bytes=VMEM_LIMIT),
        name="plan",
    )(xa, tri)


def _row_copies(n_rows, make_copy):
    def start(g, carry):
        r0 = pl.multiple_of(g * ROW_COPY_UNROLL, ROW_COPY_UNROLL)
        for k in range(ROW_COPY_UNROLL):
            make_copy(r0, k).start()
        return carry

    def wait(g, carry):
        for _ in range(ROW_COPY_UNROLL):
            make_copy(0, 0).wait()
        return carry

    lax.fori_loop(0, n_rows // ROW_COPY_UNROLL, start, 0)
    lax.fori_loop(0, n_rows // ROW_COPY_UNROLL, wait, 0)


def _dispatch_kernel(slot_ref, xa_ref, zero_hbm, xs_hbm, sem):
    del zero_hbm
    first = pl.program_id(0) * xa_ref.shape[0]

    def copy(r0, k):
        slot = slot_ref[first + r0 + k]
        return pltpu.make_async_copy(xa_ref.at[pl.ds(r0 + k, 1), :], xs_hbm.at[pl.ds(slot, 1), :], sem)

    _row_copies(xa_ref.shape[0], copy)


def _dispatch(slot, xa, n_sorted):
    t = xa.shape[0]
    tm = TM_ROWS
    zeros = jnp.zeros((n_sorted, D_AUG), F32)
    return pl.pallas_call(
        _dispatch_kernel,
        grid_spec=pltpu.PrefetchScalarGridSpec(
            num_scalar_prefetch=1,
            grid=(t // tm,),
            in_specs=[
                pl.BlockSpec((tm, D_AUG), lambda i, *_: (i, 0)),
                pl.BlockSpec(memory_space=pl.ANY),
            ],
            out_specs=pl.BlockSpec(memory_space=pl.ANY),
            scratch_shapes=[pltpu.SemaphoreType.DMA],
        ),
        out_shape=jax.ShapeDtypeStruct((n_sorted, D_AUG), F32),
        input_output_aliases={2: 0},
        compiler_params=pltpu.CompilerParams(
            dimension_semantics=("arbitrary",), vmem_limit_bytes=VMEM_LIMIT),
        name="dispatch",
    )(slot, xa, zeros)


def _experts_kernel(widx_ref, used_ref, tgrp_ref, xs_ref, wg_ref, wu_ref, wd_ref, g2_ref, gf_ref,
                    fs_ref, h_s, y_s):
    del widx_ref
    j = pl.program_id(0)
    e = pl.program_id(1)
    used = used_ref[j] == 1

    @pl.when(used & (e == 0))
    def _():
        x1 = xs_ref[:, 0:D_MODEL]
        h_s[...] = _normed(x1, g2_ref[...])
        y_s[...] = jnp.zeros_like(y_s)

    @pl.when(used)
    def _():
        h = h_s[...]
        act = jax.nn.silu(_dot(h, wg_ref[0].astype(BF16))) * _dot(h, wu_ref[0].astype(BF16))
        y = _dot(act.astype(BF16), wd_ref[0].astype(BF16))
        rec = xs_ref[:, D_MODEL:D_AUG]
        col = tgrp_ref[j] * EXPERTS_PER_GROUP + e
        weight = jnp.sum(jnp.where(_lane_iota(rec.shape) == col, rec, 0.0), axis=1, keepdims=True)
        y_s[...] += weight * y

    @pl.when(used & (e == EXPERTS_PER_GROUP - 1))
    def _():
        x = xs_ref[:, 0:D_MODEL] + y_s[...]
        fs_ref[...] = x * _rms_scale(x) * gf_ref[...]

    @pl.when(jnp.logical_not(used) & (e == EXPERTS_PER_GROUP - 1))
    def _():
        fs_ref[...] = jnp.zeros_like(fs_ref)


def _experts(widx, used, tgrp, xs, wg, wu, wd, g2, gf):
    n_sorted = xs.shape[0]
    tm = TM_EXP
    row = lambda j, e, *_: (j, 0)
    wsel = lambda j, e, widx, used, tgrp: (widx[j * EXPERTS_PER_GROUP + e], 0, 0)
    const = lambda j, e, *_: (0, 0)
    return pl.pallas_call(
        _experts_kernel,
        grid_spec=pltpu.PrefetchScalarGridSpec(
            num_scalar_prefetch=3,
            grid=(n_sorted // tm, EXPERTS_PER_GROUP),
            in_specs=[
                pl.BlockSpec((tm, D_AUG), row),
                pl.BlockSpec((1, D_MODEL, D_EXPERT), wsel),
                pl.BlockSpec((1, D_MODEL, D_EXPERT), wsel),
                pl.BlockSpec((1, D_EXPERT, D_MODEL), wsel),
                pl.BlockSpec((1, D_MODEL), const),
                pl.BlockSpec((1, D_MODEL), const),
            ],
            out_specs=pl.BlockSpec((tm, D_MODEL), row),
            scratch_shapes=[pltpu.VMEM((tm, D_MODEL), BF16), pltpu.VMEM((tm, D_MODEL), F32)],
        ),
        out_shape=jax.ShapeDtypeStruct((n_sorted, D_MODEL), F32),
        compiler_params=pltpu.CompilerParams(
            dimension_semantics=("arbitrary", "arbitrary"), vmem_limit_bytes=VMEM_LIMIT),
        name="experts",
    )(widx, used, tgrp, xs, wg, wu, wd, g2, gf)


def _combine_kernel(slot_ref, fs_hbm, o_ref, sem):
    first = pl.program_id(0) * o_ref.shape[0]

    def copy(r0, k):
        slot = slot_ref[first + r0 + k]
        return pltpu.make_async_copy(fs_hbm.at[pl.ds(slot, 1), :], o_ref.at[pl.ds(r0 + k, 1), :], sem)

    _row_copies(o_ref.shape[0], copy)


def _combine(slot, fs, t):
    tm = TM_ROWS
    return pl.pallas_call(
        _combine_kernel,
        grid_spec=pltpu.PrefetchScalarGridSpec(
            num_scalar_prefetch=1,
            grid=(t // tm,),
            in_specs=[pl.BlockSpec(memory_space=pl.ANY)],
            out_specs=pl.BlockSpec((tm, D_MODEL), lambda i, *_: (i, 0)),
            scratch_shapes=[pltpu.SemaphoreType.DMA],
        ),
        out_shape=jax.ShapeDtypeStruct((t, D_MODEL), F32),
        compiler_params=pltpu.CompilerParams(
            dimension_semantics=("arbitrary",), vmem_limit_bytes=VMEM_LIMIT),
        name="combine",
    )(slot, fs)


def _moe_sorted(xa, wg, wu, wd, g2, gf):
    t = xa.shape[0]
    tri = jnp.asarray(np.tril(np.ones((TM_PLAN, TM_PLAN), np.float32), -1), dtype=BF16)
    info, counts = _plan(xa, tri)
    rank = info[:, 0].astype(jnp.int32)
    gid = info[:, 1].astype(jnp.int32)
    counts = counts[0, :N_GROUPS].astype(jnp.int32)

    n_tiles = t // TM_EXP + N_GROUPS
    tiles_g = (counts + TM_EXP - 1) // TM_EXP
    tile_end = jnp.cumsum(tiles_g)
    base = (tile_end - tiles_g) * TM_EXP
    groups = jnp.arange(N_GROUPS, dtype=jnp.int32)
    slot = jnp.sum(jnp.where(gid[:, None] == groups[None, :], base[None, :], 0), axis=1) + rank
    tile_ids = jnp.arange(n_tiles, dtype=jnp.int32)
    used = (tile_ids < tile_end[-1]).astype(jnp.int32)
    last = jnp.maximum(tile_end[-1] - 1, 0)
    tgrp = jnp.sum((jnp.minimum(tile_ids, last)[:, None] >= tile_end[None, :]).astype(jnp.int32), axis=1)
    step_e = jnp.arange(EXPERTS_PER_GROUP, dtype=jnp.int32)[None, :]
    widx = jnp.where(used[:, None] == 1, tgrp[:, None] * EXPERTS_PER_GROUP + step_e,
                     tgrp[:, None] * EXPERTS_PER_GROUP + EXPERTS_PER_GROUP - 1).reshape(-1)

    xs = _dispatch(slot, xa, n_tiles * TM_EXP)
    fs = _experts(widx, used, tgrp, xs, wg, wu, wd, g2, gf)
    return _combine(slot, fs, t)


def _split_w_in(w):
    sizes = [D_RNN, D_RNN, Q_DIM] + [KV_DIM] * 6 + [3 * N_HEADS, 2 * D_MODEL]
    pts = np.cumsum(sizes)[:-1]
    return jnp.split(w, [int(p) for p in pts], axis=-1)


def _proj_weights(w_in):
    xr, gr, q, kc, vc, ksl, vsl, kw, vw, nsa_g, merge_g = _split_w_in(w_in)
    ng = nsa_g.reshape(D_MODEL, 3, N_KV_GROUPS, HEADS_PER_GROUP).transpose(0, 2, 1, 3)
    ng = ng.reshape(D_MODEL, N_KV_GROUPS, 3 * HEADS_PER_GROUP)
    ng = jnp.pad(ng, ((0, 0), (0, 0), (0, LANES - 3 * HEADS_PER_GROUP))).reshape(D_MODEL, N_KV_GROUPS * LANES)
    w_attn = jnp.concatenate([q, kc, vc, ksl, vsl, kw, vw, ng], axis=1)
    return (jnp.concatenate([xr, gr], axis=1).astype(BF16), w_attn.astype(BF16), merge_g.astype(BF16))


def _rope_tables(pos, width):
    inv_freq = ROPE_THETA ** (-(jnp.arange(0, HEAD_DIM, 2, dtype=F32) / HEAD_DIM))
    ang = pos.astype(F32)[:, None] * inv_freq[None, :]
    cos, sin = jnp.cos(ang), jnp.sin(ang)
    reps = width // HEAD_DIM
    return jnp.tile(jnp.concatenate([cos, cos], axis=1), (1, reps)), jnp.tile(jnp.concatenate([-sin, sin], axis=1), (1, reps))


def _block_diag(w, per):
    nb, d = w.shape[0], w.shape[1]
    w = w.reshape(nb // per, per, d, d)
    eye = jnp.eye(per, dtype=w.dtype)
    return jnp.einsum('npij,pq->npiqj', w, eye).reshape(nb // per, per * d, per * d)


def _cmp_to_sel():
    n_sel_pad = LANES
    c0 = np.arange(N_CMP_PAD) * CMP_STRIDE
    s0 = np.arange(n_sel_pad) * SEL_BLOCK
    ov = np.minimum(c0[:, None] + CMP_BLOCK, s0[None, :] + SEL_BLOCK) - np.maximum(c0[:, None], s0[None, :])
    m = np.clip(ov, 0, None) / CMP_BLOCK
    m[:, SEL_BLOCK // 2:] = 0.0
    m[N_CMP_PAD - 1, :] = 0.0
    return m.astype(np.float32)


def kernel(x, norm1_g, w_in, conv_w, conv_b, lru_wa, lru_ba, lru_wi, lru_bi, lru_lambda, w_rnn_out, cmpk_pos, cmpk_w1, cmpk_b1, cmpk_w2, cmpk_b2, cmpv_pos, cmpv_w1, cmpv_b1, cmpv_w2, cmpv_b2, w_nsa_out, w_mix_out, norm2_g, router_group_w, router_group_b, router_expert_w, router_expert_b, expert_w_gate, expert_w_up, expert_w_down, final_norm_g):
    batch, seq, _ = x.shape
    t = batch * seq
    assert w_in.shape[0] == 1, "the final norm is fused into the expert kernel: single layer only"
    x2 = x.reshape(t, D_MODEL)

    cos, sin = _rope_tables(jnp.arange(seq), LANES)
    cmp_ends = jnp.arange(N_CMP_PAD) * CMP_STRIDE + (CMP_BLOCK - 1)
    ccos, csin = _rope_tables(cmp_ends, LANES)
    reps = TR_CMP // N_CMP_PAD
    cmp_cos = jnp.stack([jnp.tile(ccos, (reps, 1)), jnp.ones((TR_CMP, LANES), F32)])
    cmp_sin = jnp.stack([jnp.tile(csin, (reps, 1)), jnp.zeros((TR_CMP, LANES), F32)])
    key_blk = np.arange(seq)[:, None] // SEL_BLOCK
    e_mat = jnp.asarray((key_blk == np.arange(LANES)[None, :]).astype(np.float32), dtype=BF16)
    cs_mat = jnp.asarray(_cmp_to_sel(), dtype=BF16)

    l = 0
    w_xg, w_attn, w_mg = _proj_weights(w_in[l])
    g1 = norm1_g[l][None, :]
    q, kcv, ksd, vsd, kwd, vwd, ng = _inproj(x2, g1, w_attn, cos, sin, seq)

    y_a = _rnn(x2, g1, w_xg, conv_w[l], conv_b[l][None, :],
               _block_diag(lru_wa[l], 4).astype(BF16), lru_ba[l][None, :],
               _block_diag(lru_wi[l], 4).astype(BF16), lru_bi[l][None, :],
               lru_lambda[l][None, :], w_rnn_out[l].astype(BF16), batch, seq)

    half = CMP_STRIDE * HEAD_DIM
    pos = jnp.stack([cmpk_pos[l].reshape(2, half), cmpv_pos[l].reshape(2, half)])
    w1 = jnp.stack([cmpk_w1[l], cmpv_w1[l]]).astype(BF16)
    b1 = jnp.stack([cmpk_b1[l], cmpv_b1[l]])[:, None, :]
    w2 = jnp.stack([cmpk_w2[l], cmpv_w2[l]])
    w2 = jnp.concatenate([w2, w2], axis=2).astype(BF16)
    b2 = jnp.stack([cmpk_b2[l], cmpv_b2[l]])
    b2 = jnp.concatenate([b2, b2], axis=1)[:, None, :]
    kvc = _compress(kcv, pos, w1, b1, w2, b2, cmp_cos, cmp_sin, batch, seq)

    o_nsa = _attn(q, kvc, ksd, vsd, kwd, vwd, ng, e_mat, cs_mat, batch, seq)

    wr = jnp.concatenate([
        router_expert_w[l].transpose(1, 0, 2).reshape(D_MODEL, N_EXPERTS),
        router_group_w[l],
        jnp.zeros((D_MODEL, LANES - N_EXPERTS - N_GROUPS), F32)], axis=1)
    wr_hi = wr.astype(BF16)
    wr_lo = (wr - wr_hi.astype(F32)).astype(BF16)
    rb = jnp.concatenate([router_expert_b[l].reshape(N_EXPERTS), router_group_b[l],
                          jnp.zeros((LANES - N_EXPERTS - N_GROUPS,), F32)])[None, :]
    xa = _post(o_nsa, y_a, x2, g1, w_mg, w_nsa_out[l].astype(BF16), w_mix_out[l].astype(BF16),
               norm2_g[l][None, :], jnp.concatenate([wr_hi, wr_lo], axis=1), rb)

    out = _moe_sorted(xa, expert_w_gate[l], expert_w_up[l], expert_w_down[l],
                      norm2_g[l][None, :], final_norm_g[None, :])
    return out.reshape(batch, seq, D_MODEL)
```

```python
import functools

import numpy as np
import jax
import jax.numpy as jnp
from jax import lax
from jax.experimental import pallas as pl
from jax.experimental.pallas import tpu as pltpu

D_MODEL = 1024
D_RNN = 1024
RNN_BLOCKS = 16
RNN_BLOCK_DIM = D_RNN // RNN_BLOCKS
CONV_WIDTH = 4
LRU_C = 8.0
N_HEADS = 16
HEAD_DIM = 64
HALF_DIM = HEAD_DIM // 2
N_KV_GROUPS = 4
HEADS_PER_GROUP = N_HEADS // N_KV_GROUPS
Q_DIM = N_HEADS * HEAD_DIM
KV_DIM = N_KV_GROUPS * HEAD_DIM
CMP_BLOCK = 32
CMP_STRIDE = 16
CMP_HIDDEN = 256
SEL_BLOCK = 64
SEL_TOPN = 16
WINDOW = 512
ROPE_THETA = 10000.0
FORCE_BONUS = 1e4
NEG_INF = -1e30
N_GROUPS = 4
EXPERTS_PER_GROUP = 4
N_EXPERTS = N_GROUPS * EXPERTS_PER_GROUP
D_EXPERT = 512
EPS = 1e-6

Q_SCALE = HEAD_DIM ** -0.5 * float(np.log2(np.e))

LANES = 128
SUBLANES = 8
VMEM_LIMIT = 56 * 1024 * 1024

BF16 = jnp.bfloat16
F32 = jnp.float32

COL_Q = 0
COL_KCV = COL_Q + Q_DIM
COL_KS = COL_KCV + 2 * KV_DIM
COL_VS = COL_KS + KV_DIM
COL_KW = COL_VS + KV_DIM
COL_VW = COL_KW + KV_DIM
COL_NG = COL_VW + KV_DIM
N_PROJ = COL_NG + N_KV_GROUPS * LANES

TM_PROJ = 1024
TS_RNN = 1024
TR_CMP = 256
TQ = 256
TQ_SELECT_TILE = 256
TK = 128
CK_SEL = 512
TM_POST = 1024
TM_PLAN = 1024
TM_ROWS = 4096
TM_EXP = 1024
ROW_COPY_UNROLL = 64
D_AUG = D_MODEL + LANES
GROUP_LANE = N_EXPERTS
N_CMP_PAD = 128


def _dot(a, b):
    return jnp.dot(a, b, preferred_element_type=F32)


def _dot_t(a, b):
    return lax.dot_general(a, b, (((1,), (1,)), ((), ())), preferred_element_type=F32)


def _lane_iota(shape):
    return lax.broadcasted_iota(jnp.int32, shape, len(shape) - 1)


def _row_iota(shape):
    return lax.broadcasted_iota(jnp.int32, shape, 0)


def _rope(x, cos, sin_signed):
    width = x.shape[-1]
    reps = width // cos.shape[-1]
    if reps > 1:
        cos = jnp.concatenate([cos] * reps, axis=1)
        sin_signed = jnp.concatenate([sin_signed] * reps, axis=1)
    first_half = (_lane_iota(x.shape) & (HEAD_DIM - 1)) < HALF_DIM
    partner = jnp.where(first_half, pltpu.roll(x, width - HALF_DIM, 1), pltpu.roll(x, HALF_DIM, 1))
    return x * cos + partner * sin_signed


def _spread_heads(x, fill=None):
    out = []
    low = _lane_iota((x.shape[0], LANES)) < HEAD_DIM
    for c in range(x.shape[1] // LANES):
        xc = x[:, c * LANES:(c + 1) * LANES]
        rolled = pltpu.roll(xc, HEAD_DIM, 1)
        out.append(jnp.where(low, xc, rolled if fill is None else fill))
        out.append(jnp.where(low, rolled, xc if fill is None else fill))
    return jnp.concatenate(out, axis=1)


def _sigmoid(x):
    return 0.5 * (jnp.tanh(0.5 * x) + 1.0)


def _rms_scale(x):
    sq = x * x
    part = sq[:, 0:LANES]
    for c in range(1, x.shape[1] // LANES):
        part = part + sq[:, c * LANES:(c + 1) * LANES]
    return lax.rsqrt(jnp.sum(part, axis=1, keepdims=True) * (1.0 / x.shape[1]) + EPS)


def _normed(x, g):
    return (x * _rms_scale(x) * g).astype(BF16)


def _inproj_kernel(x_ref, g_ref, w_ref, cos_ref, sin_ref,
                   q_ref, kcv_ref, ks_ref, vs_ref, kw_ref, vw_ref, ng_ref):
    h = _normed(x_ref[...], g_ref[...])
    cos = cos_ref[...]
    sin = sin_ref[...]

    def mm(lo, width):
        return _dot(h, w_ref[:, lo:lo + width])

    q_ref[...] = (_rope(mm(COL_Q, Q_DIM), cos, sin) * Q_SCALE).astype(BF16)
    kcv_ref[...] = mm(COL_KCV, 2 * KV_DIM)
    ks_ref[...] = _spread_heads(_rope(mm(COL_KS, KV_DIM), cos, sin)).astype(BF16)
    vs_ref[...] = _spread_heads(mm(COL_VS, KV_DIM), 1.0).astype(BF16)
    kw_ref[...] = _spread_heads(_rope(mm(COL_KW, KV_DIM), cos, sin)).astype(BF16)
    vw_ref[...] = _spread_heads(mm(COL_VW, KV_DIM), 1.0).astype(BF16)
    ng_ref[...] = jax.nn.sigmoid(mm(COL_NG, N_KV_GROUPS * LANES))


def _inproj(x2, norm_g, w_proj, cos, sin, seq):
    t = x2.shape[0]
    tm = TM_PROJ
    pos_blocks = seq // tm
    row = lambda i: (i, 0)
    const = lambda i: (0, 0)
    out_shape = (
        jax.ShapeDtypeStruct((t, Q_DIM), BF16),
        jax.ShapeDtypeStruct((t, 2 * KV_DIM), F32),
        jax.ShapeDtypeStruct((t, N_KV_GROUPS * LANES), BF16),
        jax.ShapeDtypeStruct((t, N_KV_GROUPS * LANES), BF16),
        jax.ShapeDtypeStruct((t, N_KV_GROUPS * LANES), BF16),
        jax.ShapeDtypeStruct((t, N_KV_GROUPS * LANES), BF16),
        jax.ShapeDtypeStruct((t, N_KV_GROUPS * LANES), F32),
    )
    return pl.pallas_call(
        _inproj_kernel,
        grid=(t // tm,),
        in_specs=[
            pl.BlockSpec((tm, D_MODEL), row),
            pl.BlockSpec((1, D_MODEL), const),
            pl.BlockSpec((D_MODEL, N_PROJ), const, pipeline_mode=pl.Buffered(1)),
            pl.BlockSpec((tm, LANES), lambda i: (i % pos_blocks, 0)),
            pl.BlockSpec((tm, LANES), lambda i: (i % pos_blocks, 0)),
        ],
        out_specs=tuple(pl.BlockSpec((tm, s.shape[1]), row) for s in out_shape),
        out_shape=out_shape,
        compiler_params=pltpu.CompilerParams(
            dimension_semantics=("parallel",), vmem_limit_bytes=VMEM_LIMIT),
        name="inproj",
    )(x2, norm_g, w_proj, cos, sin)


def _rnn_kernel(xin_ref, g1_ref, wx_ref, cw_ref, cb_ref, wa_ref, ba_ref, wi_ref, bi_ref, lam_ref, wo_ref,
                y_ref, tail_s, carry_s, a_s, u_s, h_s):
    ts = xin_ref.shape[0]

    @pl.when(pl.program_id(1) == 0)
    def _():
        tail_s[...] = jnp.zeros_like(tail_s)
        carry_s[...] = jnp.zeros_like(carry_s)

    hn = _normed(xin_ref[...], g1_ref[...])
    x = _dot(hn, wx_ref[:, 0:D_RNN])
    gate_pre = _dot(hn, wx_ref[:, D_RNN:2 * D_RNN])
    xext = jnp.concatenate([tail_s[...], x], axis=0)
    tail_s[...] = x[ts - SUBLANES:ts, :]
    conv = cb_ref[...]
    for k in range(CONV_WIDTH):
        back = CONV_WIDTH - 1 - k
        shifted = xext if back == 0 else pltpu.roll(xext, back, 0)
        conv = conv + cw_ref[k:k + 1, :] * shifted[SUBLANES:SUBLANES + ts, :]

    cb16 = conv.astype(BF16)
    blk = wa_ref.shape[1]
    r_pre = jnp.concatenate(
        [_dot(cb16[:, j * blk:(j + 1) * blk], wa_ref[j]) for j in range(D_RNN // blk)], axis=1)
    i_pre = jnp.concatenate(
        [_dot(cb16[:, j * blk:(j + 1) * blk], wi_ref[j]) for j in range(D_RNN // blk)], axis=1)
    r = _sigmoid(r_pre + ba_ref[...])
    gate_i = _sigmoid(i_pre + bi_ref[...])
    neg_lam = -lam_ref[...]
    softplus = jnp.maximum(neg_lam, 0.0) + jnp.log1p(jnp.exp(-jnp.abs(neg_lam)))
    log_a = r * ((-LRU_C) * softplus)
    a = jnp.exp(log_a)
    a_s[...] = a
    var = -jnp.tanh(log_a) * (a * a + 1.0)
    u_s[...] = jnp.where(var > 0.0, var * lax.rsqrt(var), 0.0) * (gate_i * conv)

    row = _row_iota((SUBLANES, D_RNN))

    def body(k, carry):
        off = pl.multiple_of(k * SUBLANES, SUBLANES)
        a = a_s[pl.ds(off, SUBLANES), :]
        b = u_s[pl.ds(off, SUBLANES), :]
        for sh in (1, 2, 4):
            keep = row >= sh
            a_prev = jnp.where(keep, pltpu.roll(a, sh, 0), 1.0)
            b_prev = jnp.where(keep, pltpu.roll(b, sh, 0), 0.0)
            b = a * b_prev + b
            a = a * a_prev
        h = a * carry + b
        h_s[pl.ds(off, SUBLANES), :] = h
        return jnp.broadcast_to(h[SUBLANES - 1:SUBLANES, :], (SUBLANES, D_RNN))

    carry_s[...] = lax.fori_loop(0, ts // SUBLANES, body, carry_s[...])
    gated = (jax.nn.gelu(gate_pre) * h_s[...]).astype(BF16)
    y_ref[...] = _dot(gated, wo_ref[...])


def _rnn(x2, norm_g, w_xg, conv_w, conv_b, wa_bd, ba, wi_bd, bi, lam, w_out, batch, seq):
    t = x2.shape[0]
    ts = TS_RNN
    nt = seq // ts
    const2 = lambda b, s: (0, 0)
    const3 = lambda b, s: (0, 0, 0)
    nblk, blk = wa_bd.shape[0], wa_bd.shape[1]
    return pl.pallas_call(
        _rnn_kernel,
        grid=(batch, nt),
        in_specs=[
            pl.BlockSpec((ts, D_MODEL), lambda b, s: (b * nt + s, 0)),
            pl.BlockSpec((1, D_MODEL), const2),
            pl.BlockSpec((D_MODEL, 2 * D_RNN), const2),
            pl.BlockSpec((CONV_WIDTH, D_RNN), const2),
            pl.BlockSpec((1, D_RNN), const2),
            pl.BlockSpec((nblk, blk, blk), const3),
            pl.BlockSpec((1, D_RNN), const2),
            pl.BlockSpec((nblk, blk, blk), const3),
            pl.BlockSpec((1, D_RNN), const2),
            pl.BlockSpec((1, D_RNN), const2),
            pl.BlockSpec((D_RNN, D_MODEL), const2),
        ],
        out_specs=pl.BlockSpec((ts, D_MODEL), lambda b, s: (b * nt + s, 0)),
        out_shape=jax.ShapeDtypeStruct((t, D_MODEL), F32),
        scratch_shapes=[
            pltpu.VMEM((SUBLANES, D_RNN), F32),
            pltpu.VMEM((SUBLANES, D_RNN), F32),
            pltpu.VMEM((ts, D_RNN), F32),
            pltpu.VMEM((ts, D_RNN), F32),
            pltpu.VMEM((ts, D_RNN), F32),
        ],
        compiler_params=pltpu.CompilerParams(
            dimension_semantics=("parallel", "arbitrary"), vmem_limit_bytes=VMEM_LIMIT),
        name="rnn",
    )(x2, norm_g, w_xg, conv_w, conv_b, wa_bd, ba, wi_bd, bi, lam, w_out)


def _compress_kernel(x_ref, pos_ref, w1_ref, b1_ref, w2_ref, b2_ref, cos_ref, sin_ref, o_ref):
    n_rows = x_ref.shape[0] // CMP_STRIDE
    strided = [x_ref[pl.ds(r, n_rows, stride=CMP_STRIDE), :] for r in range(CMP_STRIDE)]
    low = _lane_iota((n_rows, LANES)) < HEAD_DIM
    groups = []
    for g in range(2):
        tiles = []
        for j in range(CMP_STRIDE // 2):
            even = strided[2 * j]
            odd = strided[2 * j + 1]
            if g == 0:
                tiles.append(jnp.where(low, even, pltpu.roll(odd, HEAD_DIM, 1)))
            else:
                tiles.append(jnp.where(low, pltpu.roll(even, HEAD_DIM, 1), odd))
        groups.append(jnp.concatenate(tiles, axis=1))
    x = jnp.concatenate(groups, axis=0)
    tr = x.shape[0]
    half = x.shape[1]
    first = _dot((x + pos_ref[0, 0:1, :]).astype(BF16), w1_ref[0, 0:half, :])
    second = _dot((x + pos_ref[0, 1:2, :]).astype(BF16), w1_ref[0, half:2 * half, :])
    hid = jax.nn.gelu(first + pltpu.roll(second, tr - 1, 0) + b1_ref[0])
    out = _dot(hid.astype(BF16), w2_ref[0]) + b2_ref[0]
    o_ref[0] = _rope(out, cos_ref[0], sin_ref[0]).astype(BF16)


def _compress(kcv, pos, w1, b1, w2, b2, cos, sin, batch, seq):
    pairs = KV_DIM // LANES
    tr = TR_CMP
    rows = batch * pairs * tr
    sel = lambda k, r: (k, 0, 0)
    return pl.pallas_call(
        _compress_kernel,
        grid=(2, batch * pairs),
        in_specs=[
            pl.BlockSpec((seq, LANES), lambda k, r: (r // pairs, k * pairs + r % pairs)),
            pl.BlockSpec((1, 2, pos.shape[2]), sel),
            pl.BlockSpec((1,) + w1.shape[1:], sel),
            pl.BlockSpec((1, 1, CMP_HIDDEN), sel),
            pl.BlockSpec((1,) + w2.shape[1:], sel),
            pl.BlockSpec((1, 1, LANES), sel),
            pl.BlockSpec((1, tr, LANES), sel),
            pl.BlockSpec((1, tr, LANES), sel),
        ],
        out_specs=pl.BlockSpec((1, tr, LANES), lambda k, r: (k, r, 0)),
        out_shape=jax.ShapeDtypeStruct((2, rows, LANES), BF16),
        compiler_params=pltpu.CompilerParams(
            dimension_semantics=("parallel", "parallel"), vmem_limit_bytes=VMEM_LIMIT),
        name="compress",
    )(kcv, pos, w1, b1, w2, b2, cos, sin)


def _lane_tile_max(s):
    tiles = [s[:, c * LANES:(c + 1) * LANES] for c in range(s.shape[1] // LANES)]
    while len(tiles) > 1:
        tiles = [jnp.maximum(tiles[k], tiles[k + 1]) if k + 1 < len(tiles) else tiles[k]
                 for k in range(0, len(tiles), 2)]
    return tiles[0]


def _stack_heads(q):
    tq = q.shape[0]
    low = _lane_iota((tq, LANES)) < HEAD_DIM
    zero = jnp.zeros((tq, LANES), BF16)
    heads = []
    for hh in range(HEADS_PER_GROUP):
        pair = q[:, (hh // 2) * LANES:(hh // 2 + 1) * LANES]
        heads.append(jnp.where(low if hh % 2 == 0 else jnp.logical_not(low), pair, zero))
    return jnp.concatenate(heads, axis=0)


def _select_tile(i, q, kc_ref, vc_ref, ng, m_ref):
    tq = q.shape[0]
    rows = HEADS_PER_GROUP * tq
    low = _lane_iota((tq, LANES)) < HEAD_DIM
    q4 = _stack_heads(q)
    lane4 = _lane_iota((rows, LANES))
    qpos4 = i * tq + (_row_iota((rows, LANES)) & (tq - 1))

    sc = _dot_t(q4, kc_ref[0])
    sc = jnp.where(lane4 * CMP_STRIDE + (CMP_BLOCK - 1) <= qpos4, sc, NEG_INF)
    mc = jnp.max(sc, axis=1, keepdims=True)
    pc = jnp.exp2(sc - mc)
    pc = pc / jnp.sum(pc, axis=1, keepdims=True)
    pc = jnp.where(qpos4 >= CMP_BLOCK - 1, pc, 0.0)
    o_cmp = _dot(pc.astype(BF16), vc_ref[0])

    psum = pc[0:tq] + pc[tq:2 * tq] + pc[2 * tq:3 * tq] + pc[3 * tq:4 * tq]
    p_hi = psum.astype(BF16)
    rem = psum - p_hi.astype(F32)
    p_mid = rem.astype(BF16)
    p_lo = (rem - p_mid.astype(F32)).astype(BF16)
    cs = m_ref[...]
    p_slc = _dot(p_hi, cs) + _dot(p_mid, cs) + _dot(p_lo, cs)

    n_sel = SEL_BLOCK // 2
    p_slc_t = p_slc.T[0:n_sel, :]
    blk = _row_iota((n_sel, tq))
    tblk = (i * tq + _lane_iota((n_sel, tq))) >> 6
    forced = (blk == 0) | (blk == tblk) | (blk == tblk - 1)
    score = jnp.where(blk <= tblk, p_slc_t + jnp.where(forced, FORCE_BONUS, 0.0), -1.0)
    rank = jnp.zeros((n_sel, tq), F32)
    for j in range(n_sel):
        sj = score[j:j + 1, :]
        beats = (sj > score) | ((sj == score) & (blk > j))
        rank = rank + jnp.where(beats, 1.0, 0.0)
    bias_t = jnp.where(rank < SEL_TOPN, 0.0, NEG_INF)
    bias = jnp.concatenate([bias_t, jnp.zeros((LANES - n_sel, tq), F32)], axis=0).T.astype(BF16)

    gated = [ng[:, hh:hh + 1] * o_cmp[hh * tq:(hh + 1) * tq] for hh in range(HEADS_PER_GROUP)]
    o_pairs = jnp.concatenate([jnp.where(low, gated[0], gated[1]), jnp.where(low, gated[2], gated[3])], axis=1)
    return bias, o_pairs


def _attn_kernel(*refs):
    n_full = pl.program_id(2) // (CK_SEL // TQ)
    for n_chunks in range(1, refs[3].shape[0] // CK_SEL + 1):
        pl.when(n_full == n_chunks - 1)(functools.partial(_attn_step, n_chunks, *refs))


def _attn_step(n_chunks, q_ref, kc_ref, vc_ref, ks_ref, vs_ref, kw_ref, vw_ref, ng_ref, e_ref, m_ref, o_ref):
    i = pl.program_id(2)
    tq = q_ref.shape[0]
    rows = HEADS_PER_GROUP * tq
    ck = CK_SEL
    low = _lane_iota((tq, LANES)) < HEAD_DIM
    q4 = _stack_heads(q_ref[...])

    st = TQ_SELECT_TILE
    picks = [_select_tile(i * (tq // st) + sub, q_ref[sub * st:(sub + 1) * st, :], kc_ref, vc_ref,
                          ng_ref[sub * st:(sub + 1) * st, :], m_ref) for sub in range(tq // st)]
    bias = jnp.concatenate([p[0] for p in picks], axis=0)
    o_cmp = jnp.concatenate([p[1] for p in picks], axis=0)

    n_sub = tq // TK
    back = WINDOW // TK
    k_tiles, v_tiles = [], []
    for jj in range(back + n_sub):
        j = i * n_sub - back + jj
        off = pl.multiple_of(jnp.maximum(j, 0) * TK, TK)
        k_tiles.append(kw_ref[pl.ds(off, TK), :])
        v_tiles.append(vw_ref[pl.ds(off, TK), :])
    sw = _dot_t(q4, jnp.concatenate(k_tiles, axis=0))
    on_or_after = _row_iota((TK, TK)) >= _lane_iota((TK, TK))
    sw_tiles = []
    for jj in range(back + n_sub):
        blocks = []
        for blk_i in range(rows // TK):
            piece = sw[blk_i * TK:(blk_i + 1) * TK, jj * TK:(jj + 1) * TK]
            tiles_back = blk_i % n_sub + back - jj
            if tiles_back == 0:
                piece = jnp.where(on_or_after, piece, NEG_INF)
            elif tiles_back == back:
                piece = jnp.where(on_or_after, NEG_INF, piece)
            elif tiles_back < 0 or tiles_back > back:
                piece = jnp.full((TK, TK), NEG_INF, F32)
            blocks.append(piece)
        tile = jnp.concatenate(blocks, axis=0)
        if jj < back:
            tile = jnp.where(i * n_sub - back + jj >= 0, tile, NEG_INF)
        sw_tiles.append(tile)
    sw = jnp.concatenate(sw_tiles, axis=1)
    m_win = jnp.max(_lane_tile_max(sw), axis=1, keepdims=True)
    acc_win = _dot(jnp.exp2(sw - m_win).astype(BF16), jnp.concatenate(v_tiles, axis=0))

    qa = jnp.concatenate([q4, jnp.concatenate([bias] * HEADS_PER_GROUP, axis=0)], axis=1)
    rel_q = i * tq - (n_chunks - 1) * ck + (_row_iota((rows, ck)) & (tq - 1))
    m_run = None
    acc_sel = None
    for c in range(n_chunks):
        ka = jnp.concatenate([ks_ref[c * ck:(c + 1) * ck, :], e_ref[c * ck:(c + 1) * ck, :]], axis=1)
        s = _dot_t(qa, ka)
        if c == n_chunks - 1:
            s = jnp.where(_lane_iota((rows, ck)) <= rel_q, s, NEG_INF)
        m_new = jnp.max(_lane_tile_max(s), axis=1, keepdims=True)
        if c > 0:
            m_new = jnp.maximum(m_run, m_new)
            acc_sel = acc_sel * jnp.exp2(m_run - m_new)
        part = _dot(jnp.exp2(s - m_new).astype(BF16), vs_ref[c * ck:(c + 1) * ck, :])
        acc_sel = part if c == 0 else acc_sel + part
        m_run = m_new

    ng = ng_ref[...]
    outs = []
    for hh in range(HEADS_PER_GROUP):
        r0 = hh * tq
        col = lambda br: ng[:, br * HEADS_PER_GROUP + hh:br * HEADS_PER_GROUP + hh + 1]
        parts = []
        for acc in (acc_sel[r0:r0 + tq], acc_win[r0:r0 + tq]):
            swapped = pltpu.roll(acc, HEAD_DIM, 1)
            parts.append(acc / swapped if hh % 2 == 0 else swapped / acc)
        outs.append(col(1) * parts[0] + col(2) * parts[1])
    o_ref[...] = (o_cmp + jnp.concatenate(
        [jnp.where(low, outs[0], outs[1]), jnp.where(low, outs[2], outs[3])], axis=1)).astype(o_ref.dtype)


def _attn(q, kvc, ksd, vsd, kwd, vwd, ng, e_mat, cs_mat, batch, seq):
    t = q.shape[0]
    tq = TQ
    nq = seq // tq
    qrow = lambda b, g, i: (b * nq + i, g)
    kv = lambda b, g, i: (b, g)
    const = lambda b, g, i: (0, 0)
    return pl.pallas_call(
        _attn_kernel,
        grid=(batch, N_KV_GROUPS, nq),
        in_specs=[
            pl.BlockSpec((tq, HEADS_PER_GROUP * HEAD_DIM), qrow),
            pl.BlockSpec((1, N_CMP_PAD, LANES), lambda b, g, i: (0, b * N_KV_GROUPS + g, 0)),
            pl.BlockSpec((1, N_CMP_PAD, LANES), lambda b, g, i: (1, b * N_KV_GROUPS + g, 0)),
            pl.BlockSpec((seq, LANES), kv),
            pl.BlockSpec((seq, LANES), kv),
            pl.BlockSpec((seq, LANES), kv),
            pl.BlockSpec((seq, LANES), kv),
            pl.BlockSpec((tq, LANES), qrow),
            pl.BlockSpec((seq, LANES), const),
            pl.BlockSpec((N_CMP_PAD, LANES), const),
        ],
        out_specs=pl.BlockSpec((tq, HEADS_PER_GROUP * HEAD_DIM), qrow),
        out_shape=jax.ShapeDtypeStruct((t, Q_DIM), BF16),
        compiler_params=pltpu.CompilerParams(
            dimension_semantics=("parallel", "parallel", "arbitrary"), vmem_limit_bytes=VMEM_LIMIT),
        name="attn",
    )(q, kvc, kvc, ksd, vsd, kwd, vwd, ng, e_mat, cs_mat)


def _post_kernel(o_ref, ya_ref, x_ref, g1_ref, wg_ref, wn_ref, wm_ref, g2_ref, wr_ref, rb_ref,
                 xa_ref):
    hn = _normed(x_ref[...], g1_ref[...])
    gate_a = jax.nn.sigmoid(_dot(hn, wg_ref[:, 0:D_MODEL]))
    gate_b = jax.nn.sigmoid(_dot(hn, wg_ref[:, D_MODEL:2 * D_MODEL]))
    y_b = _dot(o_ref[...], wn_ref[...])
    mixed = gate_a * ya_ref[...] + gate_b * y_b
    x1 = x_ref[...] + _dot(mixed.astype(BF16), wm_ref[...])
    xa_ref[:, 0:D_MODEL] = x1
    h2 = x1 * _rms_scale(x1) * g2_ref[...]
    h_hi = h2.astype(BF16)
    h_lo = (h2 - h_hi.astype(F32)).astype(BF16)
    both = _dot(h_hi, wr_ref[...])
    logits = both[:, 0:LANES] + both[:, LANES:2 * LANES] + _dot(h_lo, wr_ref[:, 0:LANES]) + rb_ref[...]

    lane = _lane_iota(logits.shape).astype(F32)
    is_grp = (lane >= N_EXPERTS) & (lane < N_EXPERTS + N_GROUPS)
    gl = jnp.where(is_grp, logits, NEG_INF)
    ge = jnp.exp(gl - jnp.max(gl, axis=1, keepdims=True))
    gp = ge / jnp.sum(ge, axis=1, keepdims=True)
    g_w = jnp.max(gp, axis=1, keepdims=True)
    big = float(4 * LANES)
    g_first = jnp.min(jnp.where(is_grp & (gp == g_w), lane, big), axis=1, keepdims=True)
    grp_lo = (g_first - N_EXPERTS) * EXPERTS_PER_GROUP

    in_grp = (lane >= grp_lo) & (lane < grp_lo + EXPERTS_PER_GROUP)
    el = jnp.where(in_grp, logits, NEG_INF)
    ee = jnp.exp(el - jnp.max(el, axis=1, keepdims=True))
    ep = ee / jnp.sum(ee, axis=1, keepdims=True)
    w1 = jnp.max(ep, axis=1, keepdims=True)
    i1 = jnp.min(jnp.where(in_grp & (ep == w1), lane, big), axis=1, keepdims=True)
    rest = jnp.where(in_grp & (lane != i1), ep, -1.0)
    w2 = jnp.max(rest, axis=1, keepdims=True)
    i2 = jnp.min(jnp.where(rest == w2, lane, big), axis=1, keepdims=True)
    den = w1 + w2
    comb = jnp.where(lane == i1, g_w * (w1 / den), jnp.where(lane == i2, g_w * (w2 / den), 0.0))
    xa_ref[:, D_MODEL:D_MODEL + LANES] = jnp.where(lane == GROUP_LANE, g_first - N_EXPERTS, comb)


def _post(o_nsa, y_a, x2, g1, w_mg, w_nsa, w_mix, g2, wr, rb):
    t = x2.shape[0]
    tm = TM_POST
    row = lambda i: (i, 0)
    const = lambda i: (0, 0)
    return pl.pallas_call(
        _post_kernel,
        grid=(t // tm,),
        in_specs=[
            pl.BlockSpec((tm, Q_DIM), row),
            pl.BlockSpec((tm, D_MODEL), row),
            pl.BlockSpec((tm, D_MODEL), row),
            pl.BlockSpec((1, D_MODEL), const),
            pl.BlockSpec((D_MODEL, 2 * D_MODEL), const),
            pl.BlockSpec((Q_DIM, D_MODEL), const),
            pl.BlockSpec((D_MODEL, D_MODEL), const),
            pl.BlockSpec((1, D_MODEL), const),
            pl.BlockSpec((D_MODEL, 2 * LANES), const),
            pl.BlockSpec((1, LANES), const),
        ],
        out_specs=pl.BlockSpec((tm, D_AUG), row),
        out_shape=jax.ShapeDtypeStruct((t, D_AUG), F32),
        compiler_params=pltpu.CompilerParams(
            dimension_semantics=("parallel",), vmem_limit_bytes=VMEM_LIMIT),
        name="post",
    )(o_nsa, y_a, x2, g1, w_mg, w_nsa, w_mix, g2, wr, rb)


def _plan_kernel(rec_ref, tri_ref, info_ref, counts_ref, carry_s):
    @pl.when(pl.program_id(0) == 0)
    def _():
        carry_s[...] = jnp.zeros_like(carry_s)

    rec = rec_ref[...]
    lane = _lane_iota(rec.shape)
    gid = rec[:, GROUP_LANE:GROUP_LANE + 1]
    onehot = jnp.where(lane.astype(F32) == gid, 1.0, 0.0)
    before = _dot(tri_ref[...], onehot.astype(BF16)) + carry_s[0:1, :]
    rank = jnp.sum(onehot * before, axis=1, keepdims=True)
    info_ref[...] = jnp.where(lane == 0, rank, jnp.where(lane == 1, gid, 0.0))
    carry_s[...] = carry_s[...] + jnp.sum(onehot, axis=0, keepdims=True)
    counts_ref[...] = carry_s[...]


def _plan(xa, tri):
    t = xa.shape[0]
    tm = TM_PLAN
    return pl.pallas_call(
        _plan_kernel,
        grid=(t // tm,),
        in_specs=[
            pl.BlockSpec((tm, LANES), lambda i: (i, D_MODEL // LANES)),
            pl.BlockSpec((tm, tm), lambda i: (0, 0)),
        ],
        out_specs=(
            pl.BlockSpec((tm, LANES), lambda i: (i, 0)),
            pl.BlockSpec((SUBLANES, LANES), lambda i: (0, 0)),
        ),
        out_shape=(
            jax.ShapeDtypeStruct((t, LANES), F32),
            jax.ShapeDtypeStruct((SUBLANES, LANES), F32),
        ),
        scratch_shapes=[pltpu.VMEM((SUBLANES, LANES), F32)],
        compiler_params=pltpu.CompilerParams(
            dimension_semantics=("arbitrary",), vmem_limit_bytes=VMEM_LIMIT),
        name="plan",
    )(xa, tri)


def _row_copies(n_rows, make_copy):
    def start(g, carry):
        r0 = pl.multiple_of(g * ROW_COPY_UNROLL, ROW_COPY_UNROLL)
        for k in range(ROW_COPY_UNROLL):
            make_copy(r0, k).start()
        return carry

    def wait(g, carry):
        for _ in range(ROW_COPY_UNROLL):
            make_copy(0, 0).wait()
        return carry

    lax.fori_loop(0, n_rows // ROW_COPY_UNROLL, start, 0)
    lax.fori_loop(0, n_rows // ROW_COPY_UNROLL, wait, 0)


def _dispatch_kernel(slot_ref, xa_ref, zero_hbm, xs_hbm, sem):
    del zero_hbm
    first = pl.program_id(0) * xa_ref.shape[0]

    def copy(r0, k):
        slot = slot_ref[first + r0 + k]
        return pltpu.make_async_copy(xa_ref.at[pl.ds(r0 + k, 1), :], xs_hbm.at[pl.ds(slot, 1), :], sem)

    _row_copies(xa_ref.shape[0], copy)


def _dispatch(slot, xa, n_sorted):
    t = xa.shape[0]
    tm = TM_ROWS
    zeros = jnp.zeros((n_sorted, D_AUG), F32)
    return pl.pallas_call(
        _dispatch_kernel,
        grid_spec=pltpu.PrefetchScalarGridSpec(
            num_scalar_prefetch=1,
            grid=(t // tm,),
            in_specs=[
                pl.BlockSpec((tm, D_AUG), lambda i, *_: (i, 0)),
                pl.BlockSpec(memory_space=pl.ANY),
            ],
            out_specs=pl.BlockSpec(memory_space=pl.ANY),
            scratch_shapes=[pltpu.SemaphoreType.DMA],
        ),
        out_shape=jax.ShapeDtypeStruct((n_sorted, D_AUG), F32),
        input_output_aliases={2: 0},
        compiler_params=pltpu.CompilerParams(
            dimension_semantics=("arbitrary",), vmem_limit_bytes=VMEM_LIMIT),
        name="dispatch",
    )(slot, xa, zeros)


def _experts_kernel(widx_ref, used_ref, tgrp_ref, xs_ref, wg_ref, wu_ref, wd_ref, g2_ref, gf_ref,
                    fs_ref, h_s, y_s):
    del widx_ref
    j = pl.program_id(0)
    e = pl.program_id(1)
    used = used_ref[j] == 1

    @pl.when(used & (e == 0))
    def _():
        x1 = xs_ref[:, 0:D_MODEL]
        h_s[...] = _normed(x1, g2_ref[...])
        y_s[...] = jnp.zeros_like(y_s)

    @pl.when(used)
    def _():
        h = h_s[...]
        act = jax.nn.silu(_dot(h, wg_ref[0].astype(BF16))) * _dot(h, wu_ref[0].astype(BF16))
        y = _dot(act.astype(BF16), wd_ref[0].astype(BF16))
        rec = xs_ref[:, D_MODEL:D_AUG]
        col = tgrp_ref[j] * EXPERTS_PER_GROUP + e
        weight = jnp.sum(jnp.where(_lane_iota(rec.shape) == col, rec, 0.0), axis=1, keepdims=True)
        y_s[...] += weight * y

    @pl.when(used & (e == EXPERTS_PER_GROUP - 1))
    def _():
        x = xs_ref[:, 0:D_MODEL] + y_s[...]
        fs_ref[...] = x * _rms_scale(x) * gf_ref[...]

    @pl.when(jnp.logical_not(used) & (e == EXPERTS_PER_GROUP - 1))
    def _():
        fs_ref[...] = jnp.zeros_like(fs_ref)


def _experts(widx, used, tgrp, xs, wg, wu, wd, g2, gf):
    n_sorted = xs.shape[0]
    tm = TM_EXP
    row = lambda j, e, *_: (j, 0)
    wsel = lambda j, e, widx, used, tgrp: (widx[j * EXPERTS_PER_GROUP + e], 0, 0)
    const = lambda j, e, *_: (0, 0)
    return pl.pallas_call(
        _experts_kernel,
        grid_spec=pltpu.PrefetchScalarGridSpec(
            num_scalar_prefetch=3,
            grid=(n_sorted // tm, EXPERTS_PER_GROUP),
            in_specs=[
                pl.BlockSpec((tm, D_AUG), row),
                pl.BlockSpec((1, D_MODEL, D_EXPERT), wsel),
                pl.BlockSpec((1, D_MODEL, D_EXPERT), wsel),
                pl.BlockSpec((1, D_EXPERT, D_MODEL), wsel),
                pl.BlockSpec((1, D_MODEL), const),
                pl.BlockSpec((1, D_MODEL), const),
            ],
            out_specs=pl.BlockSpec((tm, D_MODEL), row),
            scratch_shapes=[pltpu.VMEM((tm, D_MODEL), BF16), pltpu.VMEM((tm, D_MODEL), F32)],
        ),
        out_shape=jax.ShapeDtypeStruct((n_sorted, D_MODEL), F32),
        compiler_params=pltpu.CompilerParams(
            dimension_semantics=("arbitrary", "arbitrary"), vmem_limit_bytes=VMEM_LIMIT),
        name="experts",
    )(widx, used, tgrp, xs, wg, wu, wd, g2, gf)


def _combine_kernel(slot_ref, fs_hbm, o_ref, sem):
    first = pl.program_id(0) * o_ref.shape[0]

    def copy(r0, k):
        slot = slot_ref[first + r0 + k]
        return pltpu.make_async_copy(fs_hbm.at[pl.ds(slot, 1), :], o_ref.at[pl.ds(r0 + k, 1), :], sem)

    _row_copies(o_ref.shape[0], copy)


def _combine(slot, fs, t):
    tm = TM_ROWS
    return pl.pallas_call(
        _combine_kernel,
        grid_spec=pltpu.PrefetchScalarGridSpec(
            num_scalar_prefetch=1,
            grid=(t // tm,),
            in_specs=[pl.BlockSpec(memory_space=pl.ANY)],
            out_specs=pl.BlockSpec((tm, D_MODEL), lambda i, *_: (i, 0)),
            scratch_shapes=[pltpu.SemaphoreType.DMA],
        ),
        out_shape=jax.ShapeDtypeStruct((t, D_MODEL), F32),
        compiler_params=pltpu.CompilerParams(
            dimension_semantics=("arbitrary",), vmem_limit_bytes=VMEM_LIMIT),
        name="combine",
    )(slot, fs)


def _moe_sorted(xa, wg, wu, wd, g2, gf):
    t = xa.shape[0]
    tri = jnp.asarray(np.tril(np.ones((TM_PLAN, TM_PLAN), np.float32), -1), dtype=BF16)
    info, counts = _plan(xa, tri)
    rank = info[:, 0].astype(jnp.int32)
    gid = info[:, 1].astype(jnp.int32)
    counts = counts[0, :N_GROUPS].astype(jnp.int32)

    n_tiles = t // TM_EXP + N_GROUPS
    tiles_g = (counts + TM_EXP - 1) // TM_EXP
    tile_end = jnp.cumsum(tiles_g)
    base = (tile_end - tiles_g) * TM_EXP
    groups = jnp.arange(N_GROUPS, dtype=jnp.int32)
    slot = jnp.sum(jnp.where(gid[:, None] == groups[None, :], base[None, :], 0), axis=1) + rank
    tile_ids = jnp.arange(n_tiles, dtype=jnp.int32)
    used = (tile_ids < tile_end[-1]).astype(jnp.int32)
    last = jnp.maximum(tile_end[-1] - 1, 0)
    tgrp = jnp.sum((jnp.minimum(tile_ids, last)[:, None] >= tile_end[None, :]).astype(jnp.int32), axis=1)
    step_e = jnp.arange(EXPERTS_PER_GROUP, dtype=jnp.int32)[None, :]
    widx = jnp.where(used[:, None] == 1, tgrp[:, None] * EXPERTS_PER_GROUP + step_e,
                     tgrp[:, None] * EXPERTS_PER_GROUP + EXPERTS_PER_GROUP - 1).reshape(-1)

    xs = _dispatch(slot, xa, n_tiles * TM_EXP)
    fs = _experts(widx, used, tgrp, xs, wg, wu, wd, g2, gf)
    return _combine(slot, fs, t)


def _split_w_in(w):
    sizes = [D_RNN, D_RNN, Q_DIM] + [KV_DIM] * 6 + [3 * N_HEADS, 2 * D_MODEL]
    pts = np.cumsum(sizes)[:-1]
    return jnp.split(w, [int(p) for p in pts], axis=-1)


def _proj_weights(w_in):
    xr, gr, q, kc, vc, ksl, vsl, kw, vw, nsa_g, merge_g = _split_w_in(w_in)
    ng = nsa_g.reshape(D_MODEL, 3, N_KV_GROUPS, HEADS_PER_GROUP).transpose(0, 2, 1, 3)
    ng = ng.reshape(D_MODEL, N_KV_GROUPS, 3 * HEADS_PER_GROUP)
    ng = jnp.pad(ng, ((0, 0), (0, 0), (0, LANES - 3 * HEADS_PER_GROUP))).reshape(D_MODEL, N_KV_GROUPS * LANES)
    w_attn = jnp.concatenate([q, kc, vc, ksl, vsl, kw, vw, ng], axis=1)
    return (jnp.concatenate([xr, gr], axis=1).astype(BF16), w_attn.astype(BF16), merge_g.astype(BF16))


def _rope_tables(pos, width):
    inv_freq = ROPE_THETA ** (-(jnp.arange(0, HEAD_DIM, 2, dtype=F32) / HEAD_DIM))
    ang = pos.astype(F32)[:, None] * inv_freq[None, :]
    cos, sin = jnp.cos(ang), jnp.sin(ang)
    reps = width // HEAD_DIM
    return jnp.tile(jnp.concatenate([cos, cos], axis=1), (1, reps)), jnp.tile(jnp.concatenate([-sin, sin], axis=1), (1, reps))


def _block_diag(w, per):
    nb, d = w.shape[0], w.shape[1]
    w = w.reshape(nb // per, per, d, d)
    eye = jnp.eye(per, dtype=w.dtype)
    return jnp.einsum('npij,pq->npiqj', w, eye).reshape(nb // per, per * d, per * d)


def _cmp_to_sel():
    n_sel_pad = LANES
    c0 = np.arange(N_CMP_PAD) * CMP_STRIDE
    s0 = np.arange(n_sel_pad) * SEL_BLOCK
    ov = np.minimum(c0[:, None] + CMP_BLOCK, s0[None, :] + SEL_BLOCK) - np.maximum(c0[:, None], s0[None, :])
    m = np.clip(ov, 0, None) / CMP_BLOCK
    m[:, SEL_BLOCK // 2:] = 0.0
    m[N_CMP_PAD - 1, :] = 0.0
    return m.astype(np.float32)


def kernel(x, norm1_g, w_in, conv_w, conv_b, lru_wa, lru_ba, lru_wi, lru_bi, lru_lambda, w_rnn_out, cmpk_pos, cmpk_w1, cmpk_b1, cmpk_w2, cmpk_b2, cmpv_pos, cmpv_w1, cmpv_b1, cmpv_w2, cmpv_b2, w_nsa_out, w_mix_out, norm2_g, router_group_w, router_group_b, router_expert_w, router_expert_b, expert_w_gate, expert_w_up, expert_w_down, final_norm_g):
    batch, seq, _ = x.shape
    t = batch * seq
    assert w_in.shape[0] == 1, "the final norm is fused into the expert kernel: single layer only"
    assert seq % max(TS_RNN, TQ, CK_SEL, TM_PROJ) == 0, "sequence tiles must divide the sequence length"
    assert t % max(TM_POST, TM_PLAN, TM_ROWS, TM_EXP) == 0, "token tiles must divide batch * seq"
    x2 = x.reshape(t, D_MODEL)

    cos, sin = _rope_tables(jnp.arange(seq), LANES)
    cmp_ends = jnp.arange(N_CMP_PAD) * CMP_STRIDE + (CMP_BLOCK - 1)
    ccos, csin = _rope_tables(cmp_ends, LANES)
    reps = TR_CMP // N_CMP_PAD
    cmp_cos = jnp.stack([jnp.tile(ccos, (reps, 1)), jnp.ones((TR_CMP, LANES), F32)])
    cmp_sin = jnp.stack([jnp.tile(csin, (reps, 1)), jnp.zeros((TR_CMP, LANES), F32)])
    key_blk = np.arange(seq)[:, None] // SEL_BLOCK
    e_mat = jnp.asarray((key_blk == np.arange(LANES)[None, :]).astype(np.float32), dtype=BF16)
    cs_mat = jnp.asarray(_cmp_to_sel(), dtype=BF16)

    l = 0
    w_xg, w_attn, w_mg = _proj_weights(w_in[l])
    g1 = norm1_g[l][None, :]
    q, kcv, ksd, vsd, kwd, vwd, ng = _inproj(x2, g1, w_attn, cos, sin, seq)

    y_a = _rnn(x2, g1, w_xg, conv_w[l], conv_b[l][None, :],
               _block_diag(lru_wa[l], 4).astype(BF16), lru_ba[l][None, :],
               _block_diag(lru_wi[l], 4).astype(BF16), lru_bi[l][None, :],
               lru_lambda[l][None, :], w_rnn_out[l].astype(BF16), batch, seq)

    half = CMP_STRIDE * HEAD_DIM
    pos = jnp.stack([cmpk_pos[l].reshape(2, half), cmpv_pos[l].reshape(2, half)])
    w1 = jnp.stack([cmpk_w1[l], cmpv_w1[l]]).astype(BF16)
    b1 = jnp.stack([cmpk_b1[l], cmpv_b1[l]])[:, None, :]
    w2 = jnp.stack([cmpk_w2[l], cmpv_w2[l]])
    w2 = jnp.concatenate([w2, w2], axis=2).astype(BF16)
    b2 = jnp.stack([cmpk_b2[l], cmpv_b2[l]])
    b2 = jnp.concatenate([b2, b2], axis=1)[:, None, :]
    kvc = _compress(kcv, pos, w1, b1, w2, b2, cmp_cos, cmp_sin, batch, seq)

    o_nsa = _attn(q, kvc, ksd, vsd, kwd, vwd, ng, e_mat, cs_mat, batch, seq)

    wr = jnp.concatenate([
        router_expert_w[l].transpose(1, 0, 2).reshape(D_MODEL, N_EXPERTS),
        router_group_w[l],
        jnp.zeros((D_MODEL, LANES - N_EXPERTS - N_GROUPS), F32)], axis=1)
    wr_hi = wr.astype(BF16)
    wr_lo = (wr - wr_hi.astype(F32)).astype(BF16)
    rb = jnp.concatenate([router_expert_b[l].reshape(N_EXPERTS), router_group_b[l],
                          jnp.zeros((LANES - N_EXPERTS - N_GROUPS,), F32)])[None, :]
    xa = _post(o_nsa, y_a, x2, g1, w_mg, w_nsa_out[l].astype(BF16), w_mix_out[l].astype(BF16),
               norm2_g[l][None, :], jnp.concatenate([wr_hi, wr_lo], axis=1), rb)

    out = _moe_sorted(xa, expert_w_gate[l], expert_w_up[l], expert_w_down[l],
                      norm2_g[l][None, :], final_norm_g[None, :])
    return out.reshape(batch, seq, D_MODEL)
```

```python
import functools

import numpy as np
import jax
import jax.numpy as jnp
from jax import lax
from jax.experimental import pallas as pl
from jax.experimental.pallas import tpu as pltpu

D_MODEL = 1024
D_RNN = 1024
RNN_BLOCKS = 16
RNN_BLOCK_DIM = D_RNN // RNN_BLOCKS
CONV_WIDTH = 4
LRU_C = 8.0
N_HEADS = 16
HEAD_DIM = 64
HALF_DIM = HEAD_DIM // 2
N_KV_GROUPS = 4
HEADS_PER_GROUP = N_HEADS // N_KV_GROUPS
Q_DIM = N_HEADS * HEAD_DIM
KV_DIM = N_KV_GROUPS * HEAD_DIM
CMP_BLOCK = 32
CMP_STRIDE = 16
CMP_HIDDEN = 256
SEL_BLOCK = 64
SEL_TOPN = 16
WINDOW = 512
ROPE_THETA = 10000.0
FORCE_BONUS = 1e4
NEG_INF = -1e30
N_GROUPS = 4
EXPERTS_PER_GROUP = 4
N_EXPERTS = N_GROUPS * EXPERTS_PER_GROUP
D_EXPERT = 512
EPS = 1e-6

Q_SCALE = HEAD_DIM ** -0.5 * float(np.log2(np.e))

LANES = 128
SUBLANES = 8
VMEM_LIMIT = 56 * 1024 * 1024

BF16 = jnp.bfloat16
F32 = jnp.float32

COL_Q = 0
COL_KCV = COL_Q + Q_DIM
COL_KS = COL_KCV + 2 * KV_DIM
COL_VS = COL_KS + KV_DIM
COL_KW = COL_VS + KV_DIM
COL_VW = COL_KW + KV_DIM
COL_NG = COL_VW + KV_DIM
N_PROJ = COL_NG + N_KV_GROUPS * LANES

TM_PROJ = 1024
TS_RNN = 1024
TR_CMP = 256
TQ = 256
TQ_SELECT_TILE = 256
TK = 128
CK_SEL = 512
TM_POST = 1024
TM_PLAN = 1024
TM_ROWS = 4096
TM_EXP = 1024
ROW_COPY_UNROLL = 64
D_AUG = D_MODEL + LANES
GROUP_LANE = N_EXPERTS
N_CMP_PAD = 128


def _dot(a, b):
    return jnp.dot(a, b, preferred_element_type=F32)


def _dot_t(a, b):
    return lax.dot_general(a, b, (((1,), (1,)), ((), ())), preferred_element_type=F32)


def _lane_iota(shape):
    return lax.broadcasted_iota(jnp.int32, shape, len(shape) - 1)


def _row_iota(shape):
    return lax.broadcasted_iota(jnp.int32, shape, 0)


def _rope(x, cos, sin_signed):
    width = x.shape[-1]
    reps = width // cos.shape[-1]
    if reps > 1:
        cos = jnp.concatenate([cos] * reps, axis=1)
        sin_signed = jnp.concatenate([sin_signed] * reps, axis=1)
    first_half = (_lane_iota(x.shape) & (HEAD_DIM - 1)) < HALF_DIM
    partner = jnp.where(first_half, pltpu.roll(x, width - HALF_DIM, 1), pltpu.roll(x, HALF_DIM, 1))
    return x * cos + partner * sin_signed


def _spread_heads(x, fill=None):
    out = []
    low = _lane_iota((x.shape[0], LANES)) < HEAD_DIM
    for c in range(x.shape[1] // LANES):
        xc = x[:, c * LANES:(c + 1) * LANES]
        rolled = pltpu.roll(xc, HEAD_DIM, 1)
        out.append(jnp.where(low, xc, rolled if fill is None else fill))
        out.append(jnp.where(low, rolled, xc if fill is None else fill))
    return jnp.concatenate(out, axis=1)


def _sigmoid(x):
    return 0.5 * (jnp.tanh(0.5 * x) + 1.0)


def _rms_scale(x):
    sq = x * x
    part = sq[:, 0:LANES]
    for c in range(1, x.shape[1] // LANES):
        part = part + sq[:, c * LANES:(c + 1) * LANES]
    return lax.rsqrt(jnp.sum(part, axis=1, keepdims=True) * (1.0 / x.shape[1]) + EPS)


def _normed(x, g):
    return (x * _rms_scale(x) * g).astype(BF16)


def _inproj_kernel(x_ref, g_ref, w_ref, cos_ref, sin_ref,
                   q_ref, kcv_ref, ks_ref, vs_ref, kw_ref, vw_ref, ng_ref):
    h = _normed(x_ref[...], g_ref[...])
    cos = cos_ref[...]
    sin = sin_ref[...]

    def mm(lo, width):
        return _dot(h, w_ref[:, lo:lo + width])

    q_ref[...] = (_rope(mm(COL_Q, Q_DIM), cos, sin) * Q_SCALE).astype(BF16)
    kcv_ref[...] = mm(COL_KCV, 2 * KV_DIM)
    ks_ref[...] = _spread_heads(_rope(mm(COL_KS, KV_DIM), cos, sin)).astype(BF16)
    vs_ref[...] = _spread_heads(mm(COL_VS, KV_DIM), 1.0).astype(BF16)
    kw_ref[...] = _spread_heads(_rope(mm(COL_KW, KV_DIM), cos, sin)).astype(BF16)
    vw_ref[...] = _spread_heads(mm(COL_VW, KV_DIM), 1.0).astype(BF16)
    ng_ref[...] = jax.nn.sigmoid(mm(COL_NG, N_KV_GROUPS * LANES))


def _inproj(x2, norm_g, w_proj, cos, sin, seq):
    t = x2.shape[0]
    tm = TM_PROJ
    pos_blocks = seq // tm
    row = lambda i: (i, 0)
    const = lambda i: (0, 0)
    out_shape = (
        jax.ShapeDtypeStruct((t, Q_DIM), BF16),
        jax.ShapeDtypeStruct((t, 2 * KV_DIM), F32),
        jax.ShapeDtypeStruct((t, N_KV_GROUPS * LANES), BF16),
        jax.ShapeDtypeStruct((t, N_KV_GROUPS * LANES), BF16),
        jax.ShapeDtypeStruct((t, N_KV_GROUPS * LANES), BF16),
        jax.ShapeDtypeStruct((t, N_KV_GROUPS * LANES), BF16),
        jax.ShapeDtypeStruct((t, N_KV_GROUPS * LANES), F32),
    )
    return pl.pallas_call(
        _inproj_kernel,
        grid=(t // tm,),
        in_specs=[
            pl.BlockSpec((tm, D_MODEL), row),
            pl.BlockSpec((1, D_MODEL), const),
            pl.BlockSpec((D_MODEL, N_PROJ), const, pipeline_mode=pl.Buffered(1)),
            pl.BlockSpec((tm, LANES), lambda i: (i % pos_blocks, 0)),
            pl.BlockSpec((tm, LANES), lambda i: (i % pos_blocks, 0)),
        ],
        out_specs=tuple(pl.BlockSpec((tm, s.shape[1]), row) for s in out_shape),
        out_shape=out_shape,
        compiler_params=pltpu.CompilerParams(
            dimension_semantics=("parallel",), vmem_limit_bytes=VMEM_LIMIT),
        name="inproj",
    )(x2, norm_g, w_proj, cos, sin)


def _rnn_kernel(xin_ref, g1_ref, wx_ref, cw_ref, cb_ref, wa_ref, ba_ref, wi_ref, bi_ref, lam_ref, wo_ref,
                y_ref, tail_s, carry_s, a_s, u_s, h_s):
    ts = xin_ref.shape[0]

    @pl.when(pl.program_id(1) == 0)
    def _():
        tail_s[...] = jnp.zeros_like(tail_s)
        carry_s[...] = jnp.zeros_like(carry_s)

    hn = _normed(xin_ref[...], g1_ref[...])
    x = _dot(hn, wx_ref[:, 0:D_RNN])
    gate_pre = _dot(hn, wx_ref[:, D_RNN:2 * D_RNN])
    xext = jnp.concatenate([tail_s[...], x], axis=0)
    tail_s[...] = x[ts - SUBLANES:ts, :]
    conv = cb_ref[...]
    for k in range(CONV_WIDTH):
        back = CONV_WIDTH - 1 - k
        shifted = xext if back == 0 else pltpu.roll(xext, back, 0)
        conv = conv + cw_ref[k:k + 1, :] * shifted[SUBLANES:SUBLANES + ts, :]

    cb16 = conv.astype(BF16)
    blk = wa_ref.shape[1]
    r_pre = jnp.concatenate(
        [_dot(cb16[:, j * blk:(j + 1) * blk], wa_ref[j]) for j in range(D_RNN // blk)], axis=1)
    i_pre = jnp.concatenate(
        [_dot(cb16[:, j * blk:(j + 1) * blk], wi_ref[j]) for j in range(D_RNN // blk)], axis=1)
    r = _sigmoid(r_pre + ba_ref[...])
    gate_i = _sigmoid(i_pre + bi_ref[...])
    neg_lam = -lam_ref[...]
    softplus = jnp.maximum(neg_lam, 0.0) + jnp.log1p(jnp.exp(-jnp.abs(neg_lam)))
    log_a = r * ((-LRU_C) * softplus)
    a = jnp.exp(log_a)
    a_s[...] = a
    var = -jnp.tanh(log_a) * (a * a + 1.0)
    u_s[...] = jnp.where(var > 0.0, var * lax.rsqrt(var), 0.0) * (gate_i * conv)

    row = _row_iota((SUBLANES, D_RNN))

    def body(k, carry):
        off = pl.multiple_of(k * SUBLANES, SUBLANES)
        a = a_s[pl.ds(off, SUBLANES), :]
        b = u_s[pl.ds(off, SUBLANES), :]
        for sh in (1, 2, 4):
            keep = row >= sh
            a_prev = jnp.where(keep, pltpu.roll(a, sh, 0), 1.0)
            b_prev = jnp.where(keep, pltpu.roll(b, sh, 0), 0.0)
            b = a * b_prev + b
            a = a * a_prev
        h = a * carry + b
        h_s[pl.ds(off, SUBLANES), :] = h
        return jnp.broadcast_to(h[SUBLANES - 1:SUBLANES, :], (SUBLANES, D_RNN))

    carry_s[...] = lax.fori_loop(0, ts // SUBLANES, body, carry_s[...], unroll=8)
    gated = (jax.nn.gelu(gate_pre) * h_s[...]).astype(BF16)
    y_ref[...] = _dot(gated, wo_ref[...])


def _rnn(x2, norm_g, w_xg, conv_w, conv_b, wa_bd, ba, wi_bd, bi, lam, w_out, batch, seq):
    t = x2.shape[0]
    ts = TS_RNN
    nt = seq // ts
    const2 = lambda b, s: (0, 0)
    const3 = lambda b, s: (0, 0, 0)
    nblk, blk = wa_bd.shape[0], wa_bd.shape[1]
    return pl.pallas_call(
        _rnn_kernel,
        grid=(batch, nt),
        in_specs=[
            pl.BlockSpec((ts, D_MODEL), lambda b, s: (b * nt + s, 0)),
            pl.BlockSpec((1, D_MODEL), const2),
            pl.BlockSpec((D_MODEL, 2 * D_RNN), const2),
            pl.BlockSpec((CONV_WIDTH, D_RNN), const2),
            pl.BlockSpec((1, D_RNN), const2),
            pl.BlockSpec((nblk, blk, blk), const3),
            pl.BlockSpec((1, D_RNN), const2),
            pl.BlockSpec((nblk, blk, blk), const3),
            pl.BlockSpec((1, D_RNN), const2),
            pl.BlockSpec((1, D_RNN), const2),
            pl.BlockSpec((D_RNN, D_MODEL), const2),
        ],
        out_specs=pl.BlockSpec((ts, D_MODEL), lambda b, s: (b * nt + s, 0)),
        out_shape=jax.ShapeDtypeStruct((t, D_MODEL), F32),
        scratch_shapes=[
            pltpu.VMEM((SUBLANES, D_RNN), F32),
            pltpu.VMEM((SUBLANES, D_RNN), F32),
            pltpu.VMEM((ts, D_RNN), F32),
            pltpu.VMEM((ts, D_RNN), F32),
            pltpu.VMEM((ts, D_RNN), F32),
        ],
        compiler_params=pltpu.CompilerParams(
            dimension_semantics=("parallel", "arbitrary"), vmem_limit_bytes=VMEM_LIMIT),
        name="rnn",
    )(x2, norm_g, w_xg, conv_w, conv_b, wa_bd, ba, wi_bd, bi, lam, w_out)


def _compress_kernel(x_ref, pos_ref, w1_ref, b1_ref, w2_ref, b2_ref, cos_ref, sin_ref, o_ref):
    n_rows = x_ref.shape[0] // CMP_STRIDE
    strided = [x_ref[pl.ds(r, n_rows, stride=CMP_STRIDE), :] for r in range(CMP_STRIDE)]
    low = _lane_iota((n_rows, LANES)) < HEAD_DIM
    groups = []
    for g in range(2):
        tiles = []
        for j in range(CMP_STRIDE // 2):
            even = strided[2 * j]
            odd = strided[2 * j + 1]
            if g == 0:
                tiles.append(jnp.where(low, even, pltpu.roll(odd, HEAD_DIM, 1)))
            else:
                tiles.append(jnp.where(low, pltpu.roll(even, HEAD_DIM, 1), odd))
        groups.append(jnp.concatenate(tiles, axis=1))
    x = jnp.concatenate(groups, axis=0)
    tr = x.shape[0]
    half = x.shape[1]
    first = _dot((x + pos_ref[0, 0:1, :]).astype(BF16), w1_ref[0, 0:half, :])
    second = _dot((x + pos_ref[0, 1:2, :]).astype(BF16), w1_ref[0, half:2 * half, :])
    hid = jax.nn.gelu(first + pltpu.roll(second, tr - 1, 0) + b1_ref[0])
    out = _dot(hid.astype(BF16), w2_ref[0]) + b2_ref[0]
    o_ref[0] = _rope(out, cos_ref[0], sin_ref[0]).astype(BF16)


def _compress(kcv, pos, w1, b1, w2, b2, cos, sin, batch, seq):
    pairs = KV_DIM // LANES
    tr = TR_CMP
    rows = batch * pairs * tr
    sel = lambda k, r: (k, 0, 0)
    return pl.pallas_call(
        _compress_kernel,
        grid=(2, batch * pairs),
        in_specs=[
            pl.BlockSpec((seq, LANES), lambda k, r: (r // pairs, k * pairs + r % pairs)),
            pl.BlockSpec((1, 2, pos.shape[2]), sel),
            pl.BlockSpec((1,) + w1.shape[1:], sel),
            pl.BlockSpec((1, 1, CMP_HIDDEN), sel),
            pl.BlockSpec((1,) + w2.shape[1:], sel),
            pl.BlockSpec((1, 1, LANES), sel),
            pl.BlockSpec((1, tr, LANES), sel),
            pl.BlockSpec((1, tr, LANES), sel),
        ],
        out_specs=pl.BlockSpec((1, tr, LANES), lambda k, r: (k, r, 0)),
        out_shape=jax.ShapeDtypeStruct((2, rows, LANES), BF16),
        compiler_params=pltpu.CompilerParams(
            dimension_semantics=("parallel", "parallel"), vmem_limit_bytes=VMEM_LIMIT),
        name="compress",
    )(kcv, pos, w1, b1, w2, b2, cos, sin)


def _lane_tile_max(s):
    tiles = [s[:, c * LANES:(c + 1) * LANES] for c in range(s.shape[1] // LANES)]
    while len(tiles) > 1:
        tiles = [jnp.maximum(tiles[k], tiles[k + 1]) if k + 1 < len(tiles) else tiles[k]
                 for k in range(0, len(tiles), 2)]
    return tiles[0]


def _stack_heads(q):
    tq = q.shape[0]
    low = _lane_iota((tq, LANES)) < HEAD_DIM
    zero = jnp.zeros((tq, LANES), BF16)
    heads = []
    for hh in range(HEADS_PER_GROUP):
        pair = q[:, (hh // 2) * LANES:(hh // 2 + 1) * LANES]
        heads.append(jnp.where(low if hh % 2 == 0 else jnp.logical_not(low), pair, zero))
    return jnp.concatenate(heads, axis=0)


def _select_tile(i, q, kc_ref, vc_ref, ng, m_ref):
    tq = q.shape[0]
    rows = HEADS_PER_GROUP * tq
    low = _lane_iota((tq, LANES)) < HEAD_DIM
    q4 = _stack_heads(q)
    lane4 = _lane_iota((rows, LANES))
    qpos4 = i * tq + (_row_iota((rows, LANES)) & (tq - 1))

    sc = _dot_t(q4, kc_ref[0])
    sc = jnp.where(lane4 * CMP_STRIDE + (CMP_BLOCK - 1) <= qpos4, sc, NEG_INF)
    mc = jnp.max(sc, axis=1, keepdims=True)
    pc = jnp.exp2(sc - mc)
    pc = pc / jnp.sum(pc, axis=1, keepdims=True)
    pc = jnp.where(qpos4 >= CMP_BLOCK - 1, pc, 0.0)
    o_cmp = _dot(pc.astype(BF16), vc_ref[0])

    psum = pc[0:tq] + pc[tq:2 * tq] + pc[2 * tq:3 * tq] + pc[3 * tq:4 * tq]
    p_hi = psum.astype(BF16)
    rem = psum - p_hi.astype(F32)
    p_mid = rem.astype(BF16)
    p_lo = (rem - p_mid.astype(F32)).astype(BF16)
    cs = m_ref[...]
    p_slc = _dot(p_hi, cs) + _dot(p_mid, cs) + _dot(p_lo, cs)

    n_sel = SEL_BLOCK // 2
    p_slc_t = p_slc.T[0:n_sel, :]
    blk = _row_iota((n_sel, tq))
    tblk = (i * tq + _lane_iota((n_sel, tq))) >> 6
    forced = (blk == 0) | (blk == tblk) | (blk == tblk - 1)
    score = jnp.where(blk <= tblk, p_slc_t + jnp.where(forced, FORCE_BONUS, 0.0), -1.0)
    rank = jnp.zeros((n_sel, tq), F32)
    for j in range(n_sel):
        sj = score[j:j + 1, :]
        beats = (sj > score) | ((sj == score) & (blk > j))
        rank = rank + jnp.where(beats, 1.0, 0.0)
    bias_t = jnp.where(rank < SEL_TOPN, 0.0, NEG_INF)
    bias = jnp.concatenate([bias_t, jnp.zeros((LANES - n_sel, tq), F32)], axis=0).T.astype(BF16)

    gated = [ng[:, hh:hh + 1] * o_cmp[hh * tq:(hh + 1) * tq] for hh in range(HEADS_PER_GROUP)]
    o_pairs = jnp.concatenate([jnp.where(low, gated[0], gated[1]), jnp.where(low, gated[2], gated[3])], axis=1)
    return bias, o_pairs


def _attn_kernel(*refs):
    n_full = pl.program_id(2) // (CK_SEL // TQ)
    for n_chunks in range(1, refs[3].shape[0] // CK_SEL + 1):
        pl.when(n_full == n_chunks - 1)(functools.partial(_attn_step, n_chunks, *refs))


def _attn_step(n_chunks, q_ref, kc_ref, vc_ref, ks_ref, vs_ref, kw_ref, vw_ref, ng_ref, e_ref, m_ref, o_ref):
    i = pl.program_id(2)
    tq = q_ref.shape[0]
    rows = HEADS_PER_GROUP * tq
    ck = CK_SEL
    low = _lane_iota((tq, LANES)) < HEAD_DIM
    q4 = _stack_heads(q_ref[...])

    st = TQ_SELECT_TILE
    picks = [_select_tile(i * (tq // st) + sub, q_ref[sub * st:(sub + 1) * st, :], kc_ref, vc_ref,
                          ng_ref[sub * st:(sub + 1) * st, :], m_ref) for sub in range(tq // st)]
    bias = jnp.concatenate([p[0] for p in picks], axis=0)
    o_cmp = jnp.concatenate([p[1] for p in picks], axis=0)

    n_sub = tq // TK
    back = WINDOW // TK
    k_tiles, v_tiles = [], []
    for jj in range(back + n_sub):
        j = i * n_sub - back + jj
        off = pl.multiple_of(jnp.maximum(j, 0) * TK, TK)
        k_tiles.append(kw_ref[pl.ds(off, TK), :])
        v_tiles.append(vw_ref[pl.ds(off, TK), :])
    sw = _dot_t(q4, jnp.concatenate(k_tiles, axis=0))
    on_or_after = _row_iota((TK, TK)) >= _lane_iota((TK, TK))
    sw_tiles = []
    for jj in range(back + n_sub):
        blocks = []
        for blk_i in range(rows // TK):
            piece = sw[blk_i * TK:(blk_i + 1) * TK, jj * TK:(jj + 1) * TK]
            tiles_back = blk_i % n_sub + back - jj
            if tiles_back == 0:
                piece = jnp.where(on_or_after, piece, NEG_INF)
            elif tiles_back == back:
                piece = jnp.where(on_or_after, NEG_INF, piece)
            elif tiles_back < 0 or tiles_back > back:
                piece = jnp.full((TK, TK), NEG_INF, F32)
            blocks.append(piece)
        tile = jnp.concatenate(blocks, axis=0)
        if jj < back:
            tile = jnp.where(i * n_sub - back + jj >= 0, tile, NEG_INF)
        sw_tiles.append(tile)
    sw = jnp.concatenate(sw_tiles, axis=1)
    m_win = jnp.max(_lane_tile_max(sw), axis=1, keepdims=True)
    acc_win = _dot(jnp.exp2(sw - m_win).astype(BF16), jnp.concatenate(v_tiles, axis=0))

    qa = jnp.concatenate([q4, jnp.concatenate([bias] * HEADS_PER_GROUP, axis=0)], axis=1)
    rel_q = i * tq - (n_chunks - 1) * ck + (_row_iota((rows, ck)) & (tq - 1))
    m_run = None
    acc_sel = None
    for c in range(n_chunks):
        ka = jnp.concatenate([ks_ref[c * ck:(c + 1) * ck, :], e_ref[c * ck:(c + 1) * ck, :]], axis=1)
        s = _dot_t(qa, ka)
        if c == n_chunks - 1:
            s = jnp.where(_lane_iota((rows, ck)) <= rel_q, s, NEG_INF)
        m_new = jnp.max(_lane_tile_max(s), axis=1, keepdims=True)
        if c > 0:
            m_new = jnp.maximum(m_run, m_new)
            acc_sel = acc_sel * jnp.exp2(m_run - m_new)
        part = _dot(jnp.exp2(s - m_new).astype(BF16), vs_ref[c * ck:(c + 1) * ck, :])
        acc_sel = part if c == 0 else acc_sel + part
        m_run = m_new

    ng = ng_ref[...]
    outs = []
    for hh in range(HEADS_PER_GROUP):
        r0 = hh * tq
        col = lambda br: ng[:, br * HEADS_PER_GROUP + hh:br * HEADS_PER_GROUP + hh + 1]
        parts = []
        for acc in (acc_sel[r0:r0 + tq], acc_win[r0:r0 + tq]):
            swapped = pltpu.roll(acc, HEAD_DIM, 1)
            parts.append(acc / swapped if hh % 2 == 0 else swapped / acc)
        outs.append(col(1) * parts[0] + col(2) * parts[1])
    o_ref[...] = (o_cmp + jnp.concatenate(
        [jnp.where(low, outs[0], outs[1]), jnp.where(low, outs[2], outs[3])], axis=1)).astype(o_ref.dtype)


def _attn(q, kvc, ksd, vsd, kwd, vwd, ng, e_mat, cs_mat, batch, seq):
    t = q.shape[0]
    tq = TQ
    nq = seq // tq
    qrow = lambda b, g, i: (b * nq + i, g)
    kv = lambda b, g, i: (b, g)
    const = lambda b, g, i: (0, 0)
    return pl.pallas_call(
        _attn_kernel,
        grid=(batch, N_KV_GROUPS, nq),
        in_specs=[
            pl.BlockSpec((tq, HEADS_PER_GROUP * HEAD_DIM), qrow),
            pl.BlockSpec((1, N_CMP_PAD, LANES), lambda b, g, i: (0, b * N_KV_GROUPS + g, 0)),
            pl.BlockSpec((1, N_CMP_PAD, LANES), lambda b, g, i: (1, b * N_KV_GROUPS + g, 0)),
            pl.BlockSpec((seq, LANES), kv),
            pl.BlockSpec((seq, LANES), kv),
            pl.BlockSpec((seq, LANES), kv),
            pl.BlockSpec((seq, LANES), kv),
            pl.BlockSpec((tq, LANES), qrow),
            pl.BlockSpec((seq, LANES), const),
            pl.BlockSpec((N_CMP_PAD, LANES), const),
        ],
        out_specs=pl.BlockSpec((tq, HEADS_PER_GROUP * HEAD_DIM), qrow),
        out_shape=jax.ShapeDtypeStruct((t, Q_DIM), BF16),
        compiler_params=pltpu.CompilerParams(
            dimension_semantics=("parallel", "parallel", "arbitrary"), vmem_limit_bytes=VMEM_LIMIT),
        name="attn",
    )(q, kvc, kvc, ksd, vsd, kwd, vwd, ng, e_mat, cs_mat)


def _post_kernel(o_ref, ya_ref, x_ref, g1_ref, wg_ref, wn_ref, wm_ref, g2_ref, wr_ref, rb_ref,
                 xa_ref):
    hn = _normed(x_ref[...], g1_ref[...])
    gate_a = jax.nn.sigmoid(_dot(hn, wg_ref[:, 0:D_MODEL]))
    gate_b = jax.nn.sigmoid(_dot(hn, wg_ref[:, D_MODEL:2 * D_MODEL]))
    y_b = _dot(o_ref[...], wn_ref[...])
    mixed = gate_a * ya_ref[...] + gate_b * y_b
    x1 = x_ref[...] + _dot(mixed.astype(BF16), wm_ref[...])
    xa_ref[:, 0:D_MODEL] = x1
    h2 = x1 * _rms_scale(x1) * g2_ref[...]
    h_hi = h2.astype(BF16)
    h_lo = (h2 - h_hi.astype(F32)).astype(BF16)
    both = _dot(h_hi, wr_ref[...])
    logits = both[:, 0:LANES] + both[:, LANES:2 * LANES] + _dot(h_lo, wr_ref[:, 0:LANES]) + rb_ref[...]

    lane = _lane_iota(logits.shape).astype(F32)
    is_grp = (lane >= N_EXPERTS) & (lane < N_EXPERTS + N_GROUPS)
    gl = jnp.where(is_grp, logits, NEG_INF)
    ge = jnp.exp(gl - jnp.max(gl, axis=1, keepdims=True))
    gp = ge / jnp.sum(ge, axis=1, keepdims=True)
    g_w = jnp.max(gp, axis=1, keepdims=True)
    big = float(4 * LANES)
    g_first = jnp.min(jnp.where(is_grp & (gp == g_w), lane, big), axis=1, keepdims=True)
    grp_lo = (g_first - N_EXPERTS) * EXPERTS_PER_GROUP

    in_grp = (lane >= grp_lo) & (lane < grp_lo + EXPERTS_PER_GROUP)
    el = jnp.where(in_grp, logits, NEG_INF)
    ee = jnp.exp(el - jnp.max(el, axis=1, keepdims=True))
    ep = ee / jnp.sum(ee, axis=1, keepdims=True)
    w1 = jnp.max(ep, axis=1, keepdims=True)
    i1 = jnp.min(jnp.where(in_grp & (ep == w1), lane, big), axis=1, keepdims=True)
    rest = jnp.where(in_grp & (lane != i1), ep, -1.0)
    w2 = jnp.max(rest, axis=1, keepdims=True)
    i2 = jnp.min(jnp.where(rest == w2, lane, big), axis=1, keepdims=True)
    den = w1 + w2
    comb = jnp.where(lane == i1, g_w * (w1 / den), jnp.where(lane == i2, g_w * (w2 / den), 0.0))
    xa_ref[:, D_MODEL:D_MODEL + LANES] = jnp.where(lane == GROUP_LANE, g_first - N_EXPERTS, comb)


def _post(o_nsa, y_a, x2, g1, w_mg, w_nsa, w_mix, g2, wr, rb):
    t = x2.shape[0]
    tm = TM_POST
    row = lambda i: (i, 0)
    const = lambda i: (0, 0)
    return pl.pallas_call(
        _post_kernel,
        grid=(t // tm,),
        in_specs=[
            pl.BlockSpec((tm, Q_DIM), row),
            pl.BlockSpec((tm, D_MODEL), row),
            pl.BlockSpec((tm, D_MODEL), row),
            pl.BlockSpec((1, D_MODEL), const),
            pl.BlockSpec((D_MODEL, 2 * D_MODEL), const),
            pl.BlockSpec((Q_DIM, D_MODEL), const),
            pl.BlockSpec((D_MODEL, D_MODEL), const),
            pl.BlockSpec((1, D_MODEL), const),
            pl.BlockSpec((D_MODEL, 2 * LANES), const),
            pl.BlockSpec((1, LANES), const),
        ],
        out_specs=pl.BlockSpec((tm, D_AUG), row),
        out_shape=jax.ShapeDtypeStruct((t, D_AUG), F32),
        compiler_params=pltpu.CompilerParams(
            dimension_semantics=("parallel",), vmem_limit_bytes=VMEM_LIMIT),
        name="post",
    )(o_nsa, y_a, x2, g1, w_mg, w_nsa, w_mix, g2, wr, rb)


def _plan_kernel(rec_ref, tri_ref, info_ref, counts_ref, carry_s):
    @pl.when(pl.program_id(0) == 0)
    def _():
        carry_s[...] = jnp.zeros_like(carry_s)

    rec = rec_ref[...]
    lane = _lane_iota(rec.shape)
    gid = rec[:, GROUP_LANE:GROUP_LANE + 1]
    onehot = jnp.where(lane.astype(F32) == gid, 1.0, 0.0)
    before = _dot(tri_ref[...], onehot.astype(BF16)) + carry_s[0:1, :]
    rank = jnp.sum(onehot * before, axis=1, keepdims=True)
    info_ref[...] = jnp.where(lane == 0, rank, jnp.where(lane == 1, gid, 0.0))
    carry_s[...] = carry_s[...] + jnp.sum(onehot, axis=0, keepdims=True)
    counts_ref[...] = carry_s[...]


def _plan(xa, tri):
    t = xa.shape[0]
    tm = TM_PLAN
    return pl.pallas_call(
        _plan_kernel,
        grid=(t // tm,),
        in_specs=[
            pl.BlockSpec((tm, LANES), lambda i: (i, D_MODEL // LANES)),
            pl.BlockSpec((tm, tm), lambda i: (0, 0)),
        ],
        out_specs=(
            pl.BlockSpec((tm, LANES), lambda i: (i, 0)),
            pl.BlockSpec((SUBLANES, LANES), lambda i: (0, 0)),
        ),
        out_shape=(
            jax.ShapeDtypeStruct((t, LANES), F32),
            jax.ShapeDtypeStruct((SUBLANES, LANES), F32),
        ),
        scratch_shapes=[pltpu.VMEM((SUBLANES, LANES), F32)],
        compiler_params=pltpu.CompilerParams(
            dimension_semantics=("arbitrary",), vmem_limit_bytes=VMEM_LIMIT),
        name="plan",
    )(xa, tri)


def _row_copies(n_rows, make_copy):
    def start(g, carry):
        r0 = pl.multiple_of(g * ROW_COPY_UNROLL, ROW_COPY_UNROLL)
        for k in range(ROW_COPY_UNROLL):
            make_copy(r0, k).start()
        return carry

    def wait(g, carry):
        for _ in range(ROW_COPY_UNROLL):
            make_copy(0, 0).wait()
        return carry

    lax.fori_loop(0, n_rows // ROW_COPY_UNROLL, start, 0)
    lax.fori_loop(0, n_rows // ROW_COPY_UNROLL, wait, 0)


def _dispatch_kernel(slot_ref, xa_ref, zero_hbm, xs_hbm, sem):
    del zero_hbm
    first = pl.program_id(0) * xa_ref.shape[0]

    def copy(r0, k):
        slot = slot_ref[first + r0 + k]
        return pltpu.make_async_copy(xa_ref.at[pl.ds(r0 + k, 1), :], xs_hbm.at[pl.ds(slot, 1), :], sem)

    _row_copies(xa_ref.shape[0], copy)


def _dispatch(slot, xa, n_sorted):
    t = xa.shape[0]
    tm = TM_ROWS
    zeros = jnp.zeros((n_sorted, D_AUG), F32)
    return pl.pallas_call(
        _dispatch_kernel,
        grid_spec=pltpu.PrefetchScalarGridSpec(
            num_scalar_prefetch=1,
            grid=(t // tm,),
            in_specs=[
                pl.BlockSpec((tm, D_AUG), lambda i, *_: (i, 0)),
                pl.BlockSpec(memory_space=pl.ANY),
            ],
            out_specs=pl.BlockSpec(memory_space=pl.ANY),
            scratch_shapes=[pltpu.SemaphoreType.DMA],
        ),
        out_shape=jax.ShapeDtypeStruct((n_sorted, D_AUG), F32),
        input_output_aliases={2: 0},
        compiler_params=pltpu.CompilerParams(
            dimension_semantics=("arbitrary",), vmem_limit_bytes=VMEM_LIMIT),
        name="dispatch",
    )(slot, xa, zeros)


def _experts_kernel(widx_ref, used_ref, tgrp_ref, xs_ref, wg_ref, wu_ref, wd_ref, g2_ref, gf_ref,
                    fs_ref, h_s, y_s):
    del widx_ref
    j = pl.program_id(0)
    e = pl.program_id(1)
    used = used_ref[j] == 1

    @pl.when(used & (e == 0))
    def _():
        x1 = xs_ref[:, 0:D_MODEL]
        h_s[...] = _normed(x1, g2_ref[...])
        y_s[...] = jnp.zeros_like(y_s)

    @pl.when(used)
    def _():
        h = h_s[...]
        act = jax.nn.silu(_dot(h, wg_ref[0].astype(BF16))) * _dot(h, wu_ref[0].astype(BF16))
        y = _dot(act.astype(BF16), wd_ref[0].astype(BF16))
        rec = xs_ref[:, D_MODEL:D_AUG]
        col = tgrp_ref[j] * EXPERTS_PER_GROUP + e
        weight = jnp.sum(jnp.where(_lane_iota(rec.shape) == col, rec, 0.0), axis=1, keepdims=True)
        y_s[...] += weight * y

    @pl.when(used & (e == EXPERTS_PER_GROUP - 1))
    def _():
        x = xs_ref[:, 0:D_MODEL] + y_s[...]
        fs_ref[...] = x * _rms_scale(x) * gf_ref[...]

    @pl.when(jnp.logical_not(used) & (e == EXPERTS_PER_GROUP - 1))
    def _():
        fs_ref[...] = jnp.zeros_like(fs_ref)


def _experts(widx, used, tgrp, xs, wg, wu, wd, g2, gf):
    n_sorted = xs.shape[0]
    tm = TM_EXP
    row = lambda j, e, *_: (j, 0)
    wsel = lambda j, e, widx, used, tgrp: (widx[j * EXPERTS_PER_GROUP + e], 0, 0)
    const = lambda j, e, *_: (0, 0)
    return pl.pallas_call(
        _experts_kernel,
        grid_spec=pltpu.PrefetchScalarGridSpec(
            num_scalar_prefetch=3,
            grid=(n_sorted // tm, EXPERTS_PER_GROUP),
            in_specs=[
                pl.BlockSpec((tm, D_AUG), row),
                pl.BlockSpec((1, D_MODEL, D_EXPERT), wsel),
                pl.BlockSpec((1, D_MODEL, D_EXPERT), wsel),
                pl.BlockSpec((1, D_EXPERT, D_MODEL), wsel),
                pl.BlockSpec((1, D_MODEL), const),
                pl.BlockSpec((1, D_MODEL), const),
            ],
            out_specs=pl.BlockSpec((tm, D_MODEL), row),
            scratch_shapes=[pltpu.VMEM((tm, D_MODEL), BF16), pltpu.VMEM((tm, D_MODEL), F32)],
        ),
        out_shape=jax.ShapeDtypeStruct((n_sorted, D_MODEL), F32),
        compiler_params=pltpu.CompilerParams(
            dimension_semantics=("arbitrary", "arbitrary"), vmem_limit_bytes=VMEM_LIMIT),
        name="experts",
    )(widx, used, tgrp, xs, wg, wu, wd, g2, gf)


def _combine_kernel(slot_ref, fs_hbm, o_ref, sem):
    first = pl.program_id(0) * o_ref.shape[0]

    def copy(r0, k):
        slot = slot_ref[first + r0 + k]
        return pltpu.make_async_copy(fs_hbm.at[pl.ds(slot, 1), :], o_ref.at[pl.ds(r0 + k, 1), :], sem)

    _row_copies(o_ref.shape[0], copy)


def _combine(slot, fs, t):
    tm = TM_ROWS
    return pl.pallas_call(
        _combine_kernel,
        grid_spec=pltpu.PrefetchScalarGridSpec(
            num_scalar_prefetch=1,
            grid=(t // tm,),
            in_specs=[pl.BlockSpec(memory_space=pl.ANY)],
            out_specs=pl.BlockSpec((tm, D_MODEL), lambda i, *_: (i, 0)),
            scratch_shapes=[pltpu.SemaphoreType.DMA],
        ),
        out_shape=jax.ShapeDtypeStruct((t, D_MODEL), F32),
        compiler_params=pltpu.CompilerParams(
            dimension_semantics=("arbitrary",), vmem_limit_bytes=VMEM_LIMIT),
        name="combine",
    )(slot, fs)


def _moe_sorted(xa, wg, wu, wd, g2, gf):
    t = xa.shape[0]
    tri = jnp.asarray(np.tril(np.ones((TM_PLAN, TM_PLAN), np.float32), -1), dtype=BF16)
    info, counts = _plan(xa, tri)
    rank = info[:, 0].astype(jnp.int32)
    gid = info[:, 1].astype(jnp.int32)
    counts = counts[0, :N_GROUPS].astype(jnp.int32)

    n_tiles = t // TM_EXP + N_GROUPS
    tiles_g = (counts + TM_EXP - 1) // TM_EXP
    tile_end = jnp.cumsum(tiles_g)
    base = (tile_end - tiles_g) * TM_EXP
    groups = jnp.arange(N_GROUPS, dtype=jnp.int32)
    slot = jnp.sum(jnp.where(gid[:, None] == groups[None, :], base[None, :], 0), axis=1) + rank
    tile_ids = jnp.arange(n_tiles, dtype=jnp.int32)
    used = (tile_ids < tile_end[-1]).astype(jnp.int32)
    last = jnp.maximum(tile_end[-1] - 1, 0)
    tgrp = jnp.sum((jnp.minimum(tile_ids, last)[:, None] >= tile_end[None, :]).astype(jnp.int32), axis=1)
    step_e = jnp.arange(EXPERTS_PER_GROUP, dtype=jnp.int32)[None, :]
    widx = jnp.where(used[:, None] == 1, tgrp[:, None] * EXPERTS_PER_GROUP + step_e,
                     tgrp[:, None] * EXPERTS_PER_GROUP + EXPERTS_PER_GROUP - 1).reshape(-1)

    xs = _dispatch(slot, xa, n_tiles * TM_EXP)
    fs = _experts(widx, used, tgrp, xs, wg, wu, wd, g2, gf)
    return _combine(slot, fs, t)


def _split_w_in(w):
    sizes = [D_RNN, D_RNN, Q_DIM] + [KV_DIM] * 6 + [3 * N_HEADS, 2 * D_MODEL]
    pts = np.cumsum(sizes)[:-1]
    return jnp.split(w, [int(p) for p in pts], axis=-1)


def _proj_weights(w_in):
    xr, gr, q, kc, vc, ksl, vsl, kw, vw, nsa_g, merge_g = _split_w_in(w_in)
    ng = nsa_g.reshape(D_MODEL, 3, N_KV_GROUPS, HEADS_PER_GROUP).transpose(0, 2, 1, 3)
    ng = ng.reshape(D_MODEL, N_KV_GROUPS, 3 * HEADS_PER_GROUP)
    ng = jnp.pad(ng, ((0, 0), (0, 0), (0, LANES - 3 * HEADS_PER_GROUP))).reshape(D_MODEL, N_KV_GROUPS * LANES)
    w_attn = jnp.concatenate([q, kc, vc, ksl, vsl, kw, vw, ng], axis=1)
    return (jnp.concatenate([xr, gr], axis=1).astype(BF16), w_attn.astype(BF16), merge_g.astype(BF16))


def _rope_tables(pos, width):
    inv_freq = ROPE_THETA ** (-(jnp.arange(0, HEAD_DIM, 2, dtype=F32) / HEAD_DIM))
    ang = pos.astype(F32)[:, None] * inv_freq[None, :]
    cos, sin = jnp.cos(ang), jnp.sin(ang)
    reps = width // HEAD_DIM
    return jnp.tile(jnp.concatenate([cos, cos], axis=1), (1, reps)), jnp.tile(jnp.concatenate([-sin, sin], axis=1), (1, reps))


def _block_diag(w, per):
    nb, d = w.shape[0], w.shape[1]
    w = w.reshape(nb // per, per, d, d)
    eye = jnp.eye(per, dtype=w.dtype)
    return jnp.einsum('npij,pq->npiqj', w, eye).reshape(nb // per, per * d, per * d)


def _cmp_to_sel():
    n_sel_pad = LANES
    c0 = np.arange(N_CMP_PAD) * CMP_STRIDE
    s0 = np.arange(n_sel_pad) * SEL_BLOCK
    ov = np.minimum(c0[:, None] + CMP_BLOCK, s0[None, :] + SEL_BLOCK) - np.maximum(c0[:, None], s0[None, :])
    m = np.clip(ov, 0, None) / CMP_BLOCK
    m[:, SEL_BLOCK // 2:] = 0.0
    m[N_CMP_PAD - 1, :] = 0.0
    return m.astype(np.float32)


def kernel(x, norm1_g, w_in, conv_w, conv_b, lru_wa, lru_ba, lru_wi, lru_bi, lru_lambda, w_rnn_out, cmpk_pos, cmpk_w1, cmpk_b1, cmpk_w2, cmpk_b2, cmpv_pos, cmpv_w1, cmpv_b1, cmpv_w2, cmpv_b2, w_nsa_out, w_mix_out, norm2_g, router_group_w, router_group_b, router_expert_w, router_expert_b, expert_w_gate, expert_w_up, expert_w_down, final_norm_g):
    batch, seq, _ = x.shape
    t = batch * seq
    assert w_in.shape[0] == 1, "the final norm is fused into the expert kernel: single layer only"
    assert seq % max(TS_RNN, TQ, CK_SEL, TM_PROJ) == 0, "sequence tiles must divide the sequence length"
    assert t % max(TM_POST, TM_PLAN, TM_ROWS, TM_EXP) == 0, "token tiles must divide batch * seq"
    x2 = x.reshape(t, D_MODEL)

    cos, sin = _rope_tables(jnp.arange(seq), LANES)
    cmp_ends = jnp.arange(N_CMP_PAD) * CMP_STRIDE + (CMP_BLOCK - 1)
    ccos, csin = _rope_tables(cmp_ends, LANES)
    reps = TR_CMP // N_CMP_PAD
    cmp_cos = jnp.stack([jnp.tile(ccos, (reps, 1)), jnp.ones((TR_CMP, LANES), F32)])
    cmp_sin = jnp.stack([jnp.tile(csin, (reps, 1)), jnp.zeros((TR_CMP, LANES), F32)])
    key_blk = np.arange(seq)[:, None] // SEL_BLOCK
    e_mat = jnp.asarray((key_blk == np.arange(LANES)[None, :]).astype(np.float32), dtype=BF16)
    cs_mat = jnp.asarray(_cmp_to_sel(), dtype=BF16)

    l = 0
    w_xg, w_attn, w_mg = _proj_weights(w_in[l])
    g1 = norm1_g[l][None, :]
    q, kcv, ksd, vsd, kwd, vwd, ng = _inproj(x2, g1, w_attn, cos, sin, seq)

    y_a = _rnn(x2, g1, w_xg, conv_w[l], conv_b[l][None, :],
               _block_diag(lru_wa[l], 4).astype(BF16), lru_ba[l][None, :],
               _block_diag(lru_wi[l], 4).astype(BF16), lru_bi[l][None, :],
               lru_lambda[l][None, :], w_rnn_out[l].astype(BF16), batch, seq)

    half = CMP_STRIDE * HEAD_DIM
    pos = jnp.stack([cmpk_pos[l].reshape(2, half), cmpv_pos[l].reshape(2, half)])
    w1 = jnp.stack([cmpk_w1[l], cmpv_w1[l]]).astype(BF16)
    b1 = jnp.stack([cmpk_b1[l], cmpv_b1[l]])[:, None, :]
    w2 = jnp.stack([cmpk_w2[l], cmpv_w2[l]])
    w2 = jnp.concatenate([w2, w2], axis=2).astype(BF16)
    b2 = jnp.stack([cmpk_b2[l], cmpv_b2[l]])
    b2 = jnp.concatenate([b2, b2], axis=1)[:, None, :]
    kvc = _compress(kcv, pos, w1, b1, w2, b2, cmp_cos, cmp_sin, batch, seq)

    o_nsa = _attn(q, kvc, ksd, vsd, kwd, vwd, ng, e_mat, cs_mat, batch, seq)

    wr = jnp.concatenate([
        router_expert_w[l].transpose(1, 0, 2).reshape(D_MODEL, N_EXPERTS),
        router_group_w[l],
        jnp.zeros((D_MODEL, LANES - N_EXPERTS - N_GROUPS), F32)], axis=1)
    wr_hi = wr.astype(BF16)
    wr_lo = (wr - wr_hi.astype(F32)).astype(BF16)
    rb = jnp.concatenate([router_expert_b[l].reshape(N_EXPERTS), router_group_b[l],
                          jnp.zeros((LANES - N_EXPERTS - N_GROUPS,), F32)])[None, :]
    xa = _post(o_nsa, y_a, x2, g1, w_mg, w_nsa_out[l].astype(BF16), w_mix_out[l].astype(BF16),
               norm2_g[l][None, :], jnp.concatenate([wr_hi, wr_lo], axis=1), rb)

    out = _moe_sorted(xa, expert_w_gate[l], expert_w_up[l], expert_w_down[l],
                      norm2_g[l][None, :], final_norm_g[None, :])
    return out.reshape(batch, seq, D_MODEL)
```

```python
import functools

import numpy as np
import jax
import jax.numpy as jnp
from jax import lax
from jax.experimental import pallas as pl
from jax.experimental.pallas import tpu as pltpu

D_MODEL = 1024
D_RNN = 1024
RNN_BLOCKS = 16
RNN_BLOCK_DIM = D_RNN // RNN_BLOCKS
CONV_WIDTH = 4
LRU_C = 8.0
N_HEADS = 16
HEAD_DIM = 64
HALF_DIM = HEAD_DIM // 2
N_KV_GROUPS = 4
HEADS_PER_GROUP = N_HEADS // N_KV_GROUPS
Q_DIM = N_HEADS * HEAD_DIM
KV_DIM = N_KV_GROUPS * HEAD_DIM
CMP_BLOCK = 32
CMP_STRIDE = 16
CMP_HIDDEN = 256
SEL_BLOCK = 64
SEL_TOPN = 16
WINDOW = 512
ROPE_THETA = 10000.0
FORCE_BONUS = 1e4
NEG_INF = -1e30
N_GROUPS = 4
EXPERTS_PER_GROUP = 4
N_EXPERTS = N_GROUPS * EXPERTS_PER_GROUP
D_EXPERT = 512
EPS = 1e-6

Q_SCALE = HEAD_DIM ** -0.5 * float(np.log2(np.e))

LANES = 128
SUBLANES = 8
VMEM_LIMIT = 56 * 1024 * 1024

BF16 = jnp.bfloat16
F32 = jnp.float32

COL_Q = 0
COL_KCV = COL_Q + Q_DIM
COL_KS = COL_KCV + 2 * KV_DIM
COL_VS = COL_KS + KV_DIM
COL_KW = COL_VS + KV_DIM
COL_VW = COL_KW + KV_DIM
COL_NG = COL_VW + KV_DIM
N_PROJ = COL_NG + N_KV_GROUPS * LANES

TM_PROJ = 1024
TS_RNN = 1024
RNN_SCAN_UNROLL = 8
TR_CMP = 256
TQ = 256
TQ_SELECT_TILE = 256
TK = 128
CK_SEL = 512
TM_POST = 1024
TM_PLAN = 1024
TM_ROWS = 4096
TM_EXP = 1024
ROW_COPY_UNROLL = 64
D_AUG = D_MODEL + LANES
GROUP_LANE = N_EXPERTS
N_CMP_PAD = 128


def _dot(a, b):
    return jnp.dot(a, b, preferred_element_type=F32)


def _dot_t(a, b):
    return lax.dot_general(a, b, (((1,), (1,)), ((), ())), preferred_element_type=F32)


def _lane_iota(shape):
    return lax.broadcasted_iota(jnp.int32, shape, len(shape) - 1)


def _row_iota(shape):
    return lax.broadcasted_iota(jnp.int32, shape, 0)


def _rope(x, cos, sin_signed):
    width = x.shape[-1]
    reps = width // cos.shape[-1]
    if reps > 1:
        cos = jnp.concatenate([cos] * reps, axis=1)
        sin_signed = jnp.concatenate([sin_signed] * reps, axis=1)
    first_half = (_lane_iota(x.shape) & (HEAD_DIM - 1)) < HALF_DIM
    partner = jnp.where(first_half, pltpu.roll(x, width - HALF_DIM, 1), pltpu.roll(x, HALF_DIM, 1))
    return x * cos + partner * sin_signed


def _spread_heads(x, fill=None):
    out = []
    low = _lane_iota((x.shape[0], LANES)) < HEAD_DIM
    for c in range(x.shape[1] // LANES):
        xc = x[:, c * LANES:(c + 1) * LANES]
        rolled = pltpu.roll(xc, HEAD_DIM, 1)
        out.append(jnp.where(low, xc, rolled if fill is None else fill))
        out.append(jnp.where(low, rolled, xc if fill is None else fill))
    return jnp.concatenate(out, axis=1)


def _sigmoid(x):
    return 0.5 * (jnp.tanh(0.5 * x) + 1.0)


def _rms_scale(x):
    sq = x * x
    part = sq[:, 0:LANES]
    for c in range(1, x.shape[1] // LANES):
        part = part + sq[:, c * LANES:(c + 1) * LANES]
    return lax.rsqrt(jnp.sum(part, axis=1, keepdims=True) * (1.0 / x.shape[1]) + EPS)


def _normed(x, g):
    return (x * _rms_scale(x) * g).astype(BF16)


def _inproj_kernel(x_ref, g_ref, w_ref, cos_ref, sin_ref,
                   q_ref, kcv_ref, ks_ref, vs_ref, kw_ref, vw_ref, ng_ref):
    h = _normed(x_ref[...], g_ref[...])
    cos = cos_ref[...]
    sin = sin_ref[...]

    def mm(lo, width):
        return _dot(h, w_ref[:, lo:lo + width])

    q_ref[...] = (_rope(mm(COL_Q, Q_DIM), cos, sin) * Q_SCALE).astype(BF16)
    kcv_ref[...] = mm(COL_KCV, 2 * KV_DIM)
    ks_ref[...] = _spread_heads(_rope(mm(COL_KS, KV_DIM), cos, sin)).astype(BF16)
    vs_ref[...] = _spread_heads(mm(COL_VS, KV_DIM), 1.0).astype(BF16)
    kw_ref[...] = _spread_heads(_rope(mm(COL_KW, KV_DIM), cos, sin)).astype(BF16)
    vw_ref[...] = _spread_heads(mm(COL_VW, KV_DIM), 1.0).astype(BF16)
    ng_ref[...] = jax.nn.sigmoid(mm(COL_NG, N_KV_GROUPS * LANES))


def _inproj(x2, norm_g, w_proj, cos, sin, seq):
    t = x2.shape[0]
    tm = TM_PROJ
    pos_blocks = seq // tm
    row = lambda i: (i, 0)
    const = lambda i: (0, 0)
    out_shape = (
        jax.ShapeDtypeStruct((t, Q_DIM), BF16),
        jax.ShapeDtypeStruct((t, 2 * KV_DIM), F32),
        jax.ShapeDtypeStruct((t, N_KV_GROUPS * LANES), BF16),
        jax.ShapeDtypeStruct((t, N_KV_GROUPS * LANES), BF16),
        jax.ShapeDtypeStruct((t, N_KV_GROUPS * LANES), BF16),
        jax.ShapeDtypeStruct((t, N_KV_GROUPS * LANES), BF16),
        jax.ShapeDtypeStruct((t, N_KV_GROUPS * LANES), F32),
    )
    return pl.pallas_call(
        _inproj_kernel,
        grid=(t // tm,),
        in_specs=[
            pl.BlockSpec((tm, D_MODEL), row),
            pl.BlockSpec((1, D_MODEL), const),
            pl.BlockSpec((D_MODEL, N_PROJ), const, pipeline_mode=pl.Buffered(1)),
            pl.BlockSpec((tm, LANES), lambda i: (i % pos_blocks, 0)),
            pl.BlockSpec((tm, LANES), lambda i: (i % pos_blocks, 0)),
        ],
        out_specs=tuple(pl.BlockSpec((tm, s.shape[1]), row) for s in out_shape),
        out_shape=out_shape,
        compiler_params=pltpu.CompilerParams(
            dimension_semantics=("parallel",), vmem_limit_bytes=VMEM_LIMIT),
        name="inproj",
    )(x2, norm_g, w_proj, cos, sin)


def _rnn_kernel(xin_ref, g1_ref, wx_ref, cw_ref, cb_ref, wa_ref, ba_ref, wi_ref, bi_ref, lam_ref, wo_ref,
                y_ref, tail_s, carry_s, a_s, u_s, h_s):
    ts = xin_ref.shape[0]

    @pl.when(pl.program_id(1) == 0)
    def _():
        tail_s[...] = jnp.zeros_like(tail_s)
        carry_s[...] = jnp.zeros_like(carry_s)

    hn = _normed(xin_ref[...], g1_ref[...])
    x = _dot(hn, wx_ref[:, 0:D_RNN])
    gate_pre = _dot(hn, wx_ref[:, D_RNN:2 * D_RNN])
    xext = jnp.concatenate([tail_s[...], x], axis=0)
    tail_s[...] = x[ts - SUBLANES:ts, :]
    conv = cb_ref[...]
    for k in range(CONV_WIDTH):
        back = CONV_WIDTH - 1 - k
        shifted = xext if back == 0 else pltpu.roll(xext, back, 0)
        conv = conv + cw_ref[k:k + 1, :] * shifted[SUBLANES:SUBLANES + ts, :]

    cb16 = conv.astype(BF16)
    blk = wa_ref.shape[1]
    r_pre = jnp.concatenate(
        [_dot(cb16[:, j * blk:(j + 1) * blk], wa_ref[j]) for j in range(D_RNN // blk)], axis=1)
    i_pre = jnp.concatenate(
        [_dot(cb16[:, j * blk:(j + 1) * blk], wi_ref[j]) for j in range(D_RNN // blk)], axis=1)
    r = _sigmoid(r_pre + ba_ref[...])
    gate_i = _sigmoid(i_pre + bi_ref[...])
    neg_lam = -lam_ref[...]
    softplus = jnp.maximum(neg_lam, 0.0) + jnp.log1p(jnp.exp(-jnp.abs(neg_lam)))
    log_a = r * ((-LRU_C) * softplus)
    a = jnp.exp(log_a)
    a_s[...] = a
    var = -jnp.tanh(log_a) * (a * a + 1.0)
    u_s[...] = jnp.where(var > 0.0, var * lax.rsqrt(var), 0.0) * (gate_i * conv)

    row = _row_iota((SUBLANES, D_RNN))

    def body(k, carry):
        off = pl.multiple_of(k * SUBLANES, SUBLANES)
        a = a_s[pl.ds(off, SUBLANES), :]
        b = u_s[pl.ds(off, SUBLANES), :]
        for sh in (1, 2, 4):
            keep = row >= sh
            a_prev = jnp.where(keep, pltpu.roll(a, sh, 0), 1.0)
            b_prev = jnp.where(keep, pltpu.roll(b, sh, 0), 0.0)
            b = a * b_prev + b
            a = a * a_prev
        h = a * carry + b
        h_s[pl.ds(off, SUBLANES), :] = h
        return jnp.broadcast_to(h[SUBLANES - 1:SUBLANES, :], (SUBLANES, D_RNN))

    carry_s[...] = lax.fori_loop(0, ts // SUBLANES, body, carry_s[...], unroll=RNN_SCAN_UNROLL)
    gated = (jax.nn.gelu(gate_pre) * h_s[...]).astype(BF16)
    y_ref[...] = _dot(gated, wo_ref[...])


def _rnn(x2, norm_g, w_xg, conv_w, conv_b, wa_bd, ba, wi_bd, bi, lam, w_out, batch, seq):
    t = x2.shape[0]
    ts = TS_RNN
    nt = seq // ts
    const2 = lambda b, s: (0, 0)
    const3 = lambda b, s: (0, 0, 0)
    nblk, blk = wa_bd.shape[0], wa_bd.shape[1]
    return pl.pallas_call(
        _rnn_kernel,
        grid=(batch, nt),
        in_specs=[
            pl.BlockSpec((ts, D_MODEL), lambda b, s: (b * nt + s, 0)),
            pl.BlockSpec((1, D_MODEL), const2),
            pl.BlockSpec((D_MODEL, 2 * D_RNN), const2),
            pl.BlockSpec((CONV_WIDTH, D_RNN), const2),
            pl.BlockSpec((1, D_RNN), const2),
            pl.BlockSpec((nblk, blk, blk), const3),
            pl.BlockSpec((1, D_RNN), const2),
            pl.BlockSpec((nblk, blk, blk), const3),
            pl.BlockSpec((1, D_RNN), const2),
            pl.BlockSpec((1, D_RNN), const2),
            pl.BlockSpec((D_RNN, D_MODEL), const2),
        ],
        out_specs=pl.BlockSpec((ts, D_MODEL), lambda b, s: (b * nt + s, 0)),
        out_shape=jax.ShapeDtypeStruct((t, D_MODEL), F32),
        scratch_shapes=[
            pltpu.VMEM((SUBLANES, D_RNN), F32),
            pltpu.VMEM((SUBLANES, D_RNN), F32),
            pltpu.VMEM((ts, D_RNN), F32),
            pltpu.VMEM((ts, D_RNN), F32),
            pltpu.VMEM((ts, D_RNN), F32),
        ],
        compiler_params=pltpu.CompilerParams(
            dimension_semantics=("parallel", "arbitrary"), vmem_limit_bytes=VMEM_LIMIT),
        name="rnn",
    )(x2, norm_g, w_xg, conv_w, conv_b, wa_bd, ba, wi_bd, bi, lam, w_out)


def _compress_kernel(x_ref, pos_ref, w1_ref, b1_ref, w2_ref, b2_ref, cos_ref, sin_ref, o_ref):
    n_rows = x_ref.shape[0] // CMP_STRIDE
    strided = [x_ref[pl.ds(r, n_rows, stride=CMP_STRIDE), :] for r in range(CMP_STRIDE)]
    low = _lane_iota((n_rows, LANES)) < HEAD_DIM
    groups = []
    for g in range(2):
        tiles = []
        for j in range(CMP_STRIDE // 2):
            even = strided[2 * j]
            odd = strided[2 * j + 1]
            if g == 0:
                tiles.append(jnp.where(low, even, pltpu.roll(odd, HEAD_DIM, 1)))
            else:
                tiles.append(jnp.where(low, pltpu.roll(even, HEAD_DIM, 1), odd))
        groups.append(jnp.concatenate(tiles, axis=1))
    x = jnp.concatenate(groups, axis=0)
    tr = x.shape[0]
    half = x.shape[1]
    first = _dot((x + pos_ref[0, 0:1, :]).astype(BF16), w1_ref[0, 0:half, :])
    second = _dot((x + pos_ref[0, 1:2, :]).astype(BF16), w1_ref[0, half:2 * half, :])
    hid = jax.nn.gelu(first + pltpu.roll(second, tr - 1, 0) + b1_ref[0])
    out = _dot(hid.astype(BF16), w2_ref[0]) + b2_ref[0]
    o_ref[0] = _rope(out, cos_ref[0], sin_ref[0]).astype(BF16)


def _compress(kcv, pos, w1, b1, w2, b2, cos, sin, batch, seq):
    pairs = KV_DIM // LANES
    tr = TR_CMP
    rows = batch * pairs * tr
    sel = lambda k, r: (k, 0, 0)
    return pl.pallas_call(
        _compress_kernel,
        grid=(2, batch * pairs),
        in_specs=[
            pl.BlockSpec((seq, LANES), lambda k, r: (r // pairs, k * pairs + r % pairs)),
            pl.BlockSpec((1, 2, pos.shape[2]), sel),
            pl.BlockSpec((1,) + w1.shape[1:], sel),
            pl.BlockSpec((1, 1, CMP_HIDDEN), sel),
            pl.BlockSpec((1,) + w2.shape[1:], sel),
            pl.BlockSpec((1, 1, LANES), sel),
            pl.BlockSpec((1, tr, LANES), sel),
            pl.BlockSpec((1, tr, LANES), sel),
        ],
        out_specs=pl.BlockSpec((1, tr, LANES), lambda k, r: (k, r, 0)),
        out_shape=jax.ShapeDtypeStruct((2, rows, LANES), BF16),
        compiler_params=pltpu.CompilerParams(
            dimension_semantics=("parallel", "parallel"), vmem_limit_bytes=VMEM_LIMIT),
        name="compress",
    )(kcv, pos, w1, b1, w2, b2, cos, sin)


def _lane_tile_max(s):
    tiles = [s[:, c * LANES:(c + 1) * LANES] for c in range(s.shape[1] // LANES)]
    while len(tiles) > 1:
        tiles = [jnp.maximum(tiles[k], tiles[k + 1]) if k + 1 < len(tiles) else tiles[k]
                 for k in range(0, len(tiles), 2)]
    return tiles[0]


def _stack_heads(q):
    tq = q.shape[0]
    low = _lane_iota((tq, LANES)) < HEAD_DIM
    zero = jnp.zeros((tq, LANES), BF16)
    heads = []
    for hh in range(HEADS_PER_GROUP):
        pair = q[:, (hh // 2) * LANES:(hh // 2 + 1) * LANES]
        heads.append(jnp.where(low if hh % 2 == 0 else jnp.logical_not(low), pair, zero))
    return jnp.concatenate(heads, axis=0)


def _select_tile(i, q, kc_ref, vc_ref, ng, m_ref):
    tq = q.shape[0]
    rows = HEADS_PER_GROUP * tq
    low = _lane_iota((tq, LANES)) < HEAD_DIM
    q4 = _stack_heads(q)
    lane4 = _lane_iota((rows, LANES))
    qpos4 = i * tq + (_row_iota((rows, LANES)) & (tq - 1))

    sc = _dot_t(q4, kc_ref[0])
    sc = jnp.where(lane4 * CMP_STRIDE + (CMP_BLOCK - 1) <= qpos4, sc, NEG_INF)
    mc = jnp.max(sc, axis=1, keepdims=True)
    pc = jnp.exp2(sc - mc)
    pc = pc / jnp.sum(pc, axis=1, keepdims=True)
    pc = jnp.where(qpos4 >= CMP_BLOCK - 1, pc, 0.0)
    o_cmp = _dot(pc.astype(BF16), vc_ref[0])

    psum = pc[0:tq] + pc[tq:2 * tq] + pc[2 * tq:3 * tq] + pc[3 * tq:4 * tq]
    p_hi = psum.astype(BF16)
    rem = psum - p_hi.astype(F32)
    p_mid = rem.astype(BF16)
    p_lo = (rem - p_mid.astype(F32)).astype(BF16)
    cs = m_ref[...]
    p_slc = _dot(p_hi, cs) + _dot(p_mid, cs) + _dot(p_lo, cs)

    n_sel = SEL_BLOCK // 2
    p_slc_t = p_slc.T[0:n_sel, :]
    blk = _row_iota((n_sel, tq))
    tblk = (i * tq + _lane_iota((n_sel, tq))) >> 6
    forced = (blk == 0) | (blk == tblk) | (blk == tblk - 1)
    score = jnp.where(blk <= tblk, p_slc_t + jnp.where(forced, FORCE_BONUS, 0.0), -1.0)
    rank = jnp.zeros((n_sel, tq), F32)
    for j in range(n_sel):
        sj = score[j:j + 1, :]
        beats = (sj > score) | ((sj == score) & (blk > j))
        rank = rank + jnp.where(beats, 1.0, 0.0)
    bias_t = jnp.where(rank < SEL_TOPN, 0.0, NEG_INF)
    bias = jnp.concatenate([bias_t, jnp.zeros((LANES - n_sel, tq), F32)], axis=0).T.astype(BF16)

    gated = [ng[:, hh:hh + 1] * o_cmp[hh * tq:(hh + 1) * tq] for hh in range(HEADS_PER_GROUP)]
    o_pairs = jnp.concatenate([jnp.where(low, gated[0], gated[1]), jnp.where(low, gated[2], gated[3])], axis=1)
    return bias, o_pairs


def _attn_kernel(*refs):
    n_full = pl.program_id(2) // (CK_SEL // TQ)
    for n_chunks in range(1, refs[3].shape[0] // CK_SEL + 1):
        pl.when(n_full == n_chunks - 1)(functools.partial(_attn_step, n_chunks, *refs))


def _attn_step(n_chunks, q_ref, kc_ref, vc_ref, ks_ref, vs_ref, kw_ref, vw_ref, ng_ref, e_ref, m_ref, o_ref):
    i = pl.program_id(2)
    tq = q_ref.shape[0]
    rows = HEADS_PER_GROUP * tq
    ck = CK_SEL
    low = _lane_iota((tq, LANES)) < HEAD_DIM
    q4 = _stack_heads(q_ref[...])

    st = TQ_SELECT_TILE
    picks = [_select_tile(i * (tq // st) + sub, q_ref[sub * st:(sub + 1) * st, :], kc_ref, vc_ref,
                          ng_ref[sub * st:(sub + 1) * st, :], m_ref) for sub in range(tq // st)]
    bias = jnp.concatenate([p[0] for p in picks], axis=0)
    o_cmp = jnp.concatenate([p[1] for p in picks], axis=0)

    n_sub = tq // TK
    back = WINDOW // TK
    k_tiles, v_tiles = [], []
    for jj in range(back + n_sub):
        j = i * n_sub - back + jj
        off = pl.multiple_of(jnp.maximum(j, 0) * TK, TK)
        k_tiles.append(kw_ref[pl.ds(off, TK), :])
        v_tiles.append(vw_ref[pl.ds(off, TK), :])
    sw = _dot_t(q4, jnp.concatenate(k_tiles, axis=0))
    on_or_after = _row_iota((TK, TK)) >= _lane_iota((TK, TK))
    sw_tiles = []
    for jj in range(back + n_sub):
        blocks = []
        for blk_i in range(rows // TK):
            piece = sw[blk_i * TK:(blk_i + 1) * TK, jj * TK:(jj + 1) * TK]
            tiles_back = blk_i % n_sub + back - jj
            if tiles_back == 0:
                piece = jnp.where(on_or_after, piece, NEG_INF)
            elif tiles_back == back:
                piece = jnp.where(on_or_after, NEG_INF, piece)
            elif tiles_back < 0 or tiles_back > back:
                piece = jnp.full((TK, TK), NEG_INF, F32)
            blocks.append(piece)
        tile = jnp.concatenate(blocks, axis=0)
        if jj < back:
            tile = jnp.where(i * n_sub - back + jj >= 0, tile, NEG_INF)
        sw_tiles.append(tile)
    sw = jnp.concatenate(sw_tiles, axis=1)
    m_win = jnp.max(_lane_tile_max(sw), axis=1, keepdims=True)
    acc_win = _dot(jnp.exp2(sw - m_win).astype(BF16), jnp.concatenate(v_tiles, axis=0))

    qa = jnp.concatenate([q4, jnp.concatenate([bias] * HEADS_PER_GROUP, axis=0)], axis=1)
    rel_q = i * tq - (n_chunks - 1) * ck + (_row_iota((rows, ck)) & (tq - 1))
    m_run = None
    acc_sel = None
    for c in range(n_chunks):
        ka = jnp.concatenate([ks_ref[c * ck:(c + 1) * ck, :], e_ref[c * ck:(c + 1) * ck, :]], axis=1)
        s = _dot_t(qa, ka)
        if c == n_chunks - 1:
            s = jnp.where(_lane_iota((rows, ck)) <= rel_q, s, NEG_INF)
        m_new = jnp.max(_lane_tile_max(s), axis=1, keepdims=True)
        if c > 0:
            m_new = jnp.maximum(m_run, m_new)
            acc_sel = acc_sel * jnp.exp2(m_run - m_new)
        part = _dot(jnp.exp2(s - m_new).astype(BF16), vs_ref[c * ck:(c + 1) * ck, :])
        acc_sel = part if c == 0 else acc_sel + part
        m_run = m_new

    ng = ng_ref[...]
    outs = []
    for hh in range(HEADS_PER_GROUP):
        r0 = hh * tq
        col = lambda br: ng[:, br * HEADS_PER_GROUP + hh:br * HEADS_PER_GROUP + hh + 1]
        parts = []
        for acc in (acc_sel[r0:r0 + tq], acc_win[r0:r0 + tq]):
            swapped = pltpu.roll(acc, HEAD_DIM, 1)
            parts.append(acc / swapped if hh % 2 == 0 else swapped / acc)
        outs.append(col(1) * parts[0] + col(2) * parts[1])
    o_ref[...] = (o_cmp + jnp.concatenate(
        [jnp.where(low, outs[0], outs[1]), jnp.where(low, outs[2], outs[3])], axis=1)).astype(o_ref.dtype)


def _attn(q, kvc, ksd, vsd, kwd, vwd, ng, e_mat, cs_mat, batch, seq):
    t = q.shape[0]
    tq = TQ
    nq = seq // tq
    qrow = lambda b, g, i: (b * nq + i, g)
    kv = lambda b, g, i: (b, g)
    const = lambda b, g, i: (0, 0)
    return pl.pallas_call(
        _attn_kernel,
        grid=(batch, N_KV_GROUPS, nq),
        in_specs=[
            pl.BlockSpec((tq, HEADS_PER_GROUP * HEAD_DIM), qrow),
            pl.BlockSpec((1, N_CMP_PAD, LANES), lambda b, g, i: (0, b * N_KV_GROUPS + g, 0)),
            pl.BlockSpec((1, N_CMP_PAD, LANES), lambda b, g, i: (1, b * N_KV_GROUPS + g, 0)),
            pl.BlockSpec((seq, LANES), kv),
            pl.BlockSpec((seq, LANES), kv),
            pl.BlockSpec((seq, LANES), kv),
            pl.BlockSpec((seq, LANES), kv),
            pl.BlockSpec((tq, LANES), qrow),
            pl.BlockSpec((seq, LANES), const),
            pl.BlockSpec((N_CMP_PAD, LANES), const),
        ],
        out_specs=pl.BlockSpec((tq, HEADS_PER_GROUP * HEAD_DIM), qrow),
        out_shape=jax.ShapeDtypeStruct((t, Q_DIM), BF16),
        compiler_params=pltpu.CompilerParams(
            dimension_semantics=("parallel", "parallel", "arbitrary"), vmem_limit_bytes=VMEM_LIMIT),
        name="attn",
    )(q, kvc, kvc, ksd, vsd, kwd, vwd, ng, e_mat, cs_mat)


def _post_kernel(o_ref, ya_ref, x_ref, g1_ref, wg_ref, wn_ref, wm_ref, g2_ref, wr_ref, rb_ref,
                 xa_ref):
    hn = _normed(x_ref[...], g1_ref[...])
    gate_a = jax.nn.sigmoid(_dot(hn, wg_ref[:, 0:D_MODEL]))
    gate_b = jax.nn.sigmoid(_dot(hn, wg_ref[:, D_MODEL:2 * D_MODEL]))
    y_b = _dot(o_ref[...], wn_ref[...])
    mixed = gate_a * ya_ref[...] + gate_b * y_b
    x1 = x_ref[...] + _dot(mixed.astype(BF16), wm_ref[...])
    xa_ref[:, 0:D_MODEL] = x1
    h2 = x1 * _rms_scale(x1) * g2_ref[...]
    h_hi = h2.astype(BF16)
    h_lo = (h2 - h_hi.astype(F32)).astype(BF16)
    both = _dot(h_hi, wr_ref[...])
    logits = both[:, 0:LANES] + both[:, LANES:2 * LANES] + _dot(h_lo, wr_ref[:, 0:LANES]) + rb_ref[...]

    lane = _lane_iota(logits.shape).astype(F32)
    is_grp = (lane >= N_EXPERTS) & (lane < N_EXPERTS + N_GROUPS)
    gl = jnp.where(is_grp, logits, NEG_INF)
    ge = jnp.exp(gl - jnp.max(gl, axis=1, keepdims=True))
    gp = ge / jnp.sum(ge, axis=1, keepdims=True)
    g_w = jnp.max(gp, axis=1, keepdims=True)
    big = float(4 * LANES)
    g_first = jnp.min(jnp.where(is_grp & (gp == g_w), lane, big), axis=1, keepdims=True)
    grp_lo = (g_first - N_EXPERTS) * EXPERTS_PER_GROUP

    in_grp = (lane >= grp_lo) & (lane < grp_lo + EXPERTS_PER_GROUP)
    el = jnp.where(in_grp, logits, NEG_INF)
    ee = jnp.exp(el - jnp.max(el, axis=1, keepdims=True))
    ep = ee / jnp.sum(ee, axis=1, keepdims=True)
    w1 = jnp.max(ep, axis=1, keepdims=True)
    i1 = jnp.min(jnp.where(in_grp & (ep == w1), lane, big), axis=1, keepdims=True)
    rest = jnp.where(in_grp & (lane != i1), ep, -1.0)
    w2 = jnp.max(rest, axis=1, keepdims=True)
    i2 = jnp.min(jnp.where(rest == w2, lane, big), axis=1, keepdims=True)
    den = w1 + w2
    comb = jnp.where(lane == i1, g_w * (w1 / den), jnp.where(lane == i2, g_w * (w2 / den), 0.0))
    xa_ref[:, D_MODEL:D_MODEL + LANES] = jnp.where(lane == GROUP_LANE, g_first - N_EXPERTS, comb)


def _post(o_nsa, y_a, x2, g1, w_mg, w_nsa, w_mix, g2, wr, rb):
    t = x2.shape[0]
    tm = TM_POST
    row = lambda i: (i, 0)
    const = lambda i: (0, 0)
    return pl.pallas_call(
        _post_kernel,
        grid=(t // tm,),
        in_specs=[
            pl.BlockSpec((tm, Q_DIM), row),
            pl.BlockSpec((tm, D_MODEL), row),
            pl.BlockSpec((tm, D_MODEL), row),
            pl.BlockSpec((1, D_MODEL), const),
            pl.BlockSpec((D_MODEL, 2 * D_MODEL), const),
            pl.BlockSpec((Q_DIM, D_MODEL), const),
            pl.BlockSpec((D_MODEL, D_MODEL), const),
            pl.BlockSpec((1, D_MODEL), const),
            pl.BlockSpec((D_MODEL, 2 * LANES), const),
            pl.BlockSpec((1, LANES), const),
        ],
        out_specs=pl.BlockSpec((tm, D_AUG), row),
        out_shape=jax.ShapeDtypeStruct((t, D_AUG), F32),
        compiler_params=pltpu.CompilerParams(
            dimension_semantics=("parallel",), vmem_limit_bytes=VMEM_LIMIT),
        name="post",
    )(o_nsa, y_a, x2, g1, w_mg, w_nsa, w_mix, g2, wr, rb)


def _plan_kernel(rec_ref, tri_ref, info_ref, counts_ref, carry_s):
    @pl.when(pl.program_id(0) == 0)
    def _():
        carry_s[...] = jnp.zeros_like(carry_s)

    rec = rec_ref[...]
    lane = _lane_iota(rec.shape)
    gid = rec[:, GROUP_LANE:GROUP_LANE + 1]
    onehot = jnp.where(lane.astype(F32) == gid, 1.0, 0.0)
    before = _dot(tri_ref[...], onehot.astype(BF16)) + carry_s[0:1, :]
    rank = jnp.sum(onehot * before, axis=1, keepdims=True)
    info_ref[...] = jnp.where(lane == 0, rank, jnp.where(lane == 1, gid, 0.0))
    carry_s[...] = carry_s[...] + jnp.sum(onehot, axis=0, keepdims=True)
    counts_ref[...] = carry_s[...]


def _plan(xa, tri):
    t = xa.shape[0]
    tm = TM_PLAN
    return pl.pallas_call(
        _plan_kernel,
        grid=(t // tm,),
        in_specs=[
            pl.BlockSpec((tm, LANES), lambda i: (i, D_MODEL // LANES)),
            pl.BlockSpec((tm, tm), lambda i: (0, 0)),
        ],
        out_specs=(
            pl.BlockSpec((tm, LANES), lambda i: (i, 0)),
            pl.BlockSpec((SUBLANES, LANES), lambda i: (0, 0)),
        ),
        out_shape=(
            jax.ShapeDtypeStruct((t, LANES), F32),
            jax.ShapeDtypeStruct((SUBLANES, LANES), F32),
        ),
        scratch_shapes=[pltpu.VMEM((SUBLANES, LANES), F32)],
        compiler_params=pltpu.CompilerParams(
            dimension_semantics=("arbitrary",), vmem_limit_bytes=VMEM_LIMIT),
        name="plan",
    )(xa, tri)


def _row_copies(n_rows, make_copy):
    def start(g, carry):
        r0 = pl.multiple_of(g * ROW_COPY_UNROLL, ROW_COPY_UNROLL)
        for k in range(ROW_COPY_UNROLL):
            make_copy(r0, k).start()
        return carry

    def wait(g, carry):
        for _ in range(ROW_COPY_UNROLL):
            make_copy(0, 0).wait()
        return carry

    lax.fori_loop(0, n_rows // ROW_COPY_UNROLL, start, 0)
    lax.fori_loop(0, n_rows // ROW_COPY_UNROLL, wait, 0)


def _dispatch_kernel(slot_ref, xa_ref, zero_hbm, xs_hbm, sem):
    del zero_hbm
    first = pl.program_id(0) * xa_ref.shape[0]

    def copy(r0, k):
        slot = slot_ref[first + r0 + k]
        return pltpu.make_async_copy(xa_ref.at[pl.ds(r0 + k, 1), :], xs_hbm.at[pl.ds(slot, 1), :], sem)

    _row_copies(xa_ref.shape[0], copy)


def _dispatch(slot, xa, n_sorted):
    t = xa.shape[0]
    tm = TM_ROWS
    zeros = jnp.zeros((n_sorted, D_AUG), F32)
    return pl.pallas_call(
        _dispatch_kernel,
        grid_spec=pltpu.PrefetchScalarGridSpec(
            num_scalar_prefetch=1,
            grid=(t // tm,),
            in_specs=[
                pl.BlockSpec((tm, D_AUG), lambda i, *_: (i, 0)),
                pl.BlockSpec(memory_space=pl.ANY),
            ],
            out_specs=pl.BlockSpec(memory_space=pl.ANY),
            scratch_shapes=[pltpu.SemaphoreType.DMA],
        ),
        out_shape=jax.ShapeDtypeStruct((n_sorted, D_AUG), F32),
        input_output_aliases={2: 0},
        compiler_params=pltpu.CompilerParams(
            dimension_semantics=("arbitrary",), vmem_limit_bytes=VMEM_LIMIT),
        name="dispatch",
    )(slot, xa, zeros)


def _experts_kernel(widx_ref, used_ref, tgrp_ref, xs_ref, wg_ref, wu_ref, wd_ref, g2_ref, gf_ref,
                    fs_ref, h_s, y_s):
    del widx_ref
    j = pl.program_id(0)
    e = pl.program_id(1)
    used = used_ref[j] == 1

    @pl.when(used & (e == 0))
    def _():
        x1 = xs_ref[:, 0:D_MODEL]
        h_s[...] = _normed(x1, g2_ref[...])
        y_s[...] = jnp.zeros_like(y_s)

    @pl.when(used)
    def _():
        h = h_s[...]
        act = jax.nn.silu(_dot(h, wg_ref[0].astype(BF16))) * _dot(h, wu_ref[0].astype(BF16))
        y = _dot(act.astype(BF16), wd_ref[0].astype(BF16))
        rec = xs_ref[:, D_MODEL:D_AUG]
        col = tgrp_ref[j] * EXPERTS_PER_GROUP + e
        weight = jnp.sum(jnp.where(_lane_iota(rec.shape) == col, rec, 0.0), axis=1, keepdims=True)
        y_s[...] += weight * y

    @pl.when(used & (e == EXPERTS_PER_GROUP - 1))
    def _():
        x = xs_ref[:, 0:D_MODEL] + y_s[...]
        fs_ref[...] = x * _rms_scale(x) * gf_ref[...]

    @pl.when(jnp.logical_not(used) & (e == EXPERTS_PER_GROUP - 1))
    def _():
        fs_ref[...] = jnp.zeros_like(fs_ref)


def _experts(widx, used, tgrp, xs, wg, wu, wd, g2, gf):
    n_sorted = xs.shape[0]
    tm = TM_EXP
    row = lambda j, e, *_: (j, 0)
    wsel = lambda j, e, widx, used, tgrp: (widx[j * EXPERTS_PER_GROUP + e], 0, 0)
    const = lambda j, e, *_: (0, 0)
    return pl.pallas_call(
        _experts_kernel,
        grid_spec=pltpu.PrefetchScalarGridSpec(
            num_scalar_prefetch=3,
            grid=(n_sorted // tm, EXPERTS_PER_GROUP),
            in_specs=[
                pl.BlockSpec((tm, D_AUG), row),
                pl.BlockSpec((1, D_MODEL, D_EXPERT), wsel),
                pl.BlockSpec((1, D_MODEL, D_EXPERT), wsel),
                pl.BlockSpec((1, D_EXPERT, D_MODEL), wsel),
                pl.BlockSpec((1, D_MODEL), const),
                pl.BlockSpec((1, D_MODEL), const),
            ],
            out_specs=pl.BlockSpec((tm, D_MODEL), row),
            scratch_shapes=[pltpu.VMEM((tm, D_MODEL), BF16), pltpu.VMEM((tm, D_MODEL), F32)],
        ),
        out_shape=jax.ShapeDtypeStruct((n_sorted, D_MODEL), F32),
        compiler_params=pltpu.CompilerParams(
            dimension_semantics=("arbitrary", "arbitrary"), vmem_limit_bytes=VMEM_LIMIT),
        name="experts",
    )(widx, used, tgrp, xs, wg, wu, wd, g2, gf)


def _combine_kernel(slot_ref, fs_hbm, o_ref, sem):
    first = pl.program_id(0) * o_ref.shape[0]

    def copy(r0, k):
        slot = slot_ref[first + r0 + k]
        return pltpu.make_async_copy(fs_hbm.at[pl.ds(slot, 1), :], o_ref.at[pl.ds(r0 + k, 1), :], sem)

    _row_copies(o_ref.shape[0], copy)


def _combine(slot, fs, t):
    tm = TM_ROWS
    return pl.pallas_call(
        _combine_kernel,
        grid_spec=pltpu.PrefetchScalarGridSpec(
            num_scalar_prefetch=1,
            grid=(t // tm,),
            in_specs=[pl.BlockSpec(memory_space=pl.ANY)],
            out_specs=pl.BlockSpec((tm, D_MODEL), lambda i, *_: (i, 0)),
            scratch_shapes=[pltpu.SemaphoreType.DMA],
        ),
        out_shape=jax.ShapeDtypeStruct((t, D_MODEL), F32),
        compiler_params=pltpu.CompilerParams(
            dimension_semantics=("arbitrary",), vmem_limit_bytes=VMEM_LIMIT),
        name="combine",
    )(slot, fs)


def _moe_sorted(xa, wg, wu, wd, g2, gf):
    t = xa.shape[0]
    tri = jnp.asarray(np.tril(np.ones((TM_PLAN, TM_PLAN), np.float32), -1), dtype=BF16)
    info, counts = _plan(xa, tri)
    rank = info[:, 0].astype(jnp.int32)
    gid = info[:, 1].astype(jnp.int32)
    counts = counts[0, :N_GROUPS].astype(jnp.int32)

    n_tiles = t // TM_EXP + N_GROUPS
    tiles_g = (counts + TM_EXP - 1) // TM_EXP
    tile_end = jnp.cumsum(tiles_g)
    base = (tile_end - tiles_g) * TM_EXP
    groups = jnp.arange(N_GROUPS, dtype=jnp.int32)
    slot = jnp.sum(jnp.where(gid[:, None] == groups[None, :], base[None, :], 0), axis=1) + rank
    tile_ids = jnp.arange(n_tiles, dtype=jnp.int32)
    used = (tile_ids < tile_end[-1]).astype(jnp.int32)
    last = jnp.maximum(tile_end[-1] - 1, 0)
    tgrp = jnp.sum((jnp.minimum(tile_ids, last)[:, None] >= tile_end[None, :]).astype(jnp.int32), axis=1)
    step_e = jnp.arange(EXPERTS_PER_GROUP, dtype=jnp.int32)[None, :]
    widx = jnp.where(used[:, None] == 1, tgrp[:, None] * EXPERTS_PER_GROUP + step_e,
                     tgrp[:, None] * EXPERTS_PER_GROUP + EXPERTS_PER_GROUP - 1).reshape(-1)

    xs = _dispatch(slot, xa, n_tiles * TM_EXP)
    fs = _experts(widx, used, tgrp, xs, wg, wu, wd, g2, gf)
    return _combine(slot, fs, t)


def _split_w_in(w):
    sizes = [D_RNN, D_RNN, Q_DIM] + [KV_DIM] * 6 + [3 * N_HEADS, 2 * D_MODEL]
    pts = np.cumsum(sizes)[:-1]
    return jnp.split(w, [int(p) for p in pts], axis=-1)


def _proj_weights(w_in):
    xr, gr, q, kc, vc, ksl, vsl, kw, vw, nsa_g, merge_g = _split_w_in(w_in)
    ng = nsa_g.reshape(D_MODEL, 3, N_KV_GROUPS, HEADS_PER_GROUP).transpose(0, 2, 1, 3)
    ng = ng.reshape(D_MODEL, N_KV_GROUPS, 3 * HEADS_PER_GROUP)
    ng = jnp.pad(ng, ((0, 0), (0, 0), (0, LANES - 3 * HEADS_PER_GROUP))).reshape(D_MODEL, N_KV_GROUPS * LANES)
    w_attn = jnp.concatenate([q, kc, vc, ksl, vsl, kw, vw, ng], axis=1)
    return (jnp.concatenate([xr, gr], axis=1).astype(BF16), w_attn.astype(BF16), merge_g.astype(BF16))


def _rope_tables(pos, width):
    inv_freq = ROPE_THETA ** (-(jnp.arange(0, HEAD_DIM, 2, dtype=F32) / HEAD_DIM))
    ang = pos.astype(F32)[:, None] * inv_freq[None, :]
    cos, sin = jnp.cos(ang), jnp.sin(ang)
    reps = width // HEAD_DIM
    return jnp.tile(jnp.concatenate([cos, cos], axis=1), (1, reps)), jnp.tile(jnp.concatenate([-sin, sin], axis=1), (1, reps))


def _block_diag(w, per):
    nb, d = w.shape[0], w.shape[1]
    w = w.reshape(nb // per, per, d, d)
    eye = jnp.eye(per, dtype=w.dtype)
    return jnp.einsum('npij,pq->npiqj', w, eye).reshape(nb // per, per * d, per * d)


def _cmp_to_sel():
    n_sel_pad = LANES
    c0 = np.arange(N_CMP_PAD) * CMP_STRIDE
    s0 = np.arange(n_sel_pad) * SEL_BLOCK
    ov = np.minimum(c0[:, None] + CMP_BLOCK, s0[None, :] + SEL_BLOCK) - np.maximum(c0[:, None], s0[None, :])
    m = np.clip(ov, 0, None) / CMP_BLOCK
    m[:, SEL_BLOCK // 2:] = 0.0
    m[N_CMP_PAD - 1, :] = 0.0
    return m.astype(np.float32)


def kernel(x, norm1_g, w_in, conv_w, conv_b, lru_wa, lru_ba, lru_wi, lru_bi, lru_lambda, w_rnn_out, cmpk_pos, cmpk_w1, cmpk_b1, cmpk_w2, cmpk_b2, cmpv_pos, cmpv_w1, cmpv_b1, cmpv_w2, cmpv_b2, w_nsa_out, w_mix_out, norm2_g, router_group_w, router_group_b, router_expert_w, router_expert_b, expert_w_gate, expert_w_up, expert_w_down, final_norm_g):
    batch, seq, _ = x.shape
    t = batch * seq
    assert w_in.shape[0] == 1, "the final norm is fused into the expert kernel: single layer only"
    assert seq % max(TS_RNN, TQ, CK_SEL, TM_PROJ) == 0, "sequence tiles must divide the sequence length"
    assert t % max(TM_POST, TM_PLAN, TM_ROWS, TM_EXP) == 0, "token tiles must divide batch * seq"
    x2 = x.reshape(t, D_MODEL)

    cos, sin = _rope_tables(jnp.arange(seq), LANES)
    cmp_ends = jnp.arange(N_CMP_PAD) * CMP_STRIDE + (CMP_BLOCK - 1)
    ccos, csin = _rope_tables(cmp_ends, LANES)
    reps = TR_CMP // N_CMP_PAD
    cmp_cos = jnp.stack([jnp.tile(ccos, (reps, 1)), jnp.ones((TR_CMP, LANES), F32)])
    cmp_sin = jnp.stack([jnp.tile(csin, (reps, 1)), jnp.zeros((TR_CMP, LANES), F32)])
    key_blk = np.arange(seq)[:, None] // SEL_BLOCK
    e_mat = jnp.asarray((key_blk == np.arange(LANES)[None, :]).astype(np.float32), dtype=BF16)
    cs_mat = jnp.asarray(_cmp_to_sel(), dtype=BF16)

    l = 0
    w_xg, w_attn, w_mg = _proj_weights(w_in[l])
    g1 = norm1_g[l][None, :]
    q, kcv, ksd, vsd, kwd, vwd, ng = _inproj(x2, g1, w_attn, cos, sin, seq)

    y_a = _rnn(x2, g1, w_xg, conv_w[l], conv_b[l][None, :],
               _block_diag(lru_wa[l], 4).astype(BF16), lru_ba[l][None, :],
               _block_diag(lru_wi[l], 4).astype(BF16), lru_bi[l][None, :],
               lru_lambda[l][None, :], w_rnn_out[l].astype(BF16), batch, seq)

    half = CMP_STRIDE * HEAD_DIM
    pos = jnp.stack([cmpk_pos[l].reshape(2, half), cmpv_pos[l].reshape(2, half)])
    w1 = jnp.stack([cmpk_w1[l], cmpv_w1[l]]).astype(BF16)
    b1 = jnp.stack([cmpk_b1[l], cmpv_b1[l]])[:, None, :]
    w2 = jnp.stack([cmpk_w2[l], cmpv_w2[l]])
    w2 = jnp.concatenate([w2, w2], axis=2).astype(BF16)
    b2 = jnp.stack([cmpk_b2[l], cmpv_b2[l]])
    b2 = jnp.concatenate([b2, b2], axis=1)[:, None, :]
    kvc = _compress(kcv, pos, w1, b1, w2, b2, cmp_cos, cmp_sin, batch, seq)

    o_nsa = _attn(q, kvc, ksd, vsd, kwd, vwd, ng, e_mat, cs_mat, batch, seq)

    wr = jnp.concatenate([
        router_expert_w[l].transpose(1, 0, 2).reshape(D_MODEL, N_EXPERTS),
        router_group_w[l],
        jnp.zeros((D_MODEL, LANES - N_EXPERTS - N_GROUPS), F32)], axis=1)
    wr_hi = wr.astype(BF16)
    wr_lo = (wr - wr_hi.astype(F32)).astype(BF16)
    rb = jnp.concatenate([router_expert_b[l].reshape(N_EXPERTS), router_group_b[l],
                          jnp.zeros((LANES - N_EXPERTS - N_GROUPS,), F32)])[None, :]
    xa = _post(o_nsa, y_a, x2, g1, w_mg, w_nsa_out[l].astype(BF16), w_mix_out[l].astype(BF16),
               norm2_g[l][None, :], jnp.concatenate([wr_hi, wr_lo], axis=1), rb)

    out = _moe_sorted(xa, expert_w_gate[l], expert_w_up[l], expert_w_down[l],
                      norm2_g[l][None, :], final_norm_g[None, :])
    return out.reshape(batch, seq, D_MODEL)
```

```python
import functools

import numpy as np
import jax
import jax.numpy as jnp
from jax import lax
from jax.experimental import pallas as pl
from jax.experimental.pallas import tpu as pltpu

D_MODEL = 1024
D_RNN = 1024
RNN_BLOCKS = 16
RNN_BLOCK_DIM = D_RNN // RNN_BLOCKS
CONV_WIDTH = 4
LRU_C = 8.0
N_HEADS = 16
HEAD_DIM = 64
HALF_DIM = HEAD_DIM // 2
N_KV_GROUPS = 4
HEADS_PER_GROUP = N_HEADS // N_KV_GROUPS
Q_DIM = N_HEADS * HEAD_DIM
KV_DIM = N_KV_GROUPS * HEAD_DIM
CMP_BLOCK = 32
CMP_STRIDE = 16
CMP_HIDDEN = 256
SEL_BLOCK = 64
SEL_TOPN = 16
WINDOW = 512
ROPE_THETA = 10000.0
FORCE_BONUS = 1e4
NEG_INF = -1e30
N_GROUPS = 4
EXPERTS_PER_GROUP = 4
N_EXPERTS = N_GROUPS * EXPERTS_PER_GROUP
D_EXPERT = 512
EPS = 1e-6

Q_SCALE = HEAD_DIM ** -0.5 * float(np.log2(np.e))

LANES = 128
SUBLANES = 8
VMEM_LIMIT = 56 * 1024 * 1024

BF16 = jnp.bfloat16
F32 = jnp.float32

COL_Q = 0
COL_KCV = COL_Q + Q_DIM
COL_KS = COL_KCV + 2 * KV_DIM
COL_VS = COL_KS + KV_DIM
COL_KW = COL_VS + KV_DIM
COL_VW = COL_KW + KV_DIM
COL_NG = COL_VW + KV_DIM
N_PROJ = COL_NG + N_KV_GROUPS * LANES

TM_PROJ = 1024
TS_RNN = 1024
RNN_SCAN_UNROLL = 8
TR_CMP = 256
TQ = 256
TQ_SELECT_TILE = 256
TK = 128
CK_SEL = 512
TM_POST = 1024
TM_PLAN = 1024
TM_ROWS = 4096
TM_EXP = 1024
ROW_COPY_UNROLL = 64
D_AUG = D_MODEL + LANES
GROUP_LANE = N_EXPERTS
N_CMP_PAD = 128


def _dot(a, b):
    return jnp.dot(a, b, preferred_element_type=F32)


def _dot_t(a, b):
    return lax.dot_general(a, b, (((1,), (1,)), ((), ())), preferred_element_type=F32)


def _lane_iota(shape):
    return lax.broadcasted_iota(jnp.int32, shape, len(shape) - 1)


def _row_iota(shape):
    return lax.broadcasted_iota(jnp.int32, shape, 0)


def _rope(x, cos, sin_signed):
    width = x.shape[-1]
    reps = width // cos.shape[-1]
    if reps > 1:
        cos = jnp.concatenate([cos] * reps, axis=1)
        sin_signed = jnp.concatenate([sin_signed] * reps, axis=1)
    first_half = (_lane_iota(x.shape) & (HEAD_DIM - 1)) < HALF_DIM
    partner = jnp.where(first_half, pltpu.roll(x, width - HALF_DIM, 1), pltpu.roll(x, HALF_DIM, 1))
    return x * cos + partner * sin_signed


def _spread_heads(x, fill=None):
    out = []
    low = _lane_iota((x.shape[0], LANES)) < HEAD_DIM
    for c in range(x.shape[1] // LANES):
        xc = x[:, c * LANES:(c + 1) * LANES]
        rolled = pltpu.roll(xc, HEAD_DIM, 1)
        out.append(jnp.where(low, xc, rolled if fill is None else fill))
        out.append(jnp.where(low, rolled, xc if fill is None else fill))
    return jnp.concatenate(out, axis=1)


def _sigmoid(x):
    return 0.5 * (jnp.tanh(0.5 * x) + 1.0)


def _rms_scale(x):
    sq = x * x
    part = sq[:, 0:LANES]
    for c in range(1, x.shape[1] // LANES):
        part = part + sq[:, c * LANES:(c + 1) * LANES]
    return lax.rsqrt(jnp.sum(part, axis=1, keepdims=True) * (1.0 / x.shape[1]) + EPS)


def _normed(x, g):
    return (x * _rms_scale(x) * g).astype(BF16)


def _inproj_kernel(x_ref, g_ref, w_ref, cos_ref, sin_ref,
                   q_ref, kcv_ref, ks_ref, vs_ref, kw_ref, vw_ref, ng_ref):
    h = _normed(x_ref[...], g_ref[...])
    cos = cos_ref[...]
    sin = sin_ref[...]

    def mm(lo, width):
        return _dot(h, w_ref[:, lo:lo + width])

    q_ref[...] = (_rope(mm(COL_Q, Q_DIM), cos, sin) * Q_SCALE).astype(BF16)
    kcv_ref[...] = mm(COL_KCV, 2 * KV_DIM)
    ks_ref[...] = _spread_heads(_rope(mm(COL_KS, KV_DIM), cos, sin)).astype(BF16)
    vs_ref[...] = _spread_heads(mm(COL_VS, KV_DIM), 1.0).astype(BF16)
    kw_ref[...] = _spread_heads(_rope(mm(COL_KW, KV_DIM), cos, sin)).astype(BF16)
    vw_ref[...] = _spread_heads(mm(COL_VW, KV_DIM), 1.0).astype(BF16)
    ng_ref[...] = jax.nn.sigmoid(mm(COL_NG, N_KV_GROUPS * LANES))


def _inproj(x2, norm_g, w_proj, cos, sin, seq):
    t = x2.shape[0]
    tm = TM_PROJ
    pos_blocks = seq // tm
    row = lambda i: (i, 0)
    const = lambda i: (0, 0)
    out_shape = (
        jax.ShapeDtypeStruct((t, Q_DIM), BF16),
        jax.ShapeDtypeStruct((t, 2 * KV_DIM), F32),
        jax.ShapeDtypeStruct((t, N_KV_GROUPS * LANES), BF16),
        jax.ShapeDtypeStruct((t, N_KV_GROUPS * LANES), BF16),
        jax.ShapeDtypeStruct((t, N_KV_GROUPS * LANES), BF16),
        jax.ShapeDtypeStruct((t, N_KV_GROUPS * LANES), BF16),
        jax.ShapeDtypeStruct((t, N_KV_GROUPS * LANES), F32),
    )
    return pl.pallas_call(
        _inproj_kernel,
        grid=(t // tm,),
        in_specs=[
            pl.BlockSpec((tm, D_MODEL), row),
            pl.BlockSpec((1, D_MODEL), const),
            pl.BlockSpec((D_MODEL, N_PROJ), const, pipeline_mode=pl.Buffered(1)),
            pl.BlockSpec((tm, LANES), lambda i: (i % pos_blocks, 0)),
            pl.BlockSpec((tm, LANES), lambda i: (i % pos_blocks, 0)),
        ],
        out_specs=tuple(pl.BlockSpec((tm, s.shape[1]), row) for s in out_shape),
        out_shape=out_shape,
        compiler_params=pltpu.CompilerParams(
            dimension_semantics=("parallel",), vmem_limit_bytes=VMEM_LIMIT),
        name="inproj",
    )(x2, norm_g, w_proj, cos, sin)


def _rnn_kernel(xin_ref, g1_ref, wx_ref, cw_ref, cb_ref, wa_ref, ba_ref, wi_ref, bi_ref, lam_ref, wo_ref,
                y_ref, tail_s, carry_s, a_s, u_s, h_s):
    ts = xin_ref.shape[0]

    @pl.when(pl.program_id(1) == 0)
    def _():
        tail_s[...] = jnp.zeros_like(tail_s)
        carry_s[...] = jnp.zeros_like(carry_s)

    hn = _normed(xin_ref[...], g1_ref[...])
    x = _dot(hn, wx_ref[:, 0:D_RNN])
    gate_pre = _dot(hn, wx_ref[:, D_RNN:2 * D_RNN])
    xext = jnp.concatenate([tail_s[...], x], axis=0)
    tail_s[...] = x[ts - SUBLANES:ts, :]
    conv = cb_ref[...]
    for k in range(CONV_WIDTH):
        back = CONV_WIDTH - 1 - k
        shifted = xext if back == 0 else pltpu.roll(xext, back, 0)
        conv = conv + cw_ref[k:k + 1, :] * shifted[SUBLANES:SUBLANES + ts, :]

    cb16 = conv.astype(BF16)
    blk = wa_ref.shape[1]
    r_pre = jnp.concatenate(
        [_dot(cb16[:, j * blk:(j + 1) * blk], wa_ref[j]) for j in range(D_RNN // blk)], axis=1)
    i_pre = jnp.concatenate(
        [_dot(cb16[:, j * blk:(j + 1) * blk], wi_ref[j]) for j in range(D_RNN // blk)], axis=1)
    r = _sigmoid(r_pre + ba_ref[...])
    gate_i = _sigmoid(i_pre + bi_ref[...])
    neg_lam = -lam_ref[...]
    softplus = jnp.maximum(neg_lam, 0.0) + jnp.log1p(jnp.exp(-jnp.abs(neg_lam)))
    log_a = r * ((-LRU_C) * softplus)
    a = jnp.exp(log_a)
    a_s[...] = a
    var = -jnp.tanh(log_a) * (a * a + 1.0)
    u_s[...] = jnp.where(var > 0.0, var * lax.rsqrt(var), 0.0) * (gate_i * conv)

    row = _row_iota((SUBLANES, D_RNN))

    def body(k, carry):
        off = pl.multiple_of(k * SUBLANES, SUBLANES)
        a = a_s[pl.ds(off, SUBLANES), :]
        b = u_s[pl.ds(off, SUBLANES), :]
        for sh in (1, 2, 4):
            keep = row >= sh
            a_prev = jnp.where(keep, pltpu.roll(a, sh, 0), 1.0)
            b_prev = jnp.where(keep, pltpu.roll(b, sh, 0), 0.0)
            b = a * b_prev + b
            a = a * a_prev
        h = a * carry + b
        h_s[pl.ds(off, SUBLANES), :] = h
        return jnp.broadcast_to(h[SUBLANES - 1:SUBLANES, :], (SUBLANES, D_RNN))

    carry_s[...] = lax.fori_loop(0, ts // SUBLANES, body, carry_s[...], unroll=RNN_SCAN_UNROLL)
    gated = (jax.nn.gelu(gate_pre) * h_s[...]).astype(BF16)
    y_ref[...] = _dot(gated, wo_ref[...])


def _rnn(x2, norm_g, w_xg, conv_w, conv_b, wa_bd, ba, wi_bd, bi, lam, w_out, batch, seq):
    t = x2.shape[0]
    ts = TS_RNN
    nt = seq // ts
    const2 = lambda b, s: (0, 0)
    const3 = lambda b, s: (0, 0, 0)
    nblk, blk = wa_bd.shape[0], wa_bd.shape[1]
    return pl.pallas_call(
        _rnn_kernel,
        grid=(batch, nt),
        in_specs=[
            pl.BlockSpec((ts, D_MODEL), lambda b, s: (b * nt + s, 0)),
            pl.BlockSpec((1, D_MODEL), const2),
            pl.BlockSpec((D_MODEL, 2 * D_RNN), const2),
            pl.BlockSpec((CONV_WIDTH, D_RNN), const2),
            pl.BlockSpec((1, D_RNN), const2),
            pl.BlockSpec((nblk, blk, blk), const3),
            pl.BlockSpec((1, D_RNN), const2),
            pl.BlockSpec((nblk, blk, blk), const3),
            pl.BlockSpec((1, D_RNN), const2),
            pl.BlockSpec((1, D_RNN), const2),
            pl.BlockSpec((D_RNN, D_MODEL), const2),
        ],
        out_specs=pl.BlockSpec((ts, D_MODEL), lambda b, s: (b * nt + s, 0)),
        out_shape=jax.ShapeDtypeStruct((t, D_MODEL), F32),
        scratch_shapes=[
            pltpu.VMEM((SUBLANES, D_RNN), F32),
            pltpu.VMEM((SUBLANES, D_RNN), F32),
            pltpu.VMEM((ts, D_RNN), F32),
            pltpu.VMEM((ts, D_RNN), F32),
            pltpu.VMEM((ts, D_RNN), F32),
        ],
        compiler_params=pltpu.CompilerParams(
            dimension_semantics=("parallel", "arbitrary"), vmem_limit_bytes=VMEM_LIMIT),
        name="rnn",
    )(x2, norm_g, w_xg, conv_w, conv_b, wa_bd, ba, wi_bd, bi, lam, w_out)


def _compress_kernel(x_ref, pos_ref, w1_ref, b1_ref, w2_ref, b2_ref, cos_ref, sin_ref, o_ref):
    n_rows = x_ref.shape[0] // CMP_STRIDE
    strided = [x_ref[pl.ds(r, n_rows, stride=CMP_STRIDE), :] for r in range(CMP_STRIDE)]
    low = _lane_iota((n_rows, LANES)) < HEAD_DIM
    groups = []
    for g in range(2):
        tiles = []
        for j in range(CMP_STRIDE // 2):
            even = strided[2 * j]
            odd = strided[2 * j + 1]
            if g == 0:
                tiles.append(jnp.where(low, even, pltpu.roll(odd, HEAD_DIM, 1)))
            else:
                tiles.append(jnp.where(low, pltpu.roll(even, HEAD_DIM, 1), odd))
        groups.append(jnp.concatenate(tiles, axis=1))
    x = jnp.concatenate(groups, axis=0)
    tr = x.shape[0]
    half = x.shape[1]
    first = _dot((x + pos_ref[0, 0:1, :]).astype(BF16), w1_ref[0, 0:half, :])
    second = _dot((x + pos_ref[0, 1:2, :]).astype(BF16), w1_ref[0, half:2 * half, :])
    hid = jax.nn.gelu(first + pltpu.roll(second, tr - 1, 0) + b1_ref[0])
    out = _dot(hid.astype(BF16), w2_ref[0]) + b2_ref[0]
    o_ref[0] = _rope(out, cos_ref[0], sin_ref[0]).astype(BF16)


def _compress(kcv, pos, w1, b1, w2, b2, cos, sin, batch, seq):
    pairs = KV_DIM // LANES
    tr = TR_CMP
    rows = batch * pairs * tr
    sel = lambda k, r: (k, 0, 0)
    return pl.pallas_call(
        _compress_kernel,
        grid=(2, batch * pairs),
        in_specs=[
            pl.BlockSpec((seq, LANES), lambda k, r: (r // pairs, k * pairs + r % pairs)),
            pl.BlockSpec((1, 2, pos.shape[2]), sel),
            pl.BlockSpec((1,) + w1.shape[1:], sel),
            pl.BlockSpec((1, 1, CMP_HIDDEN), sel),
            pl.BlockSpec((1,) + w2.shape[1:], sel),
            pl.BlockSpec((1, 1, LANES), sel),
            pl.BlockSpec((1, tr, LANES), sel),
            pl.BlockSpec((1, tr, LANES), sel),
        ],
        out_specs=pl.BlockSpec((1, tr, LANES), lambda k, r: (k, r, 0)),
        out_shape=jax.ShapeDtypeStruct((2, rows, LANES), BF16),
        compiler_params=pltpu.CompilerParams(
            dimension_semantics=("parallel", "parallel"), vmem_limit_bytes=VMEM_LIMIT),
        name="compress",
    )(kcv, pos, w1, b1, w2, b2, cos, sin)


def _lane_tile_max(s):
    tiles = [s[:, c * LANES:(c + 1) * LANES] for c in range(s.shape[1] // LANES)]
    while len(tiles) > 1:
        tiles = [jnp.maximum(tiles[k], tiles[k + 1]) if k + 1 < len(tiles) else tiles[k]
                 for k in range(0, len(tiles), 2)]
    return tiles[0]


def _stack_heads(q):
    tq = q.shape[0]
    low = _lane_iota((tq, LANES)) < HEAD_DIM
    zero = jnp.zeros((tq, LANES), BF16)
    heads = []
    for hh in range(HEADS_PER_GROUP):
        pair = q[:, (hh // 2) * LANES:(hh // 2 + 1) * LANES]
        heads.append(jnp.where(low if hh % 2 == 0 else jnp.logical_not(low), pair, zero))
    return jnp.concatenate(heads, axis=0)


def _select_tile(i, q, kc_ref, vc_ref, ng, m_ref):
    tq = q.shape[0]
    rows = HEADS_PER_GROUP * tq
    low = _lane_iota((tq, LANES)) < HEAD_DIM
    q4 = _stack_heads(q)
    lane4 = _lane_iota((rows, LANES))
    qpos4 = i * tq + (_row_iota((rows, LANES)) & (tq - 1))

    sc = _dot_t(q4, kc_ref[0])
    sc = jnp.where(lane4 * CMP_STRIDE + (CMP_BLOCK - 1) <= qpos4, sc, NEG_INF)
    mc = jnp.max(sc, axis=1, keepdims=True)
    pc = jnp.exp2(sc - mc)
    pc = pc / jnp.sum(pc, axis=1, keepdims=True)
    pc = jnp.where(qpos4 >= CMP_BLOCK - 1, pc, 0.0)
    o_cmp = _dot(pc.astype(BF16), vc_ref[0])

    psum = pc[0:tq] + pc[tq:2 * tq] + pc[2 * tq:3 * tq] + pc[3 * tq:4 * tq]
    p_hi = psum.astype(BF16)
    rem = psum - p_hi.astype(F32)
    p_mid = rem.astype(BF16)
    p_lo = (rem - p_mid.astype(F32)).astype(BF16)
    cs = m_ref[...]
    p_slc = _dot(p_hi, cs) + _dot(p_mid, cs) + _dot(p_lo, cs)

    n_sel = SEL_BLOCK // 2
    p_slc_t = p_slc.T[0:n_sel, :]
    blk = _row_iota((n_sel, tq))
    tblk = (i * tq + _lane_iota((n_sel, tq))) >> 6
    forced = (blk == 0) | (blk == tblk) | (blk == tblk - 1)
    score = jnp.where(blk <= tblk, p_slc_t + jnp.where(forced, FORCE_BONUS, 0.0), -1.0)
    rank = jnp.zeros((n_sel, tq), F32)
    for j in range(n_sel):
        sj = score[j:j + 1, :]
        beats = (sj > score) | ((sj == score) & (blk > j))
        rank = rank + jnp.where(beats, 1.0, 0.0)
    bias_t = jnp.where(rank < SEL_TOPN, 0.0, NEG_INF)
    bias = jnp.concatenate([bias_t, jnp.zeros((LANES - n_sel, tq), F32)], axis=0).T.astype(BF16)

    gated = [ng[:, hh:hh + 1] * o_cmp[hh * tq:(hh + 1) * tq] for hh in range(HEADS_PER_GROUP)]
    o_pairs = jnp.concatenate([jnp.where(low, gated[0], gated[1]), jnp.where(low, gated[2], gated[3])], axis=1)
    return bias, o_pairs


def _attn_kernel(*refs):
    n_full = pl.program_id(2) // (CK_SEL // TQ)
    for n_chunks in range(1, refs[3].shape[0] // CK_SEL + 1):
        pl.when(n_full == n_chunks - 1)(functools.partial(_attn_step, n_chunks, *refs))


def _attn_step(n_chunks, q_ref, kc_ref, vc_ref, ks_ref, vs_ref, kw_ref, vw_ref, ng_ref, e_ref, m_ref, o_ref):
    i = pl.program_id(2)
    tq = q_ref.shape[0]
    rows = HEADS_PER_GROUP * tq
    ck = CK_SEL
    low = _lane_iota((tq, LANES)) < HEAD_DIM
    q4 = _stack_heads(q_ref[...])

    st = TQ_SELECT_TILE
    picks = [_select_tile(i * (tq // st) + sub, q_ref[sub * st:(sub + 1) * st, :], kc_ref, vc_ref,
                          ng_ref[sub * st:(sub + 1) * st, :], m_ref) for sub in range(tq // st)]
    bias = jnp.concatenate([p[0] for p in picks], axis=0)
    o_cmp = jnp.concatenate([p[1] for p in picks], axis=0)

    n_sub = tq // TK
    back = WINDOW // TK
    k_tiles, v_tiles = [], []
    for jj in range(back + n_sub):
        j = i * n_sub - back + jj
        off = pl.multiple_of(jnp.maximum(j, 0) * TK, TK)
        k_tiles.append(kw_ref[pl.ds(off, TK), :])
        v_tiles.append(vw_ref[pl.ds(off, TK), :])
    sw = _dot_t(q4, jnp.concatenate(k_tiles, axis=0))
    on_or_after = _row_iota((TK, TK)) >= _lane_iota((TK, TK))
    sw_tiles = []
    for jj in range(back + n_sub):
        blocks = []
        for blk_i in range(rows // TK):
            piece = sw[blk_i * TK:(blk_i + 1) * TK, jj * TK:(jj + 1) * TK]
            tiles_back = blk_i % n_sub + back - jj
            if tiles_back == 0:
                piece = jnp.where(on_or_after, piece, NEG_INF)
            elif tiles_back == back:
                piece = jnp.where(on_or_after, NEG_INF, piece)
            elif tiles_back < 0 or tiles_back > back:
                piece = jnp.full((TK, TK), NEG_INF, F32)
            blocks.append(piece)
        tile = jnp.concatenate(blocks, axis=0)
        if jj < back:
            tile = jnp.where(i * n_sub - back + jj >= 0, tile, NEG_INF)
        sw_tiles.append(tile)
    sw = jnp.concatenate(sw_tiles, axis=1)
    m_win = jnp.max(_lane_tile_max(sw), axis=1, keepdims=True)
    acc_win = _dot(jnp.exp2(sw - m_win).astype(BF16), jnp.concatenate(v_tiles, axis=0))

    qa = jnp.concatenate([q4, jnp.concatenate([bias] * HEADS_PER_GROUP, axis=0)], axis=1)
    rel_q = i * tq - (n_chunks - 1) * ck + (_row_iota((rows, ck)) & (tq - 1))
    m_run = None
    acc_sel = None
    for c in range(n_chunks):
        ka = jnp.concatenate([ks_ref[c * ck:(c + 1) * ck, :], e_ref[c * ck:(c + 1) * ck, :]], axis=1)
        s = _dot_t(qa, ka)
        if c == n_chunks - 1:
            s = jnp.where(_lane_iota((rows, ck)) <= rel_q, s, NEG_INF)
        m_new = jnp.max(_lane_tile_max(s), axis=1, keepdims=True)
        if c > 0:
            m_new = jnp.maximum(m_run, m_new)
            acc_sel = acc_sel * jnp.exp2(m_run - m_new)
        part = _dot(jnp.exp2(s - m_new).astype(BF16), vs_ref[c * ck:(c + 1) * ck, :])
        acc_sel = part if c == 0 else acc_sel + part
        m_run = m_new

    ng = ng_ref[...]
    outs = []
    for hh in range(HEADS_PER_GROUP):
        r0 = hh * tq
        col = lambda br: ng[:, br * HEADS_PER_GROUP + hh:br * HEADS_PER_GROUP + hh + 1]
        parts = []
        for acc in (acc_sel[r0:r0 + tq], acc_win[r0:r0 + tq]):
            swapped = pltpu.roll(acc, HEAD_DIM, 1)
            parts.append(acc / swapped if hh % 2 == 0 else swapped / acc)
        outs.append(col(1) * parts[0] + col(2) * parts[1])
    o_ref[...] = (o_cmp + jnp.concatenate(
        [jnp.where(low, outs[0], outs[1]), jnp.where(low, outs[2], outs[3])], axis=1)).astype(o_ref.dtype)


def _attn(q, kvc, ksd, vsd, kwd, vwd, ng, e_mat, cs_mat, batch, seq):
    t = q.shape[0]
    tq = TQ
    nq = seq // tq
    qrow = lambda b, g, i: (b * nq + i, g)
    kv = lambda b, g, i: (b, g)
    const = lambda b, g, i: (0, 0)
    return pl.pallas_call(
        _attn_kernel,
        grid=(batch, N_KV_GROUPS, nq),
        in_specs=[
            pl.BlockSpec((tq, HEADS_PER_GROUP * HEAD_DIM), qrow),
            pl.BlockSpec((1, N_CMP_PAD, LANES), lambda b, g, i: (0, b * N_KV_GROUPS + g, 0)),
            pl.BlockSpec((1, N_CMP_PAD, LANES), lambda b, g, i: (1, b * N_KV_GROUPS + g, 0)),
            pl.BlockSpec((seq, LANES), kv),
            pl.BlockSpec((seq, LANES), kv),
            pl.BlockSpec((seq, LANES), kv),
            pl.BlockSpec((seq, LANES), kv),
            pl.BlockSpec((tq, LANES), qrow),
            pl.BlockSpec((seq, LANES), const),
            pl.BlockSpec((N_CMP_PAD, LANES), const),
        ],
        out_specs=pl.BlockSpec((tq, HEADS_PER_GROUP * HEAD_DIM), qrow),
        out_shape=jax.ShapeDtypeStruct((t, Q_DIM), BF16),
        compiler_params=pltpu.CompilerParams(
            dimension_semantics=("parallel", "parallel", "arbitrary"), vmem_limit_bytes=VMEM_LIMIT),
        name="attn",
    )(q, kvc, kvc, ksd, vsd, kwd, vwd, ng, e_mat, cs_mat)


def _post_kernel(o_ref, ya_ref, x_ref, g1_ref, wg_ref, wn_ref, wm_ref, g2_ref, wr_ref, rb_ref,
                 xa_ref):
    hn = _normed(x_ref[...], g1_ref[...])
    gate_a = jax.nn.sigmoid(_dot(hn, wg_ref[:, 0:D_MODEL]))
    gate_b = jax.nn.sigmoid(_dot(hn, wg_ref[:, D_MODEL:2 * D_MODEL]))
    y_b = _dot(o_ref[...], wn_ref[...])
    mixed = gate_a * ya_ref[...] + gate_b * y_b
    x1 = x_ref[...] + _dot(mixed.astype(BF16), wm_ref[...])
    xa_ref[:, 0:D_MODEL] = x1
    h2 = x1 * _rms_scale(x1) * g2_ref[...]
    h_hi = h2.astype(BF16)
    h_lo = (h2 - h_hi.astype(F32)).astype(BF16)
    both = _dot(h_hi, wr_ref[...])
    logits = both[:, 0:LANES] + both[:, LANES:2 * LANES] + _dot(h_lo, wr_ref[:, 0:LANES]) + rb_ref[...]

    lane = _lane_iota(logits.shape).astype(F32)
    is_grp = (lane >= N_EXPERTS) & (lane < N_EXPERTS + N_GROUPS)
    gl = jnp.where(is_grp, logits, NEG_INF)
    ge = jnp.exp(gl - jnp.max(gl, axis=1, keepdims=True))
    gp = ge / jnp.sum(ge, axis=1, keepdims=True)
    g_w = jnp.max(gp, axis=1, keepdims=True)
    big = float(4 * LANES)
    g_first = jnp.min(jnp.where(is_grp & (gp == g_w), lane, big), axis=1, keepdims=True)
    grp_lo = (g_first - N_EXPERTS) * EXPERTS_PER_GROUP

    in_grp = (lane >= grp_lo) & (lane < grp_lo + EXPERTS_PER_GROUP)
    el = jnp.where(in_grp, logits, NEG_INF)
    ee = jnp.exp(el - jnp.max(el, axis=1, keepdims=True))
    ep = ee / jnp.sum(ee, axis=1, keepdims=True)
    w1 = jnp.max(ep, axis=1, keepdims=True)
    i1 = jnp.min(jnp.where(in_grp & (ep == w1), lane, big), axis=1, keepdims=True)
    rest = jnp.where(in_grp & (lane != i1), ep, -1.0)
    w2 = jnp.max(rest, axis=1, keepdims=True)
    i2 = jnp.min(jnp.where(rest == w2, lane, big), axis=1, keepdims=True)
    den = w1 + w2
    comb = jnp.where(lane == i1, g_w * (w1 / den), jnp.where(lane == i2, g_w * (w2 / den), 0.0))
    xa_ref[:, D_MODEL:D_MODEL + LANES] = jnp.where(lane == GROUP_LANE, g_first - N_EXPERTS, comb)


def _post(o_nsa, y_a, x2, g1, w_mg, w_nsa, w_mix, g2, wr, rb):
    t = x2.shape[0]
    tm = TM_POST
    row = lambda i: (i, 0)
    const = lambda i: (0, 0)
    return pl.pallas_call(
        _post_kernel,
        grid=(t // tm,),
        in_specs=[
            pl.BlockSpec((tm, Q_DIM), row),
            pl.BlockSpec((tm, D_MODEL), row),
            pl.BlockSpec((tm, D_MODEL), row),
            pl.BlockSpec((1, D_MODEL), const),
            pl.BlockSpec((D_MODEL, 2 * D_MODEL), const),
            pl.BlockSpec((Q_DIM, D_MODEL), const),
            pl.BlockSpec((D_MODEL, D_MODEL), const),
            pl.BlockSpec((1, D_MODEL), const),
            pl.BlockSpec((D_MODEL, 2 * LANES), const),
            pl.BlockSpec((1, LANES), const),
        ],
        out_specs=pl.BlockSpec((tm, D_AUG), row),
        out_shape=jax.ShapeDtypeStruct((t, D_AUG), F32),
        compiler_params=pltpu.CompilerParams(
            dimension_semantics=("parallel",), vmem_limit_bytes=VMEM_LIMIT),
        name="post",
    )(o_nsa, y_a, x2, g1, w_mg, w_nsa, w_mix, g2, wr, rb)


def _plan_kernel(rec_ref, tri_ref, info_ref, counts_ref, carry_s):
    @pl.when(pl.program_id(0) == 0)
    def _():
        carry_s[...] = jnp.zeros_like(carry_s)

    rec = rec_ref[...]
    lane = _lane_iota(rec.shape)
    gid = rec[:, GROUP_LANE:GROUP_LANE + 1]
    onehot = jnp.where(lane.astype(F32) == gid, 1.0, 0.0)
    before = _dot(tri_ref[...], onehot.astype(BF16)) + carry_s[0:1, :]
    rank = jnp.sum(onehot * before, axis=1, keepdims=True)
    info_ref[...] = jnp.where(lane == 0, rank, jnp.where(lane == 1, gid, 0.0))
    carry_s[...] = carry_s[...] + jnp.sum(onehot, axis=0, keepdims=True)
    counts_ref[...] = carry_s[...]


def _plan(xa, tri):
    t = xa.shape[0]
    tm = TM_PLAN
    return pl.pallas_call(
        _plan_kernel,
        grid=(t // tm,),
        in_specs=[
            pl.BlockSpec((tm, LANES), lambda i: (i, D_MODEL // LANES)),
            pl.BlockSpec((tm, tm), lambda i: (0, 0)),
        ],
        out_specs=(
            pl.BlockSpec((tm, LANES), lambda i: (i, 0)),
            pl.BlockSpec((SUBLANES, LANES), lambda i: (0, 0)),
        ),
        out_shape=(
            jax.ShapeDtypeStruct((t, LANES), F32),
            jax.ShapeDtypeStruct((SUBLANES, LANES), F32),
        ),
        scratch_shapes=[pltpu.VMEM((SUBLANES, LANES), F32)],
        compiler_params=pltpu.CompilerParams(
            dimension_semantics=("arbitrary",), vmem_limit_bytes=VMEM_LIMIT),
        name="plan",
    )(xa, tri)


def _row_copies(n_rows, make_copy):
    def start(g, carry):
        r0 = pl.multiple_of(g * ROW_COPY_UNROLL, ROW_COPY_UNROLL)
        for k in range(ROW_COPY_UNROLL):
            make_copy(r0, k).start(priority=k % 2)
        return carry

    def wait(g, carry):
        for _ in range(ROW_COPY_UNROLL):
            make_copy(0, 0).wait()
        return carry

    lax.fori_loop(0, n_rows // ROW_COPY_UNROLL, start, 0)
    lax.fori_loop(0, n_rows // ROW_COPY_UNROLL, wait, 0)


def _dispatch_kernel(slot_ref, xa_ref, zero_hbm, xs_hbm, sem):
    del zero_hbm
    first = pl.program_id(0) * xa_ref.shape[0]

    def copy(r0, k):
        slot = slot_ref[first + r0 + k]
        return pltpu.make_async_copy(xa_ref.at[pl.ds(r0 + k, 1), :], xs_hbm.at[pl.ds(slot, 1), :], sem)

    _row_copies(xa_ref.shape[0], copy)


def _dispatch(slot, xa, n_sorted):
    t = xa.shape[0]
    tm = TM_ROWS
    zeros = jnp.zeros((n_sorted, D_AUG), F32)
    return pl.pallas_call(
        _dispatch_kernel,
        grid_spec=pltpu.PrefetchScalarGridSpec(
            num_scalar_prefetch=1,
            grid=(t // tm,),
            in_specs=[
                pl.BlockSpec((tm, D_AUG), lambda i, *_: (i, 0)),
                pl.BlockSpec(memory_space=pl.ANY),
            ],
            out_specs=pl.BlockSpec(memory_space=pl.ANY),
            scratch_shapes=[pltpu.SemaphoreType.DMA],
        ),
        out_shape=jax.ShapeDtypeStruct((n_sorted, D_AUG), F32),
        input_output_aliases={2: 0},
        compiler_params=pltpu.CompilerParams(
            dimension_semantics=("arbitrary",), vmem_limit_bytes=VMEM_LIMIT),
        name="dispatch",
    )(slot, xa, zeros)


def _experts_kernel(widx_ref, used_ref, tgrp_ref, xs_ref, wg_ref, wu_ref, wd_ref, g2_ref, gf_ref,
                    fs_ref, h_s, y_s):
    del widx_ref
    j = pl.program_id(0)
    e = pl.program_id(1)
    used = used_ref[j] == 1

    @pl.when(used & (e == 0))
    def _():
        x1 = xs_ref[:, 0:D_MODEL]
        h_s[...] = _normed(x1, g2_ref[...])
        y_s[...] = jnp.zeros_like(y_s)

    @pl.when(used)
    def _():
        h = h_s[...]
        act = jax.nn.silu(_dot(h, wg_ref[0].astype(BF16))) * _dot(h, wu_ref[0].astype(BF16))
        y = _dot(act.astype(BF16), wd_ref[0].astype(BF16))
        rec = xs_ref[:, D_MODEL:D_AUG]
        col = tgrp_ref[j] * EXPERTS_PER_GROUP + e
        weight = jnp.sum(jnp.where(_lane_iota(rec.shape) == col, rec, 0.0), axis=1, keepdims=True)
        y_s[...] += weight * y

    @pl.when(used & (e == EXPERTS_PER_GROUP - 1))
    def _():
        x = xs_ref[:, 0:D_MODEL] + y_s[...]
        fs_ref[...] = x * _rms_scale(x) * gf_ref[...]

    @pl.when(jnp.logical_not(used) & (e == EXPERTS_PER_GROUP - 1))
    def _():
        fs_ref[...] = jnp.zeros_like(fs_ref)


def _experts(widx, used, tgrp, xs, wg, wu, wd, g2, gf):
    n_sorted = xs.shape[0]
    tm = TM_EXP
    row = lambda j, e, *_: (j, 0)
    wsel = lambda j, e, widx, used, tgrp: (widx[j * EXPERTS_PER_GROUP + e], 0, 0)
    const = lambda j, e, *_: (0, 0)
    return pl.pallas_call(
        _experts_kernel,
        grid_spec=pltpu.PrefetchScalarGridSpec(
            num_scalar_prefetch=3,
            grid=(n_sorted // tm, EXPERTS_PER_GROUP),
            in_specs=[
                pl.BlockSpec((tm, D_AUG), row),
                pl.BlockSpec((1, D_MODEL, D_EXPERT), wsel),
                pl.BlockSpec((1, D_MODEL, D_EXPERT), wsel),
                pl.BlockSpec((1, D_EXPERT, D_MODEL), wsel),
                pl.BlockSpec((1, D_MODEL), const),
                pl.BlockSpec((1, D_MODEL), const),
            ],
            out_specs=pl.BlockSpec((tm, D_MODEL), row),
            scratch_shapes=[pltpu.VMEM((tm, D_MODEL), BF16), pltpu.VMEM((tm, D_MODEL), F32)],
        ),
        out_shape=jax.ShapeDtypeStruct((n_sorted, D_MODEL), F32),
        compiler_params=pltpu.CompilerParams(
            dimension_semantics=("arbitrary", "arbitrary"), vmem_limit_bytes=VMEM_LIMIT),
        name="experts",
    )(widx, used, tgrp, xs, wg, wu, wd, g2, gf)


def _combine_kernel(slot_ref, fs_hbm, o_ref, sem):
    first = pl.program_id(0) * o_ref.shape[0]

    def copy(r0, k):
        slot = slot_ref[first + r0 + k]
        return pltpu.make_async_copy(fs_hbm.at[pl.ds(slot, 1), :], o_ref.at[pl.ds(r0 + k, 1), :], sem)

    _row_copies(o_ref.shape[0], copy)


def _combine(slot, fs, t):
    tm = TM_ROWS
    return pl.pallas_call(
        _combine_kernel,
        grid_spec=pltpu.PrefetchScalarGridSpec(
            num_scalar_prefetch=1,
            grid=(t // tm,),
            in_specs=[pl.BlockSpec(memory_space=pl.ANY)],
            out_specs=pl.BlockSpec((tm, D_MODEL), lambda i, *_: (i, 0)),
            scratch_shapes=[pltpu.SemaphoreType.DMA],
        ),
        out_shape=jax.ShapeDtypeStruct((t, D_MODEL), F32),
        compiler_params=pltpu.CompilerParams(
            dimension_semantics=("arbitrary",), vmem_limit_bytes=VMEM_LIMIT),
        name="combine",
    )(slot, fs)


def _moe_sorted(xa, wg, wu, wd, g2, gf):
    t = xa.shape[0]
    tri = jnp.asarray(np.tril(np.ones((TM_PLAN, TM_PLAN), np.float32), -1), dtype=BF16)
    info, counts = _plan(xa, tri)
    rank = info[:, 0].astype(jnp.int32)
    gid = info[:, 1].astype(jnp.int32)
    counts = counts[0, :N_GROUPS].astype(jnp.int32)

    n_tiles = t // TM_EXP + N_GROUPS
    tiles_g = (counts + TM_EXP - 1) // TM_EXP
    tile_end = jnp.cumsum(tiles_g)
    base = (tile_end - tiles_g) * TM_EXP
    groups = jnp.arange(N_GROUPS, dtype=jnp.int32)
    slot = jnp.sum(jnp.where(gid[:, None] == groups[None, :], base[None, :], 0), axis=1) + rank
    tile_ids = jnp.arange(n_tiles, dtype=jnp.int32)
    used = (tile_ids < tile_end[-1]).astype(jnp.int32)
    last = jnp.maximum(tile_end[-1] - 1, 0)
    tgrp = jnp.sum((jnp.minimum(tile_ids, last)[:, None] >= tile_end[None, :]).astype(jnp.int32), axis=1)
    step_e = jnp.arange(EXPERTS_PER_GROUP, dtype=jnp.int32)[None, :]
    widx = jnp.where(used[:, None] == 1, tgrp[:, None] * EXPERTS_PER_GROUP + step_e,
                     tgrp[:, None] * EXPERTS_PER_GROUP + EXPERTS_PER_GROUP - 1).reshape(-1)

    xs = _dispatch(slot, xa, n_tiles * TM_EXP)
    fs = _experts(widx, used, tgrp, xs, wg, wu, wd, g2, gf)
    return _combine(slot, fs, t)


def _split_w_in(w):
    sizes = [D_RNN, D_RNN, Q_DIM] + [KV_DIM] * 6 + [3 * N_HEADS, 2 * D_MODEL]
    pts = np.cumsum(sizes)[:-1]
    return jnp.split(w, [int(p) for p in pts], axis=-1)


def _proj_weights(w_in):
    xr, gr, q, kc, vc, ksl, vsl, kw, vw, nsa_g, merge_g = _split_w_in(w_in)
    ng = nsa_g.reshape(D_MODEL, 3, N_KV_GROUPS, HEADS_PER_GROUP).transpose(0, 2, 1, 3)
    ng = ng.reshape(D_MODEL, N_KV_GROUPS, 3 * HEADS_PER_GROUP)
    ng = jnp.pad(ng, ((0, 0), (0, 0), (0, LANES - 3 * HEADS_PER_GROUP))).reshape(D_MODEL, N_KV_GROUPS * LANES)
    w_attn = jnp.concatenate([q, kc, vc, ksl, vsl, kw, vw, ng], axis=1)
    return (jnp.concatenate([xr, gr], axis=1).astype(BF16), w_attn.astype(BF16), merge_g.astype(BF16))


def _rope_tables(pos, width):
    inv_freq = ROPE_THETA ** (-(jnp.arange(0, HEAD_DIM, 2, dtype=F32) / HEAD_DIM))
    ang = pos.astype(F32)[:, None] * inv_freq[None, :]
    cos, sin = jnp.cos(ang), jnp.sin(ang)
    reps = width // HEAD_DIM
    return jnp.tile(jnp.concatenate([cos, cos], axis=1), (1, reps)), jnp.tile(jnp.concatenate([-sin, sin], axis=1), (1, reps))


def _block_diag(w, per):
    nb, d = w.shape[0], w.shape[1]
    w = w.reshape(nb // per, per, d, d)
    eye = jnp.eye(per, dtype=w.dtype)
    return jnp.einsum('npij,pq->npiqj', w, eye).reshape(nb // per, per * d, per * d)


def _cmp_to_sel():
    n_sel_pad = LANES
    c0 = np.arange(N_CMP_PAD) * CMP_STRIDE
    s0 = np.arange(n_sel_pad) * SEL_BLOCK
    ov = np.minimum(c0[:, None] + CMP_BLOCK, s0[None, :] + SEL_BLOCK) - np.maximum(c0[:, None], s0[None, :])
    m = np.clip(ov, 0, None) / CMP_BLOCK
    m[:, SEL_BLOCK // 2:] = 0.0
    m[N_CMP_PAD - 1, :] = 0.0
    return m.astype(np.float32)


def kernel(x, norm1_g, w_in, conv_w, conv_b, lru_wa, lru_ba, lru_wi, lru_bi, lru_lambda, w_rnn_out, cmpk_pos, cmpk_w1, cmpk_b1, cmpk_w2, cmpk_b2, cmpv_pos, cmpv_w1, cmpv_b1, cmpv_w2, cmpv_b2, w_nsa_out, w_mix_out, norm2_g, router_group_w, router_group_b, router_expert_w, router_expert_b, expert_w_gate, expert_w_up, expert_w_down, final_norm_g):
    batch, seq, _ = x.shape
    t = batch * seq
    assert w_in.shape[0] == 1, "the final norm is fused into the expert kernel: single layer only"
    assert seq % max(TS_RNN, TQ, CK_SEL, TM_PROJ) == 0, "sequence tiles must divide the sequence length"
    assert t % max(TM_POST, TM_PLAN, TM_ROWS, TM_EXP) == 0, "token tiles must divide batch * seq"
    x2 = x.reshape(t, D_MODEL)

    cos, sin = _rope_tables(jnp.arange(seq), LANES)
    cmp_ends = jnp.arange(N_CMP_PAD) * CMP_STRIDE + (CMP_BLOCK - 1)
    ccos, csin = _rope_tables(cmp_ends, LANES)
    reps = TR_CMP // N_CMP_PAD
    cmp_cos = jnp.stack([jnp.tile(ccos, (reps, 1)), jnp.ones((TR_CMP, LANES), F32)])
    cmp_sin = jnp.stack([jnp.tile(csin, (reps, 1)), jnp.zeros((TR_CMP, LANES), F32)])
    key_blk = np.arange(seq)[:, None] // SEL_BLOCK
    e_mat = jnp.asarray((key_blk == np.arange(LANES)[None, :]).astype(np.float32), dtype=BF16)
    cs_mat = jnp.asarray(_cmp_to_sel(), dtype=BF16)

    l = 0
    w_xg, w_attn, w_mg = _proj_weights(w_in[l])
    g1 = norm1_g[l][None, :]
    q, kcv, ksd, vsd, kwd, vwd, ng = _inproj(x2, g1, w_attn, cos, sin, seq)

    y_a = _rnn(x2, g1, w_xg, conv_w[l], conv_b[l][None, :],
               _block_diag(lru_wa[l], 4).astype(BF16), lru_ba[l][None, :],
               _block_diag(lru_wi[l], 4).astype(BF16), lru_bi[l][None, :],
               lru_lambda[l][None, :], w_rnn_out[l].astype(BF16), batch, seq)

    half = CMP_STRIDE * HEAD_DIM
    pos = jnp.stack([cmpk_pos[l].reshape(2, half), cmpv_pos[l].reshape(2, half)])
    w1 = jnp.stack([cmpk_w1[l], cmpv_w1[l]]).astype(BF16)
    b1 = jnp.stack([cmpk_b1[l], cmpv_b1[l]])[:, None, :]
    w2 = jnp.stack([cmpk_w2[l], cmpv_w2[l]])
    w2 = jnp.concatenate([w2, w2], axis=2).astype(BF16)
    b2 = jnp.stack([cmpk_b2[l], cmpv_b2[l]])
    b2 = jnp.concatenate([b2, b2], axis=1)[:, None, :]
    kvc = _compress(kcv, pos, w1, b1, w2, b2, cmp_cos, cmp_sin, batch, seq)

    o_nsa = _attn(q, kvc, ksd, vsd, kwd, vwd, ng, e_mat, cs_mat, batch, seq)

    wr = jnp.concatenate([
        router_expert_w[l].transpose(1, 0, 2).reshape(D_MODEL, N_EXPERTS),
        router_group_w[l],
        jnp.zeros((D_MODEL, LANES - N_EXPERTS - N_GROUPS), F32)], axis=1)
    wr_hi = wr.astype(BF16)
    wr_lo = (wr - wr_hi.astype(F32)).astype(BF16)
    rb = jnp.concatenate([router_expert_b[l].reshape(N_EXPERTS), router_group_b[l],
                          jnp.zeros((LANES - N_EXPERTS - N_GROUPS,), F32)])[None, :]
    xa = _post(o_nsa, y_a, x2, g1, w_mg, w_nsa_out[l].astype(BF16), w_mix_out[l].astype(BF16),
               norm2_g[l][None, :], jnp.concatenate([wr_hi, wr_lo], axis=1), rb)

    out = _moe_sorted(xa, expert_w_gate[l], expert_w_up[l], expert_w_down[l],
                      norm2_g[l][None, :], final_norm_g[None, :])
    return out.reshape(batch, seq, D_MODEL)
```
